```python
import math
import jax, jax.numpy as jnp
from jax import lax
import numpy as np

D_MODEL = 1024
BATCH = 1
SEQ = 16384
DEPTH = 4

GRID_W = 64
CTX_LEN = 256
EPS = 1e-6

N_BRANCH = 3
BRANCH_WIDTH = 512

S5_WIDTH = BRANCH_WIDTH
S5_GROUP_CH = 16
S5_GROUPS = S5_WIDTH // S5_GROUP_CH
S5_STATE = 64
S5_DT_MIN = 1e-3
S5_DT_MAX = 1e-1

SSD_WIDTH = BRANCH_WIDTH
SSD_HEAD_DIM = 64
SSD_HEADS = SSD_WIDTH // SSD_HEAD_DIM
SSD_GROUPS = 2
SSD_STATE = 128
SSD_CONV = 5
SSD_CHUNK = 128
SSD_CONV_CH = SSD_WIDTH + 2 * SSD_GROUPS * SSD_STATE

ATTN_HEADS = 8
ATTN_KV_HEADS = 2
ATTN_HEAD_DIM = 64
ATTN_WIDTH = ATTN_HEADS * ATTN_HEAD_DIM
ATTN_WINDOW = 128
ATTN_BLOCK = 128
ROPE_BASE = 10000.0

IN_SIZES = (S5_WIDTH, SSD_WIDTH, SSD_CONV_CH, 2 * SSD_HEADS, ATTN_WIDTH,
            ATTN_KV_HEADS * ATTN_HEAD_DIM, ATTN_KV_HEADS * ATTN_HEAD_DIM, N_BRANCH * D_MODEL)
IN_WIDTH = sum(IN_SIZES)

N_EXPERTS = 16
EXPERT_FF = 1024
EC_CAPACITY_FACTOR = 2

kernel_name = "hybrid_s5_ssd_swa_ec_dit"

F32 = jnp.float32


def rmsnorm(x, g):
    xf = x.astype(F32)
    y = xf * lax.rsqrt(jnp.mean(xf * xf, axis=-1, keepdims=True) + EPS)
    return (y * g.astype(F32)).astype(x.dtype)


def split_cols(t):
    out, start = [], 0
    for s in IN_SIZES:
        out.append(t[..., start:start + s])
        start += s
    return out


def cmul(ar, ai, br, bi):
    return ar * br - ai * bi, ar * bi + ai * br


def complex_affine_combine(e1, e2):
    a1r, a1i, b1r, b1i = e1
    a2r, a2i, b2r, b2i = e2
    ar, ai = cmul(a2r, a2i, a1r, a1i)
    tr, ti = cmul(a2r, a2i, b1r, b1i)
    return ar, ai, tr + b2r, ti + b2i


def s5_discretise(lam_re, lam_im, log_dt, b_re, b_im):
    lr, li = lam_re.astype(F32), lam_im.astype(F32)
    dt = jnp.exp(log_dt.astype(F32))[:, None]
    mag = jnp.exp(lr * dt)
    abr, abi = mag * jnp.cos(li * dt), mag * jnp.sin(li * dt)
    den = lr * lr + li * li
    fr = ((abr - 1.0) * lr + abi * li) / den
    fi = (abi * lr - (abr - 1.0) * li) / den
    bbr, bbi = cmul(fr[..., None], fi[..., None], b_re.astype(F32), b_im.astype(F32))
    return abr, abi, bbr, bbi


def s5_scan(u, abr, abi, bbr, bbi, h0, reverse):
    if reverse:
        u = jnp.flip(u, 1)
    bur = jnp.einsum('gpk,bngk->bngp', bbr, u)
    bui = jnp.einsum('gpk,bngk->bngp', bbi, u)
    ar = jnp.broadcast_to(abr, bur.shape)
    ai = jnp.broadcast_to(abi, bui.shape)
    acr, aci, hr, hi = lax.associative_scan(complex_affine_combine, (ar, ai, bur, bui), axis=1)
    if h0 is not None:
        tr, ti = cmul(acr, aci, h0[0][:, None], h0[1][:, None])
        hr, hi = hr + tr, hi + ti
    return hr, hi


def s5_readout(hr, hi, c_re, c_im, reverse):
    y = (jnp.einsum('gkp,bngp->bngk', c_re.astype(F32), hr)
         - jnp.einsum('gkp,bngp->bngk', c_im.astype(F32), hi))
    return jnp.flip(y, 1) if reverse else y


def s5_branch(u, uc, lam_re, lam_im, log_dt, b_re, b_im, c_re, c_im, d_skip, w_glu, b_glu, need_ctx_out):
    b, n, _ = u.shape
    nc = uc.shape[1]
    uf, ucf = u.astype(F32), uc.astype(F32)
    ug = uf.reshape(b, n, S5_GROUPS, S5_GROUP_CH)
    ugc = ucf.reshape(b, nc, S5_GROUPS, S5_GROUP_CH)
    ys, ycs = [], []
    for direction in range(2):
        rev = direction == 1
        abr, abi, bbr, bbi = s5_discretise(lam_re[direction], lam_im[direction], log_dt[direction],
                                           b_re[direction], b_im[direction])
        hcr, hci = s5_scan(ugc, abr, abi, bbr, bbi, None, rev)
        hr, hi = s5_scan(ug, abr, abi, bbr, bbi, (hcr[:, -1], hci[:, -1]), rev)
        ys.append(s5_readout(hr, hi, c_re[direction], c_im[direction], rev))
        if need_ctx_out:
            ycs.append(s5_readout(hcr, hci, c_re[direction], c_im[direction], rev))
    d = d_skip.astype(F32)

    def glu(yg, uu):
        yy = yg.reshape(uu.shape) + d * uu
        a = jax.nn.gelu(yy)
        return (a * jax.nn.sigmoid(a @ w_glu.astype(F32) + b_glu.astype(F32))).astype(u.dtype)

    y = glu(ys[0] + ys[1], uf)
    yc = glu(ycs[0] + ycs[1], ucf) if need_ctx_out else None
    return y, yc


def dwconv_centred(x, w, bias):
    k = w.shape[0]
    y = lax.conv_general_dilated(x, w[:, None, :], window_strides=(1,), padding=[(k // 2, k // 2)],
                                 dimension_numbers=('NWC', 'WIO', 'NWC'), feature_group_count=x.shape[-1])
    return y + bias


def ssd_scan(x, dt, a, bm, cm, h0, reverse, need_y):
    if reverse:
        x, dt, bm, cm = (jnp.flip(t, 1) for t in (x, dt, bm, cm))
    b, n, h, p = x.shape
    g, nst = bm.shape[2], bm.shape[3]
    r = h // g
    q = SSD_CHUNK
    nch = n // q
    xc = x.reshape(b, nch, q, g, r, p)
    dtc = dt.reshape(b, nch, q, g, r)
    bc = bm.reshape(b, nch, q, g, nst)
    cc = cm.reshape(b, nch, q, g, nst)
    acum = jnp.cumsum(dtc * a.reshape(g, r), axis=2)
    decay_to_end = jnp.exp(acum[:, :, -1:] - acum)
    states = jnp.einsum('bcjgn,bcjgrp->bcgrpn', bc, (decay_to_end * dtc)[..., None] * xc)
    chunk_decay = jnp.exp(acum[:, :, -1])
    if h0 is None:
        h0 = jnp.zeros((b, g, r, p, nst), F32)

    def step(hc, inp):
        s, dec = inp
        return hc * dec[..., None, None] + s, hc

    h_final, h_in = lax.scan(step, h0, (jnp.moveaxis(states, 1, 0), jnp.moveaxis(chunk_decay, 1, 0)))
    if not need_y:
        return None, h_final
    h_in = jnp.moveaxis(h_in, 0, 1)
    causal = jnp.tril(jnp.ones((q, q), bool))
    seg = acum[:, :, :, None] - acum[:, :, None, :]
    lmat = jnp.exp(jnp.where(causal[:, :, None, None], seg, -jnp.inf))
    cb = jnp.einsum('bcign,bcjgn->bcijg', cc, bc)
    wts = cb[..., None] * lmat * dtc[:, :, None]
    y = jnp.einsum('bcijgr,bcjgrp->bcigrp', wts, xc)
    y = y + jnp.einsum('bcign,bcgrpn->bcigrp', cc, h_in) * jnp.exp(acum)[..., None]
    y = y.reshape(b, n, h, p)
    if reverse:
        y = jnp.flip(y, 1)
    return y, h_final


def ssd_branch(z, xbc, dtr, zc, xbcc, dtrc, conv_w, conv_b, a_log, dt_bias, d_skip, norm_g, need_ctx_out):
    a = -jnp.exp(a_log.astype(F32))

    def prep(xbc_, dtr_):
        t = jax.nn.silu(dwconv_centred(xbc_, conv_w, conv_b)).astype(F32)
        b, n, _ = t.shape
        o1 = SSD_WIDTH
        o2 = SSD_WIDTH + SSD_GROUPS * SSD_STATE
        xs = t[..., :o1].reshape(b, n, SSD_HEADS, SSD_HEAD_DIM)
        bm = t[..., o1:o2].reshape(b, n, SSD_GROUPS, SSD_STATE)
        cm = t[..., o2:].reshape(b, n, SSD_GROUPS, SSD_STATE)
        dt = jax.nn.softplus(dtr_.astype(F32).reshape(b, n, 2, SSD_HEADS) + dt_bias.astype(F32))
        return xs, bm, cm, dt

    xs, bm, cm, dt = prep(xbc, dtr)
    xsc, bmc, cmc, dtc = prep(xbcc, dtrc)
    d = d_skip.astype(F32)[:, None]
    y = d * xs
    yc = d * xsc if need_ctx_out else None
    for direction in range(2):
        rev = direction == 1
        y_ctx, h_ctx = ssd_scan(xsc, dtc[:, :, direction], a[direction], bmc, cmc, None, rev, need_ctx_out)
        y_lat, _ = ssd_scan(xs, dt[:, :, direction], a[direction], bm, cm, h_ctx, rev, True)
        y = y + y_lat
        if need_ctx_out:
            yc = yc + y_ctx

    def gated_norm(yy, zz):
        b, n = yy.shape[:2]
        return rmsnorm(yy.reshape(b, n, SSD_WIDTH) * jax.nn.silu(zz.astype(F32)), norm_g).astype(zz.dtype)

    return gated_norm(y, z), (gated_norm(yc, zc) if need_ctx_out else None)


def axial_rope_angles(n):
    rows = n // GRID_W
    row = jnp.repeat(jnp.arange(rows, dtype=F32), GRID_W)
    col = jnp.tile(jnp.arange(GRID_W, dtype=F32), rows)
    m = ATTN_HEAD_DIM // 4
    inv_freq = ROPE_BASE ** (-jnp.arange(m, dtype=F32) / m)
    return row[:, None] * inv_freq, col[:, None] * inv_freq


def rope_axis(x, ang):
    m = ang.shape[-1]
    cos, sin = jnp.cos(ang)[:, None, :], jnp.sin(ang)[:, None, :]
    x1, x2 = x[..., :m], x[..., m:]
    return jnp.concatenate([x1 * cos - x2 * sin, x2 * cos + x1 * sin], axis=-1)


def axial_rope(x, ang_r, ang_c):
    half = x.shape[-1] // 2
    return jnp.concatenate([rope_axis(x[..., :half], ang_r), rope_axis(x[..., half:], ang_c)], axis=-1)


def window_attention(q, k, v, kc, vc, sink):
    b, n, hk, r, dh = q.shape
    blk = ATTN_BLOCK
    nb = n // blk
    scale = dh ** -0.5
    qb = q.reshape(b, nb, blk, hk, r, dh)

    def band(t):
        tp = jnp.pad(t, ((0, 0), (blk, blk), (0, 0), (0, 0))).reshape(b, nb + 2, blk, hk, dh)
        return jnp.concatenate([tp[:, :-2], tp[:, 1:-1], tp[:, 2:]], axis=2)

    kw, vw = band(k), band(v)
    s_loc = jnp.einsum('bnqhrd,bnkhd->bnhrqk', qb, kw) * scale
    qi = jnp.arange(blk)
    kj = jnp.arange(3 * blk) - blk
    rel = jnp.abs(qi[:, None] - kj[None, :]) <= ATTN_WINDOW
    kpos = jnp.arange(nb)[:, None] * blk + kj[None, :]
    valid = (kpos >= 0) & (kpos < n)
    mask = rel[None] & valid[:, None, :]
    s_loc = jnp.where(mask[None, :, None, None], s_loc, -jnp.inf)
    s_ctx = jnp.einsum('bnqhrd,bkhd->bnhrqk', qb, kc) * scale
    s_sink = jnp.broadcast_to(sink[None, None, :, :, None, None], s_loc.shape[:-1] + (1,))
    p = jax.nn.softmax(jnp.concatenate([s_loc, s_ctx, s_sink], axis=-1), axis=-1)
    nk = 3 * blk
    o = (jnp.einsum('bnhrqk,bnkhd->bnqhrd', p[..., :nk], vw)
         + jnp.einsum('bnhrqk,bkhd->bnqhrd', p[..., nk:nk + kc.shape[1]], vc))
    return o.reshape(b, n, hk * r * dh)


def context_attention(qc, kc, vc, sink):
    b, nc, hk, r, dh = qc.shape
    s = jnp.einsum('bqhrd,bkhd->bhrqk', qc, kc) * dh ** -0.5
    s_sink = jnp.broadcast_to(sink[None, :, :, None, None], s.shape[:-1] + (1,))
    p = jax.nn.softmax(jnp.concatenate([s, s_sink], axis=-1), axis=-1)[..., :kc.shape[1]]
    return jnp.einsum('bhrqk,bkhd->bqhrd', p, vc).reshape(b, nc, hk * r * dh)


def attention_branch(q, k, v, qc, kc, vc, sink, need_ctx_out):
    b, n, _ = q.shape
    nc = kc.shape[1]
    hk, r, dh = ATTN_KV_HEADS, ATTN_HEADS // ATTN_KV_HEADS, ATTN_HEAD_DIM
    ang_r, ang_c = axial_rope_angles(n)
    qh = axial_rope(q.astype(F32).reshape(b, n, ATTN_HEADS, dh), ang_r, ang_c).reshape(b, n, hk, r, dh)
    kh = axial_rope(k.astype(F32).reshape(b, n, hk, dh), ang_r, ang_c)
    vh = v.astype(F32).reshape(b, n, hk, dh)
    kch = kc.astype(F32).reshape(b, nc, hk, dh)
    vch = vc.astype(F32).reshape(b, nc, hk, dh)
    sink_hr = sink.astype(F32).reshape(hk, r)
    y = window_attention(qh, kh, vh, kch, vch, sink_hr).astype(q.dtype)
    yc = None
    if need_ctx_out:
        qch = qc.astype(F32).reshape(b, nc, hk, r, dh)
        yc = context_attention(qch, kch, vch, sink_hr).astype(q.dtype)
    return y, yc


def mixing_sublayer(h, hc, w_in, lam_re, lam_im, log_dt, b_re, b_im, c_re, c_im, s5_d, w_glu, b_glu,
                    conv_w, conv_b, a_log, dt_bias, ssd_d, ssd_norm_g, sink, w_branch, w_out, need_ctx_out):
    u, z, xbc, dtr, q, k, v, gates = split_cols(h @ w_in)
    uc, zc, xbcc, dtrc, qc, kc, vc, gatesc = split_cols(hc @ w_in)
    ya, yac = s5_branch(u, uc, lam_re, lam_im, log_dt, b_re, b_im, c_re, c_im, s5_d, w_glu, b_glu, need_ctx_out)
    yb, ybc = ssd_branch(z, xbc, dtr, zc, xbcc, dtrc, conv_w, conv_b, a_log, dt_bias, ssd_d, ssd_norm_g,
                         need_ctx_out)
    yc_, ycc = attention_branch(q, k, v, qc, kc, vc, sink, need_ctx_out)

    def merge(y1, y2, y3, g):
        b, n = g.shape[:2]
        br = jnp.einsum('bnkc,kcd->bnkd', jnp.stack([y1, y2, y3], axis=2), w_branch)
        gate = jax.nn.sigmoid(g.reshape(b, n, N_BRANCH, D_MODEL).astype(F32)).astype(br.dtype)
        return jnp.einsum('bnkd,bnkd->bnd', gate, br) @ w_out

    out = merge(ya, yb, yc_, gates)
    outc = merge(yac, ybc, ycc, gatesc) if need_ctx_out else None
    return out, outc


def expert_choice_ffn(h, w_router, w_gate, w_up, w_down):
    b, n, d = h.shape
    cap = EC_CAPACITY_FACTOR * n // N_EXPERTS
    aff = jax.nn.softmax(jnp.einsum('bnd,de->bne', h, w_router).astype(F32), axis=-1)
    g, idx = lax.top_k(jnp.swapaxes(aff, 1, 2), cap)
    xe = jax.vmap(lambda hb, ib: hb[ib])(h, idx)
    hid = jax.nn.silu(jnp.einsum('becd,edf->becf', xe, w_gate)) * jnp.einsum('becd,edf->becf', xe, w_up)
    ye = jnp.einsum('becf,efd->becd', hid, w_down) * g[..., None].astype(h.dtype)

    def combine(ib, yb):
        return jnp.zeros((n, d), yb.dtype).at[ib.reshape(-1)].add(yb.reshape(-1, d))

    return jax.vmap(combine)(idx, ye)


def setup_inputs(seed: int = 0) -> dict:
    key = jax.random.key(seed)
    keys = iter(jax.random.split(key, 64))
    D = D_MODEL

    def normal(shape, scale):
        return jax.random.normal(next(keys), shape, F32) * scale

    def uniform(shape, lo, hi):
        return jax.random.uniform(next(keys), shape, F32, lo, hi)

    n_idx = jnp.arange(S5_STATE, dtype=F32)
    ssd_dt = jnp.exp(uniform((DEPTH, 2, SSD_HEADS), math.log(1e-3), math.log(1e-1)))
    s5_shape = (DEPTH, 2, S5_GROUPS, S5_STATE)
    return {
        "x": normal((BATCH, SEQ, D), 1.0),
        "c": normal((BATCH, D), 1.0),
        "ctx": normal((BATCH, CTX_LEN, D), 1.0),
        "c_ctx": normal((D,), 1.0),
        "w_mod": normal((DEPTH, D, 6 * D), 0.5 * D ** -0.5),
        "b_mod": normal((DEPTH, 6 * D), 0.01),
        "norm1_g": 1.0 + normal((DEPTH, D), 0.02),
        "norm2_g": 1.0 + normal((DEPTH, D), 0.02),
        "w_in": normal((DEPTH, D, IN_WIDTH), D ** -0.5),
        "s5_lam_re": -0.5 + normal(s5_shape, 0.01),
        "s5_lam_im": math.pi * n_idx + normal(s5_shape, 0.01),
        "s5_log_dt": uniform((DEPTH, 2, S5_GROUPS), math.log(S5_DT_MIN), math.log(S5_DT_MAX)),
        "s5_b_re": normal((DEPTH, 2, S5_GROUPS, S5_STATE, S5_GROUP_CH), (2 * S5_GROUP_CH) ** -0.5),
        "s5_b_im": normal((DEPTH, 2, S5_GROUPS, S5_STATE, S5_GROUP_CH), (2 * S5_GROUP_CH) ** -0.5),
        "s5_c_re": normal((DEPTH, 2, S5_GROUPS, S5_GROUP_CH, S5_STATE), S5_STATE ** -0.5),
        "s5_c_im": normal((DEPTH, 2, S5_GROUPS, S5_GROUP_CH, S5_STATE), S5_STATE ** -0.5),
        "s5_d": normal((DEPTH, S5_WIDTH), 0.5),
        "s5_w_glu": normal((DEPTH, S5_WIDTH, S5_WIDTH), S5_WIDTH ** -0.5),
        "s5_b_glu": normal((DEPTH, S5_WIDTH), 0.01),
        "ssd_conv_w": normal((DEPTH, SSD_CONV, SSD_CONV_CH), SSD_CONV ** -0.5),
        "ssd_conv_b": normal((DEPTH, SSD_CONV_CH), 0.01),
        "ssd_a_log": jnp.log(uniform((DEPTH, 2, SSD_HEADS), 1.0, 16.0)),
        "ssd_dt_bias": ssd_dt + jnp.log(-jnp.expm1(-ssd_dt)),
        "ssd_d": 1.0 + normal((DEPTH, SSD_HEADS), 0.1),
        "ssd_norm_g": 1.0 + normal((DEPTH, SSD_WIDTH), 0.02),
        "attn_sink": normal((DEPTH, ATTN_HEADS), 0.5),
        "w_branch": normal((DEPTH, N_BRANCH, BRANCH_WIDTH, D), BRANCH_WIDTH ** -0.5),
        "w_out": normal((DEPTH, D, D), D ** -0.5),
        "w_router": normal((DEPTH, D, N_EXPERTS), D ** -0.5),
        "w_e_gate": normal((DEPTH, N_EXPERTS, D, EXPERT_FF), D ** -0.5),
        "w_e_up": normal((DEPTH, N_EXPERTS, D, EXPERT_FF), D ** -0.5),
        "w_e_down": normal((DEPTH, N_EXPERTS, EXPERT_FF, D), EXPERT_FF ** -0.5),
        "final_norm_g": 1.0 + normal((D,), 0.02),
    }


def reference(x, c, ctx, c_ctx, w_mod, b_mod, norm1_g, norm2_g, w_in, s5_lam_re, s5_lam_im, s5_log_dt,
              s5_b_re, s5_b_im, s5_c_re, s5_c_im, s5_d, s5_w_glu, s5_b_glu, ssd_conv_w, ssd_conv_b,
              ssd_a_log, ssd_dt_bias, ssd_d, ssd_norm_g, attn_sink, w_branch, w_out, w_router,
              w_e_gate, w_e_up, w_e_down, final_norm_g):
    D = D_MODEL
    xc = ctx
    silu_c = jax.nn.silu(c)
    silu_cc = jax.nn.silu(c_ctx)
    for i in range(DEPTH):
        ctx_out = i < DEPTH - 1
        mod = silu_c @ w_mod[i] + b_mod[i]
        modc = silu_cc @ w_mod[i] + b_mod[i]
        sh1, sc1, g1, sh2, sc2, g2 = [mod[:, None, j * D:(j + 1) * D] for j in range(6)]
        csh1, csc1, cg1, csh2, csc2, cg2 = [modc[j * D:(j + 1) * D] for j in range(6)]

        h = rmsnorm(x, norm1_g[i]) * (1.0 + sc1) + sh1
        hc = rmsnorm(xc, norm1_g[i]) * (1.0 + csc1) + csh1
        mix, mixc = mixing_sublayer(
            h, hc, w_in[i], s5_lam_re[i], s5_lam_im[i], s5_log_dt[i], s5_b_re[i], s5_b_im[i],
            s5_c_re[i], s5_c_im[i], s5_d[i], s5_w_glu[i], s5_b_glu[i], ssd_conv_w[i], ssd_conv_b[i],
            ssd_a_log[i], ssd_dt_bias[i], ssd_d[i], ssd_norm_g[i], attn_sink[i], w_branch[i], w_out[i],
            ctx_out)
        x = x + g1 * mix
        h2 = rmsnorm(x, norm2_g[i]) * (1.0 + sc2) + sh2
        x = x + g2 * expert_choice_ffn(h2, w_router[i], w_e_gate[i], w_e_up[i], w_e_down[i])
        if ctx_out:
            xc = xc + cg1 * mixc
            hc2 = rmsnorm(xc, norm2_g[i]) * (1.0 + csc2) + csh2
            xc = xc + cg2 * expert_choice_ffn(hc2, w_router[i], w_e_gate[i], w_e_up[i], w_e_down[i])
    return rmsnorm(x, final_norm_g)
```

```python
import functools
import math

import jax
import jax.numpy as jnp
from jax import lax
from jax.experimental import pallas as pl
from jax.experimental.pallas import tpu as pltpu

F32 = jnp.float32
BF16 = jnp.bfloat16
I32 = jnp.int32
HI = lax.Precision.HIGHEST

D_MODEL = 1024
DEPTH = 4
GRID_W = 64
EPS = 1e-6
S5_GROUPS = 32
S5_GROUP_CH = 16
S5_STATE = 64
S5_CHUNK = 32
SSD_HEADS = 8
SSD_HEAD_DIM = 64
SSD_GROUPS = 2
SSD_STATE = 128
SSD_CHUNK = 128
ATTN_HEADS = 8
ATTN_KV_HEADS = 2
ATTN_HEAD_DIM = 64
ATTN_BLOCK = 128
ROPE_BASE = 10000.0
N_EXPERTS = 16
EXPERT_FF = 1024
EC_CAPACITY_FACTOR = 2
IN_SIZES = (512, 512, 1024, 16, 512, 128, 128, 3072)
TOK_TILE = 256
LANE = 128
NEG = -1e30


def _dot(a, b):
    return jnp.dot(a, b, preferred_element_type=F32)


def _dot_hi(a, b):
    return jnp.dot(a, b, precision=HI, preferred_element_type=F32)


def _dot_nt(a, b):
    return lax.dot_general(a, b, (((1,), (1,)), ((), ())), preferred_element_type=F32)


def _sigmoid(x):
    return 1.0 / (1.0 + jnp.exp(-x))


def _softplus(x):
    return jnp.maximum(x, 0.0) + jnp.log(1.0 + jnp.exp(-jnp.abs(x)))


def _rms(x):
    return x * lax.rsqrt(jnp.mean(x * x, axis=-1, keepdims=True) + EPS)


def _params(*sem):
    return pltpu.CompilerParams(dimension_semantics=sem)


def _mod_kernel(c_ref, w_ref, b_ref, o_ref):
    c = c_ref[...]
    o_ref[0] = _dot_hi(c * _sigmoid(c), w_ref[0]) + b_ref[0]


def _modulation(cvecs, w_mod, b_mod):
    depth, d, d6 = w_mod.shape
    bn = 1536
    return pl.pallas_call(
        _mod_kernel,
        grid=(depth, d6 // bn),
        in_specs=[pl.BlockSpec((8, d), lambda l, j: (0, 0)),
                  pl.BlockSpec((1, d, bn), lambda l, j: (l, 0, j)),
                  pl.BlockSpec((1, 1, bn), lambda l, j: (l, 0, j))],
        out_specs=pl.BlockSpec((1, 8, bn), lambda l, j: (l, 0, j)),
        out_shape=jax.ShapeDtypeStruct((depth, 8, d6), F32),
        compiler_params=_params("arbitrary", "arbitrary"),
        name="modulation",
    )(cvecs, w_mod, b_mod.reshape(depth, 1, d6))


W_IN_COLS = 6016


def _prep_w_in(w):
    parts, start = [], 0
    for s in IN_SIZES:
        parts.append(w[:, start:start + s])
        start += s
    u, z, xbc, dt, q, k, v, gates = parts
    dt = jnp.pad(dt, ((0, 0), (0, LANE - dt.shape[1])))
    return jnp.concatenate([u, z, xbc, q, k, v, gates, dt], axis=1).astype(BF16)


def _inproj_kernel(x_ref, mod_ref, g_ref, w_ref, cos_ref, sin_ref,
                   u_ref, z_ref, xbc_ref, q_ref, kv_ref, gates_ref, dt_ref):
    m = mod_ref[0]
    h = (_rms(x_ref[...]) * g_ref[...] * (1.0 + m[1:2]) + m[0:1]).astype(BF16)

    def proj(a, b):
        return _dot(h, w_ref[:, a:b])

    u_ref[...] = proj(0, 512)
    z_ref[...] = proj(512, 1024)
    xbc_ref[...] = proj(1024, 2048)
    cos = cos_ref[...]
    sin = sin_ref[...]
    lane = lax.broadcasted_iota(I32, cos.shape, 1)
    first = (lane % 32) < 16

    def rope(v):
        partner = jnp.where(first, pltpu.roll(v, LANE - 16, 1), pltpu.roll(v, 16, 1))
        return v * cos + partner * sin

    scale = ATTN_HEAD_DIM ** -0.5
    for j in range(4):
        q_ref[:, j * LANE:(j + 1) * LANE] = (rope(proj(2048 + j * LANE, 2048 + (j + 1) * LANE)) * scale).astype(BF16)
    kv_ref[:, 0:LANE] = rope(proj(2560, 2688)).astype(BF16)
    kv_ref[:, LANE:2 * LANE] = proj(2688, 2816).astype(BF16)
    for j in range(6):
        gates_ref[:, j * 512:(j + 1) * 512] = proj(2816 + j * 512, 2816 + (j + 1) * 512).astype(BF16)
    dt_ref[...] = proj(5888, 6016)


def _inproj(xall, mod, g, w, cos, sin):
    n_all, d = xall.shape
    t = TOK_TILE
    row = lambda i: (i, 0)
    const = lambda i: (0, 0)
    widths = (512, 512, 1024, 512, 256, 3072, LANE)
    dtypes = (F32, F32, F32, BF16, BF16, BF16, F32)
    return pl.pallas_call(
        _inproj_kernel,
        grid=(n_all // t,),
        in_specs=[pl.BlockSpec((t, d), row),
                  pl.BlockSpec((1, 6, d), lambda i: (jnp.minimum(i, 1), 0, 0)),
                  pl.BlockSpec((1, d), const),
                  pl.BlockSpec((d, W_IN_COLS), const, pipeline_mode=pl.Buffered(1)),
                  pl.BlockSpec((t, LANE), row),
                  pl.BlockSpec((t, LANE), row)],
        out_specs=[pl.BlockSpec((t, wd), row) for wd in widths],
        out_shape=[jax.ShapeDtypeStruct((n_all, wd), dt) for wd, dt in zip(widths, dtypes)],
        compiler_params=_params("arbitrary"),
        name="inproj",
    )(xall, mod, g, w, cos, sin)


def _rope_tables(n, nc):
    rows = n // GRID_W
    r = jnp.repeat(jnp.arange(rows, dtype=F32), GRID_W)
    c = jnp.tile(jnp.arange(GRID_W, dtype=F32), rows)
    m = ATTN_HEAD_DIM // 4
    inv_freq = ROPE_BASE ** (-jnp.arange(m, dtype=F32) / m)
    ang_r, ang_c = r[:, None] * inv_freq, c[:, None] * inv_freq
    cos = jnp.concatenate([jnp.cos(ang_r), jnp.cos(ang_r), jnp.cos(ang_c), jnp.cos(ang_c)], axis=1)
    sin = jnp.concatenate([-jnp.sin(ang_r), jnp.sin(ang_r), -jnp.sin(ang_c), jnp.sin(ang_c)], axis=1)
    cos = jnp.concatenate([jnp.ones((nc, 64), F32), cos], axis=0)
    sin = jnp.concatenate([jnp.zeros((nc, 64), F32), sin], axis=0)
    return jnp.tile(cos, (1, 2)), jnp.tile(sin, (1, 2))


def _s5_prep(lam_re, lam_im, log_dt, b_re, b_im, c_re, c_im):
    L, G, P, K = S5_CHUNK, S5_GROUPS, S5_STATE, S5_GROUP_CH
    lr, li = lam_re.astype(F32), lam_im.astype(F32)
    dt = jnp.exp(log_dt.astype(F32))[..., None]
    mag = jnp.exp(lr * dt)
    abr, abi = mag * jnp.cos(li * dt), mag * jnp.sin(li * dt)
    den = lr * lr + li * li
    fr = ((abr - 1.0) * lr + abi * li) / den
    fi = (abi * lr - (abr - 1.0) * li) / den
    bbr = fr[..., None] * b_re - fi[..., None] * b_im
    bbi = fr[..., None] * b_im + fi[..., None] * b_re
    tau = jnp.arange(L + 1, dtype=F32)[:, None, None, None]
    pm = jnp.exp(lr * dt * tau)
    apr, api = pm * jnp.cos(li * dt * tau), pm * jnp.sin(li * dt * tau)
    cr, ci = c_re.astype(F32), c_im.astype(F32)
    car = cr * apr[:, :, :, None, :] - ci * api[:, :, :, None, :]
    cai = cr * api[:, :, :, None, :] + ci * apr[:, :, :, None, :]
    kd = (jnp.einsum('tdgkp,dgpj->tdgkj', car[:L], bbr, precision=HI)
          - jnp.einsum('tdgkp,dgpj->tdgkj', cai[:L], bbi, precision=HI))
    s_idx = jnp.arange(L)[:, None]
    t_idx = jnp.arange(L)[None, :]
    lag = t_idx - s_idx
    ktf = jnp.where((lag >= 0)[:, :, None, None, None], kd[:, 0][jnp.clip(lag, 0, L - 1)], 0.0)
    ktb = jnp.where((lag <= 0)[:, :, None, None, None], kd[:, 1][jnp.clip(-lag, 0, L - 1)], 0.0)
    kt = (ktf + ktb).transpose(2, 0, 4, 1, 3).reshape(G, L * K, L * K).astype(BF16)

    def summary(d, idx):
        ar, ai = apr[idx, d][..., None], api[idx, d][..., None]
        re = ar * bbr[d][None] - ai * bbi[d][None]
        im = ar * bbi[d][None] + ai * bbr[d][None]
        f = lambda v: v.transpose(1, 0, 3, 2).reshape(G, L * K, P)
        return f(re), f(im)

    sfr, sfi = summary(0, L - 1 - jnp.arange(L))
    sbr, sbi = summary(1, jnp.arange(L))
    sbg = jnp.stack([sfr, sfi, sbr, sbi], axis=2)
    sbp = jnp.zeros((G // 2, 2, L * K, 4, 2, P), F32)
    sbp = sbp.at[:, 0, :, :, 0, :].set(sbg[0::2]).at[:, 1, :, :, 1, :].set(sbg[1::2])
    sbp = sbp.reshape(G // 2, 2, L * K, 8 * P).astype(BF16)

    def readout(v):
        return v.transpose(1, 3, 0, 2).reshape(G, P, L * K)

    idx_b = L - jnp.arange(L)
    rcg = jnp.stack([readout(car[1:L + 1, 0]), -readout(cai[1:L + 1, 0]),
                     readout(car[idx_b, 1]), -readout(cai[idx_b, 1])], axis=1)
    rcp = jnp.zeros((G // 2, 4, 2, P, 2, L * K), F32)
    rcp = rcp.at[:, :, 0, :, 0, :].set(rcg[0::2]).at[:, :, 1, :, 1, :].set(rcg[1::2])
    rcp = rcp.reshape(G // 2, 4, 2 * P, 2 * L * K).astype(BF16)
    coef = jnp.stack([apr[L, 0].reshape(-1), api[L, 0].reshape(-1),
                      apr[L, 1].reshape(-1), api[L, 1].reshape(-1)], axis=0)
    coef = jnp.pad(coef, ((0, 4), (0, 0)))
    return kt, sbp, rcp, coef


def _s5_states_kernel(u_ref, sb_ref, fre, fim, bre, bim):
    s = _dot(u_ref[0], sb_ref[0, 0]) + _dot(u_ref[1], sb_ref[0, 1])
    fre[...] = s[:, 0:128]
    fim[...] = s[:, 128:256]
    bre[...] = s[:, 256:384]
    bim[...] = s[:, 384:512]


def _s5_rec_kernel(coef_ref, sfr, sfi, sbr, sbi, hfr, hfi, hbr, hbi, *, n_chunks, ctx_chunks):
    arf, aif = coef_ref[0:1, :], coef_ref[1:2, :]
    arb, aib = coef_ref[2:3, :], coef_ref[3:4, :]
    zero = jnp.zeros_like(arf)

    def fstep(c, carry):
        hr, hi = carry
        hfr[pl.ds(c, 1), :] = hr
        hfi[pl.ds(c, 1), :] = hi
        return (arf * hr - aif * hi + sfr[pl.ds(c, 1), :], arf * hi + aif * hr + sfi[pl.ds(c, 1), :])

    lax.fori_loop(0, n_chunks, fstep, (zero, zero))

    def bstep(i, carry):
        c = jnp.where(i < ctx_chunks, ctx_chunks - 1 - i, n_chunks - 1 - (i - ctx_chunks))
        hr, hi = carry
        hbr[pl.ds(c, 1), :] = hr
        hbi[pl.ds(c, 1), :] = hi
        return (arb * hr - aib * hi + sbr[pl.ds(c, 1), :], arb * hi + aib * hr + sbi[pl.ds(c, 1), :])

    lax.fori_loop(0, n_chunks, bstep, (zero, zero))


def _s5_out_kernel(u_ref, kt_ref, hfr, hfi, hbr, hbi, rc_ref, y_ref):
    inter = (_dot(hfr[...].astype(BF16), rc_ref[0, 0]) + _dot(hfi[...].astype(BF16), rc_ref[0, 1])
             + _dot(hbr[...].astype(BF16), rc_ref[0, 2]) + _dot(hbi[...].astype(BF16), rc_ref[0, 3]))
    half = inter.shape[1] // 2
    y_ref[0] = _dot(u_ref[0], kt_ref[0]) + inter[:, :half]
    y_ref[1] = _dot(u_ref[1], kt_ref[1]) + inter[:, half:]


def _s5_mix(u, nc, prep):
    kt, sbp, rcp, coef = prep
    n_all = u.shape[0]
    L, G, K, P = S5_CHUNK, S5_GROUPS, S5_GROUP_CH, S5_STATE
    C = n_all // L
    lk = L * K
    ug = u.reshape(C, L, G, K).transpose(2, 0, 1, 3).reshape(G, C, lk).astype(BF16)
    gp = G // 2
    st_shape = jax.ShapeDtypeStruct((C, G * P), F32)
    st_spec = pl.BlockSpec((C, 2 * P), lambda p: (0, p))
    states = pl.pallas_call(
        _s5_states_kernel,
        grid=(gp,),
        in_specs=[pl.BlockSpec((2, C, lk), lambda p: (p, 0, 0)),
                  pl.BlockSpec((1, 2, lk, 8 * P), lambda p: (p, 0, 0, 0))],
        out_specs=[st_spec] * 4,
        out_shape=[st_shape] * 4,
        compiler_params=_params("arbitrary"),
        name="s5_states",
    )(ug, sbp)
    cb = 512
    col = pl.BlockSpec((C, cb), lambda j: (0, j))
    hs = pl.pallas_call(
        functools.partial(_s5_rec_kernel, n_chunks=C, ctx_chunks=nc // L),
        grid=(G * P // cb,),
        in_specs=[pl.BlockSpec((8, cb), lambda j: (0, j))] + [col] * 4,
        out_specs=[col] * 4,
        out_shape=[st_shape] * 4,
        compiler_params=_params("arbitrary"),
        name="s5_recurrence",
    )(coef, *states)
    y = pl.pallas_call(
        _s5_out_kernel,
        grid=(gp,),
        in_specs=[pl.BlockSpec((2, C, lk), lambda p: (p, 0, 0)),
                  pl.BlockSpec((2, lk, lk), lambda p: (p, 0, 0))] + [st_spec] * 4
                 + [pl.BlockSpec((1, 4, 2 * P, 2 * lk), lambda p: (p, 0, 0, 0))],
        out_specs=pl.BlockSpec((2, C, lk), lambda p: (p, 0, 0)),
        out_shape=jax.ShapeDtypeStruct((G, C, lk), F32),
        compiler_params=_params("arbitrary"),
        name="s5_out",
    )(ug, kt, *hs, rcp)
    return y.reshape(G, C, L, K).transpose(1, 2, 0, 3).reshape(n_all, G * K)


def _conv_kernel(cur_ref, prev_ref, next_ref, w_ref, b_ref, o_ref, *, n_tiles):
    i = pl.program_id(0)
    cur = cur_ref[...]
    t = cur.shape[0]
    pv = prev_ref[...] * jnp.where(i >= 2, 1.0, 0.0)
    nx = next_ref[...] * jnp.where((i >= 1) & (i <= n_tiles - 2), 1.0, 0.0)
    row8 = lax.broadcasted_iota(I32, pv.shape, 0)
    acc = b_ref[...] + w_ref[2:3, :] * cur
    for s in (1, 2):
        r = pltpu.roll(cur, s, 0)
        head = jnp.where(row8 < s, pltpu.roll(pv, s, 0), r[0:8])
        acc = acc + w_ref[2 - s:3 - s, :] * jnp.concatenate([head, r[8:]], axis=0)
        r = pltpu.roll(cur, t - s, 0)
        tail = jnp.where(row8 >= 8 - s, pltpu.roll(nx, 8 - s, 0), r[t - 8:])
        acc = acc + w_ref[2 + s:3 + s, :] * jnp.concatenate([r[:t - 8], tail], axis=0)
    o_ref[...] = acc * _sigmoid(acc)


def _ssd_conv(xbc, conv_w, conv_b):
    n_all, ch = xbc.shape
    t = TOK_TILE
    n_tiles = n_all // t
    per = t // 8
    return pl.pallas_call(
        functools.partial(_conv_kernel, n_tiles=n_tiles),
        grid=(n_tiles,),
        in_specs=[pl.BlockSpec((t, ch), lambda i: (i, 0)),
                  pl.BlockSpec((8, ch), lambda i: (jnp.maximum(i * per - 1, 0), 0)),
                  pl.BlockSpec((8, ch), lambda i: (jnp.minimum((i + 1) * per, n_tiles * per - 1), 0)),
                  pl.BlockSpec((8, ch), lambda i: (0, 0)),
                  pl.BlockSpec((1, ch), lambda i: (0, 0))],
        out_specs=pl.BlockSpec((t, ch), lambda i: (i, 0)),
        out_shape=jax.ShapeDtypeStruct((n_all, ch), F32),
        compiler_params=_params("arbitrary"),
        name="ssd_conv",
    )(xbc, xbc, xbc, jnp.pad(conv_w, ((0, 3), (0, 0))), conv_b.reshape(1, ch))


def _ssd_kernel(t_ref, dtr_ref, dtrt_ref, prow_ref, pcol_ref, y_ref, h_ref, *, reverse, d):
    q = SSD_CHUNK
    hpg = SSD_HEADS // SSD_GROUPS
    p = SSD_HEAD_DIM

    @pl.when(pl.program_id(0) == 0)
    def _():
        h_ref[...] = jnp.zeros_like(h_ref)

    lo = d * SSD_HEADS
    dt_c = _softplus(dtr_ref[:, lo:lo + SSD_HEADS] + prow_ref[1:2, 0:SSD_HEADS])
    adt_c = dt_c * prow_ref[0:1, 0:SSD_HEADS]
    dt_r = _softplus(dtrt_ref[lo:lo + SSD_HEADS, :] + pcol_ref[:, 1:2])
    adt_r = dt_r * pcol_ref[:, 0:1]
    ii = lax.broadcasted_iota(I32, (q, q), 0)
    jj = lax.broadcasted_iota(I32, (q, q), 1)
    causal = (jj >= ii) if reverse else (jj <= ii)
    acum_c = _dot_hi(causal.astype(F32), adt_c)
    acum_r = _dot_hi(adt_r, ((ii >= jj) if reverse else (ii <= jj)).astype(F32))
    tot = acum_c[0:1, :] if reverse else acum_c[q - 1:q, :]
    for g in range(SSD_GROUPS):
        bg = t_ref[:, 512 + g * SSD_STATE:512 + (g + 1) * SSD_STATE]
        cg = t_ref[:, 768 + g * SSD_STATE:768 + (g + 1) * SSD_STATE].astype(BF16)
        cb = _dot_nt(cg, bg.astype(BF16))
        bgt = bg.T.astype(BF16)
        for r in range(hpg):
            hd = g * hpg + r
            xh = t_ref[:, hd * p:(hd + 1) * p]
            ac = acum_c[:, hd:hd + 1]
            seg = ac - acum_r[hd:hd + 1, :]
            lmat = jnp.exp(jnp.where(causal, seg, NEG))
            wts = (cb * lmat * dt_r[hd:hd + 1, :]).astype(BF16)
            hin = h_ref[g, :, r * p:(r + 1) * p]
            y = _dot(wts, xh.astype(BF16)) + _dot(cg, hin.astype(BF16)) * jnp.exp(ac)
            y_ref[:, hd * p:(hd + 1) * p] = y
            th = tot[:, hd:hd + 1]
            xw = (xh * (jnp.exp(th - ac) * dt_c[:, hd:hd + 1])).astype(BF16)
            h_ref[g, :, r * p:(r + 1) * p] = hin * jnp.exp(th) + _dot(bgt, xw)


def _ssd_scan(t, dtr, dtrt, a, bias, nc, d):
    n_all = t.shape[0]
    q = SSD_CHUNK
    n_chunks = n_all // q
    cc = nc // q
    reverse = d == 1
    if reverse:
        order = lambda i: jnp.where(i < cc, cc - 1 - i, n_chunks - 1 - (i - cc))
    else:
        order = lambda i: i
    prow = jnp.zeros((8, LANE), F32).at[0, :SSD_HEADS].set(a).at[1, :SSD_HEADS].set(bias)
    pcol = jnp.zeros((SSD_HEADS, LANE), F32).at[:, 0].set(a).at[:, 1].set(bias)
    return pl.pallas_call(
        functools.partial(_ssd_kernel, reverse=reverse, d=d),
        grid=(n_chunks,),
        in_specs=[pl.BlockSpec((q, t.shape[1]), lambda i: (order(i), 0)),
                  pl.BlockSpec((q, LANE), lambda i: (order(i), 0)),
                  pl.BlockSpec((2 * SSD_HEADS, q), lambda i: (0, order(i))),
                  pl.BlockSpec((8, LANE), lambda i: (0, 0)),
                  pl.BlockSpec((SSD_HEADS, LANE), lambda i: (0, 0))],
        out_specs=pl.BlockSpec((q, SSD_HEADS * SSD_HEAD_DIM), lambda i: (order(i), 0)),
        out_shape=jax.ShapeDtypeStruct((n_all, SSD_HEADS * SSD_HEAD_DIM), F32),
        scratch_shapes=[pltpu.VMEM((SSD_GROUPS, SSD_STATE, (SSD_HEADS // SSD_GROUPS) * SSD_HEAD_DIM), F32)],
        compiler_params=_params("arbitrary"),
        name="ssd_scan_bwd" if reverse else "ssd_scan_fwd",
    )(t, dtr, dtrt, prow, pcol)


def _attn_kernel(sink_ref, q_ref, kp_ref, ko_ref, kn_ref, kc_ref, o_ref, *, n_blocks, ctx_blocks):
    c = pl.program_id(0)
    blk = ATTN_BLOCK
    dh = ATTN_HEAD_DIM
    rep = ATTN_HEADS // ATTN_KV_HEADS
    rows = rep * blk
    nctx = kc_ref.shape[0]
    lat = c >= ctx_blocks
    lo = jnp.where(lat & (c > ctx_blocks), 0, blk)
    hi = jnp.where(lat, jnp.where(c < n_blocks - 1, 3 * blk, 2 * blk), 0)
    qi = lax.broadcasted_iota(I32, (rows, 3 * blk), 0) % blk
    col = lax.broadcasted_iota(I32, (rows, 3 * blk), 1)
    mask = (jnp.abs(qi - (col - blk)) <= blk) & (col >= lo) & (col < hi)
    rowhead = lax.broadcasted_iota(I32, (rows, 1), 0) // blk
    for g in range(ATTN_KV_HEADS):
        qg = jnp.concatenate([q_ref[:, (g * rep + r) * dh:(g * rep + r + 1) * dh] for r in range(rep)], axis=0)
        ks = slice(g * dh, (g + 1) * dh)
        vs = slice(ATTN_KV_HEADS * dh + g * dh, ATTN_KV_HEADS * dh + (g + 1) * dh)
        kloc = jnp.concatenate([kp_ref[:, ks], ko_ref[:, ks], kn_ref[:, ks]], axis=0)
        vloc = jnp.concatenate([kp_ref[:, vs], ko_ref[:, vs], kn_ref[:, vs]], axis=0)
        s_loc = jnp.where(mask, _dot_nt(qg, kloc), NEG)
        s_ctx = _dot_nt(qg, kc_ref[:, ks])
        sink = jnp.zeros((rows, 1), F32)
        for r in range(rep):
            sink = jnp.where(rowhead == r, sink_ref[g * rep + r], sink)
        mx = jnp.maximum(jnp.maximum(jnp.max(s_loc, axis=1, keepdims=True),
                                     jnp.max(s_ctx, axis=1, keepdims=True)), sink)
        p_loc = jnp.exp(s_loc - mx)
        p_ctx = jnp.exp(s_ctx - mx)
        den = (jnp.sum(p_loc, axis=1, keepdims=True) + jnp.sum(p_ctx, axis=1, keepdims=True)
               + jnp.exp(sink - mx))
        o = (_dot(p_loc.astype(BF16), vloc) + _dot(p_ctx.astype(BF16), kc_ref[:, vs])) / den
        for r in range(rep):
            o_ref[:, (g * rep + r) * dh:(g * rep + r + 1) * dh] = o[r * blk:(r + 1) * blk]


def _attention(q, kv, sink, nc):
    n_all = q.shape[0]
    blk = ATTN_BLOCK
    n_blocks = n_all // blk
    cbk = nc // blk
    kvw = kv.shape[1]
    return pl.pallas_call(
        functools.partial(_attn_kernel, n_blocks=n_blocks, ctx_blocks=cbk),
        grid=(n_blocks,),
        in_specs=[pl.BlockSpec(memory_space=pltpu.SMEM),
                  pl.BlockSpec((blk, q.shape[1]), lambda c: (c, 0)),
                  pl.BlockSpec((blk, kvw), lambda c: (jnp.maximum(c - 1, 0), 0)),
                  pl.BlockSpec((blk, kvw), lambda c: (c, 0)),
                  pl.BlockSpec((blk, kvw), lambda c: (jnp.minimum(c + 1, n_blocks - 1), 0)),
                  pl.BlockSpec((nc, kvw), lambda c: (0, 0))],
        out_specs=pl.BlockSpec((blk, q.shape[1]), lambda c: (c, 0)),
        out_shape=jax.ShapeDtypeStruct((n_all, q.shape[1]), F32),
        compiler_params=_params("arbitrary"),
        name="window_attention",
    )(sink, q, kv, kv, kv, kv)


def _merge_kernel(x_ref, mod_ref, ys5_ref, u_ref, yf_ref, yb_ref, xs_ref, z_ref, att_ref, gates_ref,
                  s5d_ref, bglu_ref, ssdd_ref, ssdg_ref, n2g_ref, wglu_ref, wbr_ref, wout_ref, wr_ref,
                  xo_ref, h2_ref, aff_ref):
    m = mod_ref[0]
    d = x_ref.shape[1]
    a = jax.nn.gelu(ys5_ref[...] + s5d_ref[...] * u_ref[...])
    ya = a * _sigmoid(_dot(a.astype(BF16), wglu_ref[...]) + bglu_ref[...])
    z = z_ref[...]
    yz = (yf_ref[...] + yb_ref[...] + ssdd_ref[...] * xs_ref[...]) * (z * _sigmoid(z))
    yb = _rms(yz) * ssdg_ref[...]
    br = (_sigmoid(gates_ref[:, 0:d].astype(F32)) * _dot(ya.astype(BF16), wbr_ref[0])
          + _sigmoid(gates_ref[:, d:2 * d].astype(F32)) * _dot(yb.astype(BF16), wbr_ref[1])
          + _sigmoid(gates_ref[:, 2 * d:3 * d].astype(F32)) * _dot(att_ref[...].astype(BF16), wbr_ref[2]))
    xn = x_ref[...] + m[2:3] * _dot(br.astype(BF16), wout_ref[...])
    xo_ref[...] = xn
    h2 = _rms(xn) * n2g_ref[...] * (1.0 + m[4:5]) + m[3:4]
    h2_ref[...] = h2.astype(BF16)
    logits = _dot_hi(h2, wr_ref[...])[:, 0:N_EXPERTS]
    e = jnp.exp(logits - jnp.max(logits, axis=1, keepdims=True))
    aff_ref[...] = e / jnp.sum(e, axis=1, keepdims=True)


def _merge(xall, mod, ys5, u, yf, yb, t, z, att, gates, vecs, wglu, wbr, wout, wr):
    n_all, d = xall.shape
    tt = TOK_TILE
    row = lambda i: (i, 0)
    const2 = lambda i: (0, 0)
    bw = 512
    s5d, bglu, ssdd, ssdg, n2g = vecs
    return pl.pallas_call(
        _merge_kernel,
        grid=(n_all // tt,),
        in_specs=[pl.BlockSpec((tt, d), row),
                  pl.BlockSpec((1, 6, d), lambda i: (jnp.minimum(i, 1), 0, 0)),
                  pl.BlockSpec((tt, bw), row), pl.BlockSpec((tt, bw), row), pl.BlockSpec((tt, bw), row),
                  pl.BlockSpec((tt, bw), row), pl.BlockSpec((tt, bw), row), pl.BlockSpec((tt, bw), row),
                  pl.BlockSpec((tt, bw), row), pl.BlockSpec((tt, 3 * d), row),
                  pl.BlockSpec((1, bw), const2), pl.BlockSpec((1, bw), const2), pl.BlockSpec((1, bw), const2),
                  pl.BlockSpec((1, bw), const2), pl.BlockSpec((1, d), const2),
                  pl.BlockSpec((bw, bw), const2),
                  pl.BlockSpec((3, bw, d), lambda i: (0, 0, 0)),
                  pl.BlockSpec((d, d), const2),
                  pl.BlockSpec((d, LANE), const2)],
        out_specs=[pl.BlockSpec((tt, d), row), pl.BlockSpec((tt, d), row), pl.BlockSpec((tt, N_EXPERTS), row)],
        out_shape=[jax.ShapeDtypeStruct((n_all, d), F32), jax.ShapeDtypeStruct((n_all, d), BF16),
                   jax.ShapeDtypeStruct((n_all, N_EXPERTS), F32)],
        compiler_params=_params("arbitrary"),
        name="merge_router",
    )(xall, mod, ys5, u, yf, yb, t, z, att, gates, s5d, bglu, ssdd, ssdg, n2g, wglu, wbr, wout, wr)


def _route_kernel(a_ref, g_ref, pos_ref, cum_ref, *, cap):
    n = a_ref.shape[1]
    e = a_ref.shape[0]
    bits = lax.bitcast_convert_type(a_ref[...], I32)
    capf = float(cap)

    def search(i, thr):
        cand = thr | (1 << (30 - i))
        cnt = jnp.sum((bits >= cand).astype(F32), axis=1, keepdims=True)
        return jnp.where(cnt >= capf, cand, thr)

    thr = lax.fori_loop(0, 31, search, jnp.zeros((e, 1), I32))
    need = capf - jnp.sum((bits > thr).astype(F32), axis=1, keepdims=True)
    ii = lax.broadcasted_iota(I32, (LANE, LANE), 0)
    jj = lax.broadcasted_iota(I32, (LANE, LANE), 1)
    upper = (ii < jj).astype(BF16)

    def block(b, carry):
        eq_off, pos_off = carry
        sl = pl.ds(pl.multiple_of(b * LANE, LANE), LANE)
        a = a_ref[:, sl]
        v = lax.bitcast_convert_type(a, I32)
        eq = v == thr
        eqf = eq.astype(BF16)
        rank = _dot(eqf, upper) + eq_off
        sel = (v > thr) | (eq & (rank < need))
        self_ = sel.astype(BF16)
        pos = _dot(self_, upper) + pos_off
        g_ref[:, sl] = jnp.where(sel, a, 0.0)
        pos_ref[:, sl] = jnp.where(sel, pos, -1.0).astype(I32)
        cum_ref[:, sl] = pos.astype(I32)
        return (eq_off + jnp.sum(eqf.astype(F32), axis=1, keepdims=True),
                pos_off + jnp.sum(self_.astype(F32), axis=1, keepdims=True))

    zero = jnp.zeros((e, 1), F32)
    lax.fori_loop(0, n // LANE, block, (zero, zero))


def _route(aff_t, cap):
    e, n = aff_t.shape
    return pl.pallas_call(
        functools.partial(_route_kernel, cap=cap),
        out_shape=[jax.ShapeDtypeStruct((e, n), F32), jax.ShapeDtypeStruct((e, n), I32),
                   jax.ShapeDtypeStruct((e, n), I32)],
        name="ec_route",
    )(aff_t)


def _moe_ffn_kernel(offs_ref, h_ref, pos_ref, wg_ref, wu_ref, wd_ref, ye_ref, xe_ref, *, cap, n_tiles, rchunk):
    e = pl.program_id(0)
    j = pl.program_id(1)
    t = h_ref.shape[0]
    w = t + 8

    @pl.when(j == 0)
    def _():
        xe_ref[...] = jnp.zeros_like(xe_ref)

    o = offs_ref[e * n_tiles + j]
    a = pl.multiple_of((o // 8) * 8, 8)
    pos = pos_ref[pl.ds(e, 1), :]
    slot = lax.broadcasted_iota(I32, (w, t), 0) + a
    sel = jnp.where(slot == pos, 1.0, 0.0).astype(BF16)
    xe_ref[pl.ds(a, w), :] += _dot(sel, h_ref[...])

    @pl.when(j == n_tiles - 1)
    def _():
        def chunk(ci, carry):
            rows = pl.ds(pl.multiple_of(ci * rchunk, rchunk), rchunk)
            xb = xe_ref[rows, :].astype(BF16)
            hg = _dot(xb, wg_ref[0])
            hid = (hg * _sigmoid(hg) * _dot(xb, wu_ref[0])).astype(BF16)
            ye_ref[0, rows, :] = _dot(hid, wd_ref[0]).astype(BF16)
            return carry

        lax.fori_loop(0, cap // rchunk, chunk, 0)
        ye_ref[0, cap:, :] = jnp.zeros((ye_ref.shape[1] - cap, ye_ref.shape[2]), BF16)


def _moe_ffn(h2, pos_t, offs, wg, wu, wd, tile0, n, cap):
    t = TOK_TILE
    d = h2.shape[1]
    e, _, f = wg.shape
    n_tiles = n // t
    rchunk = min(cap, 256)
    grid_spec = pltpu.PrefetchScalarGridSpec(
        num_scalar_prefetch=1,
        grid=(e, n_tiles),
        in_specs=[pl.BlockSpec((t, d), lambda ei, j, offs: (j + tile0, 0)),
                  pl.BlockSpec((e, t), lambda ei, j, offs: (0, j)),
                  pl.BlockSpec((1, d, f), lambda ei, j, offs: (ei, 0, 0)),
                  pl.BlockSpec((1, d, f), lambda ei, j, offs: (ei, 0, 0)),
                  pl.BlockSpec((1, f, d), lambda ei, j, offs: (ei, 0, 0))],
        out_specs=pl.BlockSpec((1, cap + t + 16, d), lambda ei, j, offs: (ei, 0, 0)),
        scratch_shapes=[pltpu.VMEM((cap + t + 8, d), F32)],
    )
    return pl.pallas_call(
        functools.partial(_moe_ffn_kernel, cap=cap, n_tiles=n_tiles, rchunk=rchunk),
        grid_spec=grid_spec,
        out_shape=jax.ShapeDtypeStruct((e, cap + t + 16, d), BF16),
        compiler_params=_params("arbitrary", "arbitrary"),
        name="moe_ffn",
    )(offs, h2, pos_t, wg, wu, wd)


def _moe_window_copy(ye_hbm, buf, sem, e, start, rows):
    return pltpu.make_async_copy(ye_hbm.at[e, pl.ds(start, rows), :], buf.at[e], sem.at[e])


def _moe_combine_kernel(offs_ref, x_ref, g_ref, pos_ref, mod_ref, ye_hbm, o_ref, buf, acc_ref, sem, *, n_tiles):
    j = pl.program_id(0)
    t = x_ref.shape[0]
    w = t + 16
    n_exp = g_ref.shape[1]
    starts = []
    for e in range(n_exp):
        o = offs_ref[e * n_tiles + j]
        a = pl.multiple_of((o // 16) * 16, 16)
        starts.append(a)
        _moe_window_copy(ye_hbm, buf, sem, e, a, w).start()
    lane = lax.broadcasted_iota(I32, (t, w), 1)
    for e in range(n_exp):
        _moe_window_copy(ye_hbm, buf, sem, e, starts[e], w).wait()
        sel = jnp.where(pos_ref[:, e:e + 1] == lane + starts[e], 1.0, 0.0).astype(BF16)
        contrib = g_ref[:, e:e + 1] * _dot(sel, buf[e])
        if e == 0:
            acc_ref[...] = contrib
        else:
            acc_ref[...] += contrib
    o_ref[...] = x_ref[...] + mod_ref[0, 5:6, :] * acc_ref[...]


def _moe_combine(xall, g, pos, offs, mod, ye, tile0, n, mod_row):
    t = TOK_TILE
    d = xall.shape[1]
    n_exp = g.shape[1]
    n_tiles = n // t
    grid_spec = pltpu.PrefetchScalarGridSpec(
        num_scalar_prefetch=1,
        grid=(n_tiles,),
        in_specs=[pl.BlockSpec((t, d), lambda j, offs: (j + tile0, 0)),
                  pl.BlockSpec((t, n_exp), lambda j, offs: (j, 0)),
                  pl.BlockSpec((t, n_exp), lambda j, offs: (j, 0)),
                  pl.BlockSpec((1, 6, d), lambda j, offs: (mod_row, 0, 0)),
                  pl.BlockSpec(memory_space=pl.ANY)],
        out_specs=pl.BlockSpec((t, d), lambda j, offs: (j + tile0, 0)),
        scratch_shapes=[pltpu.VMEM((n_exp, t + 16, d), BF16), pltpu.VMEM((t, d), F32),
                        pltpu.SemaphoreType.DMA((n_exp,))],
    )
    return pl.pallas_call(
        functools.partial(_moe_combine_kernel, n_tiles=n_tiles),
        grid_spec=grid_spec,
        out_shape=jax.ShapeDtypeStruct(xall.shape, F32),
        input_output_aliases={1: 0},
        compiler_params=_params("arbitrary"),
        name="moe_combine",
    )(offs, xall, g, pos, mod, ye)


def _expert_choice(xall, h2, aff, mod, wg, wu, wd, tile0, n, mod_row):
    cap = EC_CAPACITY_FACTOR * n // N_EXPERTS
    r0 = tile0 * TOK_TILE
    g_t, pos_t, cum_t = _route(aff[r0:r0 + n].T, cap)
    offs = cum_t[:, ::TOK_TILE].reshape(-1)
    ye = _moe_ffn(h2, pos_t, offs, wg, wu, wd, tile0, n, cap)
    return _moe_combine(xall, g_t.T, pos_t.T, offs, mod, ye, tile0, n, mod_row)


def _final_kernel(x_ref, g_ref, o_ref):
    o_ref[...] = _rms(x_ref[...]) * g_ref[...]


def _final_norm(xall, g, tile0, n):
    t = TOK_TILE
    d = xall.shape[1]
    return pl.pallas_call(
        _final_kernel,
        grid=(n // t,),
        in_specs=[pl.BlockSpec((t, d), lambda i: (i + tile0, 0)), pl.BlockSpec((1, d), lambda i: (0, 0))],
        out_specs=pl.BlockSpec((t, d), lambda i: (i, 0)),
        out_shape=jax.ShapeDtypeStruct((n, d), F32),
        compiler_params=_params("arbitrary"),
        name="final_norm",
    )(xall, g)


def kernel(x, c, ctx, c_ctx, w_mod, b_mod, norm1_g, norm2_g, w_in, s5_lam_re, s5_lam_im, s5_log_dt, s5_b_re, s5_b_im, s5_c_re, s5_c_im, s5_d, s5_w_glu, s5_b_glu, ssd_conv_w, ssd_conv_b, ssd_a_log, ssd_dt_bias, ssd_d, ssd_norm_g, attn_sink, w_branch, w_out, w_router, w_e_gate, w_e_up, w_e_down, final_norm_g):
    batch, n, d = x.shape
    nc = ctx.shape[1]
    depth = w_mod.shape[0]
    assert batch == 1 and nc == TOK_TILE and n % TOK_TILE == 0 and n % GRID_W == 0
    xall = jnp.concatenate([ctx[0], x[0]], axis=0)
    cvecs = jnp.zeros((8, d), F32).at[0].set(c_ctx).at[1].set(c[0])
    mods = _modulation(cvecs, w_mod, b_mod)
    cos, sin = _rope_tables(n, nc)
    row = lambda v: v.reshape(1, -1).astype(F32)
    for i in range(depth):
        mod = mods[i, 0:2].reshape(2, 6, d)
        u, z, xbc, q, kv, gates, dtr = _inproj(xall, mod, row(norm1_g[i]), _prep_w_in(w_in[i]), cos, sin)
        ys5 = _s5_mix(u, nc, _s5_prep(s5_lam_re[i], s5_lam_im[i], s5_log_dt[i], s5_b_re[i], s5_b_im[i],
                                      s5_c_re[i], s5_c_im[i]))
        t = _ssd_conv(xbc, ssd_conv_w[i], ssd_conv_b[i])
        a = -jnp.exp(ssd_a_log[i].astype(F32))
        dtrt = dtr[:, 0:2 * SSD_HEADS].T
        yf = _ssd_scan(t, dtr, dtrt, a[0], ssd_dt_bias[i, 0], nc, 0)
        yb = _ssd_scan(t, dtr, dtrt, a[1], ssd_dt_bias[i, 1], nc, 1)
        att = _attention(q, kv, attn_sink[i].astype(F32), nc)
        vecs = (row(s5_d[i]), row(s5_b_glu[i]), row(jnp.repeat(ssd_d[i], SSD_HEAD_DIM)), row(ssd_norm_g[i]),
                row(norm2_g[i]))
        wr = jnp.pad(w_router[i].astype(F32), ((0, 0), (0, LANE - N_EXPERTS)))
        xall, h2, aff = _merge(xall, mod, ys5, u, yf, yb, t, z, att, gates, vecs, s5_w_glu[i].astype(BF16),
                               w_branch[i].astype(BF16), w_out[i].astype(BF16), wr)
        wg, wu, wd = w_e_gate[i].astype(BF16), w_e_up[i].astype(BF16), w_e_down[i].astype(BF16)
        xall = _expert_choice(xall, h2, aff, mod, wg, wu, wd, nc // TOK_TILE, n, 1)
        if i < depth - 1:
            xall = _expert_choice(xall, h2, aff, mod, wg, wu, wd, 0, nc, 0)
    return _final_norm(xall, row(final_norm_g), nc // TOK_TILE, n)[None]
```

```python
import functools
import math

import jax
import jax.numpy as jnp
from jax import lax
from jax.experimental import pallas as pl
from jax.experimental.pallas import tpu as pltpu

F32 = jnp.float32
BF16 = jnp.bfloat16
I32 = jnp.int32
HI = lax.Precision.HIGHEST

D_MODEL = 1024
DEPTH = 4
GRID_W = 64
EPS = 1e-6
S5_GROUPS = 32
S5_GROUP_CH = 16
S5_STATE = 64
S5_CHUNK = 32
SSD_HEADS = 8
SSD_HEAD_DIM = 64
SSD_GROUPS = 2
SSD_STATE = 128
SSD_CHUNK = 128
ATTN_HEADS = 8
ATTN_KV_HEADS = 2
ATTN_HEAD_DIM = 64
ATTN_BLOCK = 128
ROPE_BASE = 10000.0
N_EXPERTS = 16
EXPERT_FF = 1024
EC_CAPACITY_FACTOR = 2
IN_SIZES = (512, 512, 1024, 16, 512, 128, 128, 3072)
TOK_TILE = 256
LANE = 128
NEG = -1e30


def _dot(a, b):
    return jnp.dot(a, b, preferred_element_type=F32)


def _dot_hi(a, b):
    return jnp.dot(a, b, precision=HI, preferred_element_type=F32)


def _dot_nt(a, b):
    return lax.dot_general(a, b, (((1,), (1,)), ((), ())), preferred_element_type=F32)


def _sigmoid(x):
    return 1.0 / (1.0 + jnp.exp(-x))


def _softplus(x):
    return jnp.maximum(x, 0.0) + jnp.log(1.0 + jnp.exp(-jnp.abs(x)))


def _rms(x):
    return x * lax.rsqrt(jnp.mean(x * x, axis=-1, keepdims=True) + EPS)


def _params(*sem):
    return pltpu.CompilerParams(dimension_semantics=sem)


def _mod_kernel(c_ref, w_ref, b_ref, o_ref):
    c = c_ref[...]
    o_ref[0] = _dot_hi(c * _sigmoid(c), w_ref[0]) + b_ref[0]


def _modulation(cvecs, w_mod, b_mod):
    depth, d, d6 = w_mod.shape
    bn = 1536
    return pl.pallas_call(
        _mod_kernel,
        grid=(depth, d6 // bn),
        in_specs=[pl.BlockSpec((8, d), lambda l, j: (0, 0)),
                  pl.BlockSpec((1, d, bn), lambda l, j: (l, 0, j)),
                  pl.BlockSpec((1, 1, bn), lambda l, j: (l, 0, j))],
        out_specs=pl.BlockSpec((1, 8, bn), lambda l, j: (l, 0, j)),
        out_shape=jax.ShapeDtypeStruct((depth, 8, d6), F32),
        compiler_params=_params("arbitrary", "arbitrary"),
        name="modulation",
    )(cvecs, w_mod, b_mod.reshape(depth, 1, d6))


W_IN_COLS = 6016


def _prep_w_in(w):
    parts, start = [], 0
    for s in IN_SIZES:
        parts.append(w[:, start:start + s])
        start += s
    u, z, xbc, dt, q, k, v, gates = parts
    dt = jnp.pad(dt, ((0, 0), (0, LANE - dt.shape[1])))
    return jnp.concatenate([u, z, xbc, q, k, v, gates, dt], axis=1).astype(BF16)


def _inproj_kernel(x_ref, mod_ref, g_ref, w_ref, cos_ref, sin_ref,
                   u_ref, z_ref, xbc_ref, q_ref, kv_ref, gates_ref, dt_ref):
    m = mod_ref[0]
    h = (_rms(x_ref[...]) * g_ref[...] * (1.0 + m[1:2]) + m[0:1]).astype(BF16)

    def proj(a, b):
        return _dot(h, w_ref[:, a:b])

    u_ref[...] = proj(0, 512)
    z_ref[...] = proj(512, 1024)
    xbc_ref[...] = proj(1024, 2048)
    cos = cos_ref[...]
    sin = sin_ref[...]
    lane = lax.broadcasted_iota(I32, cos.shape, 1)
    first = (lane % 32) < 16

    def rope(v):
        partner = jnp.where(first, pltpu.roll(v, LANE - 16, 1), pltpu.roll(v, 16, 1))
        return v * cos + partner * sin

    scale = ATTN_HEAD_DIM ** -0.5
    for j in range(4):
        q_ref[:, j * LANE:(j + 1) * LANE] = (rope(proj(2048 + j * LANE, 2048 + (j + 1) * LANE)) * scale).astype(BF16)
    kv_ref[:, 0:LANE] = rope(proj(2560, 2688)).astype(BF16)
    kv_ref[:, LANE:2 * LANE] = proj(2688, 2816).astype(BF16)
    for j in range(6):
        gates_ref[:, j * 512:(j + 1) * 512] = proj(2816 + j * 512, 2816 + (j + 1) * 512).astype(BF16)
    dt_ref[...] = proj(5888, 6016)


def _inproj(xall, mod, g, w, cos, sin):
    n_all, d = xall.shape
    t = TOK_TILE
    row = lambda i: (i, 0)
    const = lambda i: (0, 0)
    widths = (512, 512, 1024, 512, 256, 3072, LANE)
    dtypes = (F32, F32, F32, BF16, BF16, BF16, F32)
    return pl.pallas_call(
        _inproj_kernel,
        grid=(n_all // t,),
        in_specs=[pl.BlockSpec((t, d), row),
                  pl.BlockSpec((1, 6, d), lambda i: (jnp.minimum(i, 1), 0, 0)),
                  pl.BlockSpec((1, d), const),
                  pl.BlockSpec((d, W_IN_COLS), const, pipeline_mode=pl.Buffered(1)),
                  pl.BlockSpec((t, LANE), row),
                  pl.BlockSpec((t, LANE), row)],
        out_specs=[pl.BlockSpec((t, wd), row) for wd in widths],
        out_shape=[jax.ShapeDtypeStruct((n_all, wd), dt) for wd, dt in zip(widths, dtypes)],
        compiler_params=_params("arbitrary"),
        name="inproj",
    )(xall, mod, g, w, cos, sin)


def _rope_tables(n, nc):
    rows = n // GRID_W
    r = jnp.repeat(jnp.arange(rows, dtype=F32), GRID_W)
    c = jnp.tile(jnp.arange(GRID_W, dtype=F32), rows)
    m = ATTN_HEAD_DIM // 4
    inv_freq = ROPE_BASE ** (-jnp.arange(m, dtype=F32) / m)
    ang_r, ang_c = r[:, None] * inv_freq, c[:, None] * inv_freq
    cos = jnp.concatenate([jnp.cos(ang_r), jnp.cos(ang_r), jnp.cos(ang_c), jnp.cos(ang_c)], axis=1)
    sin = jnp.concatenate([-jnp.sin(ang_r), jnp.sin(ang_r), -jnp.sin(ang_c), jnp.sin(ang_c)], axis=1)
    cos = jnp.concatenate([jnp.ones((nc, 64), F32), cos], axis=0)
    sin = jnp.concatenate([jnp.zeros((nc, 64), F32), sin], axis=0)
    return jnp.tile(cos, (1, 2)), jnp.tile(sin, (1, 2))


def _s5_table_kernel(bb_ref, ca_ref, o_ref):
    o_ref[0] = jnp.einsum('gkp,gpn->gkn', bb_ref[0], ca_ref[0], precision=HI, preferred_element_type=F32)


def _s5_prep(lam_re, lam_im, log_dt, b_re, b_im, c_re, c_im):
    L, G, P, K = S5_CHUNK, S5_GROUPS, S5_STATE, S5_GROUP_CH
    lk = L * K
    lr, li = lam_re.astype(F32), lam_im.astype(F32)
    dt = jnp.exp(log_dt.astype(F32))[..., None]
    mag = jnp.exp(lr * dt)
    abr, abi = mag * jnp.cos(li * dt), mag * jnp.sin(li * dt)
    den = lr * lr + li * li
    fr = ((abr - 1.0) * lr + abi * li) / den
    fi = (abi * lr - (abr - 1.0) * li) / den
    bbr = fr[..., None] * b_re - fi[..., None] * b_im
    bbi = fr[..., None] * b_im + fi[..., None] * b_re
    tau = jnp.arange(L + 1, dtype=F32)[:, None, None, None]
    pm = jnp.exp(lr * dt * tau)
    apr, api = pm * jnp.cos(li * dt * tau), pm * jnp.sin(li * dt * tau)
    cr, ci = c_re.astype(F32), c_im.astype(F32)
    car = cr * apr[:, :, :, None, :] - ci * api[:, :, :, None, :]
    cai = cr * api[:, :, :, None, :] + ci * apr[:, :, :, None, :]

    lagmat = lambda v: v[:L].transpose(1, 2, 4, 0, 3).reshape(2, G, P, lk)
    ca = jnp.concatenate([lagmat(car), lagmat(cai)], axis=2)
    bb = jnp.concatenate([bbr.transpose(0, 1, 3, 2), -bbi.transpose(0, 1, 3, 2)], axis=3)
    tab = pl.pallas_call(
        _s5_table_kernel,
        grid=(2,),
        in_specs=[pl.BlockSpec((1, G, K, 2 * P), lambda d: (d, 0, 0, 0)),
                  pl.BlockSpec((1, G, 2 * P, lk), lambda d: (d, 0, 0, 0))],
        out_specs=pl.BlockSpec((1, G, K, lk), lambda d: (d, 0, 0, 0)),
        out_shape=jax.ShapeDtypeStruct((2, G, K, lk), F32),
        compiler_params=_params("arbitrary"),
        name="s5_lag_table",
    )(bb, ca).reshape(2, G, K, L, K)
    tf, tb = tab[0], tab[1]
    lags = jnp.concatenate([jnp.flip(tb[:, :, 1:], axis=2), tf[:, :, :1] + tb[:, :, :1], tf[:, :, 1:]],
                           axis=2).astype(BF16)
    kt = jnp.stack([lags[:, :, L - 1 - s:2 * L - 1 - s, :] for s in range(L)], axis=1)
    kt = kt.reshape(G, lk, lk)

    def summary(d, idx):
        ar, ai = apr[idx, d][..., None], api[idx, d][..., None]
        re = ar * bbr[d][None] - ai * bbi[d][None]
        im = ar * bbi[d][None] + ai * bbr[d][None]
        f = lambda v: v.transpose(1, 0, 3, 2).reshape(G, lk, P)
        return f(re), f(im)

    sfr, sfi = summary(0, L - 1 - jnp.arange(L))
    sbr, sbi = summary(1, jnp.arange(L))
    eye2 = jnp.eye(2, dtype=F32)
    sbg = jnp.stack([sfr, sfi, sbr, sbi], axis=2).reshape(G // 2, 2, lk, 4, P)
    sbp = jnp.einsum('aixqp,ij->aixqjp', sbg, eye2).reshape(G // 2, 2, lk, 8 * P).astype(BF16)

    def readout(v):
        return v.transpose(1, 3, 0, 2).reshape(G, P, lk)

    idx_b = L - jnp.arange(L)
    rcg = jnp.stack([readout(car[1:L + 1, 0]), -readout(cai[1:L + 1, 0]),
                     readout(car[idx_b, 1]), -readout(cai[idx_b, 1])], axis=1)
    rcp = jnp.einsum('aiqpx,ij->aqipjx', rcg.reshape(G // 2, 2, 4, P, lk), eye2)
    rcp = rcp.reshape(G // 2, 4, 2 * P, 2 * lk).astype(BF16)
    coef = jnp.stack([apr[L, 0].reshape(-1), api[L, 0].reshape(-1),
                      apr[L, 1].reshape(-1), api[L, 1].reshape(-1)], axis=0)
    coef = jnp.pad(coef, ((0, 4), (0, 0)))
    return kt, sbp, rcp, coef


def _s5_states_kernel(u_ref, sb_ref, fre, fim, bre, bim):
    s = _dot(u_ref[0], sb_ref[0, 0]) + _dot(u_ref[1], sb_ref[0, 1])
    fre[...] = s[:, 0:128]
    fim[...] = s[:, 128:256]
    bre[...] = s[:, 256:384]
    bim[...] = s[:, 384:512]


def _s5_rec_kernel(coef_ref, sfr, sfi, sbr, sbi, hfr, hfi, hbr, hbi, *, n_chunks, ctx_chunks):
    arf, aif = coef_ref[0:1, :], coef_ref[1:2, :]
    arb, aib = coef_ref[2:3, :], coef_ref[3:4, :]
    zero = jnp.zeros_like(arf)

    def fstep(c, carry):
        hr, hi = carry
        hfr[pl.ds(c, 1), :] = hr
        hfi[pl.ds(c, 1), :] = hi
        return (arf * hr - aif * hi + sfr[pl.ds(c, 1), :], arf * hi + aif * hr + sfi[pl.ds(c, 1), :])

    lax.fori_loop(0, n_chunks, fstep, (zero, zero))

    def bstep(i, carry):
        c = jnp.where(i < ctx_chunks, ctx_chunks - 1 - i, n_chunks - 1 - (i - ctx_chunks))
        hr, hi = carry
        hbr[pl.ds(c, 1), :] = hr
        hbi[pl.ds(c, 1), :] = hi
        return (arb * hr - aib * hi + sbr[pl.ds(c, 1), :], arb * hi + aib * hr + sbi[pl.ds(c, 1), :])

    lax.fori_loop(0, n_chunks, bstep, (zero, zero))


def _s5_out_kernel(u_ref, kt_ref, hfr, hfi, hbr, hbi, rc_ref, y_ref):
    inter = (_dot(hfr[...].astype(BF16), rc_ref[0, 0]) + _dot(hfi[...].astype(BF16), rc_ref[0, 1])
             + _dot(hbr[...].astype(BF16), rc_ref[0, 2]) + _dot(hbi[...].astype(BF16), rc_ref[0, 3]))
    half = inter.shape[1] // 2
    y_ref[0] = _dot(u_ref[0], kt_ref[0]) + inter[:, :half]
    y_ref[1] = _dot(u_ref[1], kt_ref[1]) + inter[:, half:]


def _s5_mix(u, nc, prep):
    kt, sbp, rcp, coef = prep
    n_all = u.shape[0]
    L, G, K, P = S5_CHUNK, S5_GROUPS, S5_GROUP_CH, S5_STATE
    C = n_all // L
    lk = L * K
    ug = u.reshape(C, L, G, K).transpose(2, 0, 1, 3).reshape(G, C, lk).astype(BF16)
    gp = G // 2
    st_shape = jax.ShapeDtypeStruct((C, G * P), F32)
    st_spec = pl.BlockSpec((C, 2 * P), lambda p: (0, p))
    states = pl.pallas_call(
        _s5_states_kernel,
        grid=(gp,),
        in_specs=[pl.BlockSpec((2, C, lk), lambda p: (p, 0, 0)),
                  pl.BlockSpec((1, 2, lk, 8 * P), lambda p: (p, 0, 0, 0))],
        out_specs=[st_spec] * 4,
        out_shape=[st_shape] * 4,
        compiler_params=_params("arbitrary"),
        name="s5_states",
    )(ug, sbp)
    cb = 512
    col = pl.BlockSpec((C, cb), lambda j: (0, j))
    hs = pl.pallas_call(
        functools.partial(_s5_rec_kernel, n_chunks=C, ctx_chunks=nc // L),
        grid=(G * P // cb,),
        in_specs=[pl.BlockSpec((8, cb), lambda j: (0, j))] + [col] * 4,
        out_specs=[col] * 4,
        out_shape=[st_shape] * 4,
        compiler_params=_params("arbitrary"),
        name="s5_recurrence",
    )(coef, *states)
    y = pl.pallas_call(
        _s5_out_kernel,
        grid=(gp,),
        in_specs=[pl.BlockSpec((2, C, lk), lambda p: (p, 0, 0)),
                  pl.BlockSpec((2, lk, lk), lambda p: (p, 0, 0))] + [st_spec] * 4
                 + [pl.BlockSpec((1, 4, 2 * P, 2 * lk), lambda p: (p, 0, 0, 0))],
        out_specs=pl.BlockSpec((2, C, lk), lambda p: (p, 0, 0)),
        out_shape=jax.ShapeDtypeStruct((G, C, lk), F32),
        compiler_params=_params("arbitrary"),
        name="s5_out",
    )(ug, kt, *hs, rcp)
    return y.reshape(G, C, L, K).transpose(1, 2, 0, 3).reshape(n_all, G * K)


def _conv_kernel(cur_ref, prev_ref, next_ref, w_ref, b_ref, o_ref, *, n_tiles):
    i = pl.program_id(0)
    cur = cur_ref[...]
    t = cur.shape[0]
    pv = prev_ref[...] * jnp.where(i >= 2, 1.0, 0.0)
    nx = next_ref[...] * jnp.where((i >= 1) & (i <= n_tiles - 2), 1.0, 0.0)
    row8 = lax.broadcasted_iota(I32, pv.shape, 0)
    acc = b_ref[...] + w_ref[2:3, :] * cur
    for s in (1, 2):
        r = pltpu.roll(cur, s, 0)
        head = jnp.where(row8 < s, pltpu.roll(pv, s, 0), r[0:8])
        acc = acc + w_ref[2 - s:3 - s, :] * jnp.concatenate([head, r[8:]], axis=0)
        r = pltpu.roll(cur, t - s, 0)
        tail = jnp.where(row8 >= 8 - s, pltpu.roll(nx, 8 - s, 0), r[t - 8:])
        acc = acc + w_ref[2 + s:3 + s, :] * jnp.concatenate([r[:t - 8], tail], axis=0)
    o_ref[...] = acc * _sigmoid(acc)


def _ssd_conv(xbc, conv_w, conv_b):
    n_all, ch = xbc.shape
    t = TOK_TILE
    n_tiles = n_all // t
    per = t // 8
    return pl.pallas_call(
        functools.partial(_conv_kernel, n_tiles=n_tiles),
        grid=(n_tiles,),
        in_specs=[pl.BlockSpec((t, ch), lambda i: (i, 0)),
                  pl.BlockSpec((8, ch), lambda i: (jnp.maximum(i * per - 1, 0), 0)),
                  pl.BlockSpec((8, ch), lambda i: (jnp.minimum((i + 1) * per, n_tiles * per - 1), 0)),
                  pl.BlockSpec((8, ch), lambda i: (0, 0)),
                  pl.BlockSpec((1, ch), lambda i: (0, 0))],
        out_specs=pl.BlockSpec((t, ch), lambda i: (i, 0)),
        out_shape=jax.ShapeDtypeStruct((n_all, ch), F32),
        compiler_params=_params("arbitrary"),
        name="ssd_conv",
    )(xbc, xbc, xbc, jnp.pad(conv_w, ((0, 3), (0, 0))), conv_b.reshape(1, ch))


def _ssd_kernel(t_ref, dtr_ref, dtrt_ref, prow_ref, pcol_ref, y_ref, h_ref, *, reverse, d):
    q = SSD_CHUNK
    hpg = SSD_HEADS // SSD_GROUPS
    p = SSD_HEAD_DIM

    @pl.when(pl.program_id(0) == 0)
    def _():
        h_ref[...] = jnp.zeros_like(h_ref)

    lo = d * SSD_HEADS
    dt_c = _softplus(dtr_ref[:, lo:lo + SSD_HEADS] + prow_ref[1:2, 0:SSD_HEADS])
    adt_c = dt_c * prow_ref[0:1, 0:SSD_HEADS]
    dt_r = _softplus(dtrt_ref[lo:lo + SSD_HEADS, :] + pcol_ref[:, 1:2])
    adt_r = dt_r * pcol_ref[:, 0:1]
    ii = lax.broadcasted_iota(I32, (q, q), 0)
    jj = lax.broadcasted_iota(I32, (q, q), 1)
    causal = (jj >= ii) if reverse else (jj <= ii)
    acum_c = _dot_hi(causal.astype(F32), adt_c)
    acum_r = _dot_hi(adt_r, ((ii >= jj) if reverse else (ii <= jj)).astype(F32))
    tot = acum_c[0:1, :] if reverse else acum_c[q - 1:q, :]
    for g in range(SSD_GROUPS):
        bg = t_ref[:, 512 + g * SSD_STATE:512 + (g + 1) * SSD_STATE]
        cg = t_ref[:, 768 + g * SSD_STATE:768 + (g + 1) * SSD_STATE].astype(BF16)
        cb = _dot_nt(cg, bg.astype(BF16))
        bgt = bg.T.astype(BF16)
        for r in range(hpg):
            hd = g * hpg + r
            xh = t_ref[:, hd * p:(hd + 1) * p]
            ac = acum_c[:, hd:hd + 1]
            seg = ac - acum_r[hd:hd + 1, :]
            lmat = jnp.exp(jnp.where(causal, seg, NEG))
            wts = (cb * lmat * dt_r[hd:hd + 1, :]).astype(BF16)
            hin = h_ref[g, :, r * p:(r + 1) * p]
            y = _dot(wts, xh.astype(BF16)) + _dot(cg, hin.astype(BF16)) * jnp.exp(ac)
            y_ref[:, hd * p:(hd + 1) * p] = y
            th = tot[:, hd:hd + 1]
            xw = (xh * (jnp.exp(th - ac) * dt_c[:, hd:hd + 1])).astype(BF16)
            h_ref[g, :, r * p:(r + 1) * p] = hin * jnp.exp(th) + _dot(bgt, xw)


def _ssd_scan(t, dtr, dtrt, a, bias, nc, d):
    n_all = t.shape[0]
    q = SSD_CHUNK
    n_chunks = n_all // q
    cc = nc // q
    reverse = d == 1
    if reverse:
        order = lambda i: jnp.where(i < cc, cc - 1 - i, n_chunks - 1 - (i - cc))
    else:
        order = lambda i: i
    prow = jnp.zeros((8, LANE), F32).at[0, :SSD_HEADS].set(a).at[1, :SSD_HEADS].set(bias)
    pcol = jnp.zeros((SSD_HEADS, LANE), F32).at[:, 0].set(a).at[:, 1].set(bias)
    return pl.pallas_call(
        functools.partial(_ssd_kernel, reverse=reverse, d=d),
        grid=(n_chunks,),
        in_specs=[pl.BlockSpec((q, t.shape[1]), lambda i: (order(i), 0)),
                  pl.BlockSpec((q, LANE), lambda i: (order(i), 0)),
                  pl.BlockSpec((2 * SSD_HEADS, q), lambda i: (0, order(i))),
                  pl.BlockSpec((8, LANE), lambda i: (0, 0)),
                  pl.BlockSpec((SSD_HEADS, LANE), lambda i: (0, 0))],
        out_specs=pl.BlockSpec((q, SSD_HEADS * SSD_HEAD_DIM), lambda i: (order(i), 0)),
        out_shape=jax.ShapeDtypeStruct((n_all, SSD_HEADS * SSD_HEAD_DIM), F32),
        scratch_shapes=[pltpu.VMEM((SSD_GROUPS, SSD_STATE, (SSD_HEADS // SSD_GROUPS) * SSD_HEAD_DIM), F32)],
        compiler_params=_params("arbitrary"),
        name="ssd_scan_bwd" if reverse else "ssd_scan_fwd",
    )(t, dtr, dtrt, prow, pcol)


def _attn_kernel(sink_ref, q_ref, kp_ref, ko_ref, kn_ref, kc_ref, o_ref, *, n_blocks, ctx_blocks):
    c = pl.program_id(0)
    blk = ATTN_BLOCK
    dh = ATTN_HEAD_DIM
    rep = ATTN_HEADS // ATTN_KV_HEADS
    rows = rep * blk
    nctx = kc_ref.shape[0]
    lat = c >= ctx_blocks
    lo = jnp.where(lat & (c > ctx_blocks), 0, blk)
    hi = jnp.where(lat, jnp.where(c < n_blocks - 1, 3 * blk, 2 * blk), 0)
    qi = lax.broadcasted_iota(I32, (rows, 3 * blk), 0) % blk
    col = lax.broadcasted_iota(I32, (rows, 3 * blk), 1)
    mask = (jnp.abs(qi - (col - blk)) <= blk) & (col >= lo) & (col < hi)
    rowhead = lax.broadcasted_iota(I32, (rows, 1), 0) // blk
    for g in range(ATTN_KV_HEADS):
        qg = jnp.concatenate([q_ref[:, (g * rep + r) * dh:(g * rep + r + 1) * dh] for r in range(rep)], axis=0)
        ks = slice(g * dh, (g + 1) * dh)
        vs = slice(ATTN_KV_HEADS * dh + g * dh, ATTN_KV_HEADS * dh + (g + 1) * dh)
        kloc = jnp.concatenate([kp_ref[:, ks], ko_ref[:, ks], kn_ref[:, ks]], axis=0)
        vloc = jnp.concatenate([kp_ref[:, vs], ko_ref[:, vs], kn_ref[:, vs]], axis=0)
        s_loc = jnp.where(mask, _dot_nt(qg, kloc), NEG)
        s_ctx = _dot_nt(qg, kc_ref[:, ks])
        sink = jnp.zeros((rows, 1), F32)
        for r in range(rep):
            sink = jnp.where(rowhead == r, sink_ref[g * rep + r], sink)
        mx = jnp.maximum(jnp.maximum(jnp.max(s_loc, axis=1, keepdims=True),
                                     jnp.max(s_ctx, axis=1, keepdims=True)), sink)
        p_loc = jnp.exp(s_loc - mx)
        p_ctx = jnp.exp(s_ctx - mx)
        den = (jnp.sum(p_loc, axis=1, keepdims=True) + jnp.sum(p_ctx, axis=1, keepdims=True)
               + jnp.exp(sink - mx))
        o = (_dot(p_loc.astype(BF16), vloc) + _dot(p_ctx.astype(BF16), kc_ref[:, vs])) / den
        for r in range(rep):
            o_ref[:, (g * rep + r) * dh:(g * rep + r + 1) * dh] = o[r * blk:(r + 1) * blk]


def _attention(q, kv, sink, nc):
    n_all = q.shape[0]
    blk = ATTN_BLOCK
    n_blocks = n_all // blk
    cbk = nc // blk
    kvw = kv.shape[1]
    return pl.pallas_call(
        functools.partial(_attn_kernel, n_blocks=n_blocks, ctx_blocks=cbk),
        grid=(n_blocks,),
        in_specs=[pl.BlockSpec(memory_space=pltpu.SMEM),
                  pl.BlockSpec((blk, q.shape[1]), lambda c: (c, 0)),
                  pl.BlockSpec((blk, kvw), lambda c: (jnp.maximum(c - 1, 0), 0)),
                  pl.BlockSpec((blk, kvw), lambda c: (c, 0)),
                  pl.BlockSpec((blk, kvw), lambda c: (jnp.minimum(c + 1, n_blocks - 1), 0)),
                  pl.BlockSpec((nc, kvw), lambda c: (0, 0))],
        out_specs=pl.BlockSpec((blk, q.shape[1]), lambda c: (c, 0)),
        out_shape=jax.ShapeDtypeStruct((n_all, q.shape[1]), F32),
        compiler_params=_params("arbitrary"),
        name="window_attention",
    )(sink, q, kv, kv, kv, kv)


def _merge_kernel(x_ref, mod_ref, ys5_ref, u_ref, yf_ref, yb_ref, xs_ref, z_ref, att_ref, gates_ref,
                  s5d_ref, bglu_ref, ssdd_ref, ssdg_ref, n2g_ref, wglu_ref, wbr_ref, wout_ref, wr_ref,
                  xo_ref, h2_ref, aff_ref):
    m = mod_ref[0]
    d = x_ref.shape[1]
    a = jax.nn.gelu(ys5_ref[...] + s5d_ref[...] * u_ref[...])
    ya = a * _sigmoid(_dot(a.astype(BF16), wglu_ref[...]) + bglu_ref[...])
    z = z_ref[...]
    yz = (yf_ref[...] + yb_ref[...] + ssdd_ref[...] * xs_ref[...]) * (z * _sigmoid(z))
    yb = _rms(yz) * ssdg_ref[...]
    br = (_sigmoid(gates_ref[:, 0:d].astype(F32)) * _dot(ya.astype(BF16), wbr_ref[0])
          + _sigmoid(gates_ref[:, d:2 * d].astype(F32)) * _dot(yb.astype(BF16), wbr_ref[1])
          + _sigmoid(gates_ref[:, 2 * d:3 * d].astype(F32)) * _dot(att_ref[...].astype(BF16), wbr_ref[2]))
    xn = x_ref[...] + m[2:3] * _dot(br.astype(BF16), wout_ref[...])
    xo_ref[...] = xn
    h2 = _rms(xn) * n2g_ref[...] * (1.0 + m[4:5]) + m[3:4]
    h2_ref[...] = h2.astype(BF16)
    logits = _dot_hi(h2, wr_ref[...])[:, 0:N_EXPERTS]
    e = jnp.exp(logits - jnp.max(logits, axis=1, keepdims=True))
    aff_ref[...] = e / jnp.sum(e, axis=1, keepdims=True)


def _merge(xall, mod, ys5, u, yf, yb, t, z, att, gates, vecs, wglu, wbr, wout, wr):
    n_all, d = xall.shape
    tt = TOK_TILE
    row = lambda i: (i, 0)
    const2 = lambda i: (0, 0)
    bw = 512
    s5d, bglu, ssdd, ssdg, n2g = vecs
    return pl.pallas_call(
        _merge_kernel,
        grid=(n_all // tt,),
        in_specs=[pl.BlockSpec((tt, d), row),
                  pl.BlockSpec((1, 6, d), lambda i: (jnp.minimum(i, 1), 0, 0)),
                  pl.BlockSpec((tt, bw), row), pl.BlockSpec((tt, bw), row), pl.BlockSpec((tt, bw), row),
                  pl.BlockSpec((tt, bw), row), pl.BlockSpec((tt, bw), row), pl.BlockSpec((tt, bw), row),
                  pl.BlockSpec((tt, bw), row), pl.BlockSpec((tt, 3 * d), row),
                  pl.BlockSpec((1, bw), const2), pl.BlockSpec((1, bw), const2), pl.BlockSpec((1, bw), const2),
                  pl.BlockSpec((1, bw), const2), pl.BlockSpec((1, d), const2),
                  pl.BlockSpec((bw, bw), const2),
                  pl.BlockSpec((3, bw, d), lambda i: (0, 0, 0)),
                  pl.BlockSpec((d, d), const2),
                  pl.BlockSpec((d, LANE), const2)],
        out_specs=[pl.BlockSpec((tt, d), row),
                   pl.BlockSpec((tt, d), lambda i: (jnp.where(i == 0, n_all // tt - 1, i - 1), 0)),
                   pl.BlockSpec((tt, N_EXPERTS), row)],
        out_shape=[jax.ShapeDtypeStruct((n_all, d), F32), jax.ShapeDtypeStruct((n_all, d), BF16),
                   jax.ShapeDtypeStruct((n_all, N_EXPERTS), F32)],
        compiler_params=_params("arbitrary"),
        name="merge_router",
    )(xall, mod, ys5, u, yf, yb, t, z, att, gates, s5d, bglu, ssdd, ssdg, n2g, wglu, wbr, wout, wr)


def _route_kernel(a_ref, g_ref, pos_ref, cum_ref, *, cap):
    n = a_ref.shape[1]
    e = a_ref.shape[0]
    bits = lax.bitcast_convert_type(a_ref[...], I32)
    capf = float(cap)

    def search(i, thr):
        cand = thr | (1 << (30 - i))
        cnt = jnp.sum((bits >= cand).astype(F32), axis=1, keepdims=True)
        return jnp.where(cnt >= capf, cand, thr)

    thr = lax.fori_loop(0, 31, search, jnp.zeros((e, 1), I32))
    need = capf - jnp.sum((bits > thr).astype(F32), axis=1, keepdims=True)
    ii = lax.broadcasted_iota(I32, (LANE, LANE), 0)
    jj = lax.broadcasted_iota(I32, (LANE, LANE), 1)
    upper = (ii < jj).astype(BF16)

    def block(b, carry):
        eq_off, pos_off = carry
        sl = pl.ds(pl.multiple_of(b * LANE, LANE), LANE)
        a = a_ref[:, sl]
        v = lax.bitcast_convert_type(a, I32)
        eq = v == thr
        eqf = eq.astype(BF16)
        rank = _dot(eqf, upper) + eq_off
        sel = (v > thr) | (eq & (rank < need))
        self_ = sel.astype(BF16)
        pos = _dot(self_, upper) + pos_off
        g_ref[:, sl] = jnp.where(sel, a, 0.0)
        pos_ref[:, sl] = jnp.where(sel, pos, -1.0).astype(I32)
        cum_ref[:, sl] = pos.astype(I32)
        return (eq_off + jnp.sum(eqf.astype(F32), axis=1, keepdims=True),
                pos_off + jnp.sum(self_.astype(F32), axis=1, keepdims=True))

    zero = jnp.zeros((e, 1), F32)
    lax.fori_loop(0, n // LANE, block, (zero, zero))


def _route(aff_t, cap):
    e, n = aff_t.shape
    return pl.pallas_call(
        functools.partial(_route_kernel, cap=cap),
        out_shape=[jax.ShapeDtypeStruct((e, n), F32), jax.ShapeDtypeStruct((e, n), I32),
                   jax.ShapeDtypeStruct((e, n), I32)],
        name="ec_route",
    )(aff_t)


MOE_BLOCK = 1024
MOE_SUB = 256
MOE_WIN = 64
COMB_WIN = 128
COMB_MAX_ROUNDS = 3


def _moe_ffn_kernel(offs_ref, h_ref, pos_ref, wg_ref, wu_ref, wd_ref, ye_ref, xe_ref, wgb, wub, wdb,
                    *, cap, n_blocks, n_sub, rchunk):
    e = pl.program_id(0)
    b = pl.program_id(1)
    subs = h_ref.shape[0] // MOE_SUB

    @pl.when(b == 0)
    def _():
        xe_ref[...] = jnp.zeros_like(xe_ref)
        slab = 256

        def cast(i, carry):
            rows = pl.ds(pl.multiple_of(i * slab, slab), slab)
            wgb[rows, :] = wg_ref[0, 0, rows, :].astype(BF16)
            wub[rows, :] = wu_ref[0, 0, rows, :].astype(BF16)
            wdb[rows, :] = wd_ref[0, 0, rows, :].astype(BF16)
            return carry

        lax.fori_loop(0, wgb.shape[0] // slab, cast, 0)

    for s in range(subs):
        js = b * subs + s
        o = offs_ref[e * (n_sub + 1) + js]
        o2 = offs_ref[e * (n_sub + 1) + js + 1]
        a = (o // 16) * 16
        n_win = jnp.where(o2 > o, (o2 - a + MOE_WIN - 1) // MOE_WIN, 0)
        pos = pos_ref[pl.ds(e, 1), s * MOE_SUB:(s + 1) * MOE_SUB]
        hs = h_ref[s * MOE_SUB:(s + 1) * MOE_SUB, :]

        def window(m, carry):
            r0 = pl.multiple_of(a + m * MOE_WIN, 16)
            slot = lax.broadcasted_iota(I32, (MOE_WIN, MOE_SUB), 0) + r0
            sel = jnp.where(slot == pos, 1.0, 0.0).astype(BF16)
            rows = pl.ds(r0, MOE_WIN)
            xe_ref[rows, :] = (xe_ref[rows, :].astype(F32) + _dot(sel, hs)).astype(BF16)
            return carry

        lax.fori_loop(0, n_win, window, 0)

    @pl.when(b == n_blocks - 1)
    def _():
        def chunk(ci, carry):
            rows = pl.ds(pl.multiple_of(ci * rchunk, rchunk), rchunk)
            xb = xe_ref[rows, :]
            hg = _dot(xb, wgb[...])
            hid = (hg * _sigmoid(hg) * _dot(xb, wub[...])).astype(BF16)
            ye_ref[0, rows, :] = _dot(hid, wdb[...]).astype(BF16)
            return carry

        lax.fori_loop(0, cap // rchunk, chunk, 0)
        ye_ref[0, cap:, :] = jnp.zeros((ye_ref.shape[1] - cap, ye_ref.shape[2]), BF16)


def _moe_ffn(h2, pos_t, offs, wg, wu, wd, layer, row0, n, cap):
    d = h2.shape[1]
    _, e, _, f = wg.shape
    tb = min(MOE_BLOCK, n)
    n_blocks = n // tb
    blk0 = row0 // tb
    n_sub = n // MOE_SUB
    rchunk = min(cap, 256)
    ye_rows = cap + COMB_MAX_ROUNDS * COMB_WIN
    wspec = lambda r, c: pl.BlockSpec((1, 1, r, c), lambda ei, b, offs: (layer, ei, 0, 0))
    grid_spec = pltpu.PrefetchScalarGridSpec(
        num_scalar_prefetch=1,
        grid=(e, n_blocks),
        in_specs=[pl.BlockSpec((tb, d), lambda ei, b, offs: (b + blk0, 0)),
                  pl.BlockSpec((e, tb), lambda ei, b, offs: (0, b)),
                  wspec(d, f), wspec(d, f), wspec(f, d)],
        out_specs=pl.BlockSpec((1, ye_rows, d), lambda ei, b, offs: (ei, 0, 0)),
        scratch_shapes=[pltpu.VMEM((cap + MOE_WIN, d), BF16), pltpu.VMEM((d, f), BF16),
                        pltpu.VMEM((d, f), BF16), pltpu.VMEM((f, d), BF16)],
    )
    return pl.pallas_call(
        functools.partial(_moe_ffn_kernel, cap=cap, n_blocks=n_blocks, n_sub=n_sub, rchunk=rchunk),
        grid_spec=grid_spec,
        out_shape=jax.ShapeDtypeStruct((e, ye_rows, d), BF16),
        compiler_params=pltpu.CompilerParams(dimension_semantics=("arbitrary", "arbitrary"),
                                             vmem_limit_bytes=56 * 1024 * 1024),
        name="moe_ffn",
    )(offs, h2, pos_t, wg, wu, wd)


def _moe_window_copy(ye_hbm, buf, sem, slot, e, start):
    return pltpu.make_async_copy(ye_hbm.at[e, pl.ds(start, COMB_WIN), :], buf.at[slot, e], sem.at[slot, e])


def _moe_combine_kernel(offs_ref, rounds_ref, x_ref, g_ref, pos_ref, mod_ref, ye_hbm, o_ref,
                        buf, lhs_hi, lhs_lo, acc_ref, sem, *, n_tiles):
    j = pl.program_id(0)
    t = x_ref.shape[0]
    n_exp = g_ref.shape[1]
    lane = lax.broadcasted_iota(I32, (t, COMB_WIN), 1)

    def starts_of(tile, rnd):
        return [pl.multiple_of((offs_ref[e * (n_tiles + 1) + tile] // 16) * 16 + rnd * COMB_WIN, 16)
                for e in range(n_exp)]

    def fetch(slot, starts):
        for e in range(n_exp):
            _moe_window_copy(ye_hbm, buf, sem, slot, e, starts[e]).start()

    def land(slot, starts):
        for e in range(n_exp):
            _moe_window_copy(ye_hbm, buf, sem, slot, e, starts[e]).wait()

    def expand(slot, starts):
        for e in range(n_exp):
            val = jnp.where(pos_ref[:, e:e + 1] == lane + starts[e], g_ref[:, e:e + 1], 0.0)
            hi = val.astype(BF16)
            lhs_hi[:, e * COMB_WIN:(e + 1) * COMB_WIN] = hi
            lhs_lo[:, e * COMB_WIN:(e + 1) * COMB_WIN] = (val - hi.astype(F32)).astype(BF16)
        win = buf[slot].reshape(n_exp * COMB_WIN, buf.shape[3])
        return _dot(lhs_hi[...], win) + _dot(lhs_lo[...], win)

    cur = j % 2
    first = starts_of(j, 0)

    @pl.when(j == 0)
    def _():
        fetch(0, first)

    @pl.when(j + 1 < n_tiles)
    def _():
        fetch(1 - cur, starts_of(j + 1, 0))

    land(cur, first)
    acc_ref[...] = expand(cur, first)

    def more(rnd, carry):
        starts = starts_of(j, rnd)
        fetch(2, starts)
        land(2, starts)
        acc_ref[...] += expand(2, starts)
        return carry

    lax.fori_loop(1, rounds_ref[j], more, 0)
    o_ref[...] = x_ref[...] + mod_ref[0, 5:6, :] * acc_ref[...]


def _moe_combine(xall, g, pos, offs, rounds, mod, ye, tile0, n, mod_row):
    t = MOE_SUB
    d = xall.shape[1]
    n_exp = g.shape[1]
    n_tiles = n // t
    grid_spec = pltpu.PrefetchScalarGridSpec(
        num_scalar_prefetch=2,
        grid=(n_tiles,),
        in_specs=[pl.BlockSpec((t, d), lambda j, offs, rounds: (j + tile0, 0)),
                  pl.BlockSpec((t, n_exp), lambda j, offs, rounds: (j, 0)),
                  pl.BlockSpec((t, n_exp), lambda j, offs, rounds: (j, 0)),
                  pl.BlockSpec((1, 6, d), lambda j, offs, rounds: (mod_row, 0, 0)),
                  pl.BlockSpec(memory_space=pl.ANY)],
        out_specs=pl.BlockSpec((t, d), lambda j, offs, rounds: (j + tile0, 0)),
        scratch_shapes=[pltpu.VMEM((3, n_exp, COMB_WIN, d), BF16),
                        pltpu.VMEM((t, n_exp * COMB_WIN), BF16), pltpu.VMEM((t, n_exp * COMB_WIN), BF16),
                        pltpu.VMEM((t, d), F32), pltpu.SemaphoreType.DMA((3, n_exp))],
    )
    return pl.pallas_call(
        functools.partial(_moe_combine_kernel, n_tiles=n_tiles),
        grid_spec=grid_spec,
        out_shape=jax.ShapeDtypeStruct(xall.shape, F32),
        input_output_aliases={2: 0},
        compiler_params=_params("arbitrary"),
        name="moe_combine",
    )(offs, rounds, xall, g, pos, mod, ye)


def _expert_choice(xall, h2, aff, mod, wg, wu, wd, layer, row0, h_row0, n, mod_row):
    cap = EC_CAPACITY_FACTOR * n // N_EXPERTS
    g_t, pos_t, cum_t = _route(aff[row0:row0 + n].T, cap)
    offs = jnp.concatenate([cum_t[:, ::MOE_SUB], jnp.full((N_EXPERTS, 1), cap, I32)], axis=1)
    span = offs[:, 1:] - (offs[:, :-1] // 16) * 16
    rounds = jnp.maximum(jnp.max((span + COMB_WIN - 1) // COMB_WIN, axis=0), 1).astype(I32)
    offs = offs.reshape(-1)
    ye = _moe_ffn(h2, pos_t, offs, wg, wu, wd, layer, h_row0, n, cap)
    return _moe_combine(xall, g_t.T, pos_t.T, offs, rounds, mod, ye, row0 // MOE_SUB, n, mod_row)


def _final_kernel(x_ref, g_ref, o_ref):
    o_ref[...] = _rms(x_ref[...]) * g_ref[...]


def _final_norm(xall, g, tile0, n):
    t = TOK_TILE
    d = xall.shape[1]
    return pl.pallas_call(
        _final_kernel,
        grid=(n // t,),
        in_specs=[pl.BlockSpec((t, d), lambda i: (i + tile0, 0)), pl.BlockSpec((1, d), lambda i: (0, 0))],
        out_specs=pl.BlockSpec((t, d), lambda i: (i, 0)),
        out_shape=jax.ShapeDtypeStruct((n, d), F32),
        compiler_params=_params("arbitrary"),
        name="final_norm",
    )(xall, g)


def kernel(x, c, ctx, c_ctx, w_mod, b_mod, norm1_g, norm2_g, w_in, s5_lam_re, s5_lam_im, s5_log_dt, s5_b_re, s5_b_im, s5_c_re, s5_c_im, s5_d, s5_w_glu, s5_b_glu, ssd_conv_w, ssd_conv_b, ssd_a_log, ssd_dt_bias, ssd_d, ssd_norm_g, attn_sink, w_branch, w_out, w_router, w_e_gate, w_e_up, w_e_down, final_norm_g):
    batch, n, d = x.shape
    nc = ctx.shape[1]
    depth = w_mod.shape[0]
    assert batch == 1 and nc == TOK_TILE and n % TOK_TILE == 0 and n % GRID_W == 0
    xall = jnp.concatenate([ctx[0], x[0]], axis=0)
    cvecs = jnp.zeros((8, d), F32).at[0].set(c_ctx).at[1].set(c[0])
    mods = _modulation(cvecs, w_mod, b_mod)
    cos, sin = _rope_tables(n, nc)
    row = lambda v: v.reshape(1, -1).astype(F32)
    for i in range(depth):
        mod = mods[i, 0:2].reshape(2, 6, d)
        u, z, xbc, q, kv, gates, dtr = _inproj(xall, mod, row(norm1_g[i]), _prep_w_in(w_in[i]), cos, sin)
        ys5 = _s5_mix(u, nc, _s5_prep(s5_lam_re[i], s5_lam_im[i], s5_log_dt[i], s5_b_re[i], s5_b_im[i],
                                      s5_c_re[i], s5_c_im[i]))
        t = _ssd_conv(xbc, ssd_conv_w[i], ssd_conv_b[i])
        a = -jnp.exp(ssd_a_log[i].astype(F32))
        dtrt = dtr[:, 0:2 * SSD_HEADS].T
        yf = _ssd_scan(t, dtr, dtrt, a[0], ssd_dt_bias[i, 0], nc, 0)
        yb = _ssd_scan(t, dtr, dtrt, a[1], ssd_dt_bias[i, 1], nc, 1)
        att = _attention(q, kv, attn_sink[i].astype(F32), nc)
        vecs = (row(s5_d[i]), row(s5_b_glu[i]), row(jnp.repeat(ssd_d[i], SSD_HEAD_DIM)), row(ssd_norm_g[i]),
                row(norm2_g[i]))
        wr = jnp.pad(w_router[i].astype(F32), ((0, 0), (0, LANE - N_EXPERTS)))
        xall, h2, aff = _merge(xall, mod, ys5, u, yf, yb, t, z, att, gates, vecs, s5_w_glu[i].astype(BF16),
                               w_branch[i].astype(BF16), w_out[i].astype(BF16), wr)
        xall = _expert_choice(xall, h2, aff, mod, w_e_gate, w_e_up, w_e_down, i, nc, 0, n, 1)
        if i < depth - 1:
            xall = _expert_choice(xall, h2, aff, mod, w_e_gate, w_e_up, w_e_down, i, 0, n, nc, 0)
    return _final_norm(xall, row(final_norm_g), nc // TOK_TILE, n)[None]
```

```python
import functools
import math

import jax
import jax.numpy as jnp
from jax import lax
from jax.experimental import pallas as pl
from jax.experimental.pallas import tpu as pltpu

F32 = jnp.float32
BF16 = jnp.bfloat16
I32 = jnp.int32
HI = lax.Precision.HIGHEST

D_MODEL = 1024
DEPTH = 4
GRID_W = 64
EPS = 1e-6
S5_GROUPS = 32
S5_GROUP_CH = 16
S5_STATE = 64
S5_CHUNK = 32
SSD_HEADS = 8
SSD_HEAD_DIM = 64
SSD_GROUPS = 2
SSD_STATE = 128
SSD_CHUNK = 128
ATTN_HEADS = 8
ATTN_KV_HEADS = 2
ATTN_HEAD_DIM = 64
ATTN_BLOCK = 128
ROPE_BASE = 10000.0
N_EXPERTS = 16
EXPERT_FF = 1024
EC_CAPACITY_FACTOR = 2
IN_SIZES = (512, 512, 1024, 16, 512, 128, 128, 3072)
TOK_TILE = 256
LANE = 128
NEG = -1e30


def _dot(a, b):
    return jnp.dot(a, b, preferred_element_type=F32)


def _dot_hi(a, b):
    return jnp.dot(a, b, precision=HI, preferred_element_type=F32)


def _dot_nt(a, b):
    return lax.dot_general(a, b, (((1,), (1,)), ((), ())), preferred_element_type=F32)


def _sigmoid(x):
    return 1.0 / (1.0 + jnp.exp(-x))


def _softplus(x):
    return jnp.maximum(x, 0.0) + jnp.log(1.0 + jnp.exp(-jnp.abs(x)))


def _rms(x):
    return x * lax.rsqrt(jnp.mean(x * x, axis=-1, keepdims=True) + EPS)


def _params(*sem):
    return pltpu.CompilerParams(dimension_semantics=sem)


def _mod_kernel(c_ref, w_ref, b_ref, o_ref):
    c = c_ref[...]
    o_ref[0] = _dot_hi(c * _sigmoid(c), w_ref[0]) + b_ref[0]


def _modulation(cvecs, w_mod, b_mod):
    depth, d, d6 = w_mod.shape
    bn = 1536
    return pl.pallas_call(
        _mod_kernel,
        grid=(depth, d6 // bn),
        in_specs=[pl.BlockSpec((8, d), lambda l, j: (0, 0)),
                  pl.BlockSpec((1, d, bn), lambda l, j: (l, 0, j)),
                  pl.BlockSpec((1, 1, bn), lambda l, j: (l, 0, j))],
        out_specs=pl.BlockSpec((1, 8, bn), lambda l, j: (l, 0, j)),
        out_shape=jax.ShapeDtypeStruct((depth, 8, d6), F32),
        compiler_params=_params("arbitrary", "arbitrary"),
        name="modulation",
    )(cvecs, w_mod, b_mod.reshape(depth, 1, d6))


W_IN_COLS = 6016


def _attn_head_order():
    rep = ATTN_HEADS // ATTN_KV_HEADS
    heads = [g * rep + b for b in range(rep) for g in range(ATTN_KV_HEADS)]
    return jnp.concatenate([jnp.arange(h * ATTN_HEAD_DIM, (h + 1) * ATTN_HEAD_DIM) for h in heads])


def _prep_w_in(w):
    parts, start = [], 0
    for s in IN_SIZES:
        parts.append(w[:, start:start + s])
        start += s
    u, z, xbc, dt, q, k, v, gates = parts
    dt = jnp.pad(dt, ((0, 0), (0, LANE - dt.shape[1])))
    q = q[:, _attn_head_order()]
    return jnp.concatenate([u, z, xbc, q, k, v, gates, dt], axis=1).astype(BF16)


def _inproj_kernel(x_ref, mod_ref, g_ref, w_ref, cos_ref, sin_ref,
                   u_ref, ub_ref, z_ref, xbc_ref, q_ref, kv_ref, gates_ref, dt_ref):
    m = mod_ref[0]
    h = (_rms(x_ref[...]) * g_ref[...] * (1.0 + m[1:2]) + m[0:1]).astype(BF16)

    def proj(a, b):
        return _dot(h, w_ref[:, a:b])

    u = proj(0, 512)
    u_ref[...] = u
    ub_ref[...] = u.astype(BF16)
    z_ref[...] = proj(512, 1024)
    xbc_ref[...] = proj(1024, 2048)
    cos = cos_ref[...]
    sin = sin_ref[...]
    lane = lax.broadcasted_iota(I32, cos.shape, 1)
    first = (lane % 32) < 16

    def rope(v):
        partner = jnp.where(first, pltpu.roll(v, LANE - 16, 1), pltpu.roll(v, 16, 1))
        return v * cos + partner * sin

    scale = ATTN_HEAD_DIM ** -0.5
    for j in range(4):
        q_ref[:, j * LANE:(j + 1) * LANE] = (rope(proj(2048 + j * LANE, 2048 + (j + 1) * LANE)) * scale).astype(BF16)
    kv_ref[:, 0:LANE] = rope(proj(2560, 2688)).astype(BF16)
    kv_ref[:, LANE:2 * LANE] = proj(2688, 2816).astype(BF16)
    for j in range(6):
        gates_ref[:, j * 512:(j + 1) * 512] = proj(2816 + j * 512, 2816 + (j + 1) * 512).astype(BF16)
    dt_ref[...] = proj(5888, 6016)


def _inproj(xall, mod, g, w, cos, sin):
    n_all, d = xall.shape
    t = TOK_TILE
    row = lambda i: (i, 0)
    const = lambda i: (0, 0)
    widths = (512, 512, 512, 1024, 512, 256, 3072, LANE)
    dtypes = (F32, BF16, F32, F32, BF16, BF16, BF16, F32)
    return pl.pallas_call(
        _inproj_kernel,
        grid=(n_all // t,),
        in_specs=[pl.BlockSpec((t, d), row),
                  pl.BlockSpec((1, 6, d), lambda i: (jnp.minimum(i, 1), 0, 0)),
                  pl.BlockSpec((1, d), const),
                  pl.BlockSpec((d, W_IN_COLS), const, pipeline_mode=pl.Buffered(1)),
                  pl.BlockSpec((t, LANE), row),
                  pl.BlockSpec((t, LANE), row)],
        out_specs=[pl.BlockSpec((t, wd), row) for wd in widths],
        out_shape=[jax.ShapeDtypeStruct((n_all, wd), dt) for wd, dt in zip(widths, dtypes)],
        compiler_params=_params("arbitrary"),
        name="inproj",
    )(xall, mod, g, w, cos, sin)


def _rope_tables(n, nc):
    rows = n // GRID_W
    r = jnp.repeat(jnp.arange(rows, dtype=F32), GRID_W)
    c = jnp.tile(jnp.arange(GRID_W, dtype=F32), rows)
    m = ATTN_HEAD_DIM // 4
    inv_freq = ROPE_BASE ** (-jnp.arange(m, dtype=F32) / m)
    ang_r, ang_c = r[:, None] * inv_freq, c[:, None] * inv_freq
    cos = jnp.concatenate([jnp.cos(ang_r), jnp.cos(ang_r), jnp.cos(ang_c), jnp.cos(ang_c)], axis=1)
    sin = jnp.concatenate([-jnp.sin(ang_r), jnp.sin(ang_r), -jnp.sin(ang_c), jnp.sin(ang_c)], axis=1)
    cos = jnp.concatenate([jnp.ones((nc, 64), F32), cos], axis=0)
    sin = jnp.concatenate([jnp.zeros((nc, 64), F32), sin], axis=0)
    return jnp.tile(cos, (1, 2)), jnp.tile(sin, (1, 2))


def _s5_toeplitz_kernel(bb_ref, ca_ref, kt_ref):
    L, K = S5_CHUNK, S5_GROUP_CH
    tab = _dot_hi(bb_ref[0], ca_ref[0])
    for s_ in range(L):
        lo = (L - 1 - s_) * K
        kt_ref[0, s_ * K:(s_ + 1) * K, :] = tab[:, lo:lo + L * K].astype(BF16)


def _s5_prep(lam_re, lam_im, log_dt, b_re, b_im, c_re, c_im):
    L, G, P, K = S5_CHUNK, S5_GROUPS, S5_STATE, S5_GROUP_CH
    lk = L * K
    lr, li = lam_re.astype(F32), lam_im.astype(F32)
    dt = jnp.exp(log_dt.astype(F32))[..., None]
    mag = jnp.exp(lr * dt)
    abr, abi = mag * jnp.cos(li * dt), mag * jnp.sin(li * dt)
    den = lr * lr + li * li
    fr = ((abr - 1.0) * lr + abi * li) / den
    fi = (abi * lr - (abr - 1.0) * li) / den
    bbr = fr[..., None] * b_re - fi[..., None] * b_im
    bbi = fr[..., None] * b_im + fi[..., None] * b_re
    tau = jnp.arange(L + 1, dtype=F32)[:, None, None, None]
    pm = jnp.exp(lr * dt * tau)
    apr, api = pm * jnp.cos(li * dt * tau), pm * jnp.sin(li * dt * tau)
    cr, ci = c_re.astype(F32), c_im.astype(F32)
    car = cr * apr[:, :, :, None, :] - ci * api[:, :, :, None, :]
    cai = cr * api[:, :, :, None, :] + ci * apr[:, :, :, None, :]

    lagmat = lambda v, d: v[:L, d].transpose(1, 3, 0, 2)
    zeros = jnp.zeros((G, P, L - 1, K), F32)
    fwd = lambda v: jnp.concatenate([zeros, lagmat(v, 0)], axis=2)
    bwd = lambda v: jnp.concatenate([jnp.flip(lagmat(v, 1), axis=2), zeros], axis=2)
    ca = jnp.concatenate([fwd(car), fwd(cai), bwd(car), bwd(cai)], axis=1).reshape(G, 4 * P, (2 * L - 1) * K)
    ca = jnp.pad(ca, ((0, 0), (0, 0), (0, K)))
    tr = lambda v: v.transpose(0, 2, 1)
    bb = jnp.concatenate([tr(bbr[0]), -tr(bbi[0]), tr(bbr[1]), -tr(bbi[1])], axis=2)
    kt = pl.pallas_call(
        _s5_toeplitz_kernel,
        grid=(G,),
        in_specs=[pl.BlockSpec((1, K, 4 * P), lambda g: (g, 0, 0)),
                  pl.BlockSpec((1, 4 * P, 2 * lk), lambda g: (g, 0, 0))],
        out_specs=pl.BlockSpec((1, lk, lk), lambda g: (g, 0, 0)),
        out_shape=jax.ShapeDtypeStruct((G, lk, lk), BF16),
        compiler_params=_params("arbitrary"),
        name="s5_toeplitz",
    )(bb, ca)

    def summary(d, idx):
        ar, ai = apr[idx, d][..., None], api[idx, d][..., None]
        re = ar * bbr[d][None] - ai * bbi[d][None]
        im = ar * bbi[d][None] + ai * bbr[d][None]
        f = lambda v: v.transpose(1, 0, 3, 2).reshape(G, lk, P)
        return f(re), f(im)

    sfr, sfi = summary(0, L - 1 - jnp.arange(L))
    sbr, sbi = summary(1, jnp.arange(L))
    sb = jnp.concatenate([sfr, sfi, sbr, sbi], axis=2).astype(BF16)

    def readout(v):
        return v.transpose(1, 3, 0, 2).reshape(G, P, lk)

    idx_b = L - jnp.arange(L)
    rc = jnp.concatenate([readout(car[1:L + 1, 0]), -readout(cai[1:L + 1, 0]),
                          readout(car[idx_b, 1]), -readout(cai[idx_b, 1])], axis=1).astype(BF16)
    coef = jnp.stack([apr[L, 0].reshape(-1), api[L, 0].reshape(-1),
                      apr[L, 1].reshape(-1), api[L, 1].reshape(-1)], axis=0)
    coef = jnp.pad(coef, ((0, 4), (0, 0)))
    return kt, sb, rc, coef


def _s5_states_kernel(u_ref, sb_ref, fre, fim, bre, bim):
    p = S5_STATE
    s0 = _dot(u_ref[0], sb_ref[0])
    s1 = _dot(u_ref[1], sb_ref[1])
    for q, ref in enumerate((fre, fim, bre, bim)):
        ref[...] = jnp.concatenate([s0[:, q * p:(q + 1) * p], s1[:, q * p:(q + 1) * p]], axis=1)


def _s5_rec_kernel(coef_ref, sfr, sfi, sbr, sbi, hfr, hfi, hbr, hbi, *, n_chunks, ctx_chunks):
    arf, aif = coef_ref[0:1, :], coef_ref[1:2, :]
    arb, aib = coef_ref[2:3, :], coef_ref[3:4, :]
    zero = jnp.zeros_like(arf)

    def fstep(c, carry):
        hr, hi = carry
        hfr[pl.ds(c, 1), :] = hr
        hfi[pl.ds(c, 1), :] = hi
        return (arf * hr - aif * hi + sfr[pl.ds(c, 1), :], arf * hi + aif * hr + sfi[pl.ds(c, 1), :])

    lax.fori_loop(0, n_chunks, fstep, (zero, zero))

    def bstep(i, carry):
        c = jnp.where(i < ctx_chunks, ctx_chunks - 1 - i, n_chunks - 1 - (i - ctx_chunks))
        hr, hi = carry
        hbr[pl.ds(c, 1), :] = hr
        hbi[pl.ds(c, 1), :] = hi
        return (arb * hr - aib * hi + sbr[pl.ds(c, 1), :], arb * hi + aib * hr + sbi[pl.ds(c, 1), :])

    lax.fori_loop(0, n_chunks, bstep, (zero, zero))


def _s5_out_kernel(u_ref, kt_ref, hfr, hfi, hbr, hbi, rc_ref, y_ref):
    p = S5_STATE
    for i in range(2):
        h = jnp.concatenate([r[:, i * p:(i + 1) * p] for r in (hfr, hfi, hbr, hbi)], axis=1).astype(BF16)
        y_ref[i] = (_dot(u_ref[i], kt_ref[i]) + _dot(h, rc_ref[i])).astype(y_ref.dtype)


def _s5_mix(u, nc, prep):
    kt, sb, rc, coef = prep
    n_all = u.shape[0]
    L, G, K, P = S5_CHUNK, S5_GROUPS, S5_GROUP_CH, S5_STATE
    C = n_all // L
    lk = L * K
    ug = u.reshape(C, L, G, K).transpose(2, 0, 1, 3).reshape(G, C, lk)
    gp = G // 2
    st_shape = jax.ShapeDtypeStruct((C, G * P), F32)
    st_spec = pl.BlockSpec((C, 2 * P), lambda p: (0, p))
    states = pl.pallas_call(
        _s5_states_kernel,
        grid=(gp,),
        in_specs=[pl.BlockSpec((2, C, lk), lambda p: (p, 0, 0)),
                  pl.BlockSpec((2, lk, 4 * P), lambda p: (p, 0, 0))],
        out_specs=[st_spec] * 4,
        out_shape=[st_shape] * 4,
        compiler_params=_params("arbitrary"),
        name="s5_states",
    )(ug, sb)
    cb = 512
    col = pl.BlockSpec((C, cb), lambda j: (0, j))
    hs = pl.pallas_call(
        functools.partial(_s5_rec_kernel, n_chunks=C, ctx_chunks=nc // L),
        grid=(G * P // cb,),
        in_specs=[pl.BlockSpec((8, cb), lambda j: (0, j))] + [col] * 4,
        out_specs=[col] * 4,
        out_shape=[st_shape] * 4,
        compiler_params=_params("arbitrary"),
        name="s5_recurrence",
    )(coef, *states)
    y = pl.pallas_call(
        _s5_out_kernel,
        grid=(gp,),
        in_specs=[pl.BlockSpec((2, C, lk), lambda p: (p, 0, 0)),
                  pl.BlockSpec((2, lk, lk), lambda p: (p, 0, 0))] + [st_spec] * 4
                 + [pl.BlockSpec((2, 4 * P, lk), lambda p: (p, 0, 0))],
        out_specs=pl.BlockSpec((2, C, lk), lambda p: (p, 0, 0)),
        out_shape=jax.ShapeDtypeStruct((G, C, lk), BF16),
        compiler_params=_params("arbitrary"),
        name="s5_out",
    )(ug, kt, *hs, rc)
    return y.reshape(G, C, L, K).transpose(1, 2, 0, 3).reshape(n_all, G * K)


def _conv_kernel(cur_ref, prev_ref, next_ref, w_ref, b_ref, o_ref, *, n_tiles):
    i = pl.program_id(0)
    cur = cur_ref[...]
    t = cur.shape[0]
    pv = prev_ref[...] * jnp.where(i >= 2, 1.0, 0.0)
    nx = next_ref[...] * jnp.where((i >= 1) & (i <= n_tiles - 2), 1.0, 0.0)
    row8 = lax.broadcasted_iota(I32, pv.shape, 0)
    acc = b_ref[...] + w_ref[2:3, :] * cur
    for s in (1, 2):
        r = pltpu.roll(cur, s, 0)
        head = jnp.where(row8 < s, pltpu.roll(pv, s, 0), r[0:8])
        acc = acc + w_ref[2 - s:3 - s, :] * jnp.concatenate([head, r[8:]], axis=0)
        r = pltpu.roll(cur, t - s, 0)
        tail = jnp.where(row8 >= 8 - s, pltpu.roll(nx, 8 - s, 0), r[t - 8:])
        acc = acc + w_ref[2 + s:3 + s, :] * jnp.concatenate([r[:t - 8], tail], axis=0)
    o_ref[...] = acc * _sigmoid(acc)


def _ssd_conv(xbc, conv_w, conv_b):
    n_all, ch = xbc.shape
    t = TOK_TILE
    n_tiles = n_all // t
    per = t // 8
    return pl.pallas_call(
        functools.partial(_conv_kernel, n_tiles=n_tiles),
        grid=(n_tiles,),
        in_specs=[pl.BlockSpec((t, ch), lambda i: (i, 0)),
                  pl.BlockSpec((8, ch), lambda i: (jnp.maximum(i * per - 1, 0), 0)),
                  pl.BlockSpec((8, ch), lambda i: (jnp.minimum((i + 1) * per, n_tiles * per - 1), 0)),
                  pl.BlockSpec((8, ch), lambda i: (0, 0)),
                  pl.BlockSpec((1, ch), lambda i: (0, 0))],
        out_specs=pl.BlockSpec((t, ch), lambda i: (i, 0)),
        out_shape=jax.ShapeDtypeStruct((n_all, ch), F32),
        compiler_params=_params("arbitrary"),
        name="ssd_conv",
    )(xbc, xbc, xbc, jnp.pad(conv_w, ((0, 3), (0, 0))), conv_b.reshape(1, ch))


def _ssd_kernel(t_ref, dtr_ref, dtrt_ref, prow_ref, pcol_ref, y_ref, h_ref, *, reverse, d):
    q = SSD_CHUNK
    hpg = SSD_HEADS // SSD_GROUPS
    p = SSD_HEAD_DIM

    @pl.when(pl.program_id(0) == 0)
    def _():
        h_ref[...] = jnp.zeros_like(h_ref)

    lo = d * SSD_HEADS
    dt_c = _softplus(dtr_ref[:, lo:lo + SSD_HEADS] + prow_ref[1:2, 0:SSD_HEADS])
    adt_c = dt_c * prow_ref[0:1, 0:SSD_HEADS]
    dt_r = _softplus(dtrt_ref[lo:lo + SSD_HEADS, :] + pcol_ref[:, 1:2])
    adt_r = dt_r * pcol_ref[:, 0:1]
    ii = lax.broadcasted_iota(I32, (q, q), 0)
    jj = lax.broadcasted_iota(I32, (q, q), 1)
    causal = (jj >= ii) if reverse else (jj <= ii)
    acum_c = _dot_hi(causal.astype(F32), adt_c)
    acum_r = _dot_hi(adt_r, ((ii >= jj) if reverse else (ii <= jj)).astype(F32))
    tot = acum_c[0:1, :] if reverse else acum_c[q - 1:q, :]
    gw = hpg * p
    lane_head = lax.broadcasted_iota(I32, (q, gw), 1) // p
    lane_head_row = lax.broadcasted_iota(I32, (1, gw), 1) // p
    for g in range(SSD_GROUPS):
        bg = t_ref[:, 512 + g * SSD_STATE:512 + (g + 1) * SSD_STATE]
        cg = t_ref[:, 768 + g * SSD_STATE:768 + (g + 1) * SSD_STATE]
        cb = _dot_nt(cg.astype(BF16), bg.astype(BF16))
        bgt = bg.T
        xg = t_ref[:, g * gw:(g + 1) * gw].astype(BF16)
        hg = h_ref[g]
        xh = jnp.concatenate([xg, hg.astype(BF16)], axis=0)
        yg = jnp.zeros((q, gw), F32)
        sg = jnp.zeros((SSD_STATE, gw), F32)
        dec = jnp.zeros((1, gw), F32)
        for r in range(hpg):
            hd = g * hpg + r
            ac = jnp.broadcast_to(acum_c[:, hd:hd + 1], (q, q))
            ar = acum_r[hd:hd + 1, :]
            dtrow = dt_r[hd:hd + 1, :]
            th = tot[:, hd:hd + 1]
            wts = (cb * jnp.exp(jnp.where(causal, ac - ar, NEG)) * dtrow).astype(BF16)
            cs = (cg * jnp.exp(ac)).astype(BF16)
            bs = (bgt * (jnp.exp(th - ar) * dtrow)).astype(BF16)
            yh = _dot(jnp.concatenate([wts, cs], axis=1), xh)
            sh = _dot(bs, xg)
            yg = jnp.where(lane_head == r, yh, yg)
            sg = jnp.where(lane_head[:SSD_STATE] == r, sh, sg)
            dec = jnp.where(lane_head_row == r, jnp.exp(th), dec)
        y_ref[:, g * gw:(g + 1) * gw] = yg
        h_ref[g] = hg * dec + sg


def _ssd_scan(t, dtr, dtrt, a, bias, nc, d):
    n_all = t.shape[0]
    q = SSD_CHUNK
    n_chunks = n_all // q
    cc = nc // q
    reverse = d == 1
    if reverse:
        order = lambda i: jnp.where(i < cc, cc - 1 - i, n_chunks - 1 - (i - cc))
    else:
        order = lambda i: i
    prow = jnp.zeros((8, LANE), F32).at[0, :SSD_HEADS].set(a).at[1, :SSD_HEADS].set(bias)
    pcol = jnp.zeros((SSD_HEADS, LANE), F32).at[:, 0].set(a).at[:, 1].set(bias)
    return pl.pallas_call(
        functools.partial(_ssd_kernel, reverse=reverse, d=d),
        grid=(n_chunks,),
        in_specs=[pl.BlockSpec((q, t.shape[1]), lambda i: (order(i), 0)),
                  pl.BlockSpec((q, LANE), lambda i: (order(i), 0)),
                  pl.BlockSpec((2 * SSD_HEADS, q), lambda i: (0, order(i))),
                  pl.BlockSpec((8, LANE), lambda i: (0, 0)),
                  pl.BlockSpec((SSD_HEADS, LANE), lambda i: (0, 0))],
        out_specs=pl.BlockSpec((q, SSD_HEADS * SSD_HEAD_DIM), lambda i: (order(i), 0)),
        out_shape=jax.ShapeDtypeStruct((n_all, SSD_HEADS * SSD_HEAD_DIM), F32),
        scratch_shapes=[pltpu.VMEM((SSD_GROUPS, SSD_STATE, (SSD_HEADS // SSD_GROUPS) * SSD_HEAD_DIM), F32)],
        compiler_params=_params("arbitrary"),
        name="ssd_scan_bwd" if reverse else "ssd_scan_fwd",
    )(t, dtr, dtrt, prow, pcol)


def _attn_kernel(sink_ref, q_ref, kp_ref, ko_ref, kn_ref, kc_ref, o_ref, *, n_blocks, ctx_blocks):
    c = pl.program_id(0)
    blk = ATTN_BLOCK
    dh = ATTN_HEAD_DIM
    rep = ATTN_HEADS // ATTN_KV_HEADS
    rows = rep * blk
    kw = ATTN_KV_HEADS * dh
    lat = c >= ctx_blocks
    lo = jnp.where(lat & (c > ctx_blocks), 0, blk)
    hi = jnp.where(lat, jnp.where(c < n_blocks - 1, 3 * blk, 2 * blk), 0)
    qi = lax.broadcasted_iota(I32, (rows, 3 * blk), 0) % blk
    col = lax.broadcasted_iota(I32, (rows, 3 * blk), 1)
    mask = (jnp.abs(qi - (col - blk)) <= blk) & (col >= lo) & (col < hi)
    rowhead = lax.broadcasted_iota(I32, (rows, 1), 0) // blk
    lane_q = lax.broadcasted_iota(I32, (blk, kw), 1)
    kloc = jnp.concatenate([kp_ref[:, 0:kw], ko_ref[:, 0:kw], kn_ref[:, 0:kw]], axis=0)
    vloc = jnp.concatenate([kp_ref[:, kw:2 * kw], ko_ref[:, kw:2 * kw], kn_ref[:, kw:2 * kw]], axis=0)
    kctx = kc_ref[:, 0:kw]
    vctx = kc_ref[:, kw:2 * kw]
    lane_l = lax.broadcasted_iota(I32, vloc.shape, 1)
    lane_c = lax.broadcasted_iota(I32, vctx.shape, 1)
    one = jnp.ones((), BF16)
    outs = []
    for g in range(ATTN_KV_HEADS):
        own = (lane_q >= g * dh) & (lane_q < (g + 1) * dh)
        vl = jnp.where((lane_l >= g * dh) & (lane_l < (g + 1) * dh), vloc, one)
        vc = jnp.where((lane_c >= g * dh) & (lane_c < (g + 1) * dh), vctx, one)
        qg = jnp.concatenate([jnp.where(own, q_ref[:, b * kw:(b + 1) * kw], jnp.zeros((), BF16))
                              for b in range(rep)], axis=0)
        s_loc = jnp.where(mask, _dot_nt(qg, kloc), NEG)
        s_ctx = _dot_nt(qg, kctx)
        sink = jnp.zeros((rows, 1), F32)
        for r in range(rep):
            sink = jnp.where(rowhead == r, sink_ref[g * rep + r], sink)
        mx = jnp.maximum(jnp.maximum(jnp.max(s_loc, axis=1, keepdims=True),
                                     jnp.max(s_ctx, axis=1, keepdims=True)), sink)
        p_loc = jnp.exp((s_loc - mx).astype(BF16))
        p_ctx = jnp.exp((s_ctx - mx).astype(BF16))
        pv = _dot(p_loc, vl) + _dot(p_ctx, vc)
        outs.append(pv / (pltpu.roll(pv, dh, 1) + jnp.exp(sink - mx)))
    for b in range(rep):
        o_ref[:, b * kw:(b + 1) * kw] = jnp.where(lane_q < dh, outs[0][b * blk:(b + 1) * blk],
                                                  outs[1][b * blk:(b + 1) * blk])


def _attention(q, kv, sink, nc):
    n_all = q.shape[0]
    blk = ATTN_BLOCK
    n_blocks = n_all // blk
    cbk = nc // blk
    kvw = kv.shape[1]
    return pl.pallas_call(
        functools.partial(_attn_kernel, n_blocks=n_blocks, ctx_blocks=cbk),
        grid=(n_blocks,),
        in_specs=[pl.BlockSpec(memory_space=pltpu.SMEM),
                  pl.BlockSpec((blk, q.shape[1]), lambda c: (c, 0)),
                  pl.BlockSpec((blk, kvw), lambda c: (jnp.maximum(c - 1, 0), 0)),
                  pl.BlockSpec((blk, kvw), lambda c: (c, 0)),
                  pl.BlockSpec((blk, kvw), lambda c: (jnp.minimum(c + 1, n_blocks - 1), 0)),
                  pl.BlockSpec((nc, kvw), lambda c: (0, 0))],
        out_specs=pl.BlockSpec((blk, q.shape[1]), lambda c: (c, 0)),
        out_shape=jax.ShapeDtypeStruct((n_all, q.shape[1]), F32),
        compiler_params=_params("arbitrary"),
        name="window_attention",
    )(sink, q, kv, kv, kv, kv)


def _merge_kernel(x_ref, mod_ref, ys5_ref, u_ref, yf_ref, yb_ref, xs_ref, z_ref, att_ref, gates_ref,
                  s5d_ref, bglu_ref, ssdd_ref, ssdg_ref, n2g_ref, wglu_ref, wbr_ref, wout_ref, wrh_ref, wrl_ref,
                  xo_ref, h2_ref, aff_ref, br_ref):
    m = mod_ref[0]
    d = x_ref.shape[1]
    a = jax.nn.gelu(ys5_ref[...].astype(F32) + s5d_ref[...] * u_ref[...])
    ya = (a * _sigmoid(_dot(a.astype(BF16), wglu_ref[...]) + bglu_ref[...])).astype(BF16)
    z = z_ref[...]
    yz = (yf_ref[...] + yb_ref[...] + ssdd_ref[...] * xs_ref[...]) * (z * _sigmoid(z))
    yb = (_rms(yz) * ssdg_ref[...]).astype(BF16)
    yc = att_ref[...].astype(BF16)
    cw = 256
    for j in range(d // cw):
        cs = slice(j * cw, (j + 1) * cw)
        br = (_sigmoid(gates_ref[:, j * cw:(j + 1) * cw].astype(F32)) * _dot(ya, wbr_ref[0, :, cs])
              + _sigmoid(gates_ref[:, d + j * cw:d + (j + 1) * cw].astype(F32)) * _dot(yb, wbr_ref[1, :, cs])
              + _sigmoid(gates_ref[:, 2 * d + j * cw:2 * d + (j + 1) * cw].astype(F32)) * _dot(yc, wbr_ref[2, :, cs]))
        br_ref[:, cs] = br.astype(BF16)
    xn = x_ref[...] + m[2:3] * _dot(br_ref[...], wout_ref[...])
    xo_ref[...] = xn
    h2 = _rms(xn) * n2g_ref[...] * (1.0 + m[4:5]) + m[3:4]
    hi = h2.astype(BF16)
    h2_ref[...] = hi
    lo = (h2 - hi.astype(F32)).astype(BF16)
    logits = (_dot(hi, wrh_ref[...]) + _dot(hi, wrl_ref[...]) + _dot(lo, wrh_ref[...]))[:, 0:N_EXPERTS]
    e = jnp.exp(logits - jnp.max(logits, axis=1, keepdims=True))
    aff_ref[...] = e / jnp.sum(e, axis=1, keepdims=True)


def _merge(xall, mod, ys5, u, yf, yb, t, z, att, gates, vecs, wglu, wbr, wout, wrh, wrl):
    n_all, d = xall.shape
    tt = TOK_TILE
    row = lambda i: (i, 0)
    const2 = lambda i: (0, 0)
    bw = 512
    s5d, bglu, ssdd, ssdg, n2g = vecs
    return pl.pallas_call(
        _merge_kernel,
        grid=(n_all // tt,),
        in_specs=[pl.BlockSpec((tt, d), row),
                  pl.BlockSpec((1, 6, d), lambda i: (jnp.minimum(i, 1), 0, 0)),
                  pl.BlockSpec((tt, bw), row), pl.BlockSpec((tt, bw), row), pl.BlockSpec((tt, bw), row),
                  pl.BlockSpec((tt, bw), row), pl.BlockSpec((tt, bw), row), pl.BlockSpec((tt, bw), row),
                  pl.BlockSpec((tt, bw), row), pl.BlockSpec((tt, 3 * d), row),
                  pl.BlockSpec((1, bw), const2), pl.BlockSpec((1, bw), const2), pl.BlockSpec((1, bw), const2),
                  pl.BlockSpec((1, bw), const2), pl.BlockSpec((1, d), const2),
                  pl.BlockSpec((bw, bw), const2),
                  pl.BlockSpec((3, bw, d), lambda i: (0, 0, 0)),
                  pl.BlockSpec((d, d), const2),
                  pl.BlockSpec((d, LANE), const2), pl.BlockSpec((d, LANE), const2)],
        out_specs=[pl.BlockSpec((tt, d), row),
                   pl.BlockSpec((tt, d), lambda i: (jnp.where(i == 0, n_all // tt - 1, i - 1), 0)),
                   pl.BlockSpec((tt, N_EXPERTS), row)],
        out_shape=[jax.ShapeDtypeStruct((n_all, d), F32), jax.ShapeDtypeStruct((n_all, d), BF16),
                   jax.ShapeDtypeStruct((n_all, N_EXPERTS), F32)],
        scratch_shapes=[pltpu.VMEM((tt, d), BF16)],
        compiler_params=_params("arbitrary"),
        name="merge_router",
    )(xall, mod, ys5, u, yf, yb, t, z, att, gates, s5d, bglu, ssdd, ssdg, n2g, wglu, wbr, wout, wrh, wrl)


def _route_kernel(a_ref, g_ref, pos_ref, cum_ref, *, cap):
    n = a_ref.shape[1]
    e = a_ref.shape[0]
    bits = lax.bitcast_convert_type(a_ref[...], I32)
    capf = float(cap)

    def search(i, thr):
        cand = thr | (1 << (30 - i))
        cnt = jnp.sum((bits >= cand).astype(F32), axis=1, keepdims=True)
        return jnp.where(cnt >= capf, cand, thr)

    thr = lax.fori_loop(0, 31, search, jnp.zeros((e, 1), I32))
    need = capf - jnp.sum((bits > thr).astype(F32), axis=1, keepdims=True)
    ii = lax.broadcasted_iota(I32, (LANE, LANE), 0)
    jj = lax.broadcasted_iota(I32, (LANE, LANE), 1)
    upper = (ii < jj).astype(BF16)

    def block(b, carry):
        eq_off, pos_off = carry
        sl = pl.ds(pl.multiple_of(b * LANE, LANE), LANE)
        a = a_ref[:, sl]
        v = lax.bitcast_convert_type(a, I32)
        eq = v == thr
        eqf = eq.astype(BF16)
        rank = _dot(eqf, upper) + eq_off
        sel = (v > thr) | (eq & (rank < need))
        self_ = sel.astype(BF16)
        pos = _dot(self_, upper) + pos_off
        g_ref[:, sl] = jnp.where(sel, a, 0.0)
        pos_ref[:, sl] = jnp.where(sel, pos, -1.0).astype(I32)
        cum_ref[:, sl] = pos.astype(I32)
        return (eq_off + jnp.sum(eqf.astype(F32), axis=1, keepdims=True),
                pos_off + jnp.sum(self_.astype(F32), axis=1, keepdims=True))

    zero = jnp.zeros((e, 1), F32)
    lax.fori_loop(0, n // LANE, block, (zero, zero))


def _route(aff_t, cap):
    e, n = aff_t.shape
    return pl.pallas_call(
        functools.partial(_route_kernel, cap=cap),
        out_shape=[jax.ShapeDtypeStruct((e, n), F32), jax.ShapeDtypeStruct((e, n), I32),
                   jax.ShapeDtypeStruct((e, n), I32)],
        name="ec_route",
    )(aff_t)


MOE_BLOCK = 1024
MOE_SUB = 256
MOE_WIN = 64
COMB_WIN = 128
COMB_MAX_ROUNDS = 3


def _moe_ffn_kernel(offs_ref, h_ref, pos_ref, wg_ref, wu_ref, wd_ref, ye_ref, xe_ref, wgb, wub, wdb,
                    *, cap, n_blocks, n_sub, rchunk):
    e = pl.program_id(0)
    b = pl.program_id(1)
    subs = h_ref.shape[0] // MOE_SUB

    @pl.when(b == 0)
    def _():
        xe_ref[...] = jnp.zeros_like(xe_ref)
        slab = 256

        def cast(i, carry):
            rows = pl.ds(pl.multiple_of(i * slab, slab), slab)
            wgb[rows, :] = wg_ref[0, 0, rows, :].astype(BF16)
            wub[rows, :] = wu_ref[0, 0, rows, :].astype(BF16)
            wdb[rows, :] = wd_ref[0, 0, rows, :].astype(BF16)
            return carry

        lax.fori_loop(0, wgb.shape[0] // slab, cast, 0)

    def window(s, a, m):
        pos = pos_ref[pl.ds(e, 1), s * MOE_SUB:(s + 1) * MOE_SUB]
        r0 = pl.multiple_of(a + m * MOE_WIN, 16)
        slot = lax.broadcasted_iota(I32, (MOE_WIN, MOE_SUB), 0) + r0
        sel = jnp.where(slot == pos, 1.0, 0.0).astype(BF16)
        rows = pl.ds(r0, MOE_WIN)
        xe_ref[rows, :] = (xe_ref[rows, :].astype(F32)
                           + _dot(sel, h_ref[s * MOE_SUB:(s + 1) * MOE_SUB, :])).astype(BF16)

    starts, counts = [], []
    for s in range(subs):
        js = b * subs + s
        o = offs_ref[e * (n_sub + 1) + js]
        o2 = offs_ref[e * (n_sub + 1) + js + 1]
        a = (o // 16) * 16
        starts.append(a)
        counts.append(jnp.where(o2 > o, (o2 - a + MOE_WIN - 1) // MOE_WIN, 0))
        window(s, a, 0)
    for s in range(subs):
        lax.fori_loop(1, counts[s], lambda m, c, s=s: (window(s, starts[s], m), c)[1], 0)

    @pl.when(b == n_blocks - 1)
    def _():
        def chunk(ci, carry):
            rows = pl.ds(pl.multiple_of(ci * rchunk, rchunk), rchunk)
            xb = xe_ref[rows, :]
            hg = _dot(xb, wgb[...])
            hid = (hg * _sigmoid(hg) * _dot(xb, wub[...])).astype(BF16)
            ye_ref[0, rows, :] = _dot(hid, wdb[...]).astype(BF16)
            return carry

        lax.fori_loop(0, cap // rchunk, chunk, 0)
        ye_ref[0, cap:, :] = jnp.zeros((ye_ref.shape[1] - cap, ye_ref.shape[2]), BF16)


def _moe_ffn(h2, pos_t, offs, wg, wu, wd, layer, row0, n, cap):
    d = h2.shape[1]
    _, e, _, f = wg.shape
    tb = min(MOE_BLOCK, n)
    n_blocks = n // tb
    blk0 = row0 // tb
    n_sub = n // MOE_SUB
    rchunk = min(cap, 256)
    ye_rows = cap + COMB_MAX_ROUNDS * COMB_WIN
    wspec = lambda r, c: pl.BlockSpec((1, 1, r, c), lambda ei, b, offs: (layer, ei, 0, 0))
    grid_spec = pltpu.PrefetchScalarGridSpec(
        num_scalar_prefetch=1,
        grid=(e, n_blocks),
        in_specs=[pl.BlockSpec((tb, d), lambda ei, b, offs: (b + blk0, 0)),
                  pl.BlockSpec((e, tb), lambda ei, b, offs: (0, b)),
                  wspec(d, f), wspec(d, f), wspec(f, d)],
        out_specs=pl.BlockSpec((1, ye_rows, d), lambda ei, b, offs: (ei, 0, 0)),
        scratch_shapes=[pltpu.VMEM((cap + MOE_WIN, d), BF16), pltpu.VMEM((d, f), BF16),
                        pltpu.VMEM((d, f), BF16), pltpu.VMEM((f, d), BF16)],
    )
    return pl.pallas_call(
        functools.partial(_moe_ffn_kernel, cap=cap, n_blocks=n_blocks, n_sub=n_sub, rchunk=rchunk),
        grid_spec=grid_spec,
        out_shape=jax.ShapeDtypeStruct((e, ye_rows, d), BF16),
        compiler_params=pltpu.CompilerParams(dimension_semantics=("arbitrary", "arbitrary"),
                                             vmem_limit_bytes=56 * 1024 * 1024),
        name="moe_ffn",
    )(offs, h2, pos_t, wg, wu, wd)


def _moe_window_copy(ye_hbm, buf, sem, slot, e, start):
    return pltpu.make_async_copy(ye_hbm.at[e, pl.ds(start, COMB_WIN), :], buf.at[slot, e], sem.at[slot, e])


def _moe_combine_kernel(offs_ref, rounds_ref, x_ref, g_ref, pos_ref, mod_ref, ye_hbm, o_ref,
                        buf, lhs_hi, lhs_lo, acc_ref, sem, *, n_tiles):
    j = pl.program_id(0)
    t = x_ref.shape[0]
    n_exp = g_ref.shape[1]
    lane = lax.broadcasted_iota(I32, (t, COMB_WIN), 1)

    def starts_of(tile, rnd):
        return [pl.multiple_of((offs_ref[e * (n_tiles + 1) + tile] // 16) * 16 + rnd * COMB_WIN, 16)
                for e in range(n_exp)]

    def fetch(slot, starts):
        for e in range(n_exp):
            _moe_window_copy(ye_hbm, buf, sem, slot, e, starts[e]).start()

    def land(slot, starts):
        for e in range(n_exp):
            _moe_window_copy(ye_hbm, buf, sem, slot, e, starts[e]).wait()

    def expand(slot, starts):
        for e in range(n_exp):
            val = jnp.where(pos_ref[:, e:e + 1] == lane + starts[e], g_ref[:, e:e + 1], 0.0)
            hi = val.astype(BF16)
            lhs_hi[:, e * COMB_WIN:(e + 1) * COMB_WIN] = hi
            lhs_lo[:, e * COMB_WIN:(e + 1) * COMB_WIN] = (val - hi.astype(F32)).astype(BF16)
        win = buf[slot].reshape(n_exp * COMB_WIN, buf.shape[3])
        return _dot(lhs_hi[...], win) + _dot(lhs_lo[...], win)

    cur = j % 2
    first = starts_of(j, 0)

    @pl.when(j == 0)
    def _():
        fetch(0, first)

    @pl.when(j + 1 < n_tiles)
    def _():
        fetch(1 - cur, starts_of(j + 1, 0))

    land(cur, first)
    acc_ref[...] = expand(cur, first)

    def more(rnd, carry):
        starts = starts_of(j, rnd)
        fetch(2, starts)
        land(2, starts)
        acc_ref[...] += expand(2, starts)
        return carry

    lax.fori_loop(1, rounds_ref[j], more, 0)
    o_ref[...] = x_ref[...] + mod_ref[0, 5:6, :] * acc_ref[...]


def _moe_combine(xall, g, pos, offs, rounds, mod, ye, tile0, n, mod_row):
    t = MOE_SUB
    d = xall.shape[1]
    n_exp = g.shape[1]
    n_tiles = n // t
    grid_spec = pltpu.PrefetchScalarGridSpec(
        num_scalar_prefetch=2,
        grid=(n_tiles,),
        in_specs=[pl.BlockSpec((t, d), lambda j, offs, rounds: (j + tile0, 0)),
                  pl.BlockSpec((t, n_exp), lambda j, offs, rounds: (j, 0)),
                  pl.BlockSpec((t, n_exp), lambda j, offs, rounds: (j, 0)),
                  pl.BlockSpec((1, 6, d), lambda j, offs, rounds: (mod_row, 0, 0)),
                  pl.BlockSpec(memory_space=pl.ANY)],
        out_specs=pl.BlockSpec((t, d), lambda j, offs, rounds: (j + tile0, 0)),
        scratch_shapes=[pltpu.VMEM((3, n_exp, COMB_WIN, d), BF16),
                        pltpu.VMEM((t, n_exp * COMB_WIN), BF16), pltpu.VMEM((t, n_exp * COMB_WIN), BF16),
                        pltpu.VMEM((t, d), F32), pltpu.SemaphoreType.DMA((3, n_exp))],
    )
    return pl.pallas_call(
        functools.partial(_moe_combine_kernel, n_tiles=n_tiles),
        grid_spec=grid_spec,
        out_shape=jax.ShapeDtypeStruct(xall.shape, F32),
        input_output_aliases={2: 0},
        compiler_params=_params("arbitrary"),
        name="moe_combine",
    )(offs, rounds, xall, g, pos, mod, ye)


def _expert_choice(xall, h2, aff, mod, wg, wu, wd, layer, row0, h_row0, n, mod_row):
    cap = EC_CAPACITY_FACTOR * n // N_EXPERTS
    g_t, pos_t, cum_t = _route(aff[row0:row0 + n].T, cap)
    offs = jnp.concatenate([cum_t[:, ::MOE_SUB], jnp.full((N_EXPERTS, 1), cap, I32)], axis=1)
    span = offs[:, 1:] - (offs[:, :-1] // 16) * 16
    rounds = jnp.maximum(jnp.max((span + COMB_WIN - 1) // COMB_WIN, axis=0), 1).astype(I32)
    offs = offs.reshape(-1)
    ye = _moe_ffn(h2, pos_t, offs, wg, wu, wd, layer, h_row0, n, cap)
    return _moe_combine(xall, g_t.T, pos_t.T, offs, rounds, mod, ye, row0 // MOE_SUB, n, mod_row)


def _final_kernel(x_ref, g_ref, o_ref):
    o_ref[...] = _rms(x_ref[...]) * g_ref[...]


def _final_norm(xall, g, tile0, n):
    t = TOK_TILE
    d = xall.shape[1]
    return pl.pallas_call(
        _final_kernel,
        grid=(n // t,),
        in_specs=[pl.BlockSpec((t, d), lambda i: (i + tile0, 0)), pl.BlockSpec((1, d), lambda i: (0, 0))],
        out_specs=pl.BlockSpec((t, d), lambda i: (i, 0)),
        out_shape=jax.ShapeDtypeStruct((n, d), F32),
        compiler_params=_params("arbitrary"),
        name="final_norm",
    )(xall, g)


def kernel(x, c, ctx, c_ctx, w_mod, b_mod, norm1_g, norm2_g, w_in, s5_lam_re, s5_lam_im, s5_log_dt, s5_b_re, s5_b_im, s5_c_re, s5_c_im, s5_d, s5_w_glu, s5_b_glu, ssd_conv_w, ssd_conv_b, ssd_a_log, ssd_dt_bias, ssd_d, ssd_norm_g, attn_sink, w_branch, w_out, w_router, w_e_gate, w_e_up, w_e_down, final_norm_g):
    batch, n, d = x.shape
    nc = ctx.shape[1]
    depth = w_mod.shape[0]
    assert batch == 1 and nc == TOK_TILE and n % TOK_TILE == 0 and n % GRID_W == 0
    assert SSD_STATE == SSD_CHUNK
    xall = jnp.concatenate([ctx[0], x[0]], axis=0)
    cvecs = jnp.zeros((8, d), F32).at[0].set(c_ctx).at[1].set(c[0])
    mods = _modulation(cvecs, w_mod, b_mod)
    cos, sin = _rope_tables(n, nc)
    row = lambda v: v.reshape(1, -1).astype(F32)
    for i in range(depth):
        mod = mods[i, 0:2].reshape(2, 6, d)
        u, ub, z, xbc, q, kv, gates, dtr = _inproj(xall, mod, row(norm1_g[i]), _prep_w_in(w_in[i]), cos, sin)
        ys5 = _s5_mix(ub, nc, _s5_prep(s5_lam_re[i], s5_lam_im[i], s5_log_dt[i], s5_b_re[i], s5_b_im[i],
                                      s5_c_re[i], s5_c_im[i]))
        t = _ssd_conv(xbc, ssd_conv_w[i], ssd_conv_b[i])
        a = -jnp.exp(ssd_a_log[i].astype(F32))
        dtrt = dtr[:, 0:2 * SSD_HEADS].T
        yf = _ssd_scan(t, dtr, dtrt, a[0], ssd_dt_bias[i, 0], nc, 0)
        yb = _ssd_scan(t, dtr, dtrt, a[1], ssd_dt_bias[i, 1], nc, 1)
        att = _attention(q, kv, attn_sink[i].astype(F32), nc)
        vecs = (row(s5_d[i]), row(s5_b_glu[i]), row(jnp.repeat(ssd_d[i], SSD_HEAD_DIM)), row(ssd_norm_g[i]),
                row(norm2_g[i]))
        wr = jnp.pad(w_router[i].astype(F32), ((0, 0), (0, LANE - N_EXPERTS)))
        wrh = wr.astype(BF16)
        wrl = (wr - wrh.astype(F32)).astype(BF16)
        wbr = w_branch[i].at[2].set(w_branch[i][2][_attn_head_order()]).astype(BF16)
        xall, h2, aff = _merge(xall, mod, ys5, u, yf, yb, t, z, att, gates, vecs, s5_w_glu[i].astype(BF16),
                               wbr, w_out[i].astype(BF16), wrh, wrl)
        xall = _expert_choice(xall, h2, aff, mod, w_e_gate, w_e_up, w_e_down, i, nc, 0, n, 1)
        if i < depth - 1:
            xall = _expert_choice(xall, h2, aff, mod, w_e_gate, w_e_up, w_e_down, i, 0, n, nc, 0)
    return _final_norm(xall, row(final_norm_g), nc // TOK_TILE, n)[None]
```

```python
import functools
import math

import jax
import jax.numpy as jnp
from jax import lax
from jax.experimental import pallas as pl
from jax.experimental.pallas import tpu as pltpu

F32 = jnp.float32
BF16 = jnp.bfloat16
I32 = jnp.int32
HI = lax.Precision.HIGHEST

D_MODEL = 1024
DEPTH = 4
GRID_W = 64
EPS = 1e-6
S5_GROUPS = 32
S5_GROUP_CH = 16
S5_STATE = 64
S5_CHUNK = 32
SSD_HEADS = 8
SSD_HEAD_DIM = 64
SSD_GROUPS = 2
SSD_STATE = 128
SSD_CHUNK = 128
ATTN_HEADS = 8
ATTN_KV_HEADS = 2
ATTN_HEAD_DIM = 64
ATTN_BLOCK = 128
ROPE_BASE = 10000.0
N_EXPERTS = 16
EXPERT_FF = 1024
EC_CAPACITY_FACTOR = 2
IN_SIZES = (512, 512, 1024, 16, 512, 128, 128, 3072)
TOK_TILE = 256
LANE = 128
NEG = -1e30


def _dot(a, b):
    return jnp.dot(a, b, preferred_element_type=F32)


def _dot_hi(a, b):
    return jnp.dot(a, b, precision=HI, preferred_element_type=F32)


def _dot_nt(a, b):
    return lax.dot_general(a, b, (((1,), (1,)), ((), ())), preferred_element_type=F32)


def _sigmoid(x):
    return 1.0 / (1.0 + jnp.exp(-x))


def _softplus(x):
    return jnp.maximum(x, 0.0) + jnp.log(1.0 + jnp.exp(-jnp.abs(x)))


def _rms(x):
    return x * lax.rsqrt(jnp.mean(x * x, axis=-1, keepdims=True) + EPS)


def _params(*sem):
    return pltpu.CompilerParams(dimension_semantics=sem)


def _mod_kernel(c_ref, w_ref, b_ref, o_ref):
    c = c_ref[...]
    o_ref[0] = _dot_hi(c * _sigmoid(c), w_ref[0]) + b_ref[0]


def _modulation(cvecs, w_mod, b_mod):
    depth, d, d6 = w_mod.shape
    bn = 1536
    return pl.pallas_call(
        _mod_kernel,
        grid=(depth, d6 // bn),
        in_specs=[pl.BlockSpec((8, d), lambda l, j: (0, 0)),
                  pl.BlockSpec((1, d, bn), lambda l, j: (l, 0, j)),
                  pl.BlockSpec((1, 1, bn), lambda l, j: (l, 0, j))],
        out_specs=pl.BlockSpec((1, 8, bn), lambda l, j: (l, 0, j)),
        out_shape=jax.ShapeDtypeStruct((depth, 8, d6), F32),
        compiler_params=_params("arbitrary", "arbitrary"),
        name="modulation",
    )(cvecs, w_mod, b_mod.reshape(depth, 1, d6))


W_IN_COLS = 6016


def _attn_head_order():
    rep = ATTN_HEADS // ATTN_KV_HEADS
    heads = [g * rep + b for b in range(rep) for g in range(ATTN_KV_HEADS)]
    return jnp.concatenate([jnp.arange(h * ATTN_HEAD_DIM, (h + 1) * ATTN_HEAD_DIM) for h in heads])


def _prep_w_in(w):
    parts, start = [], 0
    for s in IN_SIZES:
        parts.append(w[:, start:start + s])
        start += s
    u, z, xbc, dt, q, k, v, gates = parts
    dt = jnp.pad(dt, ((0, 0), (0, LANE - dt.shape[1])))
    q = q[:, _attn_head_order()]
    return jnp.concatenate([u, z, xbc, q, k, v, gates, dt], axis=1).astype(BF16)


def _inproj_kernel(x_ref, mod_ref, g_ref, w_ref, cos_ref, sin_ref,
                   u_ref, ub_ref, z_ref, xbc_ref, q_ref, kv_ref, gates_ref, dt_ref):
    m = mod_ref[0]
    h = (_rms(x_ref[...]) * g_ref[...] * (1.0 + m[1:2]) + m[0:1]).astype(BF16)

    def proj(a, b):
        return _dot(h, w_ref[:, a:b])

    u = proj(0, 512)
    u_ref[...] = u
    ub_ref[...] = u.astype(BF16)
    z_ref[...] = proj(512, 1024)
    xbc_ref[...] = proj(1024, 2048)
    cos = cos_ref[...]
    sin = sin_ref[...]
    lane = lax.broadcasted_iota(I32, cos.shape, 1)
    first = (lane % 32) < 16

    def rope(v):
        partner = jnp.where(first, pltpu.roll(v, LANE - 16, 1), pltpu.roll(v, 16, 1))
        return v * cos + partner * sin

    scale = ATTN_HEAD_DIM ** -0.5
    for j in range(4):
        q_ref[:, j * LANE:(j + 1) * LANE] = (rope(proj(2048 + j * LANE, 2048 + (j + 1) * LANE)) * scale).astype(BF16)
    kv_ref[:, 0:LANE] = rope(proj(2560, 2688)).astype(BF16)
    kv_ref[:, LANE:2 * LANE] = proj(2688, 2816).astype(BF16)
    for j in range(6):
        gates_ref[:, j * 512:(j + 1) * 512] = proj(2816 + j * 512, 2816 + (j + 1) * 512).astype(BF16)
    dt_ref[...] = proj(5888, 6016)


def _inproj(xall, mod, g, w, cos, sin):
    n_all, d = xall.shape
    t = TOK_TILE
    row = lambda i: (i, 0)
    const = lambda i: (0, 0)
    widths = (512, 512, 512, 1024, 512, 256, 3072, LANE)
    dtypes = (F32, BF16, F32, F32, BF16, BF16, BF16, F32)
    return pl.pallas_call(
        _inproj_kernel,
        grid=(n_all // t,),
        in_specs=[pl.BlockSpec((t, d), row),
                  pl.BlockSpec((1, 6, d), lambda i: (jnp.minimum(i, 1), 0, 0)),
                  pl.BlockSpec((1, d), const),
                  pl.BlockSpec((d, W_IN_COLS), const, pipeline_mode=pl.Buffered(1)),
                  pl.BlockSpec((t, LANE), row),
                  pl.BlockSpec((t, LANE), row)],
        out_specs=[pl.BlockSpec((t, wd), row) for wd in widths],
        out_shape=[jax.ShapeDtypeStruct((n_all, wd), dt) for wd, dt in zip(widths, dtypes)],
        compiler_params=_params("arbitrary"),
        name="inproj",
    )(xall, mod, g, w, cos, sin)


def _rope_tables(n, nc):
    rows = n // GRID_W
    r = jnp.repeat(jnp.arange(rows, dtype=F32), GRID_W)
    c = jnp.tile(jnp.arange(GRID_W, dtype=F32), rows)
    m = ATTN_HEAD_DIM // 4
    inv_freq = ROPE_BASE ** (-jnp.arange(m, dtype=F32) / m)
    ang_r, ang_c = r[:, None] * inv_freq, c[:, None] * inv_freq
    cos = jnp.concatenate([jnp.cos(ang_r), jnp.cos(ang_r), jnp.cos(ang_c), jnp.cos(ang_c)], axis=1)
    sin = jnp.concatenate([-jnp.sin(ang_r), jnp.sin(ang_r), -jnp.sin(ang_c), jnp.sin(ang_c)], axis=1)
    cos = jnp.concatenate([jnp.ones((nc, 64), F32), cos], axis=0)
    sin = jnp.concatenate([jnp.zeros((nc, 64), F32), sin], axis=0)
    return jnp.tile(cos, (1, 2)), jnp.tile(sin, (1, 2))


def _s5_toeplitz_kernel(bb_ref, ca_ref, kt_ref):
    L, K = S5_CHUNK, S5_GROUP_CH
    tab = _dot_hi(bb_ref[0], ca_ref[0])
    for s_ in range(L):
        lo = (L - 1 - s_) * K
        kt_ref[0, s_ * K:(s_ + 1) * K, :] = tab[:, lo:lo + L * K].astype(BF16)


def _s5_prep(lam_re, lam_im, log_dt, b_re, b_im, c_re, c_im):
    L, G, P, K = S5_CHUNK, S5_GROUPS, S5_STATE, S5_GROUP_CH
    lk = L * K
    lr, li = lam_re.astype(F32), lam_im.astype(F32)
    dt = jnp.exp(log_dt.astype(F32))[..., None]
    mag = jnp.exp(lr * dt)
    abr, abi = mag * jnp.cos(li * dt), mag * jnp.sin(li * dt)
    den = lr * lr + li * li
    fr = ((abr - 1.0) * lr + abi * li) / den
    fi = (abi * lr - (abr - 1.0) * li) / den
    bbr = fr[..., None] * b_re - fi[..., None] * b_im
    bbi = fr[..., None] * b_im + fi[..., None] * b_re
    tau = jnp.arange(L + 1, dtype=F32)[:, None, None, None]
    pm = jnp.exp(lr * dt * tau)
    apr, api = pm * jnp.cos(li * dt * tau), pm * jnp.sin(li * dt * tau)
    cr, ci = c_re.astype(F32), c_im.astype(F32)
    car = cr * apr[:, :, :, None, :] - ci * api[:, :, :, None, :]
    cai = cr * api[:, :, :, None, :] + ci * apr[:, :, :, None, :]

    lagmat = lambda v, d: v[:L, d].transpose(1, 3, 0, 2)
    zeros = jnp.zeros((G, P, L - 1, K), F32)
    fwd = lambda v: jnp.concatenate([zeros, lagmat(v, 0)], axis=2)
    bwd = lambda v: jnp.concatenate([jnp.flip(lagmat(v, 1), axis=2), zeros], axis=2)
    ca = jnp.concatenate([fwd(car), fwd(cai), bwd(car), bwd(cai)], axis=1).reshape(G, 4 * P, (2 * L - 1) * K)
    ca = jnp.pad(ca, ((0, 0), (0, 0), (0, K)))
    tr = lambda v: v.transpose(0, 2, 1)
    bb = jnp.concatenate([tr(bbr[0]), -tr(bbi[0]), tr(bbr[1]), -tr(bbi[1])], axis=2)
    kt = pl.pallas_call(
        _s5_toeplitz_kernel,
        grid=(G,),
        in_specs=[pl.BlockSpec((1, K, 4 * P), lambda g: (g, 0, 0)),
                  pl.BlockSpec((1, 4 * P, 2 * lk), lambda g: (g, 0, 0))],
        out_specs=pl.BlockSpec((1, lk, lk), lambda g: (g, 0, 0)),
        out_shape=jax.ShapeDtypeStruct((G, lk, lk), BF16),
        compiler_params=_params("arbitrary"),
        name="s5_toeplitz",
    )(bb, ca)

    def summary(d, idx):
        ar, ai = apr[idx, d][..., None], api[idx, d][..., None]
        re = ar * bbr[d][None] - ai * bbi[d][None]
        im = ar * bbi[d][None] + ai * bbr[d][None]
        f = lambda v: v.transpose(1, 0, 3, 2).reshape(G, lk, P)
        return f(re), f(im)

    sfr, sfi = summary(0, L - 1 - jnp.arange(L))
    sbr, sbi = summary(1, jnp.arange(L))
    sb = jnp.concatenate([sfr, sfi, sbr, sbi], axis=2).astype(BF16)

    def readout(v):
        return v.transpose(1, 3, 0, 2).reshape(G, P, lk)

    idx_b = L - jnp.arange(L)
    rc = jnp.concatenate([readout(car[1:L + 1, 0]), -readout(cai[1:L + 1, 0]),
                          readout(car[idx_b, 1]), -readout(cai[idx_b, 1])], axis=1).astype(BF16)
    coef = jnp.stack([apr[L, 0].reshape(-1), api[L, 0].reshape(-1),
                      apr[L, 1].reshape(-1), api[L, 1].reshape(-1)], axis=0)
    coef = jnp.pad(coef, ((0, 4), (0, 0)))
    return kt, sb, rc, coef


def _s5_states_kernel(u_ref, sb_ref, fre, fim, bre, bim):
    p = S5_STATE
    s0 = _dot(u_ref[0], sb_ref[0])
    s1 = _dot(u_ref[1], sb_ref[1])
    for q, ref in enumerate((fre, fim, bre, bim)):
        ref[...] = jnp.concatenate([s0[:, q * p:(q + 1) * p], s1[:, q * p:(q + 1) * p]], axis=1)


def _s5_rec_kernel(coef_ref, sfr, sfi, sbr, sbi, hfr, hfi, hbr, hbi, *, n_chunks, ctx_chunks):
    arf, aif = coef_ref[0:1, :], coef_ref[1:2, :]
    arb, aib = coef_ref[2:3, :], coef_ref[3:4, :]
    zero = jnp.zeros_like(arf)

    def fstep(c, carry):
        hr, hi = carry
        hfr[pl.ds(c, 1), :] = hr
        hfi[pl.ds(c, 1), :] = hi
        return (arf * hr - aif * hi + sfr[pl.ds(c, 1), :], arf * hi + aif * hr + sfi[pl.ds(c, 1), :])

    lax.fori_loop(0, n_chunks, fstep, (zero, zero))

    def bstep(i, carry):
        c = jnp.where(i < ctx_chunks, ctx_chunks - 1 - i, n_chunks - 1 - (i - ctx_chunks))
        hr, hi = carry
        hbr[pl.ds(c, 1), :] = hr
        hbi[pl.ds(c, 1), :] = hi
        return (arb * hr - aib * hi + sbr[pl.ds(c, 1), :], arb * hi + aib * hr + sbi[pl.ds(c, 1), :])

    lax.fori_loop(0, n_chunks, bstep, (zero, zero))


def _s5_out_kernel(u_ref, kt_ref, hfr, hfi, hbr, hbi, rc_ref, y_ref):
    p = S5_STATE
    for i in range(2):
        h = jnp.concatenate([r[:, i * p:(i + 1) * p] for r in (hfr, hfi, hbr, hbi)], axis=1).astype(BF16)
        y_ref[i] = (_dot(u_ref[i], kt_ref[i]) + _dot(h, rc_ref[i])).astype(y_ref.dtype)


def _s5_mix(u, nc, prep):
    kt, sb, rc, coef = prep
    n_all = u.shape[0]
    L, G, K, P = S5_CHUNK, S5_GROUPS, S5_GROUP_CH, S5_STATE
    C = n_all // L
    lk = L * K
    ug = u.reshape(C, L, G, K).transpose(2, 0, 1, 3).reshape(G, C, lk)
    gp = G // 2
    st_shape = jax.ShapeDtypeStruct((C, G * P), F32)
    st_spec = pl.BlockSpec((C, 2 * P), lambda p: (0, p))
    states = pl.pallas_call(
        _s5_states_kernel,
        grid=(gp,),
        in_specs=[pl.BlockSpec((2, C, lk), lambda p: (p, 0, 0)),
                  pl.BlockSpec((2, lk, 4 * P), lambda p: (p, 0, 0))],
        out_specs=[st_spec] * 4,
        out_shape=[st_shape] * 4,
        compiler_params=_params("arbitrary"),
        name="s5_states",
    )(ug, sb)
    cb = 512
    col = pl.BlockSpec((C, cb), lambda j: (0, j))
    hs = pl.pallas_call(
        functools.partial(_s5_rec_kernel, n_chunks=C, ctx_chunks=nc // L),
        grid=(G * P // cb,),
        in_specs=[pl.BlockSpec((8, cb), lambda j: (0, j))] + [col] * 4,
        out_specs=[col] * 4,
        out_shape=[st_shape] * 4,
        compiler_params=_params("arbitrary"),
        name="s5_recurrence",
    )(coef, *states)
    y = pl.pallas_call(
        _s5_out_kernel,
        grid=(gp,),
        in_specs=[pl.BlockSpec((2, C, lk), lambda p: (p, 0, 0)),
                  pl.BlockSpec((2, lk, lk), lambda p: (p, 0, 0))] + [st_spec] * 4
                 + [pl.BlockSpec((2, 4 * P, lk), lambda p: (p, 0, 0))],
        out_specs=pl.BlockSpec((2, C, lk), lambda p: (p, 0, 0)),
        out_shape=jax.ShapeDtypeStruct((G, C, lk), BF16),
        compiler_params=_params("arbitrary"),
        name="s5_out",
    )(ug, kt, *hs, rc)
    return y.reshape(G, C, L, K).transpose(1, 2, 0, 3).reshape(n_all, G * K)


def _conv_kernel(cur_ref, prev_ref, next_ref, w_ref, b_ref, o_ref, *, n_tiles):
    i = pl.program_id(0)
    cur = cur_ref[...]
    t = cur.shape[0]
    pv = prev_ref[...] * jnp.where(i >= 2, 1.0, 0.0)
    nx = next_ref[...] * jnp.where((i >= 1) & (i <= n_tiles - 2), 1.0, 0.0)
    row8 = lax.broadcasted_iota(I32, pv.shape, 0)
    acc = b_ref[...] + w_ref[2:3, :] * cur
    for s in (1, 2):
        r = pltpu.roll(cur, s, 0)
        head = jnp.where(row8 < s, pltpu.roll(pv, s, 0), r[0:8])
        acc = acc + w_ref[2 - s:3 - s, :] * jnp.concatenate([head, r[8:]], axis=0)
        r = pltpu.roll(cur, t - s, 0)
        tail = jnp.where(row8 >= 8 - s, pltpu.roll(nx, 8 - s, 0), r[t - 8:])
        acc = acc + w_ref[2 + s:3 + s, :] * jnp.concatenate([r[:t - 8], tail], axis=0)
    o_ref[...] = acc * _sigmoid(acc)


def _ssd_conv(xbc, conv_w, conv_b):
    n_all, ch = xbc.shape
    t = TOK_TILE
    n_tiles = n_all // t
    per = t // 8
    return pl.pallas_call(
        functools.partial(_conv_kernel, n_tiles=n_tiles),
        grid=(n_tiles,),
        in_specs=[pl.BlockSpec((t, ch), lambda i: (i, 0)),
                  pl.BlockSpec((8, ch), lambda i: (jnp.maximum(i * per - 1, 0), 0)),
                  pl.BlockSpec((8, ch), lambda i: (jnp.minimum((i + 1) * per, n_tiles * per - 1), 0)),
                  pl.BlockSpec((8, ch), lambda i: (0, 0)),
                  pl.BlockSpec((1, ch), lambda i: (0, 0))],
        out_specs=pl.BlockSpec((t, ch), lambda i: (i, 0)),
        out_shape=jax.ShapeDtypeStruct((n_all, ch), F32),
        compiler_params=_params("arbitrary"),
        name="ssd_conv",
    )(xbc, xbc, xbc, jnp.pad(conv_w, ((0, 3), (0, 0))), conv_b.reshape(1, ch))


def _ssd_kernel(t_ref, dtr_ref, dtrt_ref, prow_ref, pcol_ref, y_ref, h_ref, *, reverse, d):
    q = SSD_CHUNK
    hpg = SSD_HEADS // SSD_GROUPS
    p = SSD_HEAD_DIM

    @pl.when(pl.program_id(0) == 0)
    def _():
        h_ref[...] = jnp.zeros_like(h_ref)

    lo = d * SSD_HEADS
    dt_c = _softplus(dtr_ref[:, lo:lo + SSD_HEADS] + prow_ref[1:2, 0:SSD_HEADS])
    adt_c = dt_c * prow_ref[0:1, 0:SSD_HEADS]
    dt_r = _softplus(dtrt_ref[lo:lo + SSD_HEADS, :] + pcol_ref[:, 1:2])
    adt_r = dt_r * pcol_ref[:, 0:1]
    ii = lax.broadcasted_iota(I32, (q, q), 0)
    jj = lax.broadcasted_iota(I32, (q, q), 1)
    causal = (jj >= ii) if reverse else (jj <= ii)
    acum_c = _dot_hi(causal.astype(F32), adt_c)
    acum_r = _dot_hi(adt_r, ((ii >= jj) if reverse else (ii <= jj)).astype(F32))
    tot = acum_c[0:1, :] if reverse else acum_c[q - 1:q, :]
    gw = hpg * p
    lane_head = lax.broadcasted_iota(I32, (q, gw), 1) // p
    lane_head_row = lax.broadcasted_iota(I32, (1, gw), 1) // p
    for g in range(SSD_GROUPS):
        bg = t_ref[:, 512 + g * SSD_STATE:512 + (g + 1) * SSD_STATE]
        cg = t_ref[:, 768 + g * SSD_STATE:768 + (g + 1) * SSD_STATE]
        cb = _dot_nt(cg.astype(BF16), bg.astype(BF16))
        bgt = bg.T
        xg = t_ref[:, g * gw:(g + 1) * gw].astype(BF16)
        hg = h_ref[g]
        xh = jnp.concatenate([xg, hg.astype(BF16)], axis=0)
        yg = jnp.zeros((q, gw), F32)
        sg = jnp.zeros((SSD_STATE, gw), F32)
        dec = jnp.zeros((1, gw), F32)
        for r in range(hpg):
            hd = g * hpg + r
            ac = jnp.broadcast_to(acum_c[:, hd:hd + 1], (q, q))
            ar = acum_r[hd:hd + 1, :]
            dtrow = dt_r[hd:hd + 1, :]
            th = tot[:, hd:hd + 1]
            wts = (cb * jnp.exp(jnp.where(causal, ac - ar, NEG)) * dtrow).astype(BF16)
            cs = (cg * jnp.exp(ac)).astype(BF16)
            bs = (bgt * (jnp.exp(th - ar) * dtrow)).astype(BF16)
            yh = _dot(jnp.concatenate([wts, cs], axis=1), xh)
            sh = _dot(bs, xg)
            yg = jnp.where(lane_head == r, yh, yg)
            sg = jnp.where(lane_head[:SSD_STATE] == r, sh, sg)
            dec = jnp.where(lane_head_row == r, jnp.exp(th), dec)
        y_ref[:, g * gw:(g + 1) * gw] = yg
        h_ref[g] = hg * dec + sg


def _ssd_scan(t, dtr, dtrt, a, bias, nc, d):
    n_all = t.shape[0]
    q = SSD_CHUNK
    n_chunks = n_all // q
    cc = nc // q
    reverse = d == 1
    if reverse:
        order = lambda i: jnp.where(i < cc, cc - 1 - i, n_chunks - 1 - (i - cc))
    else:
        order = lambda i: i
    prow = jnp.zeros((8, LANE), F32).at[0, :SSD_HEADS].set(a).at[1, :SSD_HEADS].set(bias)
    pcol = jnp.zeros((SSD_HEADS, LANE), F32).at[:, 0].set(a).at[:, 1].set(bias)
    return pl.pallas_call(
        functools.partial(_ssd_kernel, reverse=reverse, d=d),
        grid=(n_chunks,),
        in_specs=[pl.BlockSpec((q, t.shape[1]), lambda i: (order(i), 0)),
                  pl.BlockSpec((q, LANE), lambda i: (order(i), 0)),
                  pl.BlockSpec((2 * SSD_HEADS, q), lambda i: (0, order(i))),
                  pl.BlockSpec((8, LANE), lambda i: (0, 0)),
                  pl.BlockSpec((SSD_HEADS, LANE), lambda i: (0, 0))],
        out_specs=pl.BlockSpec((q, SSD_HEADS * SSD_HEAD_DIM), lambda i: (order(i), 0)),
        out_shape=jax.ShapeDtypeStruct((n_all, SSD_HEADS * SSD_HEAD_DIM), F32),
        scratch_shapes=[pltpu.VMEM((SSD_GROUPS, SSD_STATE, (SSD_HEADS // SSD_GROUPS) * SSD_HEAD_DIM), F32)],
        compiler_params=_params("arbitrary"),
        name="ssd_scan_bwd" if reverse else "ssd_scan_fwd",
    )(t, dtr, dtrt, prow, pcol)


def _attn_kernel(sink_ref, q_ref, kp_ref, ko_ref, kn_ref, kc_ref, o_ref, *, n_blocks, ctx_blocks):
    c = pl.program_id(0)
    blk = ATTN_BLOCK
    dh = ATTN_HEAD_DIM
    rep = ATTN_HEADS // ATTN_KV_HEADS
    rows = rep * blk
    kw = ATTN_KV_HEADS * dh
    lat = c >= ctx_blocks
    lo = jnp.where(lat & (c > ctx_blocks), 0, blk)
    hi = jnp.where(lat, jnp.where(c < n_blocks - 1, 3 * blk, 2 * blk), 0)
    qi = lax.broadcasted_iota(I32, (rows, 3 * blk), 0) % blk
    col = lax.broadcasted_iota(I32, (rows, 3 * blk), 1)
    mask = (jnp.abs(qi - (col - blk)) <= blk) & (col >= lo) & (col < hi)
    rowhead = lax.broadcasted_iota(I32, (rows, 1), 0) // blk
    lane_q = lax.broadcasted_iota(I32, (blk, kw), 1)
    kloc = jnp.concatenate([kp_ref[:, 0:kw], ko_ref[:, 0:kw], kn_ref[:, 0:kw]], axis=0)
    vloc = jnp.concatenate([kp_ref[:, kw:2 * kw], ko_ref[:, kw:2 * kw], kn_ref[:, kw:2 * kw]], axis=0)
    kctx = kc_ref[:, 0:kw]
    vctx = kc_ref[:, kw:2 * kw]
    lane_l = lax.broadcasted_iota(I32, vloc.shape, 1)
    lane_c = lax.broadcasted_iota(I32, vctx.shape, 1)
    one = jnp.ones((), BF16)
    outs = []
    for g in range(ATTN_KV_HEADS):
        own = (lane_q >= g * dh) & (lane_q < (g + 1) * dh)
        vl = jnp.where((lane_l >= g * dh) & (lane_l < (g + 1) * dh), vloc, one)
        vc = jnp.where((lane_c >= g * dh) & (lane_c < (g + 1) * dh), vctx, one)
        qg = jnp.concatenate([jnp.where(own, q_ref[:, b * kw:(b + 1) * kw], jnp.zeros((), BF16))
                              for b in range(rep)], axis=0)
        s_loc = jnp.where(mask, _dot_nt(qg, kloc), NEG)
        s_ctx = _dot_nt(qg, kctx)
        sink = jnp.zeros((rows, 1), F32)
        for r in range(rep):
            sink = jnp.where(rowhead == r, sink_ref[g * rep + r], sink)
        mx = jnp.maximum(jnp.maximum(jnp.max(s_loc, axis=1, keepdims=True),
                                     jnp.max(s_ctx, axis=1, keepdims=True)), sink)
        p_loc = jnp.exp((s_loc - mx).astype(BF16))
        p_ctx = jnp.exp((s_ctx - mx).astype(BF16))
        pv = _dot(p_loc, vl) + _dot(p_ctx, vc)
        outs.append(pv / (pltpu.roll(pv, dh, 1) + jnp.exp(sink - mx)))
    for b in range(rep):
        o_ref[:, b * kw:(b + 1) * kw] = jnp.where(lane_q < dh, outs[0][b * blk:(b + 1) * blk],
                                                  outs[1][b * blk:(b + 1) * blk])


def _attention(q, kv, sink, nc):
    n_all = q.shape[0]
    blk = ATTN_BLOCK
    n_blocks = n_all // blk
    cbk = nc // blk
    kvw = kv.shape[1]
    return pl.pallas_call(
        functools.partial(_attn_kernel, n_blocks=n_blocks, ctx_blocks=cbk),
        grid=(n_blocks,),
        in_specs=[pl.BlockSpec(memory_space=pltpu.SMEM),
                  pl.BlockSpec((blk, q.shape[1]), lambda c: (c, 0)),
                  pl.BlockSpec((blk, kvw), lambda c: (jnp.maximum(c - 1, 0), 0)),
                  pl.BlockSpec((blk, kvw), lambda c: (c, 0)),
                  pl.BlockSpec((blk, kvw), lambda c: (jnp.minimum(c + 1, n_blocks - 1), 0)),
                  pl.BlockSpec((nc, kvw), lambda c: (0, 0))],
        out_specs=pl.BlockSpec((blk, q.shape[1]), lambda c: (c, 0)),
        out_shape=jax.ShapeDtypeStruct((n_all, q.shape[1]), F32),
        compiler_params=_params("arbitrary"),
        name="window_attention",
    )(sink, q, kv, kv, kv, kv)


def _merge_kernel(x_ref, mod_ref, ys5_ref, u_ref, yf_ref, yb_ref, xs_ref, z_ref, att_ref, gates_ref,
                  s5d_ref, bglu_ref, ssdd_ref, ssdg_ref, n2g_ref, wglu_ref, wbr_ref, wout_ref, wrh_ref, wrl_ref,
                  xo_ref, h2_ref, aff_ref, br_ref):
    m = mod_ref[0]
    d = x_ref.shape[1]
    a = jax.nn.gelu(ys5_ref[...].astype(F32) + s5d_ref[...] * u_ref[...])
    ya = (a * _sigmoid(_dot(a.astype(BF16), wglu_ref[...]) + bglu_ref[...])).astype(BF16)
    z = z_ref[...]
    yz = (yf_ref[...] + yb_ref[...] + ssdd_ref[...] * xs_ref[...]) * (z * _sigmoid(z))
    yb = (_rms(yz) * ssdg_ref[...]).astype(BF16)
    yc = att_ref[...].astype(BF16)
    cw = 256
    for j in range(d // cw):
        cs = slice(j * cw, (j + 1) * cw)
        br = (_sigmoid(gates_ref[:, j * cw:(j + 1) * cw].astype(F32)) * _dot(ya, wbr_ref[0, :, cs])
              + _sigmoid(gates_ref[:, d + j * cw:d + (j + 1) * cw].astype(F32)) * _dot(yb, wbr_ref[1, :, cs])
              + _sigmoid(gates_ref[:, 2 * d + j * cw:2 * d + (j + 1) * cw].astype(F32)) * _dot(yc, wbr_ref[2, :, cs]))
        br_ref[:, cs] = br.astype(BF16)
    xn = x_ref[...] + m[2:3] * _dot(br_ref[...], wout_ref[...])
    xo_ref[...] = xn
    h2 = _rms(xn) * n2g_ref[...] * (1.0 + m[4:5]) + m[3:4]
    hi = h2.astype(BF16)
    h2_ref[...] = hi
    lo = (h2 - hi.astype(F32)).astype(BF16)
    logits = (_dot(hi, wrh_ref[...]) + _dot(hi, wrl_ref[...]) + _dot(lo, wrh_ref[...]))[:, 0:N_EXPERTS]
    e = jnp.exp(logits - jnp.max(logits, axis=1, keepdims=True))
    aff_ref[...] = e / jnp.sum(e, axis=1, keepdims=True)


def _merge(xall, mod, ys5, u, yf, yb, t, z, att, gates, vecs, wglu, wbr, wout, wrh, wrl):
    n_all, d = xall.shape
    tt = TOK_TILE
    row = lambda i: (i, 0)
    const2 = lambda i: (0, 0)
    bw = 512
    s5d, bglu, ssdd, ssdg, n2g = vecs
    return pl.pallas_call(
        _merge_kernel,
        grid=(n_all // tt,),
        in_specs=[pl.BlockSpec((tt, d), row),
                  pl.BlockSpec((1, 6, d), lambda i: (jnp.minimum(i, 1), 0, 0)),
                  pl.BlockSpec((tt, bw), row), pl.BlockSpec((tt, bw), row), pl.BlockSpec((tt, bw), row),
                  pl.BlockSpec((tt, bw), row), pl.BlockSpec((tt, bw), row), pl.BlockSpec((tt, bw), row),
                  pl.BlockSpec((tt, bw), row), pl.BlockSpec((tt, 3 * d), row),
                  pl.BlockSpec((1, bw), const2), pl.BlockSpec((1, bw), const2), pl.BlockSpec((1, bw), const2),
                  pl.BlockSpec((1, bw), const2), pl.BlockSpec((1, d), const2),
                  pl.BlockSpec((bw, bw), const2),
                  pl.BlockSpec((3, bw, d), lambda i: (0, 0, 0)),
                  pl.BlockSpec((d, d), const2),
                  pl.BlockSpec((d, LANE), const2), pl.BlockSpec((d, LANE), const2)],
        out_specs=[pl.BlockSpec((tt, d), row),
                   pl.BlockSpec((tt, d), lambda i: (jnp.where(i == 0, n_all // tt - 1, i - 1), 0)),
                   pl.BlockSpec((tt, N_EXPERTS), row)],
        out_shape=[jax.ShapeDtypeStruct((n_all, d), F32), jax.ShapeDtypeStruct((n_all, d), BF16),
                   jax.ShapeDtypeStruct((n_all, N_EXPERTS), F32)],
        scratch_shapes=[pltpu.VMEM((tt, d), BF16)],
        compiler_params=_params("arbitrary"),
        name="merge_router",
    )(xall, mod, ys5, u, yf, yb, t, z, att, gates, s5d, bglu, ssdd, ssdg, n2g, wglu, wbr, wout, wrh, wrl)


def _route_kernel(a_ref, g_ref, pos_ref, cum_ref, *, cap):
    n = a_ref.shape[1]
    e = a_ref.shape[0]
    bits = lax.bitcast_convert_type(a_ref[...], I32)
    capf = float(cap)

    def search(i, thr):
        cand = thr | (1 << (30 - i))
        cnt = jnp.sum((bits >= cand).astype(F32), axis=1, keepdims=True)
        return jnp.where(cnt >= capf, cand, thr)

    thr = lax.fori_loop(0, 31, search, jnp.zeros((e, 1), I32))
    need = capf - jnp.sum((bits > thr).astype(F32), axis=1, keepdims=True)
    ii = lax.broadcasted_iota(I32, (LANE, LANE), 0)
    jj = lax.broadcasted_iota(I32, (LANE, LANE), 1)
    upper = (ii < jj).astype(BF16)

    def block(b, carry):
        eq_off, pos_off = carry
        sl = pl.ds(pl.multiple_of(b * LANE, LANE), LANE)
        a = a_ref[:, sl]
        v = lax.bitcast_convert_type(a, I32)
        eq = v == thr
        eqf = eq.astype(BF16)
        rank = _dot(eqf, upper) + eq_off
        sel = (v > thr) | (eq & (rank < need))
        self_ = sel.astype(BF16)
        pos = _dot(self_, upper) + pos_off
        g_ref[:, sl] = jnp.where(sel, a, 0.0)
        pos_ref[:, sl] = jnp.where(sel, pos, -1.0).astype(I32)
        cum_ref[:, sl] = pos.astype(I32)
        return (eq_off + jnp.sum(eqf.astype(F32), axis=1, keepdims=True),
                pos_off + jnp.sum(self_.astype(F32), axis=1, keepdims=True))

    zero = jnp.zeros((e, 1), F32)
    lax.fori_loop(0, n // LANE, block, (zero, zero))


def _route(aff_t, cap):
    e, n = aff_t.shape
    return pl.pallas_call(
        functools.partial(_route_kernel, cap=cap),
        out_shape=[jax.ShapeDtypeStruct((e, n), F32), jax.ShapeDtypeStruct((e, n), I32),
                   jax.ShapeDtypeStruct((e, n), I32)],
        name="ec_route",
    )(aff_t)


MOE_BLOCK = 1024
MOE_SUB = 256
MOE_WIN = 64
COMB_WIN = 128
COMB_MAX_ROUNDS = 3


MOE_PASS = 4


def _moe_gather_kernel(offs_ref, h_ref, pos_ref, xe_ref, *, n_sub):
    pss = pl.program_id(0)
    b = pl.program_id(1)
    subs = h_ref.shape[0] // MOE_SUB

    @pl.when(b == 0)
    def _():
        xe_ref[...] = jnp.zeros_like(xe_ref)

    def window(k, s, a, m):
        e = pss * MOE_PASS + k
        pos = pos_ref[pl.ds(e, 1), s * MOE_SUB:(s + 1) * MOE_SUB]
        r0 = pl.multiple_of(a + m * MOE_WIN, 16)
        slot = lax.broadcasted_iota(I32, (MOE_WIN, MOE_SUB), 0) + r0
        sel = jnp.where(slot == pos, 1.0, 0.0).astype(BF16)
        rows = pl.ds(r0, MOE_WIN)
        xe_ref[k, rows, :] = (xe_ref[k, rows, :].astype(F32)
                              + _dot(sel, h_ref[s * MOE_SUB:(s + 1) * MOE_SUB, :])).astype(BF16)

    extra = []
    for k in range(MOE_PASS):
        for s in range(subs):
            base = (pss * MOE_PASS + k) * (n_sub + 1) + b * subs + s
            o = offs_ref[base]
            o2 = offs_ref[base + 1]
            a = (o // 16) * 16
            extra.append((k, s, a, jnp.where(o2 > o, (o2 - a + MOE_WIN - 1) // MOE_WIN, 0)))
            window(k, s, a, 0)
    for k, s, a, cnt in extra:
        lax.fori_loop(1, cnt, lambda m, c, k=k, s=s, a=a: (window(k, s, a, m), c)[1], 0)


def _moe_gather(h2, pos_t, offs, h_row0, n, cap):
    d = h2.shape[1]
    e = pos_t.shape[0]
    tb = min(MOE_BLOCK, n)
    n_blocks = n // tb
    blk0 = h_row0 // tb
    grid_spec = pltpu.PrefetchScalarGridSpec(
        num_scalar_prefetch=1,
        grid=(e // MOE_PASS, n_blocks),
        in_specs=[pl.BlockSpec((tb, d), lambda p, b, offs: (b + blk0, 0)),
                  pl.BlockSpec((e, tb), lambda p, b, offs: (0, b))],
        out_specs=pl.BlockSpec((MOE_PASS, cap + MOE_WIN, d), lambda p, b, offs: (p, 0, 0)),
    )
    return pl.pallas_call(
        functools.partial(_moe_gather_kernel, n_sub=n // MOE_SUB),
        grid_spec=grid_spec,
        out_shape=jax.ShapeDtypeStruct((e, cap + MOE_WIN, d), BF16),
        compiler_params=pltpu.CompilerParams(dimension_semantics=("arbitrary", "arbitrary"),
                                             vmem_limit_bytes=56 * 1024 * 1024),
        name="moe_gather",
    )(offs, h2, pos_t)


def _moe_ffn_kernel(xe_ref, wg_ref, wu_ref, wd_ref, ye_ref, wgb, wub, wdb, *, cap, rchunk):
    slab = 256

    def cast(i, carry):
        rows = pl.ds(pl.multiple_of(i * slab, slab), slab)
        wgb[rows, :] = wg_ref[0, 0, rows, :].astype(BF16)
        wub[rows, :] = wu_ref[0, 0, rows, :].astype(BF16)
        wdb[rows, :] = wd_ref[0, 0, rows, :].astype(BF16)
        return carry

    lax.fori_loop(0, wgb.shape[0] // slab, cast, 0)

    def chunk(ci, carry):
        rows = pl.ds(pl.multiple_of(ci * rchunk, rchunk), rchunk)
        xb = xe_ref[0, rows, :]
        hg = _dot(xb, wgb[...])
        hid = (hg * _sigmoid(hg) * _dot(xb, wub[...])).astype(BF16)
        ye_ref[0, rows, :] = _dot(hid, wdb[...]).astype(BF16)
        return carry

    lax.fori_loop(0, cap // rchunk, chunk, 0)
    ye_ref[0, cap:, :] = jnp.zeros((ye_ref.shape[1] - cap, ye_ref.shape[2]), BF16)


def _moe_ffn(xe, wg, wu, wd, layer, cap):
    e, xrows, d = xe.shape
    f = wg.shape[3]
    assert d == f
    rchunk = min(cap, 256)
    ye_rows = cap + COMB_MAX_ROUNDS * COMB_WIN
    wspec = lambda r, c: pl.BlockSpec((1, 1, r, c), lambda ei: (layer, ei, 0, 0))
    return pl.pallas_call(
        functools.partial(_moe_ffn_kernel, cap=cap, rchunk=rchunk),
        grid=(e,),
        in_specs=[pl.BlockSpec((1, xrows, d), lambda ei: (ei, 0, 0)), wspec(d, f), wspec(d, f), wspec(f, d)],
        out_specs=pl.BlockSpec((1, ye_rows, d), lambda ei: (ei, 0, 0)),
        out_shape=jax.ShapeDtypeStruct((e, ye_rows, d), BF16),
        scratch_shapes=[pltpu.VMEM((d, f), BF16), pltpu.VMEM((d, f), BF16), pltpu.VMEM((f, d), BF16)],
        compiler_params=pltpu.CompilerParams(dimension_semantics=("arbitrary",),
                                             vmem_limit_bytes=56 * 1024 * 1024),
        name="moe_ffn",
    )(xe, wg, wu, wd)


def _moe_window_copy(ye_hbm, buf, sem, slot, e, start):
    return pltpu.make_async_copy(ye_hbm.at[e, pl.ds(start, COMB_WIN), :], buf.at[slot, e], sem.at[slot, e])


def _moe_combine_kernel(offs_ref, rounds_ref, x_ref, g_ref, pos_ref, mod_ref, ye_hbm, o_ref,
                        buf, lhs, acc_ref, sem, *, n_tiles):
    j = pl.program_id(0)
    t = x_ref.shape[0]
    n_exp = g_ref.shape[1]
    lane = lax.broadcasted_iota(I32, (t, COMB_WIN), 1)

    def starts_of(tile, rnd):
        return [pl.multiple_of((offs_ref[e * (n_tiles + 1) + tile] // 16) * 16 + rnd * COMB_WIN, 16)
                for e in range(n_exp)]

    def fetch(slot, starts):
        for e in range(n_exp):
            _moe_window_copy(ye_hbm, buf, sem, slot, e, starts[e]).start()

    def land(slot, starts):
        for e in range(n_exp):
            _moe_window_copy(ye_hbm, buf, sem, slot, e, starts[e]).wait()

    def expand(slot, starts):
        for e in range(n_exp):
            val = jnp.where(pos_ref[:, e:e + 1] == lane + starts[e], g_ref[:, e:e + 1], 0.0)
            lhs[:, e * COMB_WIN:(e + 1) * COMB_WIN] = val.astype(BF16)
        return _dot(lhs[...], buf[slot].reshape(n_exp * COMB_WIN, buf.shape[3]))

    cur = j % 2
    first = starts_of(j, 0)

    @pl.when(j == 0)
    def _():
        fetch(0, first)

    @pl.when(j + 1 < n_tiles)
    def _():
        fetch(1 - cur, starts_of(j + 1, 0))

    land(cur, first)
    acc_ref[...] = expand(cur, first)

    def more(rnd, carry):
        starts = starts_of(j, rnd)
        fetch(2, starts)
        land(2, starts)
        acc_ref[...] += expand(2, starts)
        return carry

    lax.fori_loop(1, rounds_ref[j], more, 0)
    o_ref[...] = x_ref[...] + mod_ref[0, 5:6, :] * acc_ref[...]


def _moe_combine(xall, g, pos, offs, rounds, mod, ye, tile0, n, mod_row):
    t = MOE_SUB
    d = xall.shape[1]
    n_exp = g.shape[1]
    n_tiles = n // t
    grid_spec = pltpu.PrefetchScalarGridSpec(
        num_scalar_prefetch=2,
        grid=(n_tiles,),
        in_specs=[pl.BlockSpec((t, d), lambda j, offs, rounds: (j + tile0, 0)),
                  pl.BlockSpec((t, n_exp), lambda j, offs, rounds: (j, 0)),
                  pl.BlockSpec((t, n_exp), lambda j, offs, rounds: (j, 0)),
                  pl.BlockSpec((1, 6, d), lambda j, offs, rounds: (mod_row, 0, 0)),
                  pl.BlockSpec(memory_space=pl.ANY)],
        out_specs=pl.BlockSpec((t, d), lambda j, offs, rounds: (j + tile0, 0)),
        scratch_shapes=[pltpu.VMEM((3, n_exp, COMB_WIN, d), BF16),
                        pltpu.VMEM((t, n_exp * COMB_WIN), BF16),
                        pltpu.VMEM((t, d), F32), pltpu.SemaphoreType.DMA((3, n_exp))],
    )
    return pl.pallas_call(
        functools.partial(_moe_combine_kernel, n_tiles=n_tiles),
        grid_spec=grid_spec,
        out_shape=jax.ShapeDtypeStruct(xall.shape, F32),
        input_output_aliases={2: 0},
        compiler_params=_params("arbitrary"),
        name="moe_combine",
    )(offs, rounds, xall, g, pos, mod, ye)


def _expert_choice(xall, h2, aff, mod, wg, wu, wd, layer, row0, h_row0, n, mod_row):
    cap = EC_CAPACITY_FACTOR * n // N_EXPERTS
    g_t, pos_t, cum_t = _route(aff[row0:row0 + n].T, cap)
    offs = jnp.concatenate([cum_t[:, ::MOE_SUB], jnp.full((N_EXPERTS, 1), cap, I32)], axis=1)
    span = offs[:, 1:] - (offs[:, :-1] // 16) * 16
    rounds = jnp.maximum(jnp.max((span + COMB_WIN - 1) // COMB_WIN, axis=0), 1).astype(I32)
    offs = offs.reshape(-1)
    ye = _moe_ffn(_moe_gather(h2, pos_t, offs, h_row0, n, cap), wg, wu, wd, layer, cap)
    return _moe_combine(xall, g_t.T, pos_t.T, offs, rounds, mod, ye, row0 // MOE_SUB, n, mod_row)


def _final_kernel(x_ref, g_ref, o_ref):
    o_ref[...] = _rms(x_ref[...]) * g_ref[...]


def _final_norm(xall, g, tile0, n):
    t = TOK_TILE
    d = xall.shape[1]
    return pl.pallas_call(
        _final_kernel,
        grid=(n // t,),
        in_specs=[pl.BlockSpec((t, d), lambda i: (i + tile0, 0)), pl.BlockSpec((1, d), lambda i: (0, 0))],
        out_specs=pl.BlockSpec((t, d), lambda i: (i, 0)),
        out_shape=jax.ShapeDtypeStruct((n, d), F32),
        compiler_params=_params("arbitrary"),
        name="final_norm",
    )(xall, g)


def kernel(x, c, ctx, c_ctx, w_mod, b_mod, norm1_g, norm2_g, w_in, s5_lam_re, s5_lam_im, s5_log_dt, s5_b_re, s5_b_im, s5_c_re, s5_c_im, s5_d, s5_w_glu, s5_b_glu, ssd_conv_w, ssd_conv_b, ssd_a_log, ssd_dt_bias, ssd_d, ssd_norm_g, attn_sink, w_branch, w_out, w_router, w_e_gate, w_e_up, w_e_down, final_norm_g):
    batch, n, d = x.shape
    nc = ctx.shape[1]
    depth = w_mod.shape[0]
    assert batch == 1 and nc == TOK_TILE and n % TOK_TILE == 0 and n % GRID_W == 0
    assert SSD_STATE == SSD_CHUNK
    xall = jnp.concatenate([ctx[0], x[0]], axis=0)
    cvecs = jnp.zeros((8, d), F32).at[0].set(c_ctx).at[1].set(c[0])
    mods = _modulation(cvecs, w_mod, b_mod)
    cos, sin = _rope_tables(n, nc)
    row = lambda v: v.reshape(1, -1).astype(F32)
    for i in range(depth):
        mod = mods[i, 0:2].reshape(2, 6, d)
        u, ub, z, xbc, q, kv, gates, dtr = _inproj(xall, mod, row(norm1_g[i]), _prep_w_in(w_in[i]), cos, sin)
        ys5 = _s5_mix(ub, nc, _s5_prep(s5_lam_re[i], s5_lam_im[i], s5_log_dt[i], s5_b_re[i], s5_b_im[i],
                                      s5_c_re[i], s5_c_im[i]))
        t = _ssd_conv(xbc, ssd_conv_w[i], ssd_conv_b[i])
        a = -jnp.exp(ssd_a_log[i].astype(F32))
        dtrt = dtr[:, 0:2 * SSD_HEADS].T
        yf = _ssd_scan(t, dtr, dtrt, a[0], ssd_dt_bias[i, 0], nc, 0)
        yb = _ssd_scan(t, dtr, dtrt, a[1], ssd_dt_bias[i, 1], nc, 1)
        att = _attention(q, kv, attn_sink[i].astype(F32), nc)
        vecs = (row(s5_d[i]), row(s5_b_glu[i]), row(jnp.repeat(ssd_d[i], SSD_HEAD_DIM)), row(ssd_norm_g[i]),
                row(norm2_g[i]))
        wr = jnp.pad(w_router[i].astype(F32), ((0, 0), (0, LANE - N_EXPERTS)))
        wrh = wr.astype(BF16)
        wrl = (wr - wrh.astype(F32)).astype(BF16)
        wbr = w_branch[i].at[2].set(w_branch[i][2][_attn_head_order()]).astype(BF16)
        xall, h2, aff = _merge(xall, mod, ys5, u, yf, yb, t, z, att, gates, vecs, s5_w_glu[i].astype(BF16),
                               wbr, w_out[i].astype(BF16), wrh, wrl)
        xall = _expert_choice(xall, h2, aff, mod, w_e_gate, w_e_up, w_e_down, i, nc, 0, n, 1)
        if i < depth - 1:
            xall = _expert_choice(xall, h2, aff, mod, w_e_gate, w_e_up, w_e_down, i, 0, n, nc, 0)
    return _final_norm(xall, row(final_norm_g), nc // TOK_TILE, n)[None]
```

```python
import functools
import math

import jax
import jax.numpy as jnp
from jax import lax
from jax.experimental import pallas as pl
from jax.experimental.pallas import tpu as pltpu

F32 = jnp.float32
BF16 = jnp.bfloat16
I32 = jnp.int32
HI = lax.Precision.HIGHEST

D_MODEL = 1024
DEPTH = 4
GRID_W = 64
EPS = 1e-6
S5_GROUPS = 32
S5_GROUP_CH = 16
S5_STATE = 64
S5_CHUNK = 32
SSD_HEADS = 8
SSD_HEAD_DIM = 64
SSD_GROUPS = 2
SSD_STATE = 128
SSD_CHUNK = 128
ATTN_HEADS = 8
ATTN_KV_HEADS = 2
ATTN_HEAD_DIM = 64
ATTN_BLOCK = 128
ROPE_BASE = 10000.0
N_EXPERTS = 16
EXPERT_FF = 1024
EC_CAPACITY_FACTOR = 2
IN_SIZES = (512, 512, 1024, 16, 512, 128, 128, 3072)
TOK_TILE = 256
LANE = 128
NEG = -1e30


def _dot(a, b):
    return jnp.dot(a, b, preferred_element_type=F32)


def _dot_hi(a, b):
    return jnp.dot(a, b, precision=HI, preferred_element_type=F32)


def _dot_nt(a, b):
    return lax.dot_general(a, b, (((1,), (1,)), ((), ())), preferred_element_type=F32)


def _sigmoid(x):
    return 1.0 / (1.0 + jnp.exp(-x))


def _softplus(x):
    return jnp.maximum(x, 0.0) + jnp.log(1.0 + jnp.exp(-jnp.abs(x)))


def _rms(x):
    return x * lax.rsqrt(jnp.mean(x * x, axis=-1, keepdims=True) + EPS)


def _params(*sem):
    return pltpu.CompilerParams(dimension_semantics=sem)


def _mod_kernel(c_ref, w_ref, b_ref, o_ref):
    c = c_ref[...]
    o_ref[0] = _dot_hi(c * _sigmoid(c), w_ref[0]) + b_ref[0]


def _modulation(cvecs, w_mod, b_mod):
    depth, d, d6 = w_mod.shape
    bn = 1536
    return pl.pallas_call(
        _mod_kernel,
        grid=(depth, d6 // bn),
        in_specs=[pl.BlockSpec((8, d), lambda l, j: (0, 0)),
                  pl.BlockSpec((1, d, bn), lambda l, j: (l, 0, j)),
                  pl.BlockSpec((1, 1, bn), lambda l, j: (l, 0, j))],
        out_specs=pl.BlockSpec((1, 8, bn), lambda l, j: (l, 0, j)),
        out_shape=jax.ShapeDtypeStruct((depth, 8, d6), F32),
        compiler_params=_params("arbitrary", "arbitrary"),
        name="modulation",
    )(cvecs, w_mod, b_mod.reshape(depth, 1, d6))


W_IN_COLS = 6016


def _attn_head_order():
    rep = ATTN_HEADS // ATTN_KV_HEADS
    heads = [g * rep + b for b in range(rep) for g in range(ATTN_KV_HEADS)]
    return jnp.concatenate([jnp.arange(h * ATTN_HEAD_DIM, (h + 1) * ATTN_HEAD_DIM) for h in heads])


def _prep_w_in(w):
    parts, start = [], 0
    for s in IN_SIZES:
        parts.append(w[:, start:start + s])
        start += s
    u, z, xbc, dt, q, k, v, gates = parts
    dt = jnp.pad(dt, ((0, 0), (0, LANE - dt.shape[1])))
    q = q[:, _attn_head_order()]
    return jnp.concatenate([u, z, xbc, q, k, v, gates, dt], axis=1).astype(BF16)


def _inproj_kernel(x_ref, mod_ref, g_ref, w_ref, cos_ref, sin_ref,
                   u_ref, z_ref, xbc_ref, q_ref, kv_ref, gates_ref, dt_ref):
    m = mod_ref[0]
    h = (_rms(x_ref[...]) * g_ref[...] * (1.0 + m[1:2]) + m[0:1]).astype(BF16)

    def proj(a, b):
        return _dot(h, w_ref[:, a:b])

    u_ref[...] = proj(0, 512)
    z_ref[...] = proj(512, 1024)
    xbc_ref[...] = proj(1024, 2048)
    cos = cos_ref[...]
    sin = sin_ref[...]
    lane = lax.broadcasted_iota(I32, cos.shape, 1)
    first = (lane % 32) < 16

    def rope(v):
        partner = jnp.where(first, pltpu.roll(v, LANE - 16, 1), pltpu.roll(v, 16, 1))
        return v * cos + partner * sin

    scale = ATTN_HEAD_DIM ** -0.5
    for j in range(4):
        q_ref[:, j * LANE:(j + 1) * LANE] = (rope(proj(2048 + j * LANE, 2048 + (j + 1) * LANE)) * scale).astype(BF16)
    kv_ref[:, 0:LANE] = rope(proj(2560, 2688)).astype(BF16)
    kv_ref[:, LANE:2 * LANE] = proj(2688, 2816).astype(BF16)
    for j in range(6):
        gates_ref[:, j * 512:(j + 1) * 512] = proj(2816 + j * 512, 2816 + (j + 1) * 512).astype(BF16)
    dt_ref[...] = proj(5888, 6016)


def _inproj(xall, mod, g, w, cos, sin):
    n_all, d = xall.shape
    t = TOK_TILE
    row = lambda i: (i, 0)
    const = lambda i: (0, 0)
    widths = (512, 512, 1024, 512, 256, 3072, LANE)
    dtypes = (F32, F32, F32, BF16, BF16, BF16, F32)
    return pl.pallas_call(
        _inproj_kernel,
        grid=(n_all // t,),
        in_specs=[pl.BlockSpec((t, d), row),
                  pl.BlockSpec((1, 6, d), lambda i: (jnp.minimum(i, 1), 0, 0)),
                  pl.BlockSpec((1, d), const),
                  pl.BlockSpec((d, W_IN_COLS), const, pipeline_mode=pl.Buffered(1)),
                  pl.BlockSpec((t, LANE), row),
                  pl.BlockSpec((t, LANE), row)],
        out_specs=[pl.BlockSpec((t, wd), row) for wd in widths],
        out_shape=[jax.ShapeDtypeStruct((n_all, wd), dt) for wd, dt in zip(widths, dtypes)],
        compiler_params=_params("arbitrary"),
        name="inproj",
    )(xall, mod, g, w, cos, sin)


def _rope_tables(n, nc):
    rows = n // GRID_W
    r = jnp.repeat(jnp.arange(rows, dtype=F32), GRID_W)
    c = jnp.tile(jnp.arange(GRID_W, dtype=F32), rows)
    m = ATTN_HEAD_DIM // 4
    inv_freq = ROPE_BASE ** (-jnp.arange(m, dtype=F32) / m)
    ang_r, ang_c = r[:, None] * inv_freq, c[:, None] * inv_freq
    cos = jnp.concatenate([jnp.cos(ang_r), jnp.cos(ang_r), jnp.cos(ang_c), jnp.cos(ang_c)], axis=1)
    sin = jnp.concatenate([-jnp.sin(ang_r), jnp.sin(ang_r), -jnp.sin(ang_c), jnp.sin(ang_c)], axis=1)
    cos = jnp.concatenate([jnp.ones((nc, 64), F32), cos], axis=0)
    sin = jnp.concatenate([jnp.zeros((nc, 64), F32), sin], axis=0)
    return jnp.tile(cos, (1, 2)), jnp.tile(sin, (1, 2))


def _s5_toeplitz_kernel(bb_ref, ca_ref, kt_ref):
    L, K = S5_CHUNK, S5_GROUP_CH
    tab = _dot_hi(bb_ref[0], ca_ref[0])
    for s_ in range(L):
        lo = (L - 1 - s_) * K
        kt_ref[0, s_ * K:(s_ + 1) * K, :] = tab[:, lo:lo + L * K].astype(BF16)


def _s5_prep(lam_re, lam_im, log_dt, b_re, b_im, c_re, c_im):
    L, G, P, K = S5_CHUNK, S5_GROUPS, S5_STATE, S5_GROUP_CH
    lk = L * K
    lr, li = lam_re.astype(F32), lam_im.astype(F32)
    dt = jnp.exp(log_dt.astype(F32))[..., None]
    mag = jnp.exp(lr * dt)
    abr, abi = mag * jnp.cos(li * dt), mag * jnp.sin(li * dt)
    den = lr * lr + li * li
    fr = ((abr - 1.0) * lr + abi * li) / den
    fi = (abi * lr - (abr - 1.0) * li) / den
    bbr = fr[..., None] * b_re - fi[..., None] * b_im
    bbi = fr[..., None] * b_im + fi[..., None] * b_re
    tau = jnp.arange(L + 1, dtype=F32)[:, None, None, None]
    pm = jnp.exp(lr * dt * tau)
    apr, api = pm * jnp.cos(li * dt * tau), pm * jnp.sin(li * dt * tau)
    cr, ci = c_re.astype(F32), c_im.astype(F32)
    car = cr * apr[:, :, :, None, :] - ci * api[:, :, :, None, :]
    cai = cr * api[:, :, :, None, :] + ci * apr[:, :, :, None, :]

    lagmat = lambda v, d: v[:L, d].transpose(1, 3, 0, 2)
    zeros = jnp.zeros((G, P, L - 1, K), F32)
    fwd = lambda v: jnp.concatenate([zeros, lagmat(v, 0)], axis=2)
    bwd = lambda v: jnp.concatenate([jnp.flip(lagmat(v, 1), axis=2), zeros], axis=2)
    ca = jnp.concatenate([fwd(car), fwd(cai), bwd(car), bwd(cai)], axis=1).reshape(G, 4 * P, (2 * L - 1) * K)
    ca = jnp.pad(ca, ((0, 0), (0, 0), (0, K)))
    tr = lambda v: v.transpose(0, 2, 1)
    bb = jnp.concatenate([tr(bbr[0]), -tr(bbi[0]), tr(bbr[1]), -tr(bbi[1])], axis=2)
    kt = pl.pallas_call(
        _s5_toeplitz_kernel,
        grid=(G,),
        in_specs=[pl.BlockSpec((1, K, 4 * P), lambda g: (g, 0, 0)),
                  pl.BlockSpec((1, 4 * P, 2 * lk), lambda g: (g, 0, 0))],
        out_specs=pl.BlockSpec((1, lk, lk), lambda g: (g, 0, 0)),
        out_shape=jax.ShapeDtypeStruct((G, lk, lk), BF16),
        compiler_params=_params("arbitrary"),
        name="s5_toeplitz",
    )(bb, ca)

    def summary(d, idx):
        ar, ai = apr[idx, d][..., None], api[idx, d][..., None]
        re = ar * bbr[d][None] - ai * bbi[d][None]
        im = ar * bbi[d][None] + ai * bbr[d][None]
        f = lambda v: v.transpose(1, 0, 3, 2).reshape(G, lk, P)
        return f(re), f(im)

    sfr, sfi = summary(0, L - 1 - jnp.arange(L))
    sbr, sbi = summary(1, jnp.arange(L))
    sb = jnp.concatenate([sfr, sfi, sbr, sbi], axis=2).astype(BF16)

    def readout(v):
        return v.transpose(1, 3, 0, 2).reshape(G, P, lk)

    idx_b = L - jnp.arange(L)
    rc = jnp.concatenate([readout(car[1:L + 1, 0]), -readout(cai[1:L + 1, 0]),
                          readout(car[idx_b, 1]), -readout(cai[idx_b, 1])], axis=1).astype(BF16)
    coef = jnp.stack([apr[L, 0].reshape(-1), api[L, 0].reshape(-1),
                      apr[L, 1].reshape(-1), api[L, 1].reshape(-1)], axis=0)
    coef = jnp.pad(coef, ((0, 4), (0, 0)))
    return kt, sb, rc, coef


def _s5_states_kernel(u_ref, sb_ref, fre, fim, bre, bim):
    p = S5_STATE
    s0 = _dot(u_ref[0], sb_ref[0])
    s1 = _dot(u_ref[1], sb_ref[1])
    for q, ref in enumerate((fre, fim, bre, bim)):
        ref[...] = jnp.concatenate([s0[:, q * p:(q + 1) * p], s1[:, q * p:(q + 1) * p]], axis=1)


def _s5_rec_kernel(coef_ref, sfr, sfi, sbr, sbi, hfr, hfi, hbr, hbi, *, n_chunks, ctx_chunks):
    arf, aif = coef_ref[0:1, :], coef_ref[1:2, :]
    arb, aib = coef_ref[2:3, :], coef_ref[3:4, :]
    zero = jnp.zeros_like(arf)

    def fstep(c, carry):
        hr, hi = carry
        hfr[pl.ds(c, 1), :] = hr
        hfi[pl.ds(c, 1), :] = hi
        return (arf * hr - aif * hi + sfr[pl.ds(c, 1), :], arf * hi + aif * hr + sfi[pl.ds(c, 1), :])

    lax.fori_loop(0, n_chunks, fstep, (zero, zero))

    def bstep(i, carry):
        c = jnp.where(i < ctx_chunks, ctx_chunks - 1 - i, n_chunks - 1 - (i - ctx_chunks))
        hr, hi = carry
        hbr[pl.ds(c, 1), :] = hr
        hbi[pl.ds(c, 1), :] = hi
        return (arb * hr - aib * hi + sbr[pl.ds(c, 1), :], arb * hi + aib * hr + sbi[pl.ds(c, 1), :])

    lax.fori_loop(0, n_chunks, bstep, (zero, zero))


def _s5_out_kernel(u_ref, kt_ref, hfr, hfi, hbr, hbi, rc_ref, y_ref):
    p = S5_STATE
    for i in range(2):
        h = jnp.concatenate([r[:, i * p:(i + 1) * p] for r in (hfr, hfi, hbr, hbi)], axis=1).astype(BF16)
        y_ref[i] = (_dot(u_ref[i], kt_ref[i]) + _dot(h, rc_ref[i])).astype(y_ref.dtype)


S5_LANE_GROUPS = LANE // S5_GROUP_CH
S5_T_LO = 8


def _s5_perm():
    i = jnp.arange(S5_T_LO * LANE)
    t_lo, gl, k = i // LANE, (i % LANE) // S5_GROUP_CH, i % S5_GROUP_CH
    j = gl * (S5_T_LO * S5_GROUP_CH) + t_lo * S5_GROUP_CH + k
    return jnp.zeros((S5_T_LO * LANE, S5_T_LO * LANE), BF16).at[i, j].set(1.0)


def _s5_pack_kernel(u_ref, perm_ref, o_ref, *, n_chunks):
    L = S5_CHUNK
    for t_hi in range(L // S5_T_LO):
        z = jnp.concatenate([u_ref[pl.ds(t_hi * S5_T_LO + t_lo, n_chunks, stride=L), :].astype(BF16)
                             for t_lo in range(S5_T_LO)], axis=1)
        w = _dot(z, perm_ref[...]).astype(BF16)
        for gl in range(S5_LANE_GROUPS):
            o_ref[gl, :, t_hi * LANE:(t_hi + 1) * LANE] = w[:, gl * LANE:(gl + 1) * LANE]


def _s5_unpack_kernel(y_ref, perm_ref, o_ref, *, n_chunks):
    L = S5_CHUNK
    for t_hi in range(L // S5_T_LO):
        w = jnp.concatenate([y_ref[gl, :, t_hi * LANE:(t_hi + 1) * LANE] for gl in range(S5_LANE_GROUPS)], axis=1)
        z = _dot_nt(w, perm_ref[...])
        for t_lo in range(S5_T_LO):
            o_ref[pl.ds(t_hi * S5_T_LO + t_lo, n_chunks, stride=L), :] = z[:, t_lo * LANE:(t_lo + 1) * LANE]


def _s5_mix(u, nc, prep):
    kt, sb, rc, coef = prep
    n_all = u.shape[0]
    L, G, K, P = S5_CHUNK, S5_GROUPS, S5_GROUP_CH, S5_STATE
    C = n_all // L
    lk = L * K
    perm = _s5_perm()
    lg = S5_LANE_GROUPS
    perm_spec = pl.BlockSpec(perm.shape, lambda b: (0, 0))
    ug = pl.pallas_call(
        functools.partial(_s5_pack_kernel, n_chunks=C),
        grid=(G // lg,),
        in_specs=[pl.BlockSpec((n_all, LANE), lambda b: (0, b)), perm_spec],
        out_specs=pl.BlockSpec((lg, C, lk), lambda b: (b, 0, 0)),
        out_shape=jax.ShapeDtypeStruct((G, C, lk), BF16),
        compiler_params=_params("arbitrary"),
        name="s5_pack",
    )(u, perm)
    gp = G // 2
    st_shape = jax.ShapeDtypeStruct((C, G * P), F32)
    st_spec = pl.BlockSpec((C, 2 * P), lambda p: (0, p))
    states = pl.pallas_call(
        _s5_states_kernel,
        grid=(gp,),
        in_specs=[pl.BlockSpec((2, C, lk), lambda p: (p, 0, 0)),
                  pl.BlockSpec((2, lk, 4 * P), lambda p: (p, 0, 0))],
        out_specs=[st_spec] * 4,
        out_shape=[st_shape] * 4,
        compiler_params=_params("arbitrary"),
        name="s5_states",
    )(ug, sb)
    cb = 512
    col = pl.BlockSpec((C, cb), lambda j: (0, j))
    hs = pl.pallas_call(
        functools.partial(_s5_rec_kernel, n_chunks=C, ctx_chunks=nc // L),
        grid=(G * P // cb,),
        in_specs=[pl.BlockSpec((8, cb), lambda j: (0, j))] + [col] * 4,
        out_specs=[col] * 4,
        out_shape=[st_shape] * 4,
        compiler_params=_params("arbitrary"),
        name="s5_recurrence",
    )(coef, *states)
    y = pl.pallas_call(
        _s5_out_kernel,
        grid=(gp,),
        in_specs=[pl.BlockSpec((2, C, lk), lambda p: (p, 0, 0)),
                  pl.BlockSpec((2, lk, lk), lambda p: (p, 0, 0))] + [st_spec] * 4
                 + [pl.BlockSpec((2, 4 * P, lk), lambda p: (p, 0, 0))],
        out_specs=pl.BlockSpec((2, C, lk), lambda p: (p, 0, 0)),
        out_shape=jax.ShapeDtypeStruct((G, C, lk), BF16),
        compiler_params=_params("arbitrary"),
        name="s5_out",
    )(ug, kt, *hs, rc)
    return pl.pallas_call(
        functools.partial(_s5_unpack_kernel, n_chunks=C),
        grid=(G // lg,),
        in_specs=[pl.BlockSpec((lg, C, lk), lambda b: (b, 0, 0)), perm_spec],
        out_specs=pl.BlockSpec((n_all, LANE), lambda b: (0, b)),
        out_shape=jax.ShapeDtypeStruct((n_all, G * K), F32),
        compiler_params=_params("arbitrary"),
        name="s5_unpack",
    )(y, perm)


def _conv_kernel(cur_ref, prev_ref, next_ref, w_ref, b_ref, o_ref, *, n_tiles):
    i = pl.program_id(0)
    cur = cur_ref[...]
    t = cur.shape[0]
    pv = prev_ref[...] * jnp.where(i >= 2, 1.0, 0.0)
    nx = next_ref[...] * jnp.where((i >= 1) & (i <= n_tiles - 2), 1.0, 0.0)
    row8 = lax.broadcasted_iota(I32, pv.shape, 0)
    acc = b_ref[...] + w_ref[2:3, :] * cur
    for s in (1, 2):
        r = pltpu.roll(cur, s, 0)
        head = jnp.where(row8 < s, pltpu.roll(pv, s, 0), r[0:8])
        acc = acc + w_ref[2 - s:3 - s, :] * jnp.concatenate([head, r[8:]], axis=0)
        r = pltpu.roll(cur, t - s, 0)
        tail = jnp.where(row8 >= 8 - s, pltpu.roll(nx, 8 - s, 0), r[t - 8:])
        acc = acc + w_ref[2 + s:3 + s, :] * jnp.concatenate([r[:t - 8], tail], axis=0)
    o_ref[...] = acc * _sigmoid(acc)


def _ssd_conv(xbc, conv_w, conv_b):
    n_all, ch = xbc.shape
    t = TOK_TILE
    n_tiles = n_all // t
    per = t // 8
    return pl.pallas_call(
        functools.partial(_conv_kernel, n_tiles=n_tiles),
        grid=(n_tiles,),
        in_specs=[pl.BlockSpec((t, ch), lambda i: (i, 0)),
                  pl.BlockSpec((8, ch), lambda i: (jnp.maximum(i * per - 1, 0), 0)),
                  pl.BlockSpec((8, ch), lambda i: (jnp.minimum((i + 1) * per, n_tiles * per - 1), 0)),
                  pl.BlockSpec((8, ch), lambda i: (0, 0)),
                  pl.BlockSpec((1, ch), lambda i: (0, 0))],
        out_specs=pl.BlockSpec((t, ch), lambda i: (i, 0)),
        out_shape=jax.ShapeDtypeStruct((n_all, ch), F32),
        compiler_params=_params("arbitrary"),
        name="ssd_conv",
    )(xbc, xbc, xbc, jnp.pad(conv_w, ((0, 3), (0, 0))), conv_b.reshape(1, ch))


def _ssd_kernel(t_ref, dtr_ref, dtrt_ref, prow_ref, pcol_ref, y_ref, h_ref, *, reverse, d):
    q = SSD_CHUNK
    hpg = SSD_HEADS // SSD_GROUPS
    p = SSD_HEAD_DIM

    @pl.when(pl.program_id(0) == 0)
    def _():
        h_ref[...] = jnp.zeros_like(h_ref)

    lo = d * SSD_HEADS
    dt_c = _softplus(dtr_ref[:, lo:lo + SSD_HEADS] + prow_ref[1:2, 0:SSD_HEADS])
    adt_c = dt_c * prow_ref[0:1, 0:SSD_HEADS]
    dt_r = _softplus(dtrt_ref[lo:lo + SSD_HEADS, :] + pcol_ref[:, 1:2])
    adt_r = dt_r * pcol_ref[:, 0:1]
    ii = lax.broadcasted_iota(I32, (q, q), 0)
    jj = lax.broadcasted_iota(I32, (q, q), 1)
    causal = (jj >= ii) if reverse else (jj <= ii)
    acum_c = _dot_hi(causal.astype(F32), adt_c)
    acum_r = _dot_hi(adt_r, ((ii >= jj) if reverse else (ii <= jj)).astype(F32))
    tot = acum_c[0:1, :] if reverse else acum_c[q - 1:q, :]
    gw = hpg * p
    lane_head = lax.broadcasted_iota(I32, (q, gw), 1) // p
    lane_head_row = lax.broadcasted_iota(I32, (1, gw), 1) // p
    for g in range(SSD_GROUPS):
        bg = t_ref[:, 512 + g * SSD_STATE:512 + (g + 1) * SSD_STATE]
        cg = t_ref[:, 768 + g * SSD_STATE:768 + (g + 1) * SSD_STATE]
        cb = _dot_nt(cg.astype(BF16), bg.astype(BF16))
        bgt = bg.T
        xg = t_ref[:, g * gw:(g + 1) * gw].astype(BF16)
        hg = h_ref[g]
        xh = jnp.concatenate([xg, hg.astype(BF16)], axis=0)
        yg = jnp.zeros((q, gw), F32)
        sg = jnp.zeros((SSD_STATE, gw), F32)
        dec = jnp.zeros((1, gw), F32)
        for r in range(hpg):
            hd = g * hpg + r
            ac = jnp.broadcast_to(acum_c[:, hd:hd + 1], (q, q))
            ar = acum_r[hd:hd + 1, :]
            dtrow = dt_r[hd:hd + 1, :]
            th = tot[:, hd:hd + 1]
            wts = (cb * jnp.exp(jnp.where(causal, ac - ar, NEG)) * dtrow).astype(BF16)
            cs = (cg * jnp.exp(ac)).astype(BF16)
            bs = (bgt * (jnp.exp(th - ar) * dtrow)).astype(BF16)
            yh = _dot(jnp.concatenate([wts, cs], axis=1), xh)
            sh = _dot(bs, xg)
            yg = jnp.where(lane_head == r, yh, yg)
            sg = jnp.where(lane_head[:SSD_STATE] == r, sh, sg)
            dec = jnp.where(lane_head_row == r, jnp.exp(th), dec)
        y_ref[:, g * gw:(g + 1) * gw] = yg
        h_ref[g] = hg * dec + sg


def _ssd_scan(t, dtr, dtrt, a, bias, nc, d):
    n_all = t.shape[0]
    q = SSD_CHUNK
    n_chunks = n_all // q
    cc = nc // q
    reverse = d == 1
    if reverse:
        order = lambda i: jnp.where(i < cc, cc - 1 - i, n_chunks - 1 - (i - cc))
    else:
        order = lambda i: i
    prow = jnp.zeros((8, LANE), F32).at[0, :SSD_HEADS].set(a).at[1, :SSD_HEADS].set(bias)
    pcol = jnp.zeros((SSD_HEADS, LANE), F32).at[:, 0].set(a).at[:, 1].set(bias)
    return pl.pallas_call(
        functools.partial(_ssd_kernel, reverse=reverse, d=d),
        grid=(n_chunks,),
        in_specs=[pl.BlockSpec((q, t.shape[1]), lambda i: (order(i), 0)),
                  pl.BlockSpec((q, LANE), lambda i: (order(i), 0)),
                  pl.BlockSpec((2 * SSD_HEADS, q), lambda i: (0, order(i))),
                  pl.BlockSpec((8, LANE), lambda i: (0, 0)),
                  pl.BlockSpec((SSD_HEADS, LANE), lambda i: (0, 0))],
        out_specs=pl.BlockSpec((q, SSD_HEADS * SSD_HEAD_DIM), lambda i: (order(i), 0)),
        out_shape=jax.ShapeDtypeStruct((n_all, SSD_HEADS * SSD_HEAD_DIM), F32),
        scratch_shapes=[pltpu.VMEM((SSD_GROUPS, SSD_STATE, (SSD_HEADS // SSD_GROUPS) * SSD_HEAD_DIM), F32)],
        compiler_params=_params("arbitrary"),
        name="ssd_scan_bwd" if reverse else "ssd_scan_fwd",
    )(t, dtr, dtrt, prow, pcol)


def _attn_kernel(sink_ref, q_ref, kp_ref, ko_ref, kn_ref, kc_ref, o_ref, *, n_blocks, ctx_blocks):
    c = pl.program_id(0)
    blk = ATTN_BLOCK
    dh = ATTN_HEAD_DIM
    rep = ATTN_HEADS // ATTN_KV_HEADS
    rows = rep * blk
    kw = ATTN_KV_HEADS * dh
    lat = c >= ctx_blocks
    lo = jnp.where(lat & (c > ctx_blocks), 0, blk)
    hi = jnp.where(lat, jnp.where(c < n_blocks - 1, 3 * blk, 2 * blk), 0)
    qi = lax.broadcasted_iota(I32, (rows, 3 * blk), 0) % blk
    col = lax.broadcasted_iota(I32, (rows, 3 * blk), 1)
    mask = (jnp.abs(qi - (col - blk)) <= blk) & (col >= lo) & (col < hi)
    rowhead = lax.broadcasted_iota(I32, (rows, 1), 0) // blk
    lane_q = lax.broadcasted_iota(I32, (blk, kw), 1)
    kloc = jnp.concatenate([kp_ref[:, 0:kw], ko_ref[:, 0:kw], kn_ref[:, 0:kw]], axis=0)
    vloc = jnp.concatenate([kp_ref[:, kw:2 * kw], ko_ref[:, kw:2 * kw], kn_ref[:, kw:2 * kw]], axis=0)
    kctx = kc_ref[:, 0:kw]
    vctx = kc_ref[:, kw:2 * kw]
    lane_l = lax.broadcasted_iota(I32, vloc.shape, 1)
    lane_c = lax.broadcasted_iota(I32, vctx.shape, 1)
    one = jnp.ones((), BF16)
    outs = []
    for g in range(ATTN_KV_HEADS):
        own = (lane_q >= g * dh) & (lane_q < (g + 1) * dh)
        vl = jnp.where((lane_l >= g * dh) & (lane_l < (g + 1) * dh), vloc, one)
        vc = jnp.where((lane_c >= g * dh) & (lane_c < (g + 1) * dh), vctx, one)
        qg = jnp.concatenate([jnp.where(own, q_ref[:, b * kw:(b + 1) * kw], jnp.zeros((), BF16))
                              for b in range(rep)], axis=0)
        s_loc = jnp.where(mask, _dot_nt(qg, kloc), NEG)
        s_ctx = _dot_nt(qg, kctx)
        sink = jnp.zeros((rows, 1), F32)
        for r in range(rep):
            sink = jnp.where(rowhead == r, sink_ref[g * rep + r], sink)
        mx = jnp.maximum(jnp.maximum(jnp.max(s_loc, axis=1, keepdims=True),
                                     jnp.max(s_ctx, axis=1, keepdims=True)), sink)
        p_loc = jnp.exp((s_loc - mx).astype(BF16))
        p_ctx = jnp.exp((s_ctx - mx).astype(BF16))
        pv = _dot(p_loc, vl) + _dot(p_ctx, vc)
        outs.append(pv / (pltpu.roll(pv, dh, 1) + jnp.exp(sink - mx)))
    for b in range(rep):
        o_ref[:, b * kw:(b + 1) * kw] = jnp.where(lane_q < dh, outs[0][b * blk:(b + 1) * blk],
                                                  outs[1][b * blk:(b + 1) * blk])


def _attention(q, kv, sink, nc):
    n_all = q.shape[0]
    blk = ATTN_BLOCK
    n_blocks = n_all // blk
    cbk = nc // blk
    kvw = kv.shape[1]
    return pl.pallas_call(
        functools.partial(_attn_kernel, n_blocks=n_blocks, ctx_blocks=cbk),
        grid=(n_blocks,),
        in_specs=[pl.BlockSpec(memory_space=pltpu.SMEM),
                  pl.BlockSpec((blk, q.shape[1]), lambda c: (c, 0)),
                  pl.BlockSpec((blk, kvw), lambda c: (jnp.maximum(c - 1, 0), 0)),
                  pl.BlockSpec((blk, kvw), lambda c: (c, 0)),
                  pl.BlockSpec((blk, kvw), lambda c: (jnp.minimum(c + 1, n_blocks - 1), 0)),
                  pl.BlockSpec((nc, kvw), lambda c: (0, 0))],
        out_specs=pl.BlockSpec((blk, q.shape[1]), lambda c: (c, 0)),
        out_shape=jax.ShapeDtypeStruct((n_all, q.shape[1]), F32),
        compiler_params=_params("arbitrary"),
        name="window_attention",
    )(sink, q, kv, kv, kv, kv)


def _merge_kernel(x_ref, mod_ref, ys5_ref, u_ref, yf_ref, yb_ref, xs_ref, z_ref, att_ref, gates_ref,
                  s5d_ref, bglu_ref, ssdd_ref, ssdg_ref, n2g_ref, wglu_ref, wbr_ref, wout_ref, wrh_ref, wrl_ref,
                  xo_ref, h2_ref, aff_ref, br_ref):
    m = mod_ref[0]
    d = x_ref.shape[1]
    a = jax.nn.gelu(ys5_ref[...].astype(F32) + s5d_ref[...] * u_ref[...])
    ya = (a * _sigmoid(_dot(a.astype(BF16), wglu_ref[...]) + bglu_ref[...])).astype(BF16)
    z = z_ref[...]
    yz = (yf_ref[...] + yb_ref[...] + ssdd_ref[...] * xs_ref[...]) * (z * _sigmoid(z))
    yb = (_rms(yz) * ssdg_ref[...]).astype(BF16)
    yc = att_ref[...].astype(BF16)
    cw = 256
    for j in range(d // cw):
        cs = slice(j * cw, (j + 1) * cw)
        br = (_sigmoid(gates_ref[:, j * cw:(j + 1) * cw].astype(F32)) * _dot(ya, wbr_ref[0, :, cs])
              + _sigmoid(gates_ref[:, d + j * cw:d + (j + 1) * cw].astype(F32)) * _dot(yb, wbr_ref[1, :, cs])
              + _sigmoid(gates_ref[:, 2 * d + j * cw:2 * d + (j + 1) * cw].astype(F32)) * _dot(yc, wbr_ref[2, :, cs]))
        br_ref[:, cs] = br.astype(BF16)
    xn = x_ref[...] + m[2:3] * _dot(br_ref[...], wout_ref[...])
    xo_ref[...] = xn
    h2 = _rms(xn) * n2g_ref[...] * (1.0 + m[4:5]) + m[3:4]
    hi = h2.astype(BF16)
    h2_ref[...] = hi
    lo = (h2 - hi.astype(F32)).astype(BF16)
    logits = (_dot(hi, wrh_ref[...]) + _dot(hi, wrl_ref[...]) + _dot(lo, wrh_ref[...]))[:, 0:N_EXPERTS]
    e = jnp.exp(logits - jnp.max(logits, axis=1, keepdims=True))
    aff_ref[...] = e / jnp.sum(e, axis=1, keepdims=True)


def _merge(xall, mod, ys5, u, yf, yb, t, z, att, gates, vecs, wglu, wbr, wout, wrh, wrl):
    n_all, d = xall.shape
    tt = TOK_TILE
    row = lambda i: (i, 0)
    const2 = lambda i: (0, 0)
    bw = 512
    s5d, bglu, ssdd, ssdg, n2g = vecs
    return pl.pallas_call(
        _merge_kernel,
        grid=(n_all // tt,),
        in_specs=[pl.BlockSpec((tt, d), row),
                  pl.BlockSpec((1, 6, d), lambda i: (jnp.minimum(i, 1), 0, 0)),
                  pl.BlockSpec((tt, bw), row), pl.BlockSpec((tt, bw), row), pl.BlockSpec((tt, bw), row),
                  pl.BlockSpec((tt, bw), row), pl.BlockSpec((tt, bw), row), pl.BlockSpec((tt, bw), row),
                  pl.BlockSpec((tt, bw), row), pl.BlockSpec((tt, 3 * d), row),
                  pl.BlockSpec((1, bw), const2), pl.BlockSpec((1, bw), const2), pl.BlockSpec((1, bw), const2),
                  pl.BlockSpec((1, bw), const2), pl.BlockSpec((1, d), const2),
                  pl.BlockSpec((bw, bw), const2),
                  pl.BlockSpec((3, bw, d), lambda i: (0, 0, 0)),
                  pl.BlockSpec((d, d), const2),
                  pl.BlockSpec((d, LANE), const2), pl.BlockSpec((d, LANE), const2)],
        out_specs=[pl.BlockSpec((tt, d), row),
                   pl.BlockSpec((tt, d), lambda i: (jnp.where(i == 0, n_all // tt - 1, i - 1), 0)),
                   pl.BlockSpec((tt, N_EXPERTS), row)],
        out_shape=[jax.ShapeDtypeStruct((n_all, d), F32), jax.ShapeDtypeStruct((n_all, d), BF16),
                   jax.ShapeDtypeStruct((n_all, N_EXPERTS), F32)],
        scratch_shapes=[pltpu.VMEM((tt, d), BF16)],
        compiler_params=_params("arbitrary"),
        name="merge_router",
    )(xall, mod, ys5, u, yf, yb, t, z, att, gates, s5d, bglu, ssdd, ssdg, n2g, wglu, wbr, wout, wrh, wrl)


def _route_kernel(a_ref, g_ref, pos_ref, cum_ref, *, cap):
    n = a_ref.shape[1]
    e = a_ref.shape[0]
    bits = lax.bitcast_convert_type(a_ref[...], I32)
    capf = float(cap)

    def search(i, thr):
        cand = thr | (1 << (30 - i))
        cnt = jnp.sum((bits >= cand).astype(F32), axis=1, keepdims=True)
        return jnp.where(cnt >= capf, cand, thr)

    thr = lax.fori_loop(0, 31, search, jnp.zeros((e, 1), I32))
    need = capf - jnp.sum((bits > thr).astype(F32), axis=1, keepdims=True)
    ii = lax.broadcasted_iota(I32, (LANE, LANE), 0)
    jj = lax.broadcasted_iota(I32, (LANE, LANE), 1)
    upper = (ii < jj).astype(BF16)

    def block(b, carry):
        eq_off, pos_off = carry
        sl = pl.ds(pl.multiple_of(b * LANE, LANE), LANE)
        a = a_ref[:, sl]
        v = lax.bitcast_convert_type(a, I32)
        eq = v == thr
        eqf = eq.astype(BF16)
        rank = _dot(eqf, upper) + eq_off
        sel = (v > thr) | (eq & (rank < need))
        self_ = sel.astype(BF16)
        pos = _dot(self_, upper) + pos_off
        g_ref[:, sl] = jnp.where(sel, a, 0.0)
        pos_ref[:, sl] = jnp.where(sel, pos, -1.0).astype(I32)
        cum_ref[:, sl] = pos.astype(I32)
        return (eq_off + jnp.sum(eqf.astype(F32), axis=1, keepdims=True),
                pos_off + jnp.sum(self_.astype(F32), axis=1, keepdims=True))

    zero = jnp.zeros((e, 1), F32)
    lax.fori_loop(0, n // LANE, block, (zero, zero))


def _route(aff_t, cap):
    e, n = aff_t.shape
    return pl.pallas_call(
        functools.partial(_route_kernel, cap=cap),
        out_shape=[jax.ShapeDtypeStruct((e, n), F32), jax.ShapeDtypeStruct((e, n), I32),
                   jax.ShapeDtypeStruct((e, n), I32)],
        name="ec_route",
    )(aff_t)


MOE_BLOCK = 1024
MOE_SUB = 256
MOE_WIN = 64
COMB_WIN = 128
COMB_MAX_ROUNDS = 3


MOE_PASS = 4


def _moe_gather_kernel(offs_ref, h_ref, pos_ref, xe_ref, *, n_sub):
    pss = pl.program_id(0)
    b = pl.program_id(1)
    subs = h_ref.shape[0] // MOE_SUB

    @pl.when(b == 0)
    def _():
        xe_ref[...] = jnp.zeros_like(xe_ref)

    def window(k, s, a, m):
        e = pss * MOE_PASS + k
        pos = pos_ref[pl.ds(e, 1), s * MOE_SUB:(s + 1) * MOE_SUB]
        r0 = pl.multiple_of(a + m * MOE_WIN, 16)
        slot = lax.broadcasted_iota(I32, (MOE_WIN, MOE_SUB), 0) + r0
        sel = jnp.where(slot == pos, 1.0, 0.0).astype(BF16)
        rows = pl.ds(r0, MOE_WIN)
        xe_ref[k, rows, :] = (xe_ref[k, rows, :].astype(F32)
                              + _dot(sel, h_ref[s * MOE_SUB:(s + 1) * MOE_SUB, :])).astype(BF16)

    extra = []
    for k in range(MOE_PASS):
        for s in range(subs):
            base = (pss * MOE_PASS + k) * (n_sub + 1) + b * subs + s
            o = offs_ref[base]
            o2 = offs_ref[base + 1]
            a = (o // 16) * 16
            extra.append((k, s, a, jnp.where(o2 > o, (o2 - a + MOE_WIN - 1) // MOE_WIN, 0)))
            window(k, s, a, 0)
    for k, s, a, cnt in extra:
        lax.fori_loop(1, cnt, lambda m, c, k=k, s=s, a=a: (window(k, s, a, m), c)[1], 0)


def _moe_gather(h2, pos_t, offs, h_row0, n, cap):
    d = h2.shape[1]
    e = pos_t.shape[0]
    tb = min(MOE_BLOCK, n)
    n_blocks = n // tb
    blk0 = h_row0 // tb
    grid_spec = pltpu.PrefetchScalarGridSpec(
        num_scalar_prefetch=1,
        grid=(e // MOE_PASS, n_blocks),
        in_specs=[pl.BlockSpec((tb, d), lambda p, b, offs: (b + blk0, 0)),
                  pl.BlockSpec((e, tb), lambda p, b, offs: (0, b))],
        out_specs=pl.BlockSpec((MOE_PASS, cap + MOE_WIN, d), lambda p, b, offs: (p, 0, 0)),
    )
    return pl.pallas_call(
        functools.partial(_moe_gather_kernel, n_sub=n // MOE_SUB),
        grid_spec=grid_spec,
        out_shape=jax.ShapeDtypeStruct((e, cap + MOE_WIN, d), BF16),
        compiler_params=pltpu.CompilerParams(dimension_semantics=("arbitrary", "arbitrary"),
                                             vmem_limit_bytes=56 * 1024 * 1024),
        name="moe_gather",
    )(offs, h2, pos_t)


def _moe_ffn_kernel(*refs, caps, rchunks):
    ns = len(caps)
    xes, (wg_ref, wu_ref, wd_ref) = refs[:ns], refs[ns:ns + 3]
    yes, (wgb, wub, wdb) = refs[ns + 3:2 * ns + 3], refs[2 * ns + 3:]
    slab = 256

    def cast(i, carry):
        rows = pl.ds(pl.multiple_of(i * slab, slab), slab)
        wgb[rows, :] = wg_ref[0, 0, rows, :].astype(BF16)
        wub[rows, :] = wu_ref[0, 0, rows, :].astype(BF16)
        wdb[rows, :] = wd_ref[0, 0, rows, :].astype(BF16)
        return carry

    lax.fori_loop(0, wgb.shape[0] // slab, cast, 0)
    for xe_ref, ye_ref, cap, rchunk in zip(xes, yes, caps, rchunks):
        def chunk(ci, carry, xe_ref=xe_ref, ye_ref=ye_ref, rchunk=rchunk):
            rows = pl.ds(pl.multiple_of(ci * rchunk, rchunk), rchunk)
            xb = xe_ref[0, rows, :]
            hg = _dot(xb, wgb[...])
            hid = (hg * _sigmoid(hg) * _dot(xb, wub[...])).astype(BF16)
            ye_ref[0, rows, :] = _dot(hid, wdb[...]).astype(BF16)
            return carry

        lax.fori_loop(0, cap // rchunk, chunk, 0)
        ye_ref[0, cap:, :] = jnp.zeros((ye_ref.shape[1] - cap, ye_ref.shape[2]), BF16)


def _moe_ffn(xes, wg, wu, wd, layer, caps):
    e, _, d = xes[0].shape
    f = wg.shape[3]
    assert d == f
    rchunks = tuple(min(cap, 256) for cap in caps)
    ye_rows = [cap + COMB_MAX_ROUNDS * COMB_WIN for cap in caps]
    wspec = lambda r, c: pl.BlockSpec((1, 1, r, c), lambda ei: (layer, ei, 0, 0))
    return pl.pallas_call(
        functools.partial(_moe_ffn_kernel, caps=tuple(caps), rchunks=rchunks),
        grid=(e,),
        in_specs=[pl.BlockSpec((1, xe.shape[1], d), lambda ei: (ei, 0, 0)) for xe in xes]
                 + [wspec(d, f), wspec(d, f), wspec(f, d)],
        out_specs=[pl.BlockSpec((1, r, d), lambda ei: (ei, 0, 0)) for r in ye_rows],
        out_shape=[jax.ShapeDtypeStruct((e, r, d), BF16) for r in ye_rows],
        scratch_shapes=[pltpu.VMEM((d, f), BF16), pltpu.VMEM((d, f), BF16), pltpu.VMEM((f, d), BF16)],
        compiler_params=pltpu.CompilerParams(dimension_semantics=("arbitrary",),
                                             vmem_limit_bytes=56 * 1024 * 1024),
        name="moe_ffn",
    )(*xes, wg, wu, wd)


def _moe_window_copy(ye_hbm, buf, sem, slot, e, start):
    return pltpu.make_async_copy(ye_hbm.at[e, pl.ds(start, COMB_WIN), :], buf.at[slot, e], sem.at[slot, e])


def _moe_combine_kernel(offs_ref, rounds_ref, x_ref, g_ref, pos_ref, mod_ref, ye_hbm, o_ref,
                        buf, lhs, acc_ref, sem, *, n_tiles):
    j = pl.program_id(0)
    t = x_ref.shape[0]
    n_exp = g_ref.shape[1]
    lane = lax.broadcasted_iota(I32, (t, COMB_WIN), 1)

    def starts_of(tile, rnd):
        return [pl.multiple_of((offs_ref[e * (n_tiles + 1) + tile] // 16) * 16 + rnd * COMB_WIN, 16)
                for e in range(n_exp)]

    def fetch(slot, starts):
        for e in range(n_exp):
            _moe_window_copy(ye_hbm, buf, sem, slot, e, starts[e]).start()

    def land(slot, starts):
        for e in range(n_exp):
            _moe_window_copy(ye_hbm, buf, sem, slot, e, starts[e]).wait()

    def expand(slot, starts):
        for e in range(n_exp):
            val = jnp.where(pos_ref[:, e:e + 1] == lane + starts[e], g_ref[:, e:e + 1], 0.0)
            lhs[:, e * COMB_WIN:(e + 1) * COMB_WIN] = val.astype(BF16)
        return _dot(lhs[...], buf[slot].reshape(n_exp * COMB_WIN, buf.shape[3]))

    cur = j % 2
    first = starts_of(j, 0)

    @pl.when(j == 0)
    def _():
        fetch(0, first)

    @pl.when(j + 1 < n_tiles)
    def _():
        fetch(1 - cur, starts_of(j + 1, 0))

    land(cur, first)
    acc_ref[...] = expand(cur, first)

    def more(rnd, carry):
        starts = starts_of(j, rnd)
        fetch(2, starts)
        land(2, starts)
        acc_ref[...] += expand(2, starts)
        return carry

    lax.fori_loop(1, rounds_ref[j], more, 0)
    o_ref[...] = x_ref[...] + mod_ref[0, 5:6, :] * acc_ref[...]


def _moe_combine(xall, g, pos, offs, rounds, mod, ye, tile0, n, mod_row):
    t = MOE_SUB
    d = xall.shape[1]
    n_exp = g.shape[1]
    n_tiles = n // t
    grid_spec = pltpu.PrefetchScalarGridSpec(
        num_scalar_prefetch=2,
        grid=(n_tiles,),
        in_specs=[pl.BlockSpec((t, d), lambda j, offs, rounds: (j + tile0, 0)),
                  pl.BlockSpec((t, n_exp), lambda j, offs, rounds: (j, 0)),
                  pl.BlockSpec((t, n_exp), lambda j, offs, rounds: (j, 0)),
                  pl.BlockSpec((1, 6, d), lambda j, offs, rounds: (mod_row, 0, 0)),
                  pl.BlockSpec(memory_space=pl.ANY)],
        out_specs=pl.BlockSpec((t, d), lambda j, offs, rounds: (j + tile0, 0)),
        scratch_shapes=[pltpu.VMEM((3, n_exp, COMB_WIN, d), BF16),
                        pltpu.VMEM((t, n_exp * COMB_WIN), BF16),
                        pltpu.VMEM((t, d), F32), pltpu.SemaphoreType.DMA((3, n_exp))],
    )
    return pl.pallas_call(
        functools.partial(_moe_combine_kernel, n_tiles=n_tiles),
        grid_spec=grid_spec,
        out_shape=jax.ShapeDtypeStruct(xall.shape, F32),
        input_output_aliases={2: 0},
        compiler_params=_params("arbitrary"),
        name="moe_combine",
    )(offs, rounds, xall, g, pos, mod, ye)


def _expert_choice(xall, h2, aff, mod, wg, wu, wd, layer, sets):
    routed = []
    for row0, h_row0, n, mod_row in sets:
        cap = EC_CAPACITY_FACTOR * n // N_EXPERTS
        g_t, pos_t, cum_t = _route(aff[row0:row0 + n].T, cap)
        offs = jnp.concatenate([cum_t[:, ::MOE_SUB], jnp.full((N_EXPERTS, 1), cap, I32)], axis=1)
        span = offs[:, 1:] - (offs[:, :-1] // 16) * 16
        rounds = jnp.maximum(jnp.max((span + COMB_WIN - 1) // COMB_WIN, axis=0), 1).astype(I32)
        offs = offs.reshape(-1)
        routed.append((cap, g_t, pos_t, offs, rounds, _moe_gather(h2, pos_t, offs, h_row0, n, cap)))
    yes = _moe_ffn([r[5] for r in routed], wg, wu, wd, layer, [r[0] for r in routed])
    for (row0, _, n, mod_row), (cap, g_t, pos_t, offs, rounds, _), ye in zip(sets, routed, yes):
        xall = _moe_combine(xall, g_t.T, pos_t.T, offs, rounds, mod, ye, row0 // MOE_SUB, n, mod_row)
    return xall


def _final_kernel(x_ref, g_ref, o_ref):
    o_ref[...] = _rms(x_ref[...]) * g_ref[...]


def _final_norm(xall, g, tile0, n):
    t = TOK_TILE
    d = xall.shape[1]
    return pl.pallas_call(
        _final_kernel,
        grid=(n // t,),
        in_specs=[pl.BlockSpec((t, d), lambda i: (i + tile0, 0)), pl.BlockSpec((1, d), lambda i: (0, 0))],
        out_specs=pl.BlockSpec((t, d), lambda i: (i, 0)),
        out_shape=jax.ShapeDtypeStruct((n, d), F32),
        compiler_params=_params("arbitrary"),
        name="final_norm",
    )(xall, g)


def kernel(x, c, ctx, c_ctx, w_mod, b_mod, norm1_g, norm2_g, w_in, s5_lam_re, s5_lam_im, s5_log_dt, s5_b_re, s5_b_im, s5_c_re, s5_c_im, s5_d, s5_w_glu, s5_b_glu, ssd_conv_w, ssd_conv_b, ssd_a_log, ssd_dt_bias, ssd_d, ssd_norm_g, attn_sink, w_branch, w_out, w_router, w_e_gate, w_e_up, w_e_down, final_norm_g):
    batch, n, d = x.shape
    nc = ctx.shape[1]
    depth = w_mod.shape[0]
    assert batch == 1 and nc == TOK_TILE and n % TOK_TILE == 0 and n % GRID_W == 0
    assert SSD_STATE == SSD_CHUNK
    xall = jnp.concatenate([ctx[0], x[0]], axis=0)
    cvecs = jnp.zeros((8, d), F32).at[0].set(c_ctx).at[1].set(c[0])
    mods = _modulation(cvecs, w_mod, b_mod)
    cos, sin = _rope_tables(n, nc)
    row = lambda v: v.reshape(1, -1).astype(F32)
    for i in range(depth):
        mod = mods[i, 0:2].reshape(2, 6, d)
        u, z, xbc, q, kv, gates, dtr = _inproj(xall, mod, row(norm1_g[i]), _prep_w_in(w_in[i]), cos, sin)
        ys5 = _s5_mix(u, nc, _s5_prep(s5_lam_re[i], s5_lam_im[i], s5_log_dt[i], s5_b_re[i], s5_b_im[i],
                                      s5_c_re[i], s5_c_im[i]))
        t = _ssd_conv(xbc, ssd_conv_w[i], ssd_conv_b[i])
        a = -jnp.exp(ssd_a_log[i].astype(F32))
        dtrt = dtr[:, 0:2 * SSD_HEADS].T
        yf = _ssd_scan(t, dtr, dtrt, a[0], ssd_dt_bias[i, 0], nc, 0)
        yb = _ssd_scan(t, dtr, dtrt, a[1], ssd_dt_bias[i, 1], nc, 1)
        att = _attention(q, kv, attn_sink[i].astype(F32), nc)
        vecs = (row(s5_d[i]), row(s5_b_glu[i]), row(jnp.repeat(ssd_d[i], SSD_HEAD_DIM)), row(ssd_norm_g[i]),
                row(norm2_g[i]))
        wr = jnp.pad(w_router[i].astype(F32), ((0, 0), (0, LANE - N_EXPERTS)))
        wrh = wr.astype(BF16)
        wrl = (wr - wrh.astype(F32)).astype(BF16)
        wbr = w_branch[i].at[2].set(w_branch[i][2][_attn_head_order()]).astype(BF16)
        xall, h2, aff = _merge(xall, mod, ys5, u, yf, yb, t, z, att, gates, vecs, s5_w_glu[i].astype(BF16),
                               wbr, w_out[i].astype(BF16), wrh, wrl)
        sets = [(nc, 0, n, 1)] + ([(0, n, nc, 0)] if i < depth - 1 else [])
        xall = _expert_choice(xall, h2, aff, mod, w_e_gate, w_e_up, w_e_down, i, sets)
    return _final_norm(xall, row(final_norm_g), nc // TOK_TILE, n)[None]
```

```python
import functools
import math

import jax
import jax.numpy as jnp
from jax import lax
from jax.experimental import pallas as pl
from jax.experimental.pallas import tpu as pltpu

F32 = jnp.float32
BF16 = jnp.bfloat16
I32 = jnp.int32
HI = lax.Precision.HIGHEST

D_MODEL = 1024
DEPTH = 4
GRID_W = 64
EPS = 1e-6
S5_GROUPS = 32
S5_GROUP_CH = 16
S5_STATE = 64
S5_CHUNK = 32
SSD_HEADS = 8
SSD_HEAD_DIM = 64
SSD_GROUPS = 2
SSD_STATE = 128
SSD_CHUNK = 128
ATTN_HEADS = 8
ATTN_KV_HEADS = 2
ATTN_HEAD_DIM = 64
ATTN_BLOCK = 128
ROPE_BASE = 10000.0
N_EXPERTS = 16
EXPERT_FF = 1024
EC_CAPACITY_FACTOR = 2
IN_SIZES = (512, 512, 1024, 16, 512, 128, 128, 3072)
TOK_TILE = 256
LANE = 128
NEG = -1e30


def _dot(a, b):
    return jnp.dot(a, b, preferred_element_type=F32)


def _dot_hi(a, b):
    return jnp.dot(a, b, precision=HI, preferred_element_type=F32)


def _dot_nt(a, b):
    return lax.dot_general(a, b, (((1,), (1,)), ((), ())), preferred_element_type=F32)


def _sigmoid(x):
    return 1.0 / (1.0 + jnp.exp(-x))


def _softplus(x):
    return jnp.maximum(x, 0.0) + jnp.log(1.0 + jnp.exp(-jnp.abs(x)))


def _rms(x):
    return x * lax.rsqrt(jnp.mean(x * x, axis=-1, keepdims=True) + EPS)


def _params(*sem):
    return pltpu.CompilerParams(dimension_semantics=sem)


def _mod_kernel(c_ref, w_ref, b_ref, o_ref):
    c = c_ref[...]
    o_ref[0] = _dot_hi(c * _sigmoid(c), w_ref[0]) + b_ref[0]


def _modulation(cvecs, w_mod, b_mod):
    depth, d, d6 = w_mod.shape
    bn = 1536
    return pl.pallas_call(
        _mod_kernel,
        grid=(depth, d6 // bn),
        in_specs=[pl.BlockSpec((8, d), lambda l, j: (0, 0)),
                  pl.BlockSpec((1, d, bn), lambda l, j: (l, 0, j)),
                  pl.BlockSpec((1, 1, bn), lambda l, j: (l, 0, j))],
        out_specs=pl.BlockSpec((1, 8, bn), lambda l, j: (l, 0, j)),
        out_shape=jax.ShapeDtypeStruct((depth, 8, d6), F32),
        compiler_params=_params("arbitrary", "arbitrary"),
        name="modulation",
    )(cvecs, w_mod, b_mod.reshape(depth, 1, d6))


W_IN_COLS = 6016


def _attn_head_order():
    rep = ATTN_HEADS // ATTN_KV_HEADS
    heads = [g * rep + b for b in range(rep) for g in range(ATTN_KV_HEADS)]
    return jnp.concatenate([jnp.arange(h * ATTN_HEAD_DIM, (h + 1) * ATTN_HEAD_DIM) for h in heads])


def _prep_w_in(w):
    parts, start = [], 0
    for s in IN_SIZES:
        parts.append(w[..., start:start + s])
        start += s
    u, z, xbc, dt, q, k, v, gates = parts
    dt = jnp.pad(dt, ((0, 0), (0, 0), (0, LANE - dt.shape[-1])))
    q = q[..., _attn_head_order()]
    return jnp.concatenate([u, z, xbc, q, k, v, gates, dt], axis=-1).astype(BF16)


def _inproj_kernel(x_ref, mod_ref, g_ref, w_ref, cos_ref, sin_ref,
                   u_ref, z_ref, xbc_ref, q_ref, kv_ref, gates_ref, dt_ref):
    m = mod_ref[0]
    h = (_rms(x_ref[...]) * g_ref[...] * (1.0 + m[1:2]) + m[0:1]).astype(BF16)

    def proj(a, b):
        return _dot(h, w_ref[:, a:b])

    u_ref[...] = proj(0, 512)
    z_ref[...] = proj(512, 1024)
    xbc_ref[...] = proj(1024, 2048)
    cos = cos_ref[...]
    sin = sin_ref[...]
    lane = lax.broadcasted_iota(I32, cos.shape, 1)
    first = (lane % 32) < 16

    def rope(v):
        partner = jnp.where(first, pltpu.roll(v, LANE - 16, 1), pltpu.roll(v, 16, 1))
        return v * cos + partner * sin

    scale = ATTN_HEAD_DIM ** -0.5
    for j in range(4):
        q_ref[:, j * LANE:(j + 1) * LANE] = (rope(proj(2048 + j * LANE, 2048 + (j + 1) * LANE)) * scale).astype(BF16)
    kv_ref[:, 0:LANE] = rope(proj(2560, 2688)).astype(BF16)
    kv_ref[:, LANE:2 * LANE] = proj(2688, 2816).astype(BF16)
    for j in range(6):
        gates_ref[:, j * 512:(j + 1) * 512] = proj(2816 + j * 512, 2816 + (j + 1) * 512).astype(BF16)
    dt_ref[...] = proj(5888, 6016)


def _inproj(xall, mods, g, w, cos, sin, layer):
    n_all, d = xall.shape
    t = TOK_TILE
    row = lambda i: (i, 0)
    const = lambda i: (0, 0)
    widths = (512, 512, 1024, 512, 256, 3072, LANE)
    dtypes = (F32, F32, F32, BF16, BF16, BF16, F32)
    return pl.pallas_call(
        _inproj_kernel,
        grid=(n_all // t,),
        in_specs=[pl.BlockSpec((t, d), row),
                  pl.BlockSpec((1, 6, d), lambda i: (2 * layer + jnp.minimum(i, 1), 0, 0)),
                  pl.BlockSpec((None, 1, d), lambda i: (layer, 0, 0)),
                  pl.BlockSpec((None, d, W_IN_COLS), lambda i: (layer, 0, 0), pipeline_mode=pl.Buffered(1)),
                  pl.BlockSpec((t, LANE), row),
                  pl.BlockSpec((t, LANE), row)],
        out_specs=[pl.BlockSpec((t, wd), row) for wd in widths],
        out_shape=[jax.ShapeDtypeStruct((n_all, wd), dt) for wd, dt in zip(widths, dtypes)],
        compiler_params=_params("arbitrary"),
        name="inproj",
    )(xall, mods, g, w, cos, sin)


def _rope_tables(n, nc):
    rows = n // GRID_W
    r = jnp.repeat(jnp.arange(rows, dtype=F32), GRID_W)
    c = jnp.tile(jnp.arange(GRID_W, dtype=F32), rows)
    m = ATTN_HEAD_DIM // 4
    inv_freq = ROPE_BASE ** (-jnp.arange(m, dtype=F32) / m)
    ang_r, ang_c = r[:, None] * inv_freq, c[:, None] * inv_freq
    cos = jnp.concatenate([jnp.cos(ang_r), jnp.cos(ang_r), jnp.cos(ang_c), jnp.cos(ang_c)], axis=1)
    sin = jnp.concatenate([-jnp.sin(ang_r), jnp.sin(ang_r), -jnp.sin(ang_c), jnp.sin(ang_c)], axis=1)
    cos = jnp.concatenate([jnp.ones((nc, 64), F32), cos], axis=0)
    sin = jnp.concatenate([jnp.zeros((nc, 64), F32), sin], axis=0)
    return jnp.tile(cos, (1, 2)), jnp.tile(sin, (1, 2))


def _s5_toeplitz_kernel(bb_ref, ca_ref, kt_ref):
    L, K = S5_CHUNK, S5_GROUP_CH
    tab = _dot_hi(bb_ref[0], ca_ref[0])
    for s_ in range(L):
        lo = (L - 1 - s_) * K
        kt_ref[0, s_ * K:(s_ + 1) * K, :] = tab[:, lo:lo + L * K].astype(BF16)


def _s5_prep_layer(lam_re, lam_im, log_dt, b_re, b_im, c_re, c_im):
    L, G, P, K = S5_CHUNK, S5_GROUPS, S5_STATE, S5_GROUP_CH
    lk = L * K
    lr, li = lam_re.astype(F32), lam_im.astype(F32)
    dt = jnp.exp(log_dt.astype(F32))[..., None]
    mag = jnp.exp(lr * dt)
    abr, abi = mag * jnp.cos(li * dt), mag * jnp.sin(li * dt)
    den = lr * lr + li * li
    fr = ((abr - 1.0) * lr + abi * li) / den
    fi = (abi * lr - (abr - 1.0) * li) / den
    bbr = fr[..., None] * b_re - fi[..., None] * b_im
    bbi = fr[..., None] * b_im + fi[..., None] * b_re
    tau = jnp.arange(L + 1, dtype=F32)[:, None, None, None]
    pm = jnp.exp(lr * dt * tau)
    apr, api = pm * jnp.cos(li * dt * tau), pm * jnp.sin(li * dt * tau)
    cr, ci = c_re.astype(F32), c_im.astype(F32)
    car = cr * apr[:, :, :, None, :] - ci * api[:, :, :, None, :]
    cai = cr * api[:, :, :, None, :] + ci * apr[:, :, :, None, :]

    lagmat = lambda v, d: v[:L, d].transpose(1, 3, 0, 2)
    zeros = jnp.zeros((G, P, L - 1, K), F32)
    fwd = lambda v: jnp.concatenate([zeros, lagmat(v, 0)], axis=2)
    bwd = lambda v: jnp.concatenate([jnp.flip(lagmat(v, 1), axis=2), zeros], axis=2)
    ca = jnp.concatenate([fwd(car), fwd(cai), bwd(car), bwd(cai)], axis=1).reshape(G, 4 * P, (2 * L - 1) * K)
    ca = jnp.pad(ca, ((0, 0), (0, 0), (0, K)))
    tr = lambda v: v.transpose(0, 2, 1)
    bb = jnp.concatenate([tr(bbr[0]), -tr(bbi[0]), tr(bbr[1]), -tr(bbi[1])], axis=2)
    def summary(d, idx):
        ar, ai = apr[idx, d][..., None], api[idx, d][..., None]
        re = ar * bbr[d][None] - ai * bbi[d][None]
        im = ar * bbi[d][None] + ai * bbr[d][None]
        f = lambda v: v.transpose(1, 0, 3, 2).reshape(G, lk, P)
        return f(re), f(im)

    sfr, sfi = summary(0, L - 1 - jnp.arange(L))
    sbr, sbi = summary(1, jnp.arange(L))
    sb = jnp.concatenate([sfr, sfi, sbr, sbi], axis=2).astype(BF16)

    def readout(v):
        return v.transpose(1, 3, 0, 2).reshape(G, P, lk)

    idx_b = L - jnp.arange(L)
    rc = jnp.concatenate([readout(car[1:L + 1, 0]), -readout(cai[1:L + 1, 0]),
                          readout(car[idx_b, 1]), -readout(cai[idx_b, 1])], axis=1).astype(BF16)
    coef = jnp.stack([apr[L, 0].reshape(-1), api[L, 0].reshape(-1),
                      apr[L, 1].reshape(-1), api[L, 1].reshape(-1)], axis=0)
    coef = jnp.pad(coef, ((0, 4), (0, 0)))
    return bb, ca, sb, rc, coef


def _s5_prep(*params):
    L, G, P, K = S5_CHUNK, S5_GROUPS, S5_STATE, S5_GROUP_CH
    lk = L * K
    bb, ca, sb, rc, coef = jax.vmap(_s5_prep_layer)(*params)
    flat = lambda v: v.reshape((-1,) + v.shape[2:])
    kt = pl.pallas_call(
        _s5_toeplitz_kernel,
        grid=(bb.shape[0] * G,),
        in_specs=[pl.BlockSpec((1, K, 4 * P), lambda g: (g, 0, 0)),
                  pl.BlockSpec((1, 4 * P, 2 * lk), lambda g: (g, 0, 0))],
        out_specs=pl.BlockSpec((1, lk, lk), lambda g: (g, 0, 0)),
        out_shape=jax.ShapeDtypeStruct((bb.shape[0] * G, lk, lk), BF16),
        compiler_params=_params("arbitrary"),
        name="s5_toeplitz",
    )(flat(bb), flat(ca))
    return kt, flat(sb), flat(rc), flat(coef)


def _s5_states_kernel(u_ref, sb_ref, fre, fim, bre, bim):
    p = S5_STATE
    s0 = _dot(u_ref[0], sb_ref[0])
    s1 = _dot(u_ref[1], sb_ref[1])
    for q, ref in enumerate((fre, fim, bre, bim)):
        ref[...] = jnp.concatenate([s0[:, q * p:(q + 1) * p], s1[:, q * p:(q + 1) * p]], axis=1)


def _s5_rec_kernel(coef_ref, sfr, sfi, sbr, sbi, hfr, hfi, hbr, hbi, *, n_chunks, ctx_chunks):
    arf, aif = coef_ref[0:1, :], coef_ref[1:2, :]
    arb, aib = coef_ref[2:3, :], coef_ref[3:4, :]
    zero = jnp.zeros_like(arf)

    def fstep(c, carry):
        hr, hi = carry
        hfr[pl.ds(c, 1), :] = hr
        hfi[pl.ds(c, 1), :] = hi
        return (arf * hr - aif * hi + sfr[pl.ds(c, 1), :], arf * hi + aif * hr + sfi[pl.ds(c, 1), :])

    lax.fori_loop(0, n_chunks, fstep, (zero, zero))

    def bstep(i, carry):
        c = jnp.where(i < ctx_chunks, ctx_chunks - 1 - i, n_chunks - 1 - (i - ctx_chunks))
        hr, hi = carry
        hbr[pl.ds(c, 1), :] = hr
        hbi[pl.ds(c, 1), :] = hi
        return (arb * hr - aib * hi + sbr[pl.ds(c, 1), :], arb * hi + aib * hr + sbi[pl.ds(c, 1), :])

    lax.fori_loop(0, n_chunks, bstep, (zero, zero))


def _s5_out_kernel(u_ref, kt_ref, hfr, hfi, hbr, hbi, rc_ref, y_ref):
    p = S5_STATE
    for i in range(2):
        h = jnp.concatenate([r[:, i * p:(i + 1) * p] for r in (hfr, hfi, hbr, hbi)], axis=1).astype(BF16)
        y_ref[i] = (_dot(u_ref[i], kt_ref[i]) + _dot(h, rc_ref[i])).astype(y_ref.dtype)


S5_LANE_GROUPS = LANE // S5_GROUP_CH
S5_T_LO = 8


def _s5_perm():
    i = jnp.arange(S5_T_LO * LANE)
    t_lo, gl, k = i // LANE, (i % LANE) // S5_GROUP_CH, i % S5_GROUP_CH
    j = gl * (S5_T_LO * S5_GROUP_CH) + t_lo * S5_GROUP_CH + k
    return jnp.zeros((S5_T_LO * LANE, S5_T_LO * LANE), BF16).at[i, j].set(1.0)


def _s5_pack_kernel(u_ref, perm_ref, o_ref, *, n_chunks):
    L = S5_CHUNK
    for t_hi in range(L // S5_T_LO):
        z = jnp.concatenate([u_ref[pl.ds(t_hi * S5_T_LO + t_lo, n_chunks, stride=L), :].astype(BF16)
                             for t_lo in range(S5_T_LO)], axis=1)
        w = _dot(z, perm_ref[...]).astype(BF16)
        for gl in range(S5_LANE_GROUPS):
            o_ref[gl, :, t_hi * LANE:(t_hi + 1) * LANE] = w[:, gl * LANE:(gl + 1) * LANE]


def _s5_unpack_kernel(y_ref, perm_ref, o_ref, *, n_chunks):
    L = S5_CHUNK
    for t_hi in range(L // S5_T_LO):
        w = jnp.concatenate([y_ref[gl, :, t_hi * LANE:(t_hi + 1) * LANE] for gl in range(S5_LANE_GROUPS)], axis=1)
        z = _dot_nt(w, perm_ref[...])
        for t_lo in range(S5_T_LO):
            o_ref[pl.ds(t_hi * S5_T_LO + t_lo, n_chunks, stride=L), :] = z[:, t_lo * LANE:(t_lo + 1) * LANE]


def _s5_mix(u, nc, prep, layer):
    kt, sb, rc, coef = prep
    n_all = u.shape[0]
    L, G, K, P = S5_CHUNK, S5_GROUPS, S5_GROUP_CH, S5_STATE
    C = n_all // L
    lk = L * K
    perm = _s5_perm()
    lg = S5_LANE_GROUPS
    perm_spec = pl.BlockSpec(perm.shape, lambda b: (0, 0))
    ug = pl.pallas_call(
        functools.partial(_s5_pack_kernel, n_chunks=C),
        grid=(G // lg,),
        in_specs=[pl.BlockSpec((n_all, LANE), lambda b: (0, b)), perm_spec],
        out_specs=pl.BlockSpec((lg, C, lk), lambda b: (b, 0, 0)),
        out_shape=jax.ShapeDtypeStruct((G, C, lk), BF16),
        compiler_params=_params("arbitrary"),
        name="s5_pack",
    )(u, perm)
    gp = G // 2
    st_shape = jax.ShapeDtypeStruct((C, G * P), F32)
    st_spec = pl.BlockSpec((C, 2 * P), lambda p: (0, p))
    states = pl.pallas_call(
        _s5_states_kernel,
        grid=(gp,),
        in_specs=[pl.BlockSpec((2, C, lk), lambda p: (p, 0, 0)),
                  pl.BlockSpec((2, lk, 4 * P), lambda p: (layer * gp + p, 0, 0))],
        out_specs=[st_spec] * 4,
        out_shape=[st_shape] * 4,
        compiler_params=_params("arbitrary"),
        name="s5_states",
    )(ug, sb)
    cb = 512
    col = pl.BlockSpec((C, cb), lambda j: (0, j))
    hs = pl.pallas_call(
        functools.partial(_s5_rec_kernel, n_chunks=C, ctx_chunks=nc // L),
        grid=(G * P // cb,),
        in_specs=[pl.BlockSpec((8, cb), lambda j: (layer, j))] + [col] * 4,
        out_specs=[col] * 4,
        out_shape=[st_shape] * 4,
        compiler_params=_params("arbitrary"),
        name="s5_recurrence",
    )(coef, *states)
    y = pl.pallas_call(
        _s5_out_kernel,
        grid=(gp,),
        in_specs=[pl.BlockSpec((2, C, lk), lambda p: (p, 0, 0)),
                  pl.BlockSpec((2, lk, lk), lambda p: (layer * gp + p, 0, 0))] + [st_spec] * 4
                 + [pl.BlockSpec((2, 4 * P, lk), lambda p: (layer * gp + p, 0, 0))],
        out_specs=pl.BlockSpec((2, C, lk), lambda p: (p, 0, 0)),
        out_shape=jax.ShapeDtypeStruct((G, C, lk), BF16),
        compiler_params=_params("arbitrary"),
        name="s5_out",
    )(ug, kt, *hs, rc)
    return pl.pallas_call(
        functools.partial(_s5_unpack_kernel, n_chunks=C),
        grid=(G // lg,),
        in_specs=[pl.BlockSpec((lg, C, lk), lambda b: (b, 0, 0)), perm_spec],
        out_specs=pl.BlockSpec((n_all, LANE), lambda b: (0, b)),
        out_shape=jax.ShapeDtypeStruct((n_all, G * K), F32),
        compiler_params=_params("arbitrary"),
        name="s5_unpack",
    )(y, perm)


def _conv_kernel(cur_ref, prev_ref, next_ref, w_ref, b_ref, o_ref, *, n_tiles):
    i = pl.program_id(0)
    cur = cur_ref[...]
    t = cur.shape[0]
    pv = prev_ref[...] * jnp.where(i >= 2, 1.0, 0.0)
    nx = next_ref[...] * jnp.where((i >= 1) & (i <= n_tiles - 2), 1.0, 0.0)
    row8 = lax.broadcasted_iota(I32, pv.shape, 0)
    acc = b_ref[...] + w_ref[2:3, :] * cur
    for s in (1, 2):
        r = pltpu.roll(cur, s, 0)
        head = jnp.where(row8 < s, pltpu.roll(pv, s, 0), r[0:8])
        acc = acc + w_ref[2 - s:3 - s, :] * jnp.concatenate([head, r[8:]], axis=0)
        r = pltpu.roll(cur, t - s, 0)
        tail = jnp.where(row8 >= 8 - s, pltpu.roll(nx, 8 - s, 0), r[t - 8:])
        acc = acc + w_ref[2 + s:3 + s, :] * jnp.concatenate([r[:t - 8], tail], axis=0)
    o_ref[...] = acc * _sigmoid(acc)


def _ssd_conv(xbc, conv_w, conv_b, layer):
    n_all, ch = xbc.shape
    t = TOK_TILE
    n_tiles = n_all // t
    per = t // 8
    return pl.pallas_call(
        functools.partial(_conv_kernel, n_tiles=n_tiles),
        grid=(n_tiles,),
        in_specs=[pl.BlockSpec((t, ch), lambda i: (i, 0)),
                  pl.BlockSpec((8, ch), lambda i: (jnp.maximum(i * per - 1, 0), 0)),
                  pl.BlockSpec((8, ch), lambda i: (jnp.minimum((i + 1) * per, n_tiles * per - 1), 0)),
                  pl.BlockSpec((8, ch), lambda i: (layer, 0)),
                  pl.BlockSpec((None, 1, ch), lambda i: (layer, 0, 0))],
        out_specs=pl.BlockSpec((t, ch), lambda i: (i, 0)),
        out_shape=jax.ShapeDtypeStruct((n_all, ch), F32),
        compiler_params=_params("arbitrary"),
        name="ssd_conv",
    )(xbc, xbc, xbc, conv_w, conv_b)


def _ssd_kernel(t_ref, dtr_ref, dtrt_ref, prow_ref, pcol_ref, y_ref, h_ref, *, reverse, d):
    q = SSD_CHUNK
    hpg = SSD_HEADS // SSD_GROUPS
    p = SSD_HEAD_DIM

    @pl.when(pl.program_id(0) == 0)
    def _():
        h_ref[...] = jnp.zeros_like(h_ref)

    lo = d * SSD_HEADS
    dt_c = _softplus(dtr_ref[:, lo:lo + SSD_HEADS] + prow_ref[1:2, 0:SSD_HEADS])
    adt_c = dt_c * prow_ref[0:1, 0:SSD_HEADS]
    dt_r = _softplus(dtrt_ref[lo:lo + SSD_HEADS, :] + pcol_ref[:, 1:2])
    adt_r = dt_r * pcol_ref[:, 0:1]
    ii = lax.broadcasted_iota(I32, (q, q), 0)
    jj = lax.broadcasted_iota(I32, (q, q), 1)
    causal = (jj >= ii) if reverse else (jj <= ii)
    acum_c = _dot_hi(causal.astype(F32), adt_c)
    acum_r = _dot_hi(adt_r, ((ii >= jj) if reverse else (ii <= jj)).astype(F32))
    tot = acum_c[0:1, :] if reverse else acum_c[q - 1:q, :]
    gw = hpg * p
    lane_head = lax.broadcasted_iota(I32, (q, gw), 1) // p
    lane_head_row = lax.broadcasted_iota(I32, (1, gw), 1) // p
    for g in range(SSD_GROUPS):
        bg = t_ref[:, 512 + g * SSD_STATE:512 + (g + 1) * SSD_STATE]
        cg = t_ref[:, 768 + g * SSD_STATE:768 + (g + 1) * SSD_STATE]
        cb = _dot_nt(cg.astype(BF16), bg.astype(BF16))
        bgt = bg.T
        xg = t_ref[:, g * gw:(g + 1) * gw].astype(BF16)
        hg = h_ref[g]
        xh = jnp.concatenate([xg, hg.astype(BF16)], axis=0)
        yg = jnp.zeros((q, gw), F32)
        sg = jnp.zeros((SSD_STATE, gw), F32)
        dec = jnp.zeros((1, gw), F32)
        for r in range(hpg):
            hd = g * hpg + r
            ac = jnp.broadcast_to(acum_c[:, hd:hd + 1], (q, q))
            ar = acum_r[hd:hd + 1, :]
            dtrow = dt_r[hd:hd + 1, :]
            th = tot[:, hd:hd + 1]
            wts = (cb * jnp.exp(jnp.where(causal, ac - ar, NEG)) * dtrow).astype(BF16)
            cs = (cg * jnp.exp(ac)).astype(BF16)
            bs = (bgt * (jnp.exp(th - ar) * dtrow)).astype(BF16)
            yh = _dot(jnp.concatenate([wts, cs], axis=1), xh)
            sh = _dot(bs, xg)
            yg = jnp.where(lane_head == r, yh, yg)
            sg = jnp.where(lane_head[:SSD_STATE] == r, sh, sg)
            dec = jnp.where(lane_head_row == r, jnp.exp(th), dec)
        y_ref[:, g * gw:(g + 1) * gw] = yg
        h_ref[g] = hg * dec + sg


def _ssd_scan(t, dtr, dtrt, prow, pcol, nc, d, layer):
    n_all = t.shape[0]
    q = SSD_CHUNK
    n_chunks = n_all // q
    cc = nc // q
    reverse = d == 1
    if reverse:
        order = lambda i: jnp.where(i < cc, cc - 1 - i, n_chunks - 1 - (i - cc))
    else:
        order = lambda i: i
    return pl.pallas_call(
        functools.partial(_ssd_kernel, reverse=reverse, d=d),
        grid=(n_chunks,),
        in_specs=[pl.BlockSpec((q, t.shape[1]), lambda i: (order(i), 0)),
                  pl.BlockSpec((q, LANE), lambda i: (order(i), 0)),
                  pl.BlockSpec((2 * SSD_HEADS, q), lambda i: (0, order(i))),
                  pl.BlockSpec((8, LANE), lambda i: (2 * layer + d, 0)),
                  pl.BlockSpec((SSD_HEADS, LANE), lambda i: (2 * layer + d, 0))],
        out_specs=pl.BlockSpec((q, SSD_HEADS * SSD_HEAD_DIM), lambda i: (order(i), 0)),
        out_shape=jax.ShapeDtypeStruct((n_all, SSD_HEADS * SSD_HEAD_DIM), F32),
        scratch_shapes=[pltpu.VMEM((SSD_GROUPS, SSD_STATE, (SSD_HEADS // SSD_GROUPS) * SSD_HEAD_DIM), F32)],
        compiler_params=_params("arbitrary"),
        name="ssd_scan_bwd" if reverse else "ssd_scan_fwd",
    )(t, dtr, dtrt, prow, pcol)


def _attn_kernel(sink_ref, q_ref, kp_ref, ko_ref, kn_ref, kc_ref, o_ref, *, n_blocks, ctx_blocks, layer):
    c = pl.program_id(0)
    blk = ATTN_BLOCK
    dh = ATTN_HEAD_DIM
    rep = ATTN_HEADS // ATTN_KV_HEADS
    rows = rep * blk
    kw = ATTN_KV_HEADS * dh
    lat = c >= ctx_blocks
    lo = jnp.where(lat & (c > ctx_blocks), 0, blk)
    hi = jnp.where(lat, jnp.where(c < n_blocks - 1, 3 * blk, 2 * blk), 0)
    qi = lax.broadcasted_iota(I32, (rows, 3 * blk), 0) % blk
    col = lax.broadcasted_iota(I32, (rows, 3 * blk), 1)
    mask = (jnp.abs(qi - (col - blk)) <= blk) & (col >= lo) & (col < hi)
    rowhead = lax.broadcasted_iota(I32, (rows, 1), 0) // blk
    lane_q = lax.broadcasted_iota(I32, (blk, kw), 1)
    kloc = jnp.concatenate([kp_ref[:, 0:kw], ko_ref[:, 0:kw], kn_ref[:, 0:kw]], axis=0)
    vloc = jnp.concatenate([kp_ref[:, kw:2 * kw], ko_ref[:, kw:2 * kw], kn_ref[:, kw:2 * kw]], axis=0)
    kctx = kc_ref[:, 0:kw]
    vctx = kc_ref[:, kw:2 * kw]
    lane_l = lax.broadcasted_iota(I32, vloc.shape, 1)
    lane_c = lax.broadcasted_iota(I32, vctx.shape, 1)
    one = jnp.ones((), BF16)
    outs = []
    for g in range(ATTN_KV_HEADS):
        own = (lane_q >= g * dh) & (lane_q < (g + 1) * dh)
        vl = jnp.where((lane_l >= g * dh) & (lane_l < (g + 1) * dh), vloc, one)
        vc = jnp.where((lane_c >= g * dh) & (lane_c < (g + 1) * dh), vctx, one)
        qg = jnp.concatenate([jnp.where(own, q_ref[:, b * kw:(b + 1) * kw], jnp.zeros((), BF16))
                              for b in range(rep)], axis=0)
        s_loc = jnp.where(mask, _dot_nt(qg, kloc), NEG)
        s_ctx = _dot_nt(qg, kctx)
        sink = jnp.zeros((rows, 1), F32)
        for r in range(rep):
            sink = jnp.where(rowhead == r, sink_ref[layer, g * rep + r], sink)
        mx = jnp.maximum(jnp.maximum(jnp.max(s_loc, axis=1, keepdims=True),
                                     jnp.max(s_ctx, axis=1, keepdims=True)), sink)
        p_loc = jnp.exp((s_loc - mx).astype(BF16))
        p_ctx = jnp.exp((s_ctx - mx).astype(BF16))
        pv = _dot(p_loc, vl) + _dot(p_ctx, vc)
        outs.append(pv / (pltpu.roll(pv, dh, 1) + jnp.exp(sink - mx)))
    for b in range(rep):
        o_ref[:, b * kw:(b + 1) * kw] = jnp.where(lane_q < dh, outs[0][b * blk:(b + 1) * blk],
                                                  outs[1][b * blk:(b + 1) * blk])


def _attention(q, kv, sink, nc, layer):
    n_all = q.shape[0]
    blk = ATTN_BLOCK
    n_blocks = n_all // blk
    cbk = nc // blk
    kvw = kv.shape[1]
    return pl.pallas_call(
        functools.partial(_attn_kernel, n_blocks=n_blocks, ctx_blocks=cbk, layer=layer),
        grid=(n_blocks,),
        in_specs=[pl.BlockSpec(memory_space=pltpu.SMEM),
                  pl.BlockSpec((blk, q.shape[1]), lambda c: (c, 0)),
                  pl.BlockSpec((blk, kvw), lambda c: (jnp.maximum(c - 1, 0), 0)),
                  pl.BlockSpec((blk, kvw), lambda c: (c, 0)),
                  pl.BlockSpec((blk, kvw), lambda c: (jnp.minimum(c + 1, n_blocks - 1), 0)),
                  pl.BlockSpec((nc, kvw), lambda c: (0, 0))],
        out_specs=pl.BlockSpec((blk, q.shape[1]), lambda c: (c, 0)),
        out_shape=jax.ShapeDtypeStruct((n_all, q.shape[1]), F32),
        compiler_params=_params("arbitrary"),
        name="window_attention",
    )(sink, q, kv, kv, kv, kv)


def _merge_kernel(x_ref, mod_ref, ys5_ref, u_ref, yf_ref, yb_ref, xs_ref, z_ref, att_ref, gates_ref,
                  s5d_ref, bglu_ref, ssdd_ref, ssdg_ref, n2g_ref, wglu_ref, wbr_ref, wout_ref, wrh_ref, wrl_ref,
                  xo_ref, h2_ref, aff_ref, br_ref):
    m = mod_ref[0]
    d = x_ref.shape[1]
    a = jax.nn.gelu(ys5_ref[...].astype(F32) + s5d_ref[...] * u_ref[...])
    ya = (a * _sigmoid(_dot(a.astype(BF16), wglu_ref[...]) + bglu_ref[...])).astype(BF16)
    z = z_ref[...]
    yz = (yf_ref[...] + yb_ref[...] + ssdd_ref[...] * xs_ref[...]) * (z * _sigmoid(z))
    yb = (_rms(yz) * ssdg_ref[...]).astype(BF16)
    yc = att_ref[...].astype(BF16)
    cw = 256
    for j in range(d // cw):
        cs = slice(j * cw, (j + 1) * cw)
        br = (_sigmoid(gates_ref[:, j * cw:(j + 1) * cw].astype(F32)) * _dot(ya, wbr_ref[0, :, cs])
              + _sigmoid(gates_ref[:, d + j * cw:d + (j + 1) * cw].astype(F32)) * _dot(yb, wbr_ref[1, :, cs])
              + _sigmoid(gates_ref[:, 2 * d + j * cw:2 * d + (j + 1) * cw].astype(F32)) * _dot(yc, wbr_ref[2, :, cs]))
        br_ref[:, cs] = br.astype(BF16)
    xn = x_ref[...] + m[2:3] * _dot(br_ref[...], wout_ref[...])
    xo_ref[...] = xn
    h2 = _rms(xn) * n2g_ref[...] * (1.0 + m[4:5]) + m[3:4]
    hi = h2.astype(BF16)
    h2_ref[...] = hi
    lo = (h2 - hi.astype(F32)).astype(BF16)
    logits = (_dot(hi, wrh_ref[...]) + _dot(hi, wrl_ref[...]) + _dot(lo, wrh_ref[...]))[:, 0:N_EXPERTS]
    e = jnp.exp(logits - jnp.max(logits, axis=1, keepdims=True))
    aff_ref[...] = e / jnp.sum(e, axis=1, keepdims=True)


def _merge(xall, mods, ys5, u, yf, yb, t, z, att, gates, vecs, wglu, wbr, wout, wrh, wrl, layer):
    n_all, d = xall.shape
    tt = TOK_TILE
    row = lambda i: (i, 0)
    lay3 = lambda i: (layer, 0, 0)
    vec = lambda wd: pl.BlockSpec((None, 1, wd), lay3)
    bw = 512
    s5d, bglu, ssdd, ssdg, n2g = vecs
    return pl.pallas_call(
        _merge_kernel,
        grid=(n_all // tt,),
        in_specs=[pl.BlockSpec((tt, d), row),
                  pl.BlockSpec((1, 6, d), lambda i: (2 * layer + jnp.minimum(i, 1), 0, 0)),
                  pl.BlockSpec((tt, bw), row), pl.BlockSpec((tt, bw), row), pl.BlockSpec((tt, bw), row),
                  pl.BlockSpec((tt, bw), row), pl.BlockSpec((tt, bw), row), pl.BlockSpec((tt, bw), row),
                  pl.BlockSpec((tt, bw), row), pl.BlockSpec((tt, 3 * d), row),
                  vec(bw), vec(bw), vec(bw), vec(bw), vec(d),
                  pl.BlockSpec((None, bw, bw), lay3),
                  pl.BlockSpec((3, bw, d), lay3),
                  pl.BlockSpec((None, d, d), lay3),
                  pl.BlockSpec((None, d, LANE), lay3), pl.BlockSpec((None, d, LANE), lay3)],
        out_specs=[pl.BlockSpec((tt, d), row),
                   pl.BlockSpec((tt, d), lambda i: (jnp.where(i == 0, n_all // tt - 1, i - 1), 0)),
                   pl.BlockSpec((tt, N_EXPERTS), row)],
        out_shape=[jax.ShapeDtypeStruct((n_all, d), F32), jax.ShapeDtypeStruct((n_all, d), BF16),
                   jax.ShapeDtypeStruct((n_all, N_EXPERTS), F32)],
        scratch_shapes=[pltpu.VMEM((tt, d), BF16)],
        compiler_params=_params("arbitrary"),
        name="merge_router",
    )(xall, mods, ys5, u, yf, yb, t, z, att, gates, s5d, bglu, ssdd, ssdg, n2g, wglu, wbr, wout, wrh, wrl)


def _route_kernel(a_ref, g_ref, pos_ref, cum_ref, *, cap):
    n = a_ref.shape[1]
    e = a_ref.shape[0]
    bits = lax.bitcast_convert_type(a_ref[...], I32)
    capf = float(cap)

    def search(i, thr):
        cand = thr | (1 << (30 - i))
        cnt = jnp.sum((bits >= cand).astype(F32), axis=1, keepdims=True)
        return jnp.where(cnt >= capf, cand, thr)

    thr = lax.fori_loop(0, 31, search, jnp.zeros((e, 1), I32))
    need = capf - jnp.sum((bits > thr).astype(F32), axis=1, keepdims=True)
    ii = lax.broadcasted_iota(I32, (LANE, LANE), 0)
    jj = lax.broadcasted_iota(I32, (LANE, LANE), 1)
    upper = (ii < jj).astype(BF16)

    def block(b, carry):
        eq_off, pos_off = carry
        sl = pl.ds(pl.multiple_of(b * LANE, LANE), LANE)
        a = a_ref[:, sl]
        v = lax.bitcast_convert_type(a, I32)
        eq = v == thr
        eqf = eq.astype(BF16)
        rank = _dot(eqf, upper) + eq_off
        sel = (v > thr) | (eq & (rank < need))
        self_ = sel.astype(BF16)
        pos = _dot(self_, upper) + pos_off
        g_ref[:, sl] = jnp.where(sel, a, 0.0)
        pos_ref[:, sl] = jnp.where(sel, pos, -1.0).astype(I32)
        cum_ref[:, sl] = pos.astype(I32)
        return (eq_off + jnp.sum(eqf.astype(F32), axis=1, keepdims=True),
                pos_off + jnp.sum(self_.astype(F32), axis=1, keepdims=True))

    zero = jnp.zeros((e, 1), F32)
    lax.fori_loop(0, n // LANE, block, (zero, zero))


def _route(aff_t, cap):
    e, n = aff_t.shape
    return pl.pallas_call(
        functools.partial(_route_kernel, cap=cap),
        out_shape=[jax.ShapeDtypeStruct((e, n), F32), jax.ShapeDtypeStruct((e, n), I32),
                   jax.ShapeDtypeStruct((e, n), I32)],
        name="ec_route",
    )(aff_t)


MOE_BLOCK = 1024
MOE_SUB = 256
MOE_WIN = 64
COMB_WIN = 128
COMB_MAX_ROUNDS = 3


MOE_PASS = 4


def _moe_gather_kernel(offs_ref, h_ref, pos_ref, xe_ref, *, n_sub):
    pss = pl.program_id(0)
    b = pl.program_id(1)
    subs = h_ref.shape[0] // MOE_SUB

    @pl.when(b == 0)
    def _():
        xe_ref[...] = jnp.zeros_like(xe_ref)

    def window(k, s, a, m):
        e = pss * MOE_PASS + k
        pos = pos_ref[pl.ds(e, 1), s * MOE_SUB:(s + 1) * MOE_SUB]
        r0 = pl.multiple_of(a + m * MOE_WIN, 16)
        slot = lax.broadcasted_iota(I32, (MOE_WIN, MOE_SUB), 0) + r0
        sel = jnp.where(slot == pos, 1.0, 0.0).astype(BF16)
        rows = pl.ds(r0, MOE_WIN)
        xe_ref[k, rows, :] = (xe_ref[k, rows, :].astype(F32)
                              + _dot(sel, h_ref[s * MOE_SUB:(s + 1) * MOE_SUB, :])).astype(BF16)

    extra = []
    for k in range(MOE_PASS):
        for s in range(subs):
            base = (pss * MOE_PASS + k) * (n_sub + 1) + b * subs + s
            o = offs_ref[base]
            o2 = offs_ref[base + 1]
            a = (o // 16) * 16
            extra.append((k, s, a, jnp.where(o2 > o, (o2 - a + MOE_WIN - 1) // MOE_WIN, 0)))
            window(k, s, a, 0)
    for k, s, a, cnt in extra:
        lax.fori_loop(1, cnt, lambda m, c, k=k, s=s, a=a: (window(k, s, a, m), c)[1], 0)


def _moe_gather(h2, pos_t, offs, h_row0, n, cap):
    d = h2.shape[1]
    e = pos_t.shape[0]
    tb = min(MOE_BLOCK, n)
    n_blocks = n // tb
    blk0 = h_row0 // tb
    grid_spec = pltpu.PrefetchScalarGridSpec(
        num_scalar_prefetch=1,
        grid=(e // MOE_PASS, n_blocks),
        in_specs=[pl.BlockSpec((tb, d), lambda p, b, offs: (b + blk0, 0)),
                  pl.BlockSpec((e, tb), lambda p, b, offs: (0, b))],
        out_specs=pl.BlockSpec((MOE_PASS, cap + MOE_WIN, d), lambda p, b, offs: (p, 0, 0)),
    )
    return pl.pallas_call(
        functools.partial(_moe_gather_kernel, n_sub=n // MOE_SUB),
        grid_spec=grid_spec,
        out_shape=jax.ShapeDtypeStruct((e, cap + MOE_WIN, d), BF16),
        compiler_params=pltpu.CompilerParams(dimension_semantics=("arbitrary", "arbitrary"),
                                             vmem_limit_bytes=56 * 1024 * 1024),
        name="moe_gather",
    )(offs, h2, pos_t)


def _moe_ffn_kernel(*refs, caps, rchunks):
    ns = len(caps)
    xes, (wg_ref, wu_ref, wd_ref) = refs[:ns], refs[ns:ns + 3]
    yes, (wgb, wub, wdb) = refs[ns + 3:2 * ns + 3], refs[2 * ns + 3:]
    slab = 256

    def cast(i, carry):
        rows = pl.ds(pl.multiple_of(i * slab, slab), slab)
        wgb[rows, :] = wg_ref[0, 0, rows, :].astype(BF16)
        wub[rows, :] = wu_ref[0, 0, rows, :].astype(BF16)
        wdb[rows, :] = wd_ref[0, 0, rows, :].astype(BF16)
        return carry

    lax.fori_loop(0, wgb.shape[0] // slab, cast, 0)
    for xe_ref, ye_ref, cap, rchunk in zip(xes, yes, caps, rchunks):
        def chunk(ci, carry, xe_ref=xe_ref, ye_ref=ye_ref, rchunk=rchunk):
            rows = pl.ds(pl.multiple_of(ci * rchunk, rchunk), rchunk)
            xb = xe_ref[0, rows, :]
            hg = _dot(xb, wgb[...])
            hid = (hg * _sigmoid(hg) * _dot(xb, wub[...])).astype(BF16)
            ye_ref[0, rows, :] = _dot(hid, wdb[...]).astype(BF16)
            return carry

        lax.fori_loop(0, cap // rchunk, chunk, 0)
        ye_ref[0, cap:, :] = jnp.zeros((ye_ref.shape[1] - cap, ye_ref.shape[2]), BF16)


def _moe_ffn(xes, wg, wu, wd, layer, caps):
    e, _, d = xes[0].shape
    f = wg.shape[3]
    assert d == f
    rchunks = tuple(min(cap, 256) for cap in caps)
    ye_rows = [cap + COMB_MAX_ROUNDS * COMB_WIN for cap in caps]
    wspec = lambda r, c: pl.BlockSpec((1, 1, r, c), lambda ei: (layer, ei, 0, 0))
    return pl.pallas_call(
        functools.partial(_moe_ffn_kernel, caps=tuple(caps), rchunks=rchunks),
        grid=(e,),
        in_specs=[pl.BlockSpec((1, xe.shape[1], d), lambda ei: (ei, 0, 0)) for xe in xes]
                 + [wspec(d, f), wspec(d, f), wspec(f, d)],
        out_specs=[pl.BlockSpec((1, r, d), lambda ei: (ei, 0, 0)) for r in ye_rows],
        out_shape=[jax.ShapeDtypeStruct((e, r, d), BF16) for r in ye_rows],
        scratch_shapes=[pltpu.VMEM((d, f), BF16), pltpu.VMEM((d, f), BF16), pltpu.VMEM((f, d), BF16)],
        compiler_params=pltpu.CompilerParams(dimension_semantics=("arbitrary",),
                                             vmem_limit_bytes=56 * 1024 * 1024),
        name="moe_ffn",
    )(*xes, wg, wu, wd)


def _moe_window_copy(ye_hbm, buf, sem, slot, e, start):
    return pltpu.make_async_copy(ye_hbm.at[e, pl.ds(start, COMB_WIN), :], buf.at[slot, e], sem.at[slot, e])


def _moe_combine_kernel(offs_ref, rounds_ref, x_ref, g_ref, pos_ref, mod_ref, ye_hbm, o_ref,
                        buf, lhs, acc_ref, sem, *, n_tiles):
    j = pl.program_id(0)
    t = x_ref.shape[0]
    n_exp = g_ref.shape[1]
    lane = lax.broadcasted_iota(I32, (t, COMB_WIN), 1)

    def starts_of(tile, rnd):
        return [pl.multiple_of((offs_ref[e * (n_tiles + 1) + tile] // 16) * 16 + rnd * COMB_WIN, 16)
                for e in range(n_exp)]

    def fetch(slot, starts):
        for e in range(n_exp):
            _moe_window_copy(ye_hbm, buf, sem, slot, e, starts[e]).start()

    def land(slot, starts):
        for e in range(n_exp):
            _moe_window_copy(ye_hbm, buf, sem, slot, e, starts[e]).wait()

    def expand(slot, starts):
        for e in range(n_exp):
            val = jnp.where(pos_ref[:, e:e + 1] == lane + starts[e], g_ref[:, e:e + 1], 0.0)
            lhs[:, e * COMB_WIN:(e + 1) * COMB_WIN] = val.astype(BF16)
        return _dot(lhs[...], buf[slot].reshape(n_exp * COMB_WIN, buf.shape[3]))

    cur = j % 2
    first = starts_of(j, 0)

    @pl.when(j == 0)
    def _():
        fetch(0, first)

    @pl.when(j + 1 < n_tiles)
    def _():
        fetch(1 - cur, starts_of(j + 1, 0))

    land(cur, first)
    acc_ref[...] = expand(cur, first)

    def more(rnd, carry):
        starts = starts_of(j, rnd)
        fetch(2, starts)
        land(2, starts)
        acc_ref[...] += expand(2, starts)
        return carry

    lax.fori_loop(1, rounds_ref[j], more, 0)
    o_ref[...] = x_ref[...] + mod_ref[0, 5:6, :] * acc_ref[...]


def _moe_combine(xall, g, pos, offs, rounds, mod, ye, tile0, n, mod_row):
    t = MOE_SUB
    d = xall.shape[1]
    n_exp = g.shape[1]
    n_tiles = n // t
    grid_spec = pltpu.PrefetchScalarGridSpec(
        num_scalar_prefetch=2,
        grid=(n_tiles,),
        in_specs=[pl.BlockSpec((t, d), lambda j, offs, rounds: (j + tile0, 0)),
                  pl.BlockSpec((t, n_exp), lambda j, offs, rounds: (j, 0)),
                  pl.BlockSpec((t, n_exp), lambda j, offs, rounds: (j, 0)),
                  pl.BlockSpec((1, 6, d), lambda j, offs, rounds: (mod_row, 0, 0)),
                  pl.BlockSpec(memory_space=pl.ANY)],
        out_specs=pl.BlockSpec((t, d), lambda j, offs, rounds: (j + tile0, 0)),
        scratch_shapes=[pltpu.VMEM((3, n_exp, COMB_WIN, d), BF16),
                        pltpu.VMEM((t, n_exp * COMB_WIN), BF16),
                        pltpu.VMEM((t, d), F32), pltpu.SemaphoreType.DMA((3, n_exp))],
    )
    return pl.pallas_call(
        functools.partial(_moe_combine_kernel, n_tiles=n_tiles),
        grid_spec=grid_spec,
        out_shape=jax.ShapeDtypeStruct(xall.shape, F32),
        input_output_aliases={2: 0},
        compiler_params=_params("arbitrary"),
        name="moe_combine",
    )(offs, rounds, xall, g, pos, mod, ye)


def _expert_choice(xall, h2, aff, mod, wg, wu, wd, layer, sets):
    routed = []
    for row0, h_row0, n, mod_row in sets:
        cap = EC_CAPACITY_FACTOR * n // N_EXPERTS
        g_t, pos_t, cum_t = _route(aff[row0:row0 + n].T, cap)
        offs = jnp.concatenate([cum_t[:, ::MOE_SUB], jnp.full((N_EXPERTS, 1), cap, I32)], axis=1)
        span = offs[:, 1:] - (offs[:, :-1] // 16) * 16
        rounds = jnp.maximum(jnp.max((span + COMB_WIN - 1) // COMB_WIN, axis=0), 1).astype(I32)
        offs = offs.reshape(-1)
        routed.append((cap, g_t, pos_t, offs, rounds, _moe_gather(h2, pos_t, offs, h_row0, n, cap)))
    yes = _moe_ffn([r[5] for r in routed], wg, wu, wd, layer, [r[0] for r in routed])
    for (row0, _, n, mod_row), (cap, g_t, pos_t, offs, rounds, _), ye in zip(sets, routed, yes):
        xall = _moe_combine(xall, g_t.T, pos_t.T, offs, rounds, mod, ye, row0 // MOE_SUB, n, 2 * layer + mod_row)
    return xall


def _final_kernel(x_ref, g_ref, o_ref):
    o_ref[...] = _rms(x_ref[...]) * g_ref[...]


def _final_norm(xall, g, tile0, n):
    t = TOK_TILE
    d = xall.shape[1]
    return pl.pallas_call(
        _final_kernel,
        grid=(n // t,),
        in_specs=[pl.BlockSpec((t, d), lambda i: (i + tile0, 0)), pl.BlockSpec((1, d), lambda i: (0, 0))],
        out_specs=pl.BlockSpec((t, d), lambda i: (i, 0)),
        out_shape=jax.ShapeDtypeStruct((n, d), F32),
        compiler_params=_params("arbitrary"),
        name="final_norm",
    )(xall, g)


def kernel(x, c, ctx, c_ctx, w_mod, b_mod, norm1_g, norm2_g, w_in, s5_lam_re, s5_lam_im, s5_log_dt, s5_b_re, s5_b_im, s5_c_re, s5_c_im, s5_d, s5_w_glu, s5_b_glu, ssd_conv_w, ssd_conv_b, ssd_a_log, ssd_dt_bias, ssd_d, ssd_norm_g, attn_sink, w_branch, w_out, w_router, w_e_gate, w_e_up, w_e_down, final_norm_g):
    batch, n, d = x.shape
    nc = ctx.shape[1]
    depth = w_mod.shape[0]
    assert batch == 1 and nc == TOK_TILE and n % TOK_TILE == 0 and n % GRID_W == 0
    assert SSD_STATE == SSD_CHUNK
    xall = jnp.concatenate([ctx[0], x[0]], axis=0)
    cvecs = jnp.zeros((8, d), F32).at[0].set(c_ctx).at[1].set(c[0])
    mods = _modulation(cvecs, w_mod, b_mod)
    cos, sin = _rope_tables(n, nc)
    row = lambda v: v.reshape(1, -1).astype(F32)
    mods = mods[:, 0:2].reshape(depth * 2, 6, d)
    vec3 = lambda v: v.reshape(depth, 1, -1).astype(F32)
    w_in_all = _prep_w_in(w_in)
    s5 = _s5_prep(s5_lam_re, s5_lam_im, s5_log_dt, s5_b_re, s5_b_im, s5_c_re, s5_c_im)
    conv_w = jnp.pad(ssd_conv_w, ((0, 0), (0, 3), (0, 0))).reshape(depth * 8, -1)
    conv_b = vec3(ssd_conv_b)
    a = -jnp.exp(ssd_a_log.astype(F32))
    ab = jnp.stack([a, ssd_dt_bias.astype(F32)], axis=2)
    prow = jnp.pad(ab, ((0, 0), (0, 0), (0, 6), (0, LANE - SSD_HEADS))).reshape(depth * 2 * 8, LANE)
    pcol = jnp.pad(ab.transpose(0, 1, 3, 2), ((0, 0), (0, 0), (0, 0), (0, LANE - 2))).reshape(-1, LANE)
    vecs = (vec3(s5_d), vec3(s5_b_glu), vec3(jnp.repeat(ssd_d, SSD_HEAD_DIM, axis=1)), vec3(ssd_norm_g),
            vec3(norm2_g))
    wr = jnp.pad(w_router.astype(F32), ((0, 0), (0, 0), (0, LANE - N_EXPERTS)))
    wrh = wr.astype(BF16)
    wrl = (wr - wrh.astype(F32)).astype(BF16)
    wbr = w_branch.at[:, 2].set(w_branch[:, 2][:, _attn_head_order()]).astype(BF16)
    wbr = wbr.reshape(depth * 3, wbr.shape[2], wbr.shape[3])
    wglu, wout = s5_w_glu.astype(BF16), w_out.astype(BF16)
    sink = attn_sink.astype(F32)
    g1 = vec3(norm1_g)
    for i in range(depth):
        u, z, xbc, q, kv, gates, dtr = _inproj(xall, mods, g1, w_in_all, cos, sin, i)
        ys5 = _s5_mix(u, nc, s5, i)
        t = _ssd_conv(xbc, conv_w, conv_b, i)
        dtrt = dtr[:, 0:2 * SSD_HEADS].T
        yf = _ssd_scan(t, dtr, dtrt, prow, pcol, nc, 0, i)
        yb = _ssd_scan(t, dtr, dtrt, prow, pcol, nc, 1, i)
        att = _attention(q, kv, sink, nc, i)
        xall, h2, aff = _merge(xall, mods, ys5, u, yf, yb, t, z, att, gates, vecs, wglu, wbr, wout, wrh, wrl, i)
        sets = [(nc, 0, n, 1)] + ([(0, n, nc, 0)] if i < depth - 1 else [])
        xall = _expert_choice(xall, h2, aff, mods, w_e_gate, w_e_up, w_e_down, i, sets)
    return _final_norm(xall, row(final_norm_g), nc // TOK_TILE, n)[None]
```

```python
import functools
import math

import jax
import jax.numpy as jnp
from jax import lax
from jax.experimental import pallas as pl
from jax.experimental.pallas import tpu as pltpu

F32 = jnp.float32
BF16 = jnp.bfloat16
I32 = jnp.int32
HI = lax.Precision.HIGHEST

D_MODEL = 1024
DEPTH = 4
GRID_W = 64
EPS = 1e-6
S5_GROUPS = 32
S5_GROUP_CH = 16
S5_STATE = 64
S5_CHUNK = 32
SSD_HEADS = 8
SSD_HEAD_DIM = 64
SSD_GROUPS = 2
SSD_STATE = 128
SSD_CHUNK = 128
ATTN_HEADS = 8
ATTN_KV_HEADS = 2
ATTN_HEAD_DIM = 64
ATTN_BLOCK = 128
ROPE_BASE = 10000.0
N_EXPERTS = 16
EXPERT_FF = 1024
EC_CAPACITY_FACTOR = 2
IN_SIZES = (512, 512, 1024, 16, 512, 128, 128, 3072)
TOK_TILE = 256
LANE = 128
NEG = -1e30


def _dot(a, b):
    return jnp.dot(a, b, preferred_element_type=F32)


def _dot_hi(a, b):
    return jnp.dot(a, b, precision=HI, preferred_element_type=F32)


def _dot_nt(a, b):
    return lax.dot_general(a, b, (((1,), (1,)), ((), ())), preferred_element_type=F32)


def _sigmoid(x):
    return 1.0 / (1.0 + jnp.exp(-x))


def _softplus(x):
    return jnp.maximum(x, 0.0) + jnp.log(1.0 + jnp.exp(-jnp.abs(x)))


def _rms(x):
    return x * lax.rsqrt(jnp.mean(x * x, axis=-1, keepdims=True) + EPS)


def _params(*sem):
    return pltpu.CompilerParams(dimension_semantics=sem)


def _mod_kernel(c_ref, w_ref, b_ref, o_ref):
    c = c_ref[...]
    o_ref[0] = _dot_hi(c * _sigmoid(c), w_ref[0]) + b_ref[0]


def _modulation(cvecs, w_mod, b_mod):
    depth, d, d6 = w_mod.shape
    bn = 1536
    return pl.pallas_call(
        _mod_kernel,
        grid=(depth, d6 // bn),
        in_specs=[pl.BlockSpec((8, d), lambda l, j: (0, 0)),
                  pl.BlockSpec((1, d, bn), lambda l, j: (l, 0, j)),
                  pl.BlockSpec((1, 1, bn), lambda l, j: (l, 0, j))],
        out_specs=pl.BlockSpec((1, 8, bn), lambda l, j: (l, 0, j)),
        out_shape=jax.ShapeDtypeStruct((depth, 8, d6), F32),
        compiler_params=_params("arbitrary", "arbitrary"),
        name="modulation",
    )(cvecs, w_mod, b_mod.reshape(depth, 1, d6))


W_IN_COLS = 6016


def _attn_head_order():
    rep = ATTN_HEADS // ATTN_KV_HEADS
    heads = [g * rep + b for b in range(rep) for g in range(ATTN_KV_HEADS)]
    return jnp.concatenate([jnp.arange(h * ATTN_HEAD_DIM, (h + 1) * ATTN_HEAD_DIM) for h in heads])


def _prep_w_in(w):
    parts, start = [], 0
    for s in IN_SIZES:
        parts.append(w[..., start:start + s])
        start += s
    u, z, xbc, dt, q, k, v, gates = parts
    dt = jnp.pad(dt, ((0, 0), (0, 0), (0, LANE - dt.shape[-1])))
    q = q[..., _attn_head_order()]
    return jnp.concatenate([u, z, xbc, q, k, v, gates, dt], axis=-1).astype(BF16)


def _inproj_kernel(x_ref, mod_ref, g_ref, w_ref, cos_ref, sin_ref,
                   u_ref, z_ref, xbc_ref, q_ref, kv_ref, gates_ref, dt_ref):
    m = mod_ref[0]
    h = (_rms(x_ref[...]) * g_ref[...] * (1.0 + m[1:2]) + m[0:1]).astype(BF16)

    def proj(a, b):
        return _dot(h, w_ref[:, a:b])

    u_ref[...] = proj(0, 512)
    z_ref[...] = proj(512, 1024)
    xbc_ref[...] = proj(1024, 2048)
    cos = cos_ref[...]
    sin = sin_ref[...]
    lane = lax.broadcasted_iota(I32, cos.shape, 1)
    first = (lane % 32) < 16

    def rope(v):
        partner = jnp.where(first, pltpu.roll(v, LANE - 16, 1), pltpu.roll(v, 16, 1))
        return v * cos + partner * sin

    scale = ATTN_HEAD_DIM ** -0.5
    for j in range(4):
        q_ref[:, j * LANE:(j + 1) * LANE] = (rope(proj(2048 + j * LANE, 2048 + (j + 1) * LANE)) * scale).astype(BF16)
    kv_ref[:, 0:LANE] = rope(proj(2560, 2688)).astype(BF16)
    kv_ref[:, LANE:2 * LANE] = proj(2688, 2816).astype(BF16)
    for j in range(6):
        gates_ref[:, j * 512:(j + 1) * 512] = proj(2816 + j * 512, 2816 + (j + 1) * 512).astype(BF16)
    dt_ref[...] = proj(5888, 6016)


def _inproj(xall, mods, g, w, cos, sin, layer):
    n_all, d = xall.shape
    t = TOK_TILE
    row = lambda i: (i, 0)
    const = lambda i: (0, 0)
    widths = (512, 512, 1024, 512, 256, 3072, LANE)
    dtypes = (F32, F32, F32, BF16, BF16, BF16, F32)
    return pl.pallas_call(
        _inproj_kernel,
        grid=(n_all // t,),
        in_specs=[pl.BlockSpec((t, d), row),
                  pl.BlockSpec((1, 6, d), lambda i: (2 * layer + jnp.minimum(i, 1), 0, 0)),
                  pl.BlockSpec((None, 1, d), lambda i: (layer, 0, 0)),
                  pl.BlockSpec((None, d, W_IN_COLS), lambda i: (layer, 0, 0), pipeline_mode=pl.Buffered(1)),
                  pl.BlockSpec((t, LANE), row),
                  pl.BlockSpec((t, LANE), row)],
        out_specs=[pl.BlockSpec((t, wd), row) for wd in widths],
        out_shape=[jax.ShapeDtypeStruct((n_all, wd), dt) for wd, dt in zip(widths, dtypes)],
        compiler_params=_params("arbitrary"),
        name="inproj",
    )(xall, mods, g, w, cos, sin)


def _rope_tables(n, nc):
    rows = n // GRID_W
    r = jnp.repeat(jnp.arange(rows, dtype=F32), GRID_W)
    c = jnp.tile(jnp.arange(GRID_W, dtype=F32), rows)
    m = ATTN_HEAD_DIM // 4
    inv_freq = ROPE_BASE ** (-jnp.arange(m, dtype=F32) / m)
    ang_r, ang_c = r[:, None] * inv_freq, c[:, None] * inv_freq
    cos = jnp.concatenate([jnp.cos(ang_r), jnp.cos(ang_r), jnp.cos(ang_c), jnp.cos(ang_c)], axis=1)
    sin = jnp.concatenate([-jnp.sin(ang_r), jnp.sin(ang_r), -jnp.sin(ang_c), jnp.sin(ang_c)], axis=1)
    cos = jnp.concatenate([jnp.ones((nc, 64), F32), cos], axis=0)
    sin = jnp.concatenate([jnp.zeros((nc, 64), F32), sin], axis=0)
    return jnp.tile(cos, (1, 2)), jnp.tile(sin, (1, 2))


def _s5_toeplitz_kernel(bbf_ref, caf_ref, bbb_ref, cab_ref, kt_ref):
    L, K = S5_CHUNK, S5_GROUP_CH
    lk = L * K
    tf = _dot_hi(bbf_ref[0], caf_ref[0])
    tb = _dot_hi(bbb_ref[0], cab_ref[0])
    lane = lax.broadcasted_iota(I32, tf.shape, 1)
    r = pltpu.roll(tf, lk - K, 1)
    tab = jnp.concatenate([tb + jnp.where(lane >= lk - K, r, 0.0), jnp.where(lane < lk - K, r, 0.0)], axis=1)
    for s_ in range(L):
        lo = (L - 1 - s_) * K
        kt_ref[0, s_ * K:(s_ + 1) * K, :] = tab[:, lo:lo + L * K].astype(BF16)


def _s5_prep_layer(lam_re, lam_im, log_dt, b_re, b_im, c_re, c_im):
    L, G, P, K = S5_CHUNK, S5_GROUPS, S5_STATE, S5_GROUP_CH
    lk = L * K
    lr, li = lam_re.astype(F32), lam_im.astype(F32)
    dt = jnp.exp(log_dt.astype(F32))[..., None]
    mag = jnp.exp(lr * dt)
    abr, abi = mag * jnp.cos(li * dt), mag * jnp.sin(li * dt)
    den = lr * lr + li * li
    fr = ((abr - 1.0) * lr + abi * li) / den
    fi = (abi * lr - (abr - 1.0) * li) / den
    bbr = fr[..., None] * b_re - fi[..., None] * b_im
    bbi = fr[..., None] * b_im + fi[..., None] * b_re
    tau = jnp.arange(L + 1, dtype=F32)[:, None, None, None]
    pm = jnp.exp(lr * dt * tau)
    apr, api = pm * jnp.cos(li * dt * tau), pm * jnp.sin(li * dt * tau)
    cr, ci = c_re.astype(F32), c_im.astype(F32)
    car = cr * apr[:, :, :, None, :] - ci * api[:, :, :, None, :]
    cai = cr * api[:, :, :, None, :] + ci * apr[:, :, :, None, :]

    lagmat = lambda v, d: v[:L, d].transpose(1, 3, 0, 2)
    caf = jnp.concatenate([lagmat(car, 0), lagmat(cai, 0)], axis=1).reshape(G, 2 * P, lk)
    cab = jnp.flip(jnp.concatenate([lagmat(car, 1), lagmat(cai, 1)], axis=1), axis=2).reshape(G, 2 * P, lk)
    tr = lambda v: v.transpose(0, 2, 1)
    bbf = jnp.concatenate([tr(bbr[0]), -tr(bbi[0])], axis=2)
    bbb = jnp.concatenate([tr(bbr[1]), -tr(bbi[1])], axis=2)

    def summary(d, rev):
        ar, ai = apr[:L, d], api[:L, d]
        if rev:
            ar, ai = jnp.flip(ar, axis=0), jnp.flip(ai, axis=0)
        ar, ai = ar[..., None], ai[..., None]
        re = ar * bbr[d][None] - ai * bbi[d][None]
        im = ar * bbi[d][None] + ai * bbr[d][None]
        f = lambda v: v.transpose(1, 0, 3, 2).reshape(G, lk, P)
        return f(re), f(im)

    sfr, sfi = summary(0, True)
    sbr, sbi = summary(1, False)
    sb = jnp.concatenate([sfr, sfi, sbr, sbi], axis=2).astype(BF16)

    def readout(v):
        return v.transpose(1, 3, 0, 2).reshape(G, P, lk)

    back = lambda v: jnp.flip(v[1:L + 1, 1], axis=0)
    rc = jnp.concatenate([readout(car[1:L + 1, 0]), -readout(cai[1:L + 1, 0]),
                          readout(back(car)), -readout(back(cai))], axis=1).astype(BF16)
    coef = jnp.stack([apr[L, 0].reshape(-1), api[L, 0].reshape(-1),
                      apr[L, 1].reshape(-1), api[L, 1].reshape(-1)], axis=0)
    coef = jnp.pad(coef, ((0, 4), (0, 0)))
    return bbf, caf, bbb, cab, sb, rc, coef


def _s5_prep(*params):
    L, G, P, K = S5_CHUNK, S5_GROUPS, S5_STATE, S5_GROUP_CH
    lk = L * K
    bbf, caf, bbb, cab, sb, rc, coef = jax.vmap(_s5_prep_layer)(*params)
    flat = lambda v: v.reshape((-1,) + v.shape[2:])
    n_mat = bbf.shape[0] * G
    bspec = pl.BlockSpec((1, K, 2 * P), lambda g: (g, 0, 0))
    cspec = pl.BlockSpec((1, 2 * P, lk), lambda g: (g, 0, 0))
    kt = pl.pallas_call(
        _s5_toeplitz_kernel,
        grid=(n_mat,),
        in_specs=[bspec, cspec, bspec, cspec],
        out_specs=pl.BlockSpec((1, lk, lk), lambda g: (g, 0, 0)),
        out_shape=jax.ShapeDtypeStruct((n_mat, lk, lk), BF16),
        compiler_params=_params("arbitrary"),
        name="s5_toeplitz",
    )(flat(bbf), flat(caf), flat(bbb), flat(cab))
    return kt, flat(sb), flat(rc), flat(coef)


def _s5_states_kernel(u_ref, sb_ref, fre, fim, bre, bim):
    p = S5_STATE
    s0 = _dot(u_ref[0], sb_ref[0])
    s1 = _dot(u_ref[1], sb_ref[1])
    for q, ref in enumerate((fre, fim, bre, bim)):
        ref[...] = jnp.concatenate([s0[:, q * p:(q + 1) * p], s1[:, q * p:(q + 1) * p]], axis=1)


def _s5_rec_kernel(coef_ref, sfr, sfi, sbr, sbi, hfr, hfi, hbr, hbi, *, n_chunks, ctx_chunks):
    arf, aif = coef_ref[0:1, :], coef_ref[1:2, :]
    arb, aib = coef_ref[2:3, :], coef_ref[3:4, :]
    zero = jnp.zeros_like(arf)

    def fstep(c, carry):
        hr, hi = carry
        hfr[pl.ds(c, 1), :] = hr
        hfi[pl.ds(c, 1), :] = hi
        return (arf * hr - aif * hi + sfr[pl.ds(c, 1), :], arf * hi + aif * hr + sfi[pl.ds(c, 1), :])

    lax.fori_loop(0, n_chunks, fstep, (zero, zero))

    def bstep(i, carry):
        c = jnp.where(i < ctx_chunks, ctx_chunks - 1 - i, n_chunks - 1 - (i - ctx_chunks))
        hr, hi = carry
        hbr[pl.ds(c, 1), :] = hr
        hbi[pl.ds(c, 1), :] = hi
        return (arb * hr - aib * hi + sbr[pl.ds(c, 1), :], arb * hi + aib * hr + sbi[pl.ds(c, 1), :])

    lax.fori_loop(0, n_chunks, bstep, (zero, zero))


def _s5_out_kernel(u_ref, kt_ref, hfr, hfi, hbr, hbi, rc_ref, y_ref):
    p = S5_STATE
    for i in range(2):
        h = jnp.concatenate([r[:, i * p:(i + 1) * p] for r in (hfr, hfi, hbr, hbi)], axis=1).astype(BF16)
        y_ref[i] = (_dot(u_ref[i], kt_ref[i]) + _dot(h, rc_ref[i])).astype(y_ref.dtype)


S5_LANE_GROUPS = LANE // S5_GROUP_CH
S5_T_LO = 8


def _s5_perm():
    i = jnp.arange(S5_T_LO * LANE)
    t_lo, gl, k = i // LANE, (i % LANE) // S5_GROUP_CH, i % S5_GROUP_CH
    j = gl * (S5_T_LO * S5_GROUP_CH) + t_lo * S5_GROUP_CH + k
    return jnp.zeros((S5_T_LO * LANE, S5_T_LO * LANE), BF16).at[i, j].set(1.0)


def _s5_pack_kernel(u_ref, perm_ref, o_ref, *, n_chunks):
    L = S5_CHUNK
    for t_hi in range(L // S5_T_LO):
        z = jnp.concatenate([u_ref[pl.ds(t_hi * S5_T_LO + t_lo, n_chunks, stride=L), :].astype(BF16)
                             for t_lo in range(S5_T_LO)], axis=1)
        w = _dot(z, perm_ref[...]).astype(BF16)
        for gl in range(S5_LANE_GROUPS):
            o_ref[gl, :, t_hi * LANE:(t_hi + 1) * LANE] = w[:, gl * LANE:(gl + 1) * LANE]


def _s5_unpack_kernel(y_ref, perm_ref, o_ref, *, n_chunks):
    L = S5_CHUNK
    for t_hi in range(L // S5_T_LO):
        w = jnp.concatenate([y_ref[gl, :, t_hi * LANE:(t_hi + 1) * LANE] for gl in range(S5_LANE_GROUPS)], axis=1)
        z = _dot_nt(w, perm_ref[...])
        for t_lo in range(S5_T_LO):
            o_ref[pl.ds(t_hi * S5_T_LO + t_lo, n_chunks, stride=L), :] = z[:, t_lo * LANE:(t_lo + 1) * LANE]


def _s5_mix(u, nc, prep, layer):
    kt, sb, rc, coef = prep
    n_all = u.shape[0]
    L, G, K, P = S5_CHUNK, S5_GROUPS, S5_GROUP_CH, S5_STATE
    C = n_all // L
    lk = L * K
    perm = _s5_perm()
    lg = S5_LANE_GROUPS
    perm_spec = pl.BlockSpec(perm.shape, lambda b: (0, 0))
    ug = pl.pallas_call(
        functools.partial(_s5_pack_kernel, n_chunks=C),
        grid=(G // lg,),
        in_specs=[pl.BlockSpec((n_all, LANE), lambda b: (0, b)), perm_spec],
        out_specs=pl.BlockSpec((lg, C, lk), lambda b: (b, 0, 0)),
        out_shape=jax.ShapeDtypeStruct((G, C, lk), BF16),
        compiler_params=_params("arbitrary"),
        name="s5_pack",
    )(u, perm)
    gp = G // 2
    st_shape = jax.ShapeDtypeStruct((C, G * P), F32)
    st_spec = pl.BlockSpec((C, 2 * P), lambda p: (0, p))
    states = pl.pallas_call(
        _s5_states_kernel,
        grid=(gp,),
        in_specs=[pl.BlockSpec((2, C, lk), lambda p: (p, 0, 0)),
                  pl.BlockSpec((2, lk, 4 * P), lambda p: (layer * gp + p, 0, 0))],
        out_specs=[st_spec] * 4,
        out_shape=[st_shape] * 4,
        compiler_params=_params("arbitrary"),
        name="s5_states",
    )(ug, sb)
    cb = 512
    col = pl.BlockSpec((C, cb), lambda j: (0, j))
    hs = pl.pallas_call(
        functools.partial(_s5_rec_kernel, n_chunks=C, ctx_chunks=nc // L),
        grid=(G * P // cb,),
        in_specs=[pl.BlockSpec((8, cb), lambda j: (layer, j))] + [col] * 4,
        out_specs=[col] * 4,
        out_shape=[st_shape] * 4,
        compiler_params=_params("arbitrary"),
        name="s5_recurrence",
    )(coef, *states)
    y = pl.pallas_call(
        _s5_out_kernel,
        grid=(gp,),
        in_specs=[pl.BlockSpec((2, C, lk), lambda p: (p, 0, 0)),
                  pl.BlockSpec((2, lk, lk), lambda p: (layer * gp + p, 0, 0))] + [st_spec] * 4
                 + [pl.BlockSpec((2, 4 * P, lk), lambda p: (layer * gp + p, 0, 0))],
        out_specs=pl.BlockSpec((2, C, lk), lambda p: (p, 0, 0)),
        out_shape=jax.ShapeDtypeStruct((G, C, lk), BF16),
        compiler_params=_params("arbitrary"),
        name="s5_out",
    )(ug, kt, *hs, rc)
    return pl.pallas_call(
        functools.partial(_s5_unpack_kernel, n_chunks=C),
        grid=(G // lg,),
        in_specs=[pl.BlockSpec((lg, C, lk), lambda b: (b, 0, 0)), perm_spec],
        out_specs=pl.BlockSpec((n_all, LANE), lambda b: (0, b)),
        out_shape=jax.ShapeDtypeStruct((n_all, G * K), F32),
        compiler_params=_params("arbitrary"),
        name="s5_unpack",
    )(y, perm)


def _conv_kernel(cur_ref, prev_ref, next_ref, w_ref, b_ref, o_ref, *, n_tiles):
    i = pl.program_id(0)
    cur = cur_ref[...]
    t = cur.shape[0]
    pv = prev_ref[...] * jnp.where(i >= 2, 1.0, 0.0)
    nx = next_ref[...] * jnp.where((i >= 1) & (i <= n_tiles - 2), 1.0, 0.0)
    row8 = lax.broadcasted_iota(I32, pv.shape, 0)
    acc = b_ref[...] + w_ref[2:3, :] * cur
    for s in (1, 2):
        r = pltpu.roll(cur, s, 0)
        head = jnp.where(row8 < s, pltpu.roll(pv, s, 0), r[0:8])
        acc = acc + w_ref[2 - s:3 - s, :] * jnp.concatenate([head, r[8:]], axis=0)
        r = pltpu.roll(cur, t - s, 0)
        tail = jnp.where(row8 >= 8 - s, pltpu.roll(nx, 8 - s, 0), r[t - 8:])
        acc = acc + w_ref[2 + s:3 + s, :] * jnp.concatenate([r[:t - 8], tail], axis=0)
    o_ref[...] = acc * _sigmoid(acc)


def _ssd_conv(xbc, conv_w, conv_b, layer):
    n_all, ch = xbc.shape
    t = TOK_TILE
    n_tiles = n_all // t
    per = t // 8
    return pl.pallas_call(
        functools.partial(_conv_kernel, n_tiles=n_tiles),
        grid=(n_tiles,),
        in_specs=[pl.BlockSpec((t, ch), lambda i: (i, 0)),
                  pl.BlockSpec((8, ch), lambda i: (jnp.maximum(i * per - 1, 0), 0)),
                  pl.BlockSpec((8, ch), lambda i: (jnp.minimum((i + 1) * per, n_tiles * per - 1), 0)),
                  pl.BlockSpec((8, ch), lambda i: (layer, 0)),
                  pl.BlockSpec((None, 1, ch), lambda i: (layer, 0, 0))],
        out_specs=pl.BlockSpec((t, ch), lambda i: (i, 0)),
        out_shape=jax.ShapeDtypeStruct((n_all, ch), F32),
        compiler_params=_params("arbitrary"),
        name="ssd_conv",
    )(xbc, xbc, xbc, conv_w, conv_b)


def _ssd_kernel(t_ref, dtr_ref, dtrt_ref, prow_ref, pcol_ref, y_ref, h_ref, *, reverse, d):
    q = SSD_CHUNK
    hpg = SSD_HEADS // SSD_GROUPS
    p = SSD_HEAD_DIM

    @pl.when(pl.program_id(0) == 0)
    def _():
        h_ref[...] = jnp.zeros_like(h_ref)

    lo = d * SSD_HEADS
    dt_c = _softplus(dtr_ref[:, lo:lo + SSD_HEADS] + prow_ref[1:2, 0:SSD_HEADS])
    adt_c = dt_c * prow_ref[0:1, 0:SSD_HEADS]
    dt_r = _softplus(dtrt_ref[lo:lo + SSD_HEADS, :] + pcol_ref[:, 1:2])
    adt_r = dt_r * pcol_ref[:, 0:1]
    ii = lax.broadcasted_iota(I32, (q, q), 0)
    jj = lax.broadcasted_iota(I32, (q, q), 1)
    causal = (jj >= ii) if reverse else (jj <= ii)
    acum_c = _dot_hi(causal.astype(F32), adt_c)
    acum_r = _dot_hi(adt_r, ((ii >= jj) if reverse else (ii <= jj)).astype(F32))
    tot = acum_c[0:1, :] if reverse else acum_c[q - 1:q, :]
    gw = hpg * p
    lane_head = lax.broadcasted_iota(I32, (q, gw), 1) // p
    lane_head_row = lax.broadcasted_iota(I32, (1, gw), 1) // p
    for g in range(SSD_GROUPS):
        bg = t_ref[:, 512 + g * SSD_STATE:512 + (g + 1) * SSD_STATE]
        cg = t_ref[:, 768 + g * SSD_STATE:768 + (g + 1) * SSD_STATE]
        cb = _dot_nt(cg.astype(BF16), bg.astype(BF16))
        bgt = bg.T
        xg = t_ref[:, g * gw:(g + 1) * gw].astype(BF16)
        hg = h_ref[g]
        xh = jnp.concatenate([xg, hg.astype(BF16)], axis=0)
        yg = jnp.zeros((q, gw), F32)
        sg = jnp.zeros((SSD_STATE, gw), F32)
        dec = jnp.zeros((1, gw), F32)
        for r in range(hpg):
            hd = g * hpg + r
            ac = jnp.broadcast_to(acum_c[:, hd:hd + 1], (q, q))
            ar = acum_r[hd:hd + 1, :]
            dtrow = dt_r[hd:hd + 1, :]
            th = tot[:, hd:hd + 1]
            wts = (cb * jnp.exp(jnp.where(causal, ac - ar, NEG)) * dtrow).astype(BF16)
            cs = (cg * jnp.exp(ac)).astype(BF16)
            bs = (bgt * (jnp.exp(th - ar) * dtrow)).astype(BF16)
            yh = _dot(jnp.concatenate([wts, cs], axis=1), xh)
            sh = _dot(bs, xg)
            yg = jnp.where(lane_head == r, yh, yg)
            sg = jnp.where(lane_head[:SSD_STATE] == r, sh, sg)
            dec = jnp.where(lane_head_row == r, jnp.exp(th), dec)
        y_ref[:, g * gw:(g + 1) * gw] = yg
        h_ref[g] = hg * dec + sg


def _ssd_scan(t, dtr, dtrt, prow, pcol, nc, d, layer):
    n_all = t.shape[0]
    q = SSD_CHUNK
    n_chunks = n_all // q
    cc = nc // q
    reverse = d == 1
    if reverse:
        order = lambda i: jnp.where(i < cc, cc - 1 - i, n_chunks - 1 - (i - cc))
    else:
        order = lambda i: i
    return pl.pallas_call(
        functools.partial(_ssd_kernel, reverse=reverse, d=d),
        grid=(n_chunks,),
        in_specs=[pl.BlockSpec((q, t.shape[1]), lambda i: (order(i), 0)),
                  pl.BlockSpec((q, LANE), lambda i: (order(i), 0)),
                  pl.BlockSpec((2 * SSD_HEADS, q), lambda i: (0, order(i))),
                  pl.BlockSpec((8, LANE), lambda i: (2 * layer + d, 0)),
                  pl.BlockSpec((SSD_HEADS, LANE), lambda i: (2 * layer + d, 0))],
        out_specs=pl.BlockSpec((q, SSD_HEADS * SSD_HEAD_DIM), lambda i: (order(i), 0)),
        out_shape=jax.ShapeDtypeStruct((n_all, SSD_HEADS * SSD_HEAD_DIM), F32),
        scratch_shapes=[pltpu.VMEM((SSD_GROUPS, SSD_STATE, (SSD_HEADS // SSD_GROUPS) * SSD_HEAD_DIM), F32)],
        compiler_params=_params("arbitrary"),
        name="ssd_scan_bwd" if reverse else "ssd_scan_fwd",
    )(t, dtr, dtrt, prow, pcol)


def _attn_kernel(sink_ref, q_ref, kp_ref, ko_ref, kn_ref, kc_ref, o_ref, *, n_blocks, ctx_blocks, layer):
    c = pl.program_id(0)
    blk = ATTN_BLOCK
    dh = ATTN_HEAD_DIM
    rep = ATTN_HEADS // ATTN_KV_HEADS
    rows = rep * blk
    kw = ATTN_KV_HEADS * dh
    lat = c >= ctx_blocks
    lo = jnp.where(lat & (c > ctx_blocks), 0, blk)
    hi = jnp.where(lat, jnp.where(c < n_blocks - 1, 3 * blk, 2 * blk), 0)
    qi = lax.broadcasted_iota(I32, (rows, 3 * blk), 0) % blk
    col = lax.broadcasted_iota(I32, (rows, 3 * blk), 1)
    mask = (jnp.abs(qi - (col - blk)) <= blk) & (col >= lo) & (col < hi)
    rowhead = lax.broadcasted_iota(I32, (rows, 1), 0) // blk
    lane_q = lax.broadcasted_iota(I32, (blk, kw), 1)
    kloc = jnp.concatenate([kp_ref[:, 0:kw], ko_ref[:, 0:kw], kn_ref[:, 0:kw]], axis=0)
    vloc = jnp.concatenate([kp_ref[:, kw:2 * kw], ko_ref[:, kw:2 * kw], kn_ref[:, kw:2 * kw]], axis=0)
    kctx = kc_ref[:, 0:kw]
    vctx = kc_ref[:, kw:2 * kw]
    lane_l = lax.broadcasted_iota(I32, vloc.shape, 1)
    lane_c = lax.broadcasted_iota(I32, vctx.shape, 1)
    one = jnp.ones((), BF16)
    outs = []
    for g in range(ATTN_KV_HEADS):
        own = (lane_q >= g * dh) & (lane_q < (g + 1) * dh)
        vl = jnp.where((lane_l >= g * dh) & (lane_l < (g + 1) * dh), vloc, one)
        vc = jnp.where((lane_c >= g * dh) & (lane_c < (g + 1) * dh), vctx, one)
        qg = jnp.concatenate([jnp.where(own, q_ref[:, b * kw:(b + 1) * kw], jnp.zeros((), BF16))
                              for b in range(rep)], axis=0)
        s_loc = jnp.where(mask, _dot_nt(qg, kloc), NEG)
        s_ctx = _dot_nt(qg, kctx)
        sink = jnp.zeros((rows, 1), F32)
        for r in range(rep):
            sink = jnp.where(rowhead == r, sink_ref[layer, g * rep + r], sink)
        mx = jnp.maximum(jnp.maximum(jnp.max(s_loc, axis=1, keepdims=True),
                                     jnp.max(s_ctx, axis=1, keepdims=True)), sink)
        p_loc = jnp.exp((s_loc - mx).astype(BF16))
        p_ctx = jnp.exp((s_ctx - mx).astype(BF16))
        pv = _dot(p_loc, vl) + _dot(p_ctx, vc)
        outs.append(pv / (pltpu.roll(pv, dh, 1) + jnp.exp(sink - mx)))
    for b in range(rep):
        o_ref[:, b * kw:(b + 1) * kw] = jnp.where(lane_q < dh, outs[0][b * blk:(b + 1) * blk],
                                                  outs[1][b * blk:(b + 1) * blk])


def _attention(q, kv, sink, nc, layer):
    n_all = q.shape[0]
    blk = ATTN_BLOCK
    n_blocks = n_all // blk
    cbk = nc // blk
    kvw = kv.shape[1]
    return pl.pallas_call(
        functools.partial(_attn_kernel, n_blocks=n_blocks, ctx_blocks=cbk, layer=layer),
        grid=(n_blocks,),
        in_specs=[pl.BlockSpec(memory_space=pltpu.SMEM),
                  pl.BlockSpec((blk, q.shape[1]), lambda c: (c, 0)),
                  pl.BlockSpec((blk, kvw), lambda c: (jnp.maximum(c - 1, 0), 0)),
                  pl.BlockSpec((blk, kvw), lambda c: (c, 0)),
                  pl.BlockSpec((blk, kvw), lambda c: (jnp.minimum(c + 1, n_blocks - 1), 0)),
                  pl.BlockSpec((nc, kvw), lambda c: (0, 0))],
        out_specs=pl.BlockSpec((blk, q.shape[1]), lambda c: (c, 0)),
        out_shape=jax.ShapeDtypeStruct((n_all, q.shape[1]), F32),
        compiler_params=_params("arbitrary"),
        name="window_attention",
    )(sink, q, kv, kv, kv, kv)


def _merge_kernel(x_ref, mod_ref, ys5_ref, u_ref, yf_ref, yb_ref, xs_ref, z_ref, att_ref, gates_ref,
                  s5d_ref, bglu_ref, ssdd_ref, ssdg_ref, n2g_ref, wglu_ref, wbr_ref, wout_ref, wrh_ref, wrl_ref,
                  xo_ref, h2_ref, aff_ref, br_ref):
    m = mod_ref[0]
    d = x_ref.shape[1]
    a = jax.nn.gelu(ys5_ref[...].astype(F32) + s5d_ref[...] * u_ref[...])
    ya = (a * _sigmoid(_dot(a.astype(BF16), wglu_ref[...]) + bglu_ref[...])).astype(BF16)
    z = z_ref[...]
    yz = (yf_ref[...] + yb_ref[...] + ssdd_ref[...] * xs_ref[...]) * (z * _sigmoid(z))
    yb = (_rms(yz) * ssdg_ref[...]).astype(BF16)
    yc = att_ref[...].astype(BF16)
    cw = 256
    for j in range(d // cw):
        cs = slice(j * cw, (j + 1) * cw)
        br = (_sigmoid(gates_ref[:, j * cw:(j + 1) * cw].astype(F32)) * _dot(ya, wbr_ref[0, :, cs])
              + _sigmoid(gates_ref[:, d + j * cw:d + (j + 1) * cw].astype(F32)) * _dot(yb, wbr_ref[1, :, cs])
              + _sigmoid(gates_ref[:, 2 * d + j * cw:2 * d + (j + 1) * cw].astype(F32)) * _dot(yc, wbr_ref[2, :, cs]))
        br_ref[:, cs] = br.astype(BF16)
    xn = x_ref[...] + m[2:3] * _dot(br_ref[...], wout_ref[...])
    xo_ref[...] = xn
    h2 = _rms(xn) * n2g_ref[...] * (1.0 + m[4:5]) + m[3:4]
    hi = h2.astype(BF16)
    h2_ref[...] = hi
    lo = (h2 - hi.astype(F32)).astype(BF16)
    logits = (_dot(hi, wrh_ref[...]) + _dot(hi, wrl_ref[...]) + _dot(lo, wrh_ref[...]))[:, 0:N_EXPERTS]
    e = jnp.exp(logits - jnp.max(logits, axis=1, keepdims=True))
    aff_ref[...] = e / jnp.sum(e, axis=1, keepdims=True)


def _merge(xall, mods, ys5, u, yf, yb, t, z, att, gates, vecs, wglu, wbr, wout, wrh, wrl, layer):
    n_all, d = xall.shape
    tt = TOK_TILE
    row = lambda i: (i, 0)
    lay3 = lambda i: (layer, 0, 0)
    vec = lambda wd: pl.BlockSpec((None, 1, wd), lay3)
    bw = 512
    s5d, bglu, ssdd, ssdg, n2g = vecs
    return pl.pallas_call(
        _merge_kernel,
        grid=(n_all // tt,),
        in_specs=[pl.BlockSpec((tt, d), row),
                  pl.BlockSpec((1, 6, d), lambda i: (2 * layer + jnp.minimum(i, 1), 0, 0)),
                  pl.BlockSpec((tt, bw), row), pl.BlockSpec((tt, bw), row), pl.BlockSpec((tt, bw), row),
                  pl.BlockSpec((tt, bw), row), pl.BlockSpec((tt, bw), row), pl.BlockSpec((tt, bw), row),
                  pl.BlockSpec((tt, bw), row), pl.BlockSpec((tt, 3 * d), row),
                  vec(bw), vec(bw), vec(bw), vec(bw), vec(d),
                  pl.BlockSpec((None, bw, bw), lay3),
                  pl.BlockSpec((3, bw, d), lay3),
                  pl.BlockSpec((None, d, d), lay3),
                  pl.BlockSpec((None, d, LANE), lay3), pl.BlockSpec((None, d, LANE), lay3)],
        out_specs=[pl.BlockSpec((tt, d), row),
                   pl.BlockSpec((tt, d), lambda i: (jnp.where(i == 0, n_all // tt - 1, i - 1), 0)),
                   pl.BlockSpec((tt, N_EXPERTS), row)],
        out_shape=[jax.ShapeDtypeStruct((n_all, d), F32), jax.ShapeDtypeStruct((n_all, d), BF16),
                   jax.ShapeDtypeStruct((n_all, N_EXPERTS), F32)],
        scratch_shapes=[pltpu.VMEM((tt, d), BF16)],
        compiler_params=_params("arbitrary"),
        name="merge_router",
    )(xall, mods, ys5, u, yf, yb, t, z, att, gates, s5d, bglu, ssdd, ssdg, n2g, wglu, wbr, wout, wrh, wrl)


def _route_kernel(a_ref, g_ref, pos_ref, cum_ref, *, cap):
    n = a_ref.shape[1]
    e = a_ref.shape[0]
    bits = lax.bitcast_convert_type(a_ref[...], I32)
    capf = float(cap)

    def search(i, thr):
        cand = thr | (1 << (30 - i))
        cnt = jnp.sum((bits >= cand).astype(F32), axis=1, keepdims=True)
        return jnp.where(cnt >= capf, cand, thr)

    thr = lax.fori_loop(0, 31, search, jnp.zeros((e, 1), I32))
    need = capf - jnp.sum((bits > thr).astype(F32), axis=1, keepdims=True)
    ii = lax.broadcasted_iota(I32, (LANE, LANE), 0)
    jj = lax.broadcasted_iota(I32, (LANE, LANE), 1)
    upper = (ii < jj).astype(BF16)

    def block(b, carry):
        eq_off, pos_off = carry
        sl = pl.ds(pl.multiple_of(b * LANE, LANE), LANE)
        a = a_ref[:, sl]
        v = lax.bitcast_convert_type(a, I32)
        eq = v == thr
        eqf = eq.astype(BF16)
        rank = _dot(eqf, upper) + eq_off
        sel = (v > thr) | (eq & (rank < need))
        self_ = sel.astype(BF16)
        pos = _dot(self_, upper) + pos_off
        g_ref[:, sl] = jnp.where(sel, a, 0.0)
        pos_ref[:, sl] = jnp.where(sel, pos, -1.0).astype(I32)
        cum_ref[:, sl] = pos.astype(I32)
        return (eq_off + jnp.sum(eqf.astype(F32), axis=1, keepdims=True),
                pos_off + jnp.sum(self_.astype(F32), axis=1, keepdims=True))

    zero = jnp.zeros((e, 1), F32)
    lax.fori_loop(0, n // LANE, block, (zero, zero))


def _route(aff_t, cap):
    e, n = aff_t.shape
    return pl.pallas_call(
        functools.partial(_route_kernel, cap=cap),
        out_shape=[jax.ShapeDtypeStruct((e, n), F32), jax.ShapeDtypeStruct((e, n), I32),
                   jax.ShapeDtypeStruct((e, n), I32)],
        name="ec_route",
    )(aff_t)


MOE_BLOCK = 1024
MOE_SUB = 256
MOE_WIN = 64
COMB_WIN = 64
COMB_MAX_ROUNDS = 5


MOE_PASS = 4


def _moe_gather_kernel(offs_ref, h_ref, pos_ref, xe_ref, *, n_sub):
    pss = pl.program_id(0)
    b = pl.program_id(1)
    subs = h_ref.shape[0] // MOE_SUB

    @pl.when(b == 0)
    def _():
        xe_ref[...] = jnp.zeros_like(xe_ref)

    def window(k, s, a, m):
        e = pss * MOE_PASS + k
        pos = pos_ref[pl.ds(e, 1), s * MOE_SUB:(s + 1) * MOE_SUB]
        r0 = pl.multiple_of(a + m * MOE_WIN, 16)
        slot = lax.broadcasted_iota(I32, (MOE_WIN, MOE_SUB), 0) + r0
        sel = jnp.where(slot == pos, 1.0, 0.0).astype(BF16)
        rows = pl.ds(r0, MOE_WIN)
        xe_ref[k, rows, :] = (xe_ref[k, rows, :].astype(F32)
                              + _dot(sel, h_ref[s * MOE_SUB:(s + 1) * MOE_SUB, :])).astype(BF16)

    extra = []
    for k in range(MOE_PASS):
        for s in range(subs):
            base = (pss * MOE_PASS + k) * (n_sub + 1) + b * subs + s
            o = offs_ref[base]
            o2 = offs_ref[base + 1]
            a = (o // 16) * 16
            extra.append((k, s, a, jnp.where(o2 > o, (o2 - a + MOE_WIN - 1) // MOE_WIN, 0)))
            window(k, s, a, 0)
    for k, s, a, cnt in extra:
        lax.fori_loop(1, cnt, lambda m, c, k=k, s=s, a=a: (window(k, s, a, m), c)[1], 0)


def _moe_gather(h2, pos_t, offs, h_row0, n, cap):
    d = h2.shape[1]
    e = pos_t.shape[0]
    tb = min(MOE_BLOCK, n)
    n_blocks = n // tb
    blk0 = h_row0 // tb
    grid_spec = pltpu.PrefetchScalarGridSpec(
        num_scalar_prefetch=1,
        grid=(e // MOE_PASS, n_blocks),
        in_specs=[pl.BlockSpec((tb, d), lambda p, b, offs: (b + blk0, 0)),
                  pl.BlockSpec((e, tb), lambda p, b, offs: (0, b))],
        out_specs=pl.BlockSpec((MOE_PASS, cap + MOE_WIN, d), lambda p, b, offs: (p, 0, 0)),
    )
    return pl.pallas_call(
        functools.partial(_moe_gather_kernel, n_sub=n // MOE_SUB),
        grid_spec=grid_spec,
        out_shape=jax.ShapeDtypeStruct((e, cap + MOE_WIN, d), BF16),
        compiler_params=pltpu.CompilerParams(dimension_semantics=("arbitrary", "arbitrary"),
                                             vmem_limit_bytes=56 * 1024 * 1024),
        name="moe_gather",
    )(offs, h2, pos_t)


def _moe_ffn_kernel(*refs, caps, rchunks):
    ns = len(caps)
    xes, (wg_ref, wu_ref, wd_ref) = refs[:ns], refs[ns:ns + 3]
    yes, (wgb, wub, wdb) = refs[ns + 3:2 * ns + 3], refs[2 * ns + 3:]
    slab = 256

    def cast(i, carry):
        rows = pl.ds(pl.multiple_of(i * slab, slab), slab)
        wgb[rows, :] = wg_ref[0, 0, rows, :].astype(BF16)
        wub[rows, :] = wu_ref[0, 0, rows, :].astype(BF16)
        wdb[rows, :] = wd_ref[0, 0, rows, :].astype(BF16)
        return carry

    lax.fori_loop(0, wgb.shape[0] // slab, cast, 0)
    for xe_ref, ye_ref, cap, rchunk in zip(xes, yes, caps, rchunks):
        def chunk(ci, carry, xe_ref=xe_ref, ye_ref=ye_ref, rchunk=rchunk):
            rows = pl.ds(pl.multiple_of(ci * rchunk, rchunk), rchunk)
            xb = xe_ref[0, rows, :]
            hg = _dot(xb, wgb[...])
            hid = (hg * _sigmoid(hg) * _dot(xb, wub[...])).astype(BF16)
            ye_ref[0, rows, :] = _dot(hid, wdb[...]).astype(BF16)
            return carry

        lax.fori_loop(0, cap // rchunk, chunk, 0)
        ye_ref[0, cap:, :] = jnp.zeros((ye_ref.shape[1] - cap, ye_ref.shape[2]), BF16)


def _moe_ffn(xes, wg, wu, wd, layer, caps):
    e, _, d = xes[0].shape
    f = wg.shape[3]
    assert d == f
    rchunks = tuple(min(cap, 256) for cap in caps)
    ye_rows = [cap + COMB_MAX_ROUNDS * COMB_WIN for cap in caps]
    wspec = lambda r, c: pl.BlockSpec((1, 1, r, c), lambda ei: (layer, ei, 0, 0))
    return pl.pallas_call(
        functools.partial(_moe_ffn_kernel, caps=tuple(caps), rchunks=rchunks),
        grid=(e,),
        in_specs=[pl.BlockSpec((1, xe.shape[1], d), lambda ei: (ei, 0, 0)) for xe in xes]
                 + [wspec(d, f), wspec(d, f), wspec(f, d)],
        out_specs=[pl.BlockSpec((1, r, d), lambda ei: (ei, 0, 0)) for r in ye_rows],
        out_shape=[jax.ShapeDtypeStruct((e, r, d), BF16) for r in ye_rows],
        scratch_shapes=[pltpu.VMEM((d, f), BF16), pltpu.VMEM((d, f), BF16), pltpu.VMEM((f, d), BF16)],
        compiler_params=pltpu.CompilerParams(dimension_semantics=("arbitrary",),
                                             vmem_limit_bytes=56 * 1024 * 1024),
        name="moe_ffn",
    )(*xes, wg, wu, wd)


def _moe_window_copy(ye_hbm, buf, sem, slot, e, start):
    return pltpu.make_async_copy(ye_hbm.at[e, pl.ds(start, COMB_WIN), :], buf.at[slot, e], sem.at[slot, e])


def _moe_combine_kernel(offs_ref, rounds_ref, x_ref, g_ref, pos_ref, mod_ref, fin_ref, ye_hbm, o_ref,
                        buf, lhs, acc_ref, sem, *, n_tiles, final):
    j = pl.program_id(0)
    t = x_ref.shape[0]
    n_exp = g_ref.shape[1]
    lane = lax.broadcasted_iota(I32, (t, 2 * COMB_WIN), 1)
    low = lane < COMB_WIN
    slot_in_win = jnp.where(low, lane, lane - COMB_WIN)

    def starts_of(tile, rnd):
        return [pl.multiple_of((offs_ref[e * (n_tiles + 1) + tile] // 16) * 16 + rnd * COMB_WIN, 16)
                for e in range(n_exp)]

    def fetch(slot, starts):
        for e in range(n_exp):
            _moe_window_copy(ye_hbm, buf, sem, slot, e, starts[e]).start()

    def land(slot, starts):
        for e in range(n_exp):
            _moe_window_copy(ye_hbm, buf, sem, slot, e, starts[e]).wait()

    def expand(slot, starts):
        for e in range(0, n_exp, 2):
            pos = jnp.where(low, pos_ref[:, e:e + 1], pos_ref[:, e + 1:e + 2])
            gate = jnp.where(low, g_ref[:, e:e + 1], g_ref[:, e + 1:e + 2])
            first = jnp.where(low, starts[e], starts[e + 1])
            val = jnp.where(pos == slot_in_win + first, gate, 0.0)
            lhs[:, e * COMB_WIN:(e + 2) * COMB_WIN] = val.astype(BF16)
        return _dot(lhs[...], buf[slot].reshape(n_exp * COMB_WIN, buf.shape[3]))

    cur = j % 2
    first = starts_of(j, 0)

    @pl.when(j == 0)
    def _():
        fetch(0, first)

    @pl.when(j + 1 < n_tiles)
    def _():
        fetch(1 - cur, starts_of(j + 1, 0))

    land(cur, first)
    acc_ref[...] = expand(cur, first)

    def more(rnd, carry):
        starts = starts_of(j, rnd)
        fetch(2, starts)
        land(2, starts)
        acc_ref[...] += expand(2, starts)
        return carry

    lax.fori_loop(1, rounds_ref[j], more, 0)
    out = x_ref[...] + mod_ref[0, 5:6, :] * acc_ref[...]
    o_ref[...] = _rms(out) * fin_ref[...] if final else out


def _moe_combine(xall, g, pos, offs, rounds, mod, ye, tile0, n, mod_row, final_g=None):
    t = MOE_SUB
    d = xall.shape[1]
    n_exp = g.shape[1]
    n_tiles = n // t
    final = final_g is not None
    fin = final_g if final else jnp.ones((1, d), F32)
    grid_spec = pltpu.PrefetchScalarGridSpec(
        num_scalar_prefetch=2,
        grid=(n_tiles,),
        in_specs=[pl.BlockSpec((t, d), lambda j, offs, rounds: (j + tile0, 0)),
                  pl.BlockSpec((t, n_exp), lambda j, offs, rounds: (j, 0)),
                  pl.BlockSpec((t, n_exp), lambda j, offs, rounds: (j, 0)),
                  pl.BlockSpec((1, 6, d), lambda j, offs, rounds: (mod_row, 0, 0)),
                  pl.BlockSpec((1, d), lambda j, offs, rounds: (0, 0)),
                  pl.BlockSpec(memory_space=pl.ANY)],
        out_specs=pl.BlockSpec((t, d), lambda j, offs, rounds: (j + (0 if final else tile0), 0)),
        scratch_shapes=[pltpu.VMEM((3, n_exp, COMB_WIN, d), BF16),
                        pltpu.VMEM((t, n_exp * COMB_WIN), BF16),
                        pltpu.VMEM((t, d), F32), pltpu.SemaphoreType.DMA((3, n_exp))],
    )
    return pl.pallas_call(
        functools.partial(_moe_combine_kernel, n_tiles=n_tiles, final=final),
        grid_spec=grid_spec,
        out_shape=jax.ShapeDtypeStruct((n, d) if final else xall.shape, F32),
        input_output_aliases={} if final else {2: 0},
        compiler_params=_params("arbitrary"),
        name="moe_combine",
    )(offs, rounds, xall, g, pos, mod, fin, ye)


def _expert_choice(xall, h2, aff, mod, wg, wu, wd, layer, sets, final_g=None):
    routed = []
    for row0, h_row0, n, mod_row in sets:
        cap = EC_CAPACITY_FACTOR * n // N_EXPERTS
        g_t, pos_t, cum_t = _route(aff[row0:row0 + n].T, cap)
        offs = jnp.concatenate([cum_t[:, ::MOE_SUB], jnp.full((N_EXPERTS, 1), cap, I32)], axis=1)
        span = offs[:, 1:] - (offs[:, :-1] // 16) * 16
        rounds = jnp.maximum(jnp.max((span + COMB_WIN - 1) // COMB_WIN, axis=0), 1).astype(I32)
        offs = offs.reshape(-1)
        routed.append((cap, g_t, pos_t, offs, rounds, _moe_gather(h2, pos_t, offs, h_row0, n, cap)))
    yes = _moe_ffn([r[5] for r in routed], wg, wu, wd, layer, [r[0] for r in routed])
    for (row0, _, n, mod_row), (cap, g_t, pos_t, offs, rounds, _), ye in zip(sets, routed, yes):
        xall = _moe_combine(xall, g_t.T, pos_t.T, offs, rounds, mod, ye, row0 // MOE_SUB, n, 2 * layer + mod_row,
                            final_g)
    return xall


def kernel(x, c, ctx, c_ctx, w_mod, b_mod, norm1_g, norm2_g, w_in, s5_lam_re, s5_lam_im, s5_log_dt, s5_b_re, s5_b_im, s5_c_re, s5_c_im, s5_d, s5_w_glu, s5_b_glu, ssd_conv_w, ssd_conv_b, ssd_a_log, ssd_dt_bias, ssd_d, ssd_norm_g, attn_sink, w_branch, w_out, w_router, w_e_gate, w_e_up, w_e_down, final_norm_g):
    batch, n, d = x.shape
    nc = ctx.shape[1]
    depth = w_mod.shape[0]
    assert batch == 1 and nc == TOK_TILE and n % TOK_TILE == 0 and n % GRID_W == 0
    assert SSD_STATE == SSD_CHUNK
    xall = jnp.concatenate([ctx[0], x[0]], axis=0)
    cvecs = jnp.zeros((8, d), F32).at[0].set(c_ctx).at[1].set(c[0])
    mods = _modulation(cvecs, w_mod, b_mod)
    cos, sin = _rope_tables(n, nc)
    row = lambda v: v.reshape(1, -1).astype(F32)
    mods = mods[:, 0:2].reshape(depth * 2, 6, d)
    vec3 = lambda v: v.reshape(depth, 1, -1).astype(F32)
    w_in_all = _prep_w_in(w_in)
    s5 = _s5_prep(s5_lam_re, s5_lam_im, s5_log_dt, s5_b_re, s5_b_im, s5_c_re, s5_c_im)
    conv_w = jnp.pad(ssd_conv_w, ((0, 0), (0, 3), (0, 0))).reshape(depth * 8, -1)
    conv_b = vec3(ssd_conv_b)
    a = -jnp.exp(ssd_a_log.astype(F32))
    ab = jnp.stack([a, ssd_dt_bias.astype(F32)], axis=2)
    prow = jnp.pad(ab, ((0, 0), (0, 0), (0, 6), (0, LANE - SSD_HEADS))).reshape(depth * 2 * 8, LANE)
    pcol = jnp.pad(ab.transpose(0, 1, 3, 2), ((0, 0), (0, 0), (0, 0), (0, LANE - 2))).reshape(-1, LANE)
    vecs = (vec3(s5_d), vec3(s5_b_glu), vec3(jnp.repeat(ssd_d, SSD_HEAD_DIM, axis=1)), vec3(ssd_norm_g),
            vec3(norm2_g))
    wr = jnp.pad(w_router.astype(F32), ((0, 0), (0, 0), (0, LANE - N_EXPERTS)))
    wrh = wr.astype(BF16)
    wrl = (wr - wrh.astype(F32)).astype(BF16)
    wbr = w_branch.at[:, 2].set(w_branch[:, 2][:, _attn_head_order()]).astype(BF16)
    wbr = wbr.reshape(depth * 3, wbr.shape[2], wbr.shape[3])
    wglu, wout = s5_w_glu.astype(BF16), w_out.astype(BF16)
    sink = attn_sink.astype(F32)
    g1 = vec3(norm1_g)
    for i in range(depth):
        u, z, xbc, q, kv, gates, dtr = _inproj(xall, mods, g1, w_in_all, cos, sin, i)
        ys5 = _s5_mix(u, nc, s5, i)
        t = _ssd_conv(xbc, conv_w, conv_b, i)
        dtrt = dtr[:, 0:2 * SSD_HEADS].T
        yf = _ssd_scan(t, dtr, dtrt, prow, pcol, nc, 0, i)
        yb = _ssd_scan(t, dtr, dtrt, prow, pcol, nc, 1, i)
        att = _attention(q, kv, sink, nc, i)
        xall, h2, aff = _merge(xall, mods, ys5, u, yf, yb, t, z, att, gates, vecs, wglu, wbr, wout, wrh, wrl, i)
        sets = [(nc, 0, n, 1)] + ([(0, n, nc, 0)] if i < depth - 1 else [])
        xall = _expert_choice(xall, h2, aff, mods, w_e_gate, w_e_up, w_e_down, i, sets,
                              row(final_norm_g) if i == depth - 1 else None)
    return xall[None]
```

```python
import functools
import math

import jax
import jax.numpy as jnp
from jax import lax
from jax.experimental import pallas as pl
from jax.experimental.pallas import tpu as pltpu

F32 = jnp.float32
BF16 = jnp.bfloat16
I32 = jnp.int32
HI = lax.Precision.HIGHEST

D_MODEL = 1024
DEPTH = 4
GRID_W = 64
EPS = 1e-6
S5_GROUPS = 32
S5_GROUP_CH = 16
S5_STATE = 64
S5_CHUNK = 32
SSD_HEADS = 8
SSD_HEAD_DIM = 64
SSD_GROUPS = 2
SSD_STATE = 128
SSD_CHUNK = 128
ATTN_HEADS = 8
ATTN_KV_HEADS = 2
ATTN_HEAD_DIM = 64
ATTN_BLOCK = 128
ROPE_BASE = 10000.0
N_EXPERTS = 16
EXPERT_FF = 1024
EC_CAPACITY_FACTOR = 2
IN_SIZES = (512, 512, 1024, 16, 512, 128, 128, 3072)
TOK_TILE = 256
LANE = 128
NEG = -1e30


def _dot(a, b):
    return jnp.dot(a, b, preferred_element_type=F32)


def _dot_hi(a, b):
    return jnp.dot(a, b, precision=HI, preferred_element_type=F32)


def _dot_nt(a, b):
    return lax.dot_general(a, b, (((1,), (1,)), ((), ())), preferred_element_type=F32)


def _sigmoid(x):
    return 0.5 + 0.5 * jnp.tanh(0.5 * x)


def _softplus(x):
    return jnp.maximum(x, 0.0) + jnp.log(1.0 + jnp.exp(-jnp.abs(x)))


def _rms(x):
    return x * lax.rsqrt(jnp.mean(x * x, axis=-1, keepdims=True) + EPS)


def _params(*sem):
    return pltpu.CompilerParams(dimension_semantics=sem)


def _mod_kernel(c_ref, w_ref, b_ref, o_ref):
    c = c_ref[...]
    o_ref[0] = _dot_hi(c * _sigmoid(c), w_ref[0]) + b_ref[0]


def _modulation(cvecs, w_mod, b_mod):
    depth, d, d6 = w_mod.shape
    bn = 1536
    return pl.pallas_call(
        _mod_kernel,
        grid=(depth, d6 // bn),
        in_specs=[pl.BlockSpec((8, d), lambda l, j: (0, 0)),
                  pl.BlockSpec((1, d, bn), lambda l, j: (l, 0, j)),
                  pl.BlockSpec((1, 1, bn), lambda l, j: (l, 0, j))],
        out_specs=pl.BlockSpec((1, 8, bn), lambda l, j: (l, 0, j)),
        out_shape=jax.ShapeDtypeStruct((depth, 8, d6), F32),
        compiler_params=_params("arbitrary", "arbitrary"),
        name="modulation",
    )(cvecs, w_mod, b_mod.reshape(depth, 1, d6))


W_IN_COLS = 6016


def _attn_head_order():
    rep = ATTN_HEADS // ATTN_KV_HEADS
    heads = [g * rep + b for b in range(rep) for g in range(ATTN_KV_HEADS)]
    return jnp.concatenate([jnp.arange(h * ATTN_HEAD_DIM, (h + 1) * ATTN_HEAD_DIM) for h in heads])


def _prep_w_in(w):
    parts, start = [], 0
    for s in IN_SIZES:
        parts.append(w[..., start:start + s])
        start += s
    u, z, xbc, dt, q, k, v, gates = parts
    dt = jnp.pad(dt, ((0, 0), (0, 0), (0, LANE - dt.shape[-1])))
    q = q[..., _attn_head_order()]
    return jnp.concatenate([u, z, xbc, q, k, v, gates, dt], axis=-1).astype(BF16)


def _inproj_kernel(x_ref, mod_ref, g_ref, w_ref, cos_ref, sin_ref,
                   u_ref, z_ref, xbc_ref, q_ref, kv_ref, gates_ref, dt_ref):
    m = mod_ref[0]
    h = (_rms(x_ref[...]) * g_ref[...] * (1.0 + m[1:2]) + m[0:1]).astype(BF16)

    def proj(a, b):
        return _dot(h, w_ref[:, a:b])

    u_ref[...] = proj(0, 512)
    z_ref[...] = proj(512, 1024)
    xbc_ref[...] = proj(1024, 2048)
    cos = cos_ref[...]
    sin = sin_ref[...]
    lane = lax.broadcasted_iota(I32, cos.shape, 1)
    first = (lane % 32) < 16

    def rope(v):
        partner = jnp.where(first, pltpu.roll(v, LANE - 16, 1), pltpu.roll(v, 16, 1))
        return v * cos + partner * sin

    scale = ATTN_HEAD_DIM ** -0.5
    for j in range(2):
        qq = proj(2048 + 2 * j * LANE, 2048 + 2 * (j + 1) * LANE)
        for half in range(2):
            q_ref[:, (2 * j + half) * LANE:(2 * j + half + 1) * LANE] = (
                rope(qq[:, half * LANE:(half + 1) * LANE]) * scale).astype(BF16)
    kvp = proj(2560, 2816)
    kv_ref[:, 0:LANE] = rope(kvp[:, 0:LANE]).astype(BF16)
    kv_ref[:, LANE:2 * LANE] = kvp[:, LANE:2 * LANE].astype(BF16)
    for j in range(6):
        gates_ref[:, j * 512:(j + 1) * 512] = proj(2816 + j * 512, 2816 + (j + 1) * 512).astype(BF16)
    dt_ref[...] = proj(5888, 6016)


def _inproj(xall, mods, g, w, cos, sin, layer):
    n_all, d = xall.shape
    t = TOK_TILE
    row = lambda i: (i, 0)
    const = lambda i: (0, 0)
    widths = (512, 512, 1024, 512, 256, 3072, LANE)
    dtypes = (F32, F32, F32, BF16, BF16, BF16, F32)
    return pl.pallas_call(
        _inproj_kernel,
        grid=(n_all // t,),
        in_specs=[pl.BlockSpec((t, d), row),
                  pl.BlockSpec((1, 6, d), lambda i: (2 * layer + jnp.minimum(i, 1), 0, 0)),
                  pl.BlockSpec((None, 1, d), lambda i: (layer, 0, 0)),
                  pl.BlockSpec((None, d, W_IN_COLS), lambda i: (layer, 0, 0), pipeline_mode=pl.Buffered(1)),
                  pl.BlockSpec((t, LANE), row),
                  pl.BlockSpec((t, LANE), row)],
        out_specs=[pl.BlockSpec((t, wd), row) for wd in widths],
        out_shape=[jax.ShapeDtypeStruct((n_all, wd), dt) for wd, dt in zip(widths, dtypes)],
        compiler_params=_params("arbitrary"),
        name="inproj",
    )(xall, mods, g, w, cos, sin)


def _rope_tables(n, nc):
    rows = n // GRID_W
    r = jnp.repeat(jnp.arange(rows, dtype=F32), GRID_W)
    c = jnp.tile(jnp.arange(GRID_W, dtype=F32), rows)
    m = ATTN_HEAD_DIM // 4
    inv_freq = ROPE_BASE ** (-jnp.arange(m, dtype=F32) / m)
    ang_r, ang_c = r[:, None] * inv_freq, c[:, None] * inv_freq
    cos = jnp.concatenate([jnp.cos(ang_r), jnp.cos(ang_r), jnp.cos(ang_c), jnp.cos(ang_c)], axis=1)
    sin = jnp.concatenate([-jnp.sin(ang_r), jnp.sin(ang_r), -jnp.sin(ang_c), jnp.sin(ang_c)], axis=1)
    cos = jnp.concatenate([jnp.ones((nc, 64), F32), cos], axis=0)
    sin = jnp.concatenate([jnp.zeros((nc, 64), F32), sin], axis=0)
    return jnp.tile(cos, (1, 2)), jnp.tile(sin, (1, 2))


def _s5_toeplitz_kernel(bbf_ref, caf_ref, bbb_ref, cab_ref, kt_ref):
    L, K = S5_CHUNK, S5_GROUP_CH
    lk = L * K
    tf = _dot_hi(bbf_ref[0], caf_ref[0])
    tb = _dot_hi(bbb_ref[0], cab_ref[0])
    lane = lax.broadcasted_iota(I32, tf.shape, 1)
    r = pltpu.roll(tf, lk - K, 1)
    tab = jnp.concatenate([tb + jnp.where(lane >= lk - K, r, 0.0), jnp.where(lane < lk - K, r, 0.0)], axis=1)
    for s_ in range(L):
        lo = (L - 1 - s_) * K
        kt_ref[0, s_ * K:(s_ + 1) * K, :] = tab[:, lo:lo + L * K].astype(BF16)


def _s5_prep_layer(lam_re, lam_im, log_dt, b_re, b_im, c_re, c_im):
    L, G, P, K = S5_CHUNK, S5_GROUPS, S5_STATE, S5_GROUP_CH
    lk = L * K
    lr, li = lam_re.astype(F32), lam_im.astype(F32)
    dt = jnp.exp(log_dt.astype(F32))[..., None]
    mag = jnp.exp(lr * dt)
    abr, abi = mag * jnp.cos(li * dt), mag * jnp.sin(li * dt)
    den = lr * lr + li * li
    fr = ((abr - 1.0) * lr + abi * li) / den
    fi = (abi * lr - (abr - 1.0) * li) / den
    bbr = fr[..., None] * b_re - fi[..., None] * b_im
    bbi = fr[..., None] * b_im + fi[..., None] * b_re
    tau = jnp.arange(L + 1, dtype=F32)[:, None, None, None]
    pm = jnp.exp(lr * dt * tau)
    apr, api = pm * jnp.cos(li * dt * tau), pm * jnp.sin(li * dt * tau)
    cr, ci = c_re.astype(F32), c_im.astype(F32)
    car = cr * apr[:, :, :, None, :] - ci * api[:, :, :, None, :]
    cai = cr * api[:, :, :, None, :] + ci * apr[:, :, :, None, :]

    lagmat = lambda v, d: v[:L, d].transpose(1, 3, 0, 2)
    caf = jnp.concatenate([lagmat(car, 0), lagmat(cai, 0)], axis=1).reshape(G, 2 * P, lk)
    cab = jnp.flip(jnp.concatenate([lagmat(car, 1), lagmat(cai, 1)], axis=1), axis=2).reshape(G, 2 * P, lk)
    tr = lambda v: v.transpose(0, 2, 1)
    bbf = jnp.concatenate([tr(bbr[0]), -tr(bbi[0])], axis=2)
    bbb = jnp.concatenate([tr(bbr[1]), -tr(bbi[1])], axis=2)

    def summary(d, rev):
        ar, ai = apr[:L, d], api[:L, d]
        if rev:
            ar, ai = jnp.flip(ar, axis=0), jnp.flip(ai, axis=0)
        ar, ai = ar[..., None], ai[..., None]
        re = ar * bbr[d][None] - ai * bbi[d][None]
        im = ar * bbi[d][None] + ai * bbr[d][None]
        f = lambda v: v.transpose(1, 0, 3, 2).reshape(G, lk, P)
        return f(re), f(im)

    sfr, sfi = summary(0, True)
    sbr, sbi = summary(1, False)
    sb = jnp.concatenate([sfr, sfi, sbr, sbi], axis=2).astype(BF16)

    def readout(v):
        return v.transpose(1, 3, 0, 2).reshape(G, P, lk)

    back = lambda v: jnp.flip(v[1:L + 1, 1], axis=0)
    rc = jnp.concatenate([readout(car[1:L + 1, 0]), -readout(cai[1:L + 1, 0]),
                          readout(back(car)), -readout(back(cai))], axis=1).astype(BF16)
    coef = jnp.stack([apr[L, 0].reshape(-1), api[L, 0].reshape(-1),
                      apr[L, 1].reshape(-1), api[L, 1].reshape(-1)], axis=0)
    coef = jnp.pad(coef, ((0, 4), (0, 0)))
    return bbf, caf, bbb, cab, sb, rc, coef


def _s5_prep(*params):
    L, G, P, K = S5_CHUNK, S5_GROUPS, S5_STATE, S5_GROUP_CH
    lk = L * K
    bbf, caf, bbb, cab, sb, rc, coef = jax.vmap(_s5_prep_layer)(*params)
    flat = lambda v: v.reshape((-1,) + v.shape[2:])
    n_mat = bbf.shape[0] * G
    bspec = pl.BlockSpec((1, K, 2 * P), lambda g: (g, 0, 0))
    cspec = pl.BlockSpec((1, 2 * P, lk), lambda g: (g, 0, 0))
    kt = pl.pallas_call(
        _s5_toeplitz_kernel,
        grid=(n_mat,),
        in_specs=[bspec, cspec, bspec, cspec],
        out_specs=pl.BlockSpec((1, lk, lk), lambda g: (g, 0, 0)),
        out_shape=jax.ShapeDtypeStruct((n_mat, lk, lk), BF16),
        compiler_params=_params("arbitrary"),
        name="s5_toeplitz",
    )(flat(bbf), flat(caf), flat(bbb), flat(cab))
    return kt, flat(sb), flat(rc), flat(coef)


def _s5_states_kernel(u_ref, sb_ref, fre, fim, bre, bim):
    p = S5_STATE
    s0 = _dot(u_ref[0], sb_ref[0])
    s1 = _dot(u_ref[1], sb_ref[1])
    for q, ref in enumerate((fre, fim, bre, bim)):
        ref[...] = jnp.concatenate([s0[:, q * p:(q + 1) * p], s1[:, q * p:(q + 1) * p]], axis=1)


def _s5_rec_kernel(coef_ref, sfr, sfi, sbr, sbi, hfr, hfi, hbr, hbi, *, n_chunks, ctx_chunks):
    arf, aif = coef_ref[0:1, :], coef_ref[1:2, :]
    arb, aib = coef_ref[2:3, :], coef_ref[3:4, :]
    zero = jnp.zeros_like(arf)

    def fstep(c, carry):
        hr, hi = carry
        hfr[pl.ds(c, 1), :] = hr
        hfi[pl.ds(c, 1), :] = hi
        return (arf * hr - aif * hi + sfr[pl.ds(c, 1), :], arf * hi + aif * hr + sfi[pl.ds(c, 1), :])

    lax.fori_loop(0, n_chunks, fstep, (zero, zero))

    def bstep(i, carry):
        c = jnp.where(i < ctx_chunks, ctx_chunks - 1 - i, n_chunks - 1 - (i - ctx_chunks))
        hr, hi = carry
        hbr[pl.ds(c, 1), :] = hr
        hbi[pl.ds(c, 1), :] = hi
        return (arb * hr - aib * hi + sbr[pl.ds(c, 1), :], arb * hi + aib * hr + sbi[pl.ds(c, 1), :])

    lax.fori_loop(0, n_chunks, bstep, (zero, zero))


def _s5_out_kernel(u_ref, kt_ref, hfr, hfi, hbr, hbi, rc_ref, y_ref):
    p = S5_STATE
    for i in range(2):
        h = jnp.concatenate([r[:, i * p:(i + 1) * p] for r in (hfr, hfi, hbr, hbi)], axis=1).astype(BF16)
        y_ref[i] = (_dot(u_ref[i], kt_ref[i]) + _dot(h, rc_ref[i])).astype(y_ref.dtype)


S5_LANE_GROUPS = LANE // S5_GROUP_CH
S5_T_LO = 8


def _s5_perm():
    i = jnp.arange(S5_T_LO * LANE)
    t_lo, gl, k = i // LANE, (i % LANE) // S5_GROUP_CH, i % S5_GROUP_CH
    j = gl * (S5_T_LO * S5_GROUP_CH) + t_lo * S5_GROUP_CH + k
    return jnp.zeros((S5_T_LO * LANE, S5_T_LO * LANE), BF16).at[i, j].set(1.0)


def _s5_pack_kernel(u_ref, perm_ref, o_ref, *, n_chunks):
    L = S5_CHUNK
    for t_hi in range(L // S5_T_LO):
        z = jnp.concatenate([u_ref[pl.ds(t_hi * S5_T_LO + t_lo, n_chunks, stride=L), :].astype(BF16)
                             for t_lo in range(S5_T_LO)], axis=1)
        w = _dot(z, perm_ref[...]).astype(BF16)
        for gl in range(S5_LANE_GROUPS):
            o_ref[gl, :, t_hi * LANE:(t_hi + 1) * LANE] = w[:, gl * LANE:(gl + 1) * LANE]


def _s5_unpack_kernel(y_ref, perm_ref, o_ref, *, n_chunks):
    L = S5_CHUNK
    for t_hi in range(L // S5_T_LO):
        w = jnp.concatenate([y_ref[gl, :, t_hi * LANE:(t_hi + 1) * LANE] for gl in range(S5_LANE_GROUPS)], axis=1)
        z = _dot_nt(w, perm_ref[...])
        for t_lo in range(S5_T_LO):
            o_ref[pl.ds(t_hi * S5_T_LO + t_lo, n_chunks, stride=L), :] = z[:, t_lo * LANE:(t_lo + 1) * LANE]


def _s5_mix(u, nc, prep, layer):
    kt, sb, rc, coef = prep
    n_all = u.shape[0]
    L, G, K, P = S5_CHUNK, S5_GROUPS, S5_GROUP_CH, S5_STATE
    C = n_all // L
    lk = L * K
    perm = _s5_perm()
    lg = S5_LANE_GROUPS
    perm_spec = pl.BlockSpec(perm.shape, lambda b: (0, 0))
    ug = pl.pallas_call(
        functools.partial(_s5_pack_kernel, n_chunks=C),
        grid=(G // lg,),
        in_specs=[pl.BlockSpec((n_all, LANE), lambda b: (0, b)), perm_spec],
        out_specs=pl.BlockSpec((lg, C, lk), lambda b: (b, 0, 0)),
        out_shape=jax.ShapeDtypeStruct((G, C, lk), BF16),
        compiler_params=_params("arbitrary"),
        name="s5_pack",
    )(u, perm)
    gp = G // 2
    st_shape = jax.ShapeDtypeStruct((C, G * P), F32)
    st_spec = pl.BlockSpec((C, 2 * P), lambda p: (0, p))
    states = pl.pallas_call(
        _s5_states_kernel,
        grid=(gp,),
        in_specs=[pl.BlockSpec((2, C, lk), lambda p: (p, 0, 0)),
                  pl.BlockSpec((2, lk, 4 * P), lambda p: (layer * gp + p, 0, 0))],
        out_specs=[st_spec] * 4,
        out_shape=[st_shape] * 4,
        compiler_params=_params("arbitrary"),
        name="s5_states",
    )(ug, sb)
    cb = 512
    col = pl.BlockSpec((C, cb), lambda j: (0, j))
    hs = pl.pallas_call(
        functools.partial(_s5_rec_kernel, n_chunks=C, ctx_chunks=nc // L),
        grid=(G * P // cb,),
        in_specs=[pl.BlockSpec((8, cb), lambda j: (layer, j))] + [col] * 4,
        out_specs=[col] * 4,
        out_shape=[st_shape] * 4,
        compiler_params=_params("arbitrary"),
        name="s5_recurrence",
    )(coef, *states)
    y = pl.pallas_call(
        _s5_out_kernel,
        grid=(gp,),
        in_specs=[pl.BlockSpec((2, C, lk), lambda p: (p, 0, 0)),
                  pl.BlockSpec((2, lk, lk), lambda p: (layer * gp + p, 0, 0))] + [st_spec] * 4
                 + [pl.BlockSpec((2, 4 * P, lk), lambda p: (layer * gp + p, 0, 0))],
        out_specs=pl.BlockSpec((2, C, lk), lambda p: (p, 0, 0)),
        out_shape=jax.ShapeDtypeStruct((G, C, lk), BF16),
        compiler_params=_params("arbitrary"),
        name="s5_out",
    )(ug, kt, *hs, rc)
    return pl.pallas_call(
        functools.partial(_s5_unpack_kernel, n_chunks=C),
        grid=(G // lg,),
        in_specs=[pl.BlockSpec((lg, C, lk), lambda b: (b, 0, 0)), perm_spec],
        out_specs=pl.BlockSpec((n_all, LANE), lambda b: (0, b)),
        out_shape=jax.ShapeDtypeStruct((n_all, G * K), F32),
        compiler_params=_params("arbitrary"),
        name="s5_unpack",
    )(y, perm)


def _conv_kernel(cur_ref, prev_ref, next_ref, w_ref, b_ref, o_ref, *, n_tiles):
    i = pl.program_id(0)
    cur = cur_ref[...]
    t = cur.shape[0]
    pv = prev_ref[...] * jnp.where(i >= 2, 1.0, 0.0)
    nx = next_ref[...] * jnp.where((i >= 1) & (i <= n_tiles - 2), 1.0, 0.0)
    row8 = lax.broadcasted_iota(I32, pv.shape, 0)
    acc = b_ref[...] + w_ref[2:3, :] * cur
    for s in (1, 2):
        r = pltpu.roll(cur, s, 0)
        head = jnp.where(row8 < s, pltpu.roll(pv, s, 0), r[0:8])
        acc = acc + w_ref[2 - s:3 - s, :] * jnp.concatenate([head, r[8:]], axis=0)
        r = pltpu.roll(cur, t - s, 0)
        tail = jnp.where(row8 >= 8 - s, pltpu.roll(nx, 8 - s, 0), r[t - 8:])
        acc = acc + w_ref[2 + s:3 + s, :] * jnp.concatenate([r[:t - 8], tail], axis=0)
    o_ref[...] = acc * _sigmoid(acc)


def _ssd_conv(xbc, conv_w, conv_b, layer):
    n_all, ch = xbc.shape
    t = TOK_TILE
    n_tiles = n_all // t
    per = t // 8
    return pl.pallas_call(
        functools.partial(_conv_kernel, n_tiles=n_tiles),
        grid=(n_tiles,),
        in_specs=[pl.BlockSpec((t, ch), lambda i: (i, 0)),
                  pl.BlockSpec((8, ch), lambda i: (jnp.maximum(i * per - 1, 0), 0)),
                  pl.BlockSpec((8, ch), lambda i: (jnp.minimum((i + 1) * per, n_tiles * per - 1), 0)),
                  pl.BlockSpec((8, ch), lambda i: (layer, 0)),
                  pl.BlockSpec((None, 1, ch), lambda i: (layer, 0, 0))],
        out_specs=pl.BlockSpec((t, ch), lambda i: (i, 0)),
        out_shape=jax.ShapeDtypeStruct((n_all, ch), F32),
        compiler_params=_params("arbitrary"),
        name="ssd_conv",
    )(xbc, xbc, xbc, conv_w, conv_b)


def _ssd_kernel(t_ref, dtr_ref, dtrt_ref, prow_ref, pcol_ref, y_ref, h_ref, *, reverse, d):
    q = SSD_CHUNK
    hpg = SSD_HEADS // SSD_GROUPS
    p = SSD_HEAD_DIM

    @pl.when(pl.program_id(0) == 0)
    def _():
        h_ref[...] = jnp.zeros_like(h_ref)

    lo = d * SSD_HEADS
    dt_c = _softplus(dtr_ref[:, lo:lo + SSD_HEADS] + prow_ref[1:2, 0:SSD_HEADS])
    adt_c = dt_c * prow_ref[0:1, 0:SSD_HEADS]
    dt_r = _softplus(dtrt_ref[lo:lo + SSD_HEADS, :] + pcol_ref[:, 1:2])
    adt_r = dt_r * pcol_ref[:, 0:1]
    ii = lax.broadcasted_iota(I32, (q, q), 0)
    jj = lax.broadcasted_iota(I32, (q, q), 1)
    causal = (jj >= ii) if reverse else (jj <= ii)
    acum_c = _dot_hi(causal.astype(F32), adt_c)
    acum_r = _dot_hi(adt_r, ((ii >= jj) if reverse else (ii <= jj)).astype(F32))
    tot = acum_c[0:1, :] if reverse else acum_c[q - 1:q, :]
    gw = hpg * p
    lane_head = lax.broadcasted_iota(I32, (q, gw), 1) // p
    lane_head_row = lax.broadcasted_iota(I32, (1, gw), 1) // p
    for g in range(SSD_GROUPS):
        bg = t_ref[:, 512 + g * SSD_STATE:512 + (g + 1) * SSD_STATE]
        cg = t_ref[:, 768 + g * SSD_STATE:768 + (g + 1) * SSD_STATE]
        cb = _dot_nt(cg.astype(BF16), bg.astype(BF16))
        bgt = bg.T
        xg = t_ref[:, g * gw:(g + 1) * gw].astype(BF16)
        hg = h_ref[g]
        xh = jnp.concatenate([xg, hg.astype(BF16)], axis=0)
        yg = jnp.zeros((q, gw), F32)
        sg = jnp.zeros((SSD_STATE, gw), F32)
        dec = jnp.zeros((1, gw), F32)
        for r in range(hpg):
            hd = g * hpg + r
            ac = jnp.broadcast_to(acum_c[:, hd:hd + 1], (q, q))
            ar = acum_r[hd:hd + 1, :]
            dtrow = dt_r[hd:hd + 1, :]
            th = tot[:, hd:hd + 1]
            wts = (cb * jnp.exp(jnp.where(causal, ac - ar, NEG)) * dtrow).astype(BF16)
            cs = (cg * jnp.exp(ac)).astype(BF16)
            bs = (bgt * (jnp.exp(th - ar) * dtrow)).astype(BF16)
            yh = _dot(jnp.concatenate([wts, cs], axis=1), xh)
            sh = _dot(bs, xg)
            yg = jnp.where(lane_head == r, yh, yg)
            sg = jnp.where(lane_head[:SSD_STATE] == r, sh, sg)
            dec = jnp.where(lane_head_row == r, jnp.exp(th), dec)
        y_ref[:, g * gw:(g + 1) * gw] = yg
        h_ref[g] = hg * dec + sg


def _ssd_scan(t, dtr, dtrt, prow, pcol, nc, d, layer):
    n_all = t.shape[0]
    q = SSD_CHUNK
    n_chunks = n_all // q
    cc = nc // q
    reverse = d == 1
    if reverse:
        order = lambda i: jnp.where(i < cc, cc - 1 - i, n_chunks - 1 - (i - cc))
    else:
        order = lambda i: i
    return pl.pallas_call(
        functools.partial(_ssd_kernel, reverse=reverse, d=d),
        grid=(n_chunks,),
        in_specs=[pl.BlockSpec((q, t.shape[1]), lambda i: (order(i), 0)),
                  pl.BlockSpec((q, LANE), lambda i: (order(i), 0)),
                  pl.BlockSpec((2 * SSD_HEADS, q), lambda i: (0, order(i))),
                  pl.BlockSpec((8, LANE), lambda i: (2 * layer + d, 0)),
                  pl.BlockSpec((SSD_HEADS, LANE), lambda i: (2 * layer + d, 0))],
        out_specs=pl.BlockSpec((q, SSD_HEADS * SSD_HEAD_DIM), lambda i: (order(i), 0)),
        out_shape=jax.ShapeDtypeStruct((n_all, SSD_HEADS * SSD_HEAD_DIM), F32),
        scratch_shapes=[pltpu.VMEM((SSD_GROUPS, SSD_STATE, (SSD_HEADS // SSD_GROUPS) * SSD_HEAD_DIM), F32)],
        compiler_params=_params("arbitrary"),
        name="ssd_scan_bwd" if reverse else "ssd_scan_fwd",
    )(t, dtr, dtrt, prow, pcol)


def _attn_kernel(sink_ref, q_ref, kp_ref, ko_ref, kn_ref, kc_ref, bias_ref, o_ref, *, n_blocks, ctx_blocks, layer):
    c = pl.program_id(0)
    blk = ATTN_BLOCK
    dh = ATTN_HEAD_DIM
    rep = ATTN_HEADS // ATTN_KV_HEADS
    rows = rep * blk
    kw = ATTN_KV_HEADS * dh
    lat = c >= ctx_blocks
    lo = jnp.where(lat & (c > ctx_blocks), 0, blk)
    hi = jnp.where(lat, jnp.where(c < n_blocks - 1, 3 * blk, 2 * blk), 0)
    col = lax.broadcasted_iota(I32, (1, 3 * blk), 1)
    bias = bias_ref[...] + jnp.where((col >= lo) & (col < hi), 0.0, NEG)
    rowhead = lax.broadcasted_iota(I32, (rows, 1), 0) // blk
    lane_q = lax.broadcasted_iota(I32, (blk, kw), 1)
    kloc = jnp.concatenate([kp_ref[:, 0:kw], ko_ref[:, 0:kw], kn_ref[:, 0:kw]], axis=0)
    vloc = jnp.concatenate([kp_ref[:, kw:2 * kw], ko_ref[:, kw:2 * kw], kn_ref[:, kw:2 * kw]], axis=0)
    kctx = kc_ref[:, 0:kw]
    vctx = kc_ref[:, kw:2 * kw]
    lane_l = lax.broadcasted_iota(I32, vloc.shape, 1)
    lane_c = lax.broadcasted_iota(I32, vctx.shape, 1)
    one = jnp.ones((), BF16)
    outs = []
    for g in range(ATTN_KV_HEADS):
        own = (lane_q >= g * dh) & (lane_q < (g + 1) * dh)
        vl = jnp.where((lane_l >= g * dh) & (lane_l < (g + 1) * dh), vloc, one)
        vc = jnp.where((lane_c >= g * dh) & (lane_c < (g + 1) * dh), vctx, one)
        qg = jnp.concatenate([jnp.where(own, q_ref[:, b * kw:(b + 1) * kw], jnp.zeros((), BF16))
                              for b in range(rep)], axis=0)
        s_loc = _dot_nt(qg, kloc) + bias
        s_ctx = _dot_nt(qg, kctx)
        sink = jnp.zeros((rows, 1), F32)
        for r in range(rep):
            sink = jnp.where(rowhead == r, sink_ref[layer, g * rep + r], sink)
        mx = jnp.maximum(jnp.maximum(jnp.max(s_loc, axis=1, keepdims=True),
                                     jnp.max(s_ctx, axis=1, keepdims=True)), sink)
        p_loc = jnp.exp((s_loc - mx).astype(BF16))
        p_ctx = jnp.exp((s_ctx - mx).astype(BF16))
        pv = _dot(p_loc, vl) + _dot(p_ctx, vc)
        outs.append(pv / (pltpu.roll(pv, dh, 1) + jnp.exp(sink - mx)))
    for b in range(rep):
        o_ref[:, b * kw:(b + 1) * kw] = jnp.where(lane_q < dh, outs[0][b * blk:(b + 1) * blk],
                                                  outs[1][b * blk:(b + 1) * blk])


def _attention(q, kv, sink, nc, layer):
    n_all = q.shape[0]
    blk = ATTN_BLOCK
    n_blocks = n_all // blk
    cbk = nc // blk
    kvw = kv.shape[1]
    rows = (ATTN_HEADS // ATTN_KV_HEADS) * blk
    qi = jnp.arange(rows)[:, None] % blk
    kj = jnp.arange(3 * blk)[None, :] - blk
    band = jnp.where(jnp.abs(qi - kj) <= blk, 0.0, NEG).astype(F32)
    return pl.pallas_call(
        functools.partial(_attn_kernel, n_blocks=n_blocks, ctx_blocks=cbk, layer=layer),
        grid=(n_blocks,),
        in_specs=[pl.BlockSpec(memory_space=pltpu.SMEM),
                  pl.BlockSpec((blk, q.shape[1]), lambda c: (c, 0)),
                  pl.BlockSpec((blk, kvw), lambda c: (jnp.maximum(c - 1, 0), 0)),
                  pl.BlockSpec((blk, kvw), lambda c: (c, 0)),
                  pl.BlockSpec((blk, kvw), lambda c: (jnp.minimum(c + 1, n_blocks - 1), 0)),
                  pl.BlockSpec((nc, kvw), lambda c: (0, 0)),
                  pl.BlockSpec((rows, 3 * blk), lambda c: (0, 0))],
        out_specs=pl.BlockSpec((blk, q.shape[1]), lambda c: (c, 0)),
        out_shape=jax.ShapeDtypeStruct((n_all, q.shape[1]), F32),
        compiler_params=_params("arbitrary"),
        name="window_attention",
    )(sink, q, kv, kv, kv, kv, band)


def _merge_kernel(x_ref, mod_ref, ys5_ref, u_ref, yf_ref, yb_ref, xs_ref, z_ref, att_ref, gates_ref,
                  s5d_ref, bglu_ref, ssdd_ref, ssdg_ref, n2g_ref, wglu_ref, wbr_ref, wout_ref, wr2_ref,
                  xo_ref, h2_ref, aff_ref, br_ref):
    m = mod_ref[0]
    d = x_ref.shape[1]
    a = jax.nn.gelu(ys5_ref[...].astype(F32) + s5d_ref[...] * u_ref[...])
    ya = (a * _sigmoid(_dot(a.astype(BF16), wglu_ref[...]) + bglu_ref[...])).astype(BF16)
    z = z_ref[...]
    yz = (yf_ref[...] + yb_ref[...] + ssdd_ref[...] * xs_ref[...]) * (z * _sigmoid(z))
    yb = (_rms(yz) * ssdg_ref[...]).astype(BF16)
    yc = att_ref[...].astype(BF16)
    cw = 256
    for j in range(d // cw):
        cs = slice(j * cw, (j + 1) * cw)
        br = (_sigmoid(gates_ref[:, j * cw:(j + 1) * cw].astype(F32)) * _dot(ya, wbr_ref[0, :, cs])
              + _sigmoid(gates_ref[:, d + j * cw:d + (j + 1) * cw].astype(F32)) * _dot(yb, wbr_ref[1, :, cs])
              + _sigmoid(gates_ref[:, 2 * d + j * cw:2 * d + (j + 1) * cw].astype(F32)) * _dot(yc, wbr_ref[2, :, cs]))
        br_ref[:, cs] = br.astype(BF16)
    xn = x_ref[...] + m[2:3] * _dot(br_ref[...], wout_ref[...])
    xo_ref[...] = xn
    h2 = _rms(xn) * n2g_ref[...] * (1.0 + m[4:5]) + m[3:4]
    hi = h2.astype(BF16)
    h2_ref[...] = hi
    lo = (h2 - hi.astype(F32)).astype(BF16)
    both = _dot(hi, wr2_ref[...])
    logits = (both[:, 0:LANE] + both[:, LANE:2 * LANE] + _dot(lo, wr2_ref[:, 0:LANE]))[:, 0:N_EXPERTS]
    e = jnp.exp(logits - jnp.max(logits, axis=1, keepdims=True))
    aff_ref[...] = e / jnp.sum(e, axis=1, keepdims=True)


def _merge(xall, mods, ys5, u, yf, yb, t, z, att, gates, vecs, wglu, wbr, wout, wr2, layer):
    n_all, d = xall.shape
    tt = TOK_TILE
    row = lambda i: (i, 0)
    lay3 = lambda i: (layer, 0, 0)
    vec = lambda wd: pl.BlockSpec((None, 1, wd), lay3)
    bw = 512
    s5d, bglu, ssdd, ssdg, n2g = vecs
    return pl.pallas_call(
        _merge_kernel,
        grid=(n_all // tt,),
        in_specs=[pl.BlockSpec((tt, d), row),
                  pl.BlockSpec((1, 6, d), lambda i: (2 * layer + jnp.minimum(i, 1), 0, 0)),
                  pl.BlockSpec((tt, bw), row), pl.BlockSpec((tt, bw), row), pl.BlockSpec((tt, bw), row),
                  pl.BlockSpec((tt, bw), row), pl.BlockSpec((tt, bw), row), pl.BlockSpec((tt, bw), row),
                  pl.BlockSpec((tt, bw), row), pl.BlockSpec((tt, 3 * d), row),
                  vec(bw), vec(bw), vec(bw), vec(bw), vec(d),
                  pl.BlockSpec((None, bw, bw), lay3),
                  pl.BlockSpec((3, bw, d), lay3),
                  pl.BlockSpec((None, d, d), lay3),
                  pl.BlockSpec((None, d, 2 * LANE), lay3)],
        out_specs=[pl.BlockSpec((tt, d), row),
                   pl.BlockSpec((tt, d), lambda i: (jnp.where(i == 0, n_all // tt - 1, i - 1), 0)),
                   pl.BlockSpec((tt, N_EXPERTS), row)],
        out_shape=[jax.ShapeDtypeStruct((n_all, d), F32), jax.ShapeDtypeStruct((n_all, d), BF16),
                   jax.ShapeDtypeStruct((n_all, N_EXPERTS), F32)],
        scratch_shapes=[pltpu.VMEM((tt, d), BF16)],
        compiler_params=_params("arbitrary"),
        name="merge_router",
    )(xall, mods, ys5, u, yf, yb, t, z, att, gates, s5d, bglu, ssdd, ssdg, n2g, wglu, wbr, wout, wr2)


def _route_kernel(a_ref, g_ref, pos_ref, cum_ref, *, cap):
    n = a_ref.shape[1]
    e = a_ref.shape[0]
    bits = lax.bitcast_convert_type(a_ref[...], I32)
    capf = float(cap)

    def search(i, thr):
        cand = thr | (1 << (30 - i))
        cnt = jnp.sum((bits >= cand).astype(F32), axis=1, keepdims=True)
        return jnp.where(cnt >= capf, cand, thr)

    thr = lax.fori_loop(0, 31, search, jnp.zeros((e, 1), I32))
    need = capf - jnp.sum((bits > thr).astype(F32), axis=1, keepdims=True)
    ii = lax.broadcasted_iota(I32, (LANE, LANE), 0)
    jj = lax.broadcasted_iota(I32, (LANE, LANE), 1)
    upper = (ii < jj).astype(BF16)

    def block(b, carry):
        eq_off, pos_off = carry
        sl = pl.ds(pl.multiple_of(b * LANE, LANE), LANE)
        a = a_ref[:, sl]
        v = lax.bitcast_convert_type(a, I32)
        eq = v == thr
        eqf = eq.astype(BF16)
        rank = _dot(eqf, upper) + eq_off
        sel = (v > thr) | (eq & (rank < need))
        self_ = sel.astype(BF16)
        pos = _dot(self_, upper) + pos_off
        g_ref[:, sl] = jnp.where(sel, a, 0.0)
        pos_ref[:, sl] = jnp.where(sel, pos, -1.0).astype(I32)
        cum_ref[:, sl] = pos.astype(I32)
        return (eq_off + jnp.sum(eqf.astype(F32), axis=1, keepdims=True),
                pos_off + jnp.sum(self_.astype(F32), axis=1, keepdims=True))

    zero = jnp.zeros((e, 1), F32)
    lax.fori_loop(0, n // LANE, block, (zero, zero))


def _route(aff_t, cap):
    e, n = aff_t.shape
    return pl.pallas_call(
        functools.partial(_route_kernel, cap=cap),
        out_shape=[jax.ShapeDtypeStruct((e, n), F32), jax.ShapeDtypeStruct((e, n), I32),
                   jax.ShapeDtypeStruct((e, n), I32)],
        name="ec_route",
    )(aff_t)


MOE_BLOCK = 1024
MOE_SUB = 256
MOE_WIN = 64
COMB_WIN = 64
COMB_MAX_ROUNDS = 5


MOE_PASS = 4


def _moe_gather_kernel(offs_ref, h_ref, pos_ref, xe_ref, *, n_sub):
    pss = pl.program_id(0)
    b = pl.program_id(1)
    subs = h_ref.shape[0] // MOE_SUB

    @pl.when(b == 0)
    def _():
        xe_ref[...] = jnp.zeros_like(xe_ref)

    def window(k, s, a, m):
        e = pss * MOE_PASS + k
        pos = pos_ref[pl.ds(e, 1), s * MOE_SUB:(s + 1) * MOE_SUB]
        r0 = pl.multiple_of(a + m * MOE_WIN, 16)
        slot = lax.broadcasted_iota(I32, (MOE_WIN, MOE_SUB), 0) + r0
        sel = jnp.where(slot == pos, 1.0, 0.0).astype(BF16)
        rows = pl.ds(r0, MOE_WIN)
        xe_ref[k, rows, :] = (xe_ref[k, rows, :].astype(F32)
                              + _dot(sel, h_ref[s * MOE_SUB:(s + 1) * MOE_SUB, :])).astype(BF16)

    extra = []
    slot0 = lax.broadcasted_iota(I32, (MOE_WIN, MOE_SUB), 0)
    for s in range(subs):
        sels, starts = [], []
        for k in range(MOE_PASS):
            base = (pss * MOE_PASS + k) * (n_sub + 1) + b * subs + s
            o = offs_ref[base]
            o2 = offs_ref[base + 1]
            a = pl.multiple_of((o // 16) * 16, 16)
            starts.append(a)
            extra.append((k, s, a, jnp.where(o2 > o, (o2 - a + MOE_WIN - 1) // MOE_WIN, 0)))
            pos = pos_ref[pl.ds(pss * MOE_PASS + k, 1), s * MOE_SUB:(s + 1) * MOE_SUB]
            sels.append(jnp.where(slot0 + a == pos, 1.0, 0.0).astype(BF16))
        got = _dot(jnp.concatenate(sels, axis=0), h_ref[s * MOE_SUB:(s + 1) * MOE_SUB, :])
        for k in range(MOE_PASS):
            rows = pl.ds(starts[k], MOE_WIN)
            xe_ref[k, rows, :] = (xe_ref[k, rows, :].astype(F32)
                                  + got[k * MOE_WIN:(k + 1) * MOE_WIN]).astype(BF16)
    for k, s, a, cnt in extra:
        lax.fori_loop(1, cnt, lambda m, c, k=k, s=s, a=a: (window(k, s, a, m), c)[1], 0)


def _moe_gather(h2, pos_t, offs, h_row0, n, cap):
    d = h2.shape[1]
    e = pos_t.shape[0]
    tb = min(MOE_BLOCK, n)
    n_blocks = n // tb
    blk0 = h_row0 // tb
    grid_spec = pltpu.PrefetchScalarGridSpec(
        num_scalar_prefetch=1,
        grid=(e // MOE_PASS, n_blocks),
        in_specs=[pl.BlockSpec((tb, d), lambda p, b, offs: (b + blk0, 0)),
                  pl.BlockSpec((e, tb), lambda p, b, offs: (0, b))],
        out_specs=pl.BlockSpec((MOE_PASS, cap + MOE_WIN, d), lambda p, b, offs: (p, 0, 0)),
    )
    return pl.pallas_call(
        functools.partial(_moe_gather_kernel, n_sub=n // MOE_SUB),
        grid_spec=grid_spec,
        out_shape=jax.ShapeDtypeStruct((e, cap + MOE_WIN, d), BF16),
        compiler_params=pltpu.CompilerParams(dimension_semantics=("arbitrary", "arbitrary"),
                                             vmem_limit_bytes=56 * 1024 * 1024),
        name="moe_gather",
    )(offs, h2, pos_t)


def _moe_ffn_kernel(*refs, caps, rchunks):
    ns = len(caps)
    xes, (wg_ref, wu_ref, wd_ref) = refs[:ns], refs[ns:ns + 3]
    yes, (wgb, wub, wdb) = refs[ns + 3:2 * ns + 3], refs[2 * ns + 3:]
    slab = 256

    def cast(i, carry):
        rows = pl.ds(pl.multiple_of(i * slab, slab), slab)
        wgb[rows, :] = wg_ref[0, 0, rows, :].astype(BF16)
        wub[rows, :] = wu_ref[0, 0, rows, :].astype(BF16)
        wdb[rows, :] = wd_ref[0, 0, rows, :].astype(BF16)
        return carry

    lax.fori_loop(0, wgb.shape[0] // slab, cast, 0)
    for xe_ref, ye_ref, cap, rchunk in zip(xes, yes, caps, rchunks):
        def chunk(ci, carry, xe_ref=xe_ref, ye_ref=ye_ref, rchunk=rchunk):
            rows = pl.ds(pl.multiple_of(ci * rchunk, rchunk), rchunk)
            xb = xe_ref[0, rows, :]
            hg = _dot(xb, wgb[...])
            hid = (hg * _sigmoid(hg) * _dot(xb, wub[...])).astype(BF16)
            ye_ref[0, rows, :] = _dot(hid, wdb[...]).astype(BF16)
            return carry

        lax.fori_loop(0, cap // rchunk, chunk, 0)
        ye_ref[0, cap:, :] = jnp.zeros((ye_ref.shape[1] - cap, ye_ref.shape[2]), BF16)


def _moe_ffn(xes, wg, wu, wd, layer, caps):
    e, _, d = xes[0].shape
    f = wg.shape[3]
    assert d == f
    rchunks = tuple(min(cap, 256) for cap in caps)
    ye_rows = [cap + COMB_MAX_ROUNDS * COMB_WIN for cap in caps]
    wspec = lambda r, c: pl.BlockSpec((1, 1, r, c), lambda ei: (layer, ei, 0, 0))
    return pl.pallas_call(
        functools.partial(_moe_ffn_kernel, caps=tuple(caps), rchunks=rchunks),
        grid=(e,),
        in_specs=[pl.BlockSpec((1, xe.shape[1], d), lambda ei: (ei, 0, 0)) for xe in xes]
                 + [wspec(d, f), wspec(d, f), wspec(f, d)],
        out_specs=[pl.BlockSpec((1, r, d), lambda ei: (ei, 0, 0)) for r in ye_rows],
        out_shape=[jax.ShapeDtypeStruct((e, r, d), BF16) for r in ye_rows],
        scratch_shapes=[pltpu.VMEM((d, f), BF16), pltpu.VMEM((d, f), BF16), pltpu.VMEM((f, d), BF16)],
        compiler_params=pltpu.CompilerParams(dimension_semantics=("arbitrary",),
                                             vmem_limit_bytes=56 * 1024 * 1024),
        name="moe_ffn",
    )(*xes, wg, wu, wd)


def _moe_window_copy(ye_hbm, buf, sem, slot, e, start):
    return pltpu.make_async_copy(ye_hbm.at[e, pl.ds(start, COMB_WIN), :], buf.at[slot, e], sem.at[slot, e])


def _moe_combine_kernel(offs_ref, rounds_ref, x_ref, g_ref, pos_ref, mod_ref, fin_ref, ye_hbm, o_ref,
                        buf, lhs, acc_ref, sem, *, n_tiles, final):
    j = pl.program_id(0)
    t = x_ref.shape[0]
    n_exp = g_ref.shape[1]
    lane = lax.broadcasted_iota(I32, (t, 2 * COMB_WIN), 1)
    low = lane < COMB_WIN
    slot_in_win = jnp.where(low, lane, lane - COMB_WIN)

    def starts_of(tile, rnd):
        return [pl.multiple_of((offs_ref[e * (n_tiles + 1) + tile] // 16) * 16 + rnd * COMB_WIN, 16)
                for e in range(n_exp)]

    def fetch(slot, starts):
        for e in range(n_exp):
            _moe_window_copy(ye_hbm, buf, sem, slot, e, starts[e]).start()

    def land(slot, starts):
        for e in range(n_exp):
            _moe_window_copy(ye_hbm, buf, sem, slot, e, starts[e]).wait()

    def expand(slot, starts):
        for e in range(0, n_exp, 2):
            pos = jnp.where(low, pos_ref[:, e:e + 1], pos_ref[:, e + 1:e + 2])
            gate = jnp.where(low, g_ref[:, e:e + 1], g_ref[:, e + 1:e + 2])
            first = jnp.where(low, starts[e], starts[e + 1])
            val = jnp.where(pos == slot_in_win + first, gate, 0.0)
            lhs[:, e * COMB_WIN:(e + 2) * COMB_WIN] = val.astype(BF16)
        return _dot(lhs[...], buf[slot].reshape(n_exp * COMB_WIN, buf.shape[3]))

    cur = j % 2
    first = starts_of(j, 0)

    @pl.when(j == 0)
    def _():
        fetch(0, first)

    @pl.when(j + 1 < n_tiles)
    def _():
        fetch(1 - cur, starts_of(j + 1, 0))

    land(cur, first)
    acc_ref[...] = expand(cur, first)

    def more(rnd, carry):
        starts = starts_of(j, rnd)
        fetch(2, starts)
        land(2, starts)
        acc_ref[...] += expand(2, starts)
        return carry

    lax.fori_loop(1, rounds_ref[j], more, 0)
    out = x_ref[...] + mod_ref[0, 5:6, :] * acc_ref[...]
    o_ref[...] = _rms(out) * fin_ref[...] if final else out


def _moe_combine(xall, g, pos, offs, rounds, mod, ye, tile0, n, mod_row, final_g=None):
    t = MOE_SUB
    d = xall.shape[1]
    n_exp = g.shape[1]
    n_tiles = n // t
    final = final_g is not None
    fin = final_g if final else jnp.ones((1, d), F32)
    grid_spec = pltpu.PrefetchScalarGridSpec(
        num_scalar_prefetch=2,
        grid=(n_tiles,),
        in_specs=[pl.BlockSpec((t, d), lambda j, offs, rounds: (j + tile0, 0)),
                  pl.BlockSpec((t, n_exp), lambda j, offs, rounds: (j, 0)),
                  pl.BlockSpec((t, n_exp), lambda j, offs, rounds: (j, 0)),
                  pl.BlockSpec((1, 6, d), lambda j, offs, rounds: (mod_row, 0, 0)),
                  pl.BlockSpec((1, d), lambda j, offs, rounds: (0, 0)),
                  pl.BlockSpec(memory_space=pl.ANY)],
        out_specs=pl.BlockSpec((t, d), lambda j, offs, rounds: (j + (0 if final else tile0), 0)),
        scratch_shapes=[pltpu.VMEM((3, n_exp, COMB_WIN, d), BF16),
                        pltpu.VMEM((t, n_exp * COMB_WIN), BF16),
                        pltpu.VMEM((t, d), F32), pltpu.SemaphoreType.DMA((3, n_exp))],
    )
    return pl.pallas_call(
        functools.partial(_moe_combine_kernel, n_tiles=n_tiles, final=final),
        grid_spec=grid_spec,
        out_shape=jax.ShapeDtypeStruct((n, d) if final else xall.shape, F32),
        input_output_aliases={} if final else {2: 0},
        compiler_params=_params("arbitrary"),
        name="moe_combine",
    )(offs, rounds, xall, g, pos, mod, fin, ye)


def _expert_choice(xall, h2, aff, mod, wg, wu, wd, layer, sets, final_g=None):
    routed = []
    for row0, h_row0, n, mod_row in sets:
        cap = EC_CAPACITY_FACTOR * n // N_EXPERTS
        g_t, pos_t, cum_t = _route(aff[row0:row0 + n].T, cap)
        offs = jnp.concatenate([cum_t[:, ::MOE_SUB], jnp.full((N_EXPERTS, 1), cap, I32)], axis=1)
        span = offs[:, 1:] - (offs[:, :-1] // 16) * 16
        rounds = jnp.maximum(jnp.max((span + COMB_WIN - 1) // COMB_WIN, axis=0), 1).astype(I32)
        offs = offs.reshape(-1)
        routed.append((cap, g_t, pos_t, offs, rounds, _moe_gather(h2, pos_t, offs, h_row0, n, cap)))
    yes = _moe_ffn([r[5] for r in routed], wg, wu, wd, layer, [r[0] for r in routed])
    for (row0, _, n, mod_row), (cap, g_t, pos_t, offs, rounds, _), ye in zip(sets, routed, yes):
        xall = _moe_combine(xall, g_t.T, pos_t.T, offs, rounds, mod, ye, row0 // MOE_SUB, n, 2 * layer + mod_row,
                            final_g)
    return xall


def kernel(x, c, ctx, c_ctx, w_mod, b_mod, norm1_g, norm2_g, w_in, s5_lam_re, s5_lam_im, s5_log_dt, s5_b_re, s5_b_im, s5_c_re, s5_c_im, s5_d, s5_w_glu, s5_b_glu, ssd_conv_w, ssd_conv_b, ssd_a_log, ssd_dt_bias, ssd_d, ssd_norm_g, attn_sink, w_branch, w_out, w_router, w_e_gate, w_e_up, w_e_down, final_norm_g):
    batch, n, d = x.shape
    nc = ctx.shape[1]
    depth = w_mod.shape[0]
    assert batch == 1 and nc == TOK_TILE and n % TOK_TILE == 0 and n % GRID_W == 0
    assert SSD_STATE == SSD_CHUNK
    xall = jnp.concatenate([ctx[0], x[0]], axis=0)
    cvecs = jnp.zeros((8, d), F32).at[0].set(c_ctx).at[1].set(c[0])
    mods = _modulation(cvecs, w_mod, b_mod)
    cos, sin = _rope_tables(n, nc)
    row = lambda v: v.reshape(1, -1).astype(F32)
    mods = mods[:, 0:2].reshape(depth * 2, 6, d)
    vec3 = lambda v: v.reshape(depth, 1, -1).astype(F32)
    w_in_all = _prep_w_in(w_in)
    s5 = _s5_prep(s5_lam_re, s5_lam_im, s5_log_dt, s5_b_re, s5_b_im, s5_c_re, s5_c_im)
    conv_w = jnp.pad(ssd_conv_w, ((0, 0), (0, 3), (0, 0))).reshape(depth * 8, -1)
    conv_b = vec3(ssd_conv_b)
    a = -jnp.exp(ssd_a_log.astype(F32))
    ab = jnp.stack([a, ssd_dt_bias.astype(F32)], axis=2)
    prow = jnp.pad(ab, ((0, 0), (0, 0), (0, 6), (0, LANE - SSD_HEADS))).reshape(depth * 2 * 8, LANE)
    pcol = jnp.pad(ab.transpose(0, 1, 3, 2), ((0, 0), (0, 0), (0, 0), (0, LANE - 2))).reshape(-1, LANE)
    vecs = (vec3(s5_d), vec3(s5_b_glu), vec3(jnp.repeat(ssd_d, SSD_HEAD_DIM, axis=1)), vec3(ssd_norm_g),
            vec3(norm2_g))
    wr = jnp.pad(w_router.astype(F32), ((0, 0), (0, 0), (0, LANE - N_EXPERTS)))
    wrh = wr.astype(BF16)
    wr2 = jnp.concatenate([wrh, (wr - wrh.astype(F32)).astype(BF16)], axis=2)
    wbr = w_branch.at[:, 2].set(w_branch[:, 2][:, _attn_head_order()]).astype(BF16)
    wbr = wbr.reshape(depth * 3, wbr.shape[2], wbr.shape[3])
    wglu, wout = s5_w_glu.astype(BF16), w_out.astype(BF16)
    sink = attn_sink.astype(F32)
    g1 = vec3(norm1_g)
    for i in range(depth):
        u, z, xbc, q, kv, gates, dtr = _inproj(xall, mods, g1, w_in_all, cos, sin, i)
        ys5 = _s5_mix(u, nc, s5, i)
        t = _ssd_conv(xbc, conv_w, conv_b, i)
        dtrt = dtr[:, 0:2 * SSD_HEADS].T
        yf = _ssd_scan(t, dtr, dtrt, prow, pcol, nc, 0, i)
        yb = _ssd_scan(t, dtr, dtrt, prow, pcol, nc, 1, i)
        att = _attention(q, kv, sink, nc, i)
        xall, h2, aff = _merge(xall, mods, ys5, u, yf, yb, t, z, att, gates, vecs, wglu, wbr, wout, wr2, i)
        sets = [(nc, 0, n, 1)] + ([(0, n, nc, 0)] if i < depth - 1 else [])
        xall = _expert_choice(xall, h2, aff, mods, w_e_gate, w_e_up, w_e_down, i, sets,
                              row(final_norm_g) if i == depth - 1 else None)
    return xall[None]
```

```python
import functools
import math

import jax
import jax.numpy as jnp
from jax import lax
from jax.experimental import pallas as pl
from jax.experimental.pallas import tpu as pltpu

F32 = jnp.float32
BF16 = jnp.bfloat16
I32 = jnp.int32
HI = lax.Precision.HIGHEST

D_MODEL = 1024
DEPTH = 4
GRID_W = 64
EPS = 1e-6
S5_GROUPS = 32
S5_GROUP_CH = 16
S5_STATE = 64
S5_CHUNK = 32
SSD_HEADS = 8
SSD_HEAD_DIM = 64
SSD_GROUPS = 2
SSD_STATE = 128
SSD_CHUNK = 128
ATTN_HEADS = 8
ATTN_KV_HEADS = 2
ATTN_HEAD_DIM = 64
ATTN_BLOCK = 128
ROPE_BASE = 10000.0
N_EXPERTS = 16
EXPERT_FF = 1024
EC_CAPACITY_FACTOR = 2
IN_SIZES = (512, 512, 1024, 16, 512, 128, 128, 3072)
TOK_TILE = 256
LANE = 128
NEG = -1e30


def _dot(a, b):
    return jnp.dot(a, b, preferred_element_type=F32)


def _dot_hi(a, b):
    return jnp.dot(a, b, precision=HI, preferred_element_type=F32)


def _dot_nt(a, b):
    return lax.dot_general(a, b, (((1,), (1,)), ((), ())), preferred_element_type=F32)


def _sigmoid(x):
    return 0.5 + 0.5 * jnp.tanh(0.5 * x)


def _softplus(x):
    return jnp.maximum(x, 0.0) + jnp.log(1.0 + jnp.exp(-jnp.abs(x)))


def _rms(x):
    return x * lax.rsqrt(jnp.mean(x * x, axis=-1, keepdims=True) + EPS)


def _params(*sem):
    return pltpu.CompilerParams(dimension_semantics=sem)


def _mod_kernel(c_ref, w_ref, b_ref, o_ref):
    c = c_ref[...]
    o_ref[0] = _dot_hi(c * _sigmoid(c), w_ref[0]) + b_ref[0]


def _modulation(cvecs, w_mod, b_mod):
    depth, d, d6 = w_mod.shape
    bn = 1536
    return pl.pallas_call(
        _mod_kernel,
        grid=(depth, d6 // bn),
        in_specs=[pl.BlockSpec((8, d), lambda l, j: (0, 0)),
                  pl.BlockSpec((1, d, bn), lambda l, j: (l, 0, j)),
                  pl.BlockSpec((1, 1, bn), lambda l, j: (l, 0, j))],
        out_specs=pl.BlockSpec((1, 8, bn), lambda l, j: (l, 0, j)),
        out_shape=jax.ShapeDtypeStruct((depth, 8, d6), F32),
        compiler_params=_params("arbitrary", "arbitrary"),
        name="modulation",
    )(cvecs, w_mod, b_mod.reshape(depth, 1, d6))


W_IN_COLS = 6016


def _attn_head_order():
    rep = ATTN_HEADS // ATTN_KV_HEADS
    heads = [g * rep + b for b in range(rep) for g in range(ATTN_KV_HEADS)]
    return jnp.concatenate([jnp.arange(h * ATTN_HEAD_DIM, (h + 1) * ATTN_HEAD_DIM) for h in heads])


def _prep_w_in(w):
    parts, start = [], 0
    for s in IN_SIZES:
        parts.append(w[..., start:start + s])
        start += s
    u, z, xbc, dt, q, k, v, gates = parts
    dt = jnp.pad(dt, ((0, 0), (0, 0), (0, LANE - dt.shape[-1])))
    q = q[..., _attn_head_order()]
    return jnp.concatenate([u, z, xbc, q, k, v, gates, dt], axis=-1).astype(BF16)


def _inproj_kernel(x_ref, mod_ref, g_ref, w_ref, cos_ref, sin_ref,
                   u_ref, z_ref, xbc_ref, q_ref, kv_ref, gates_ref, dt_ref):
    m = mod_ref[0]
    h = (_rms(x_ref[...]) * g_ref[...] * (1.0 + m[1:2]) + m[0:1]).astype(BF16)

    def proj(a, b):
        return _dot(h, w_ref[:, a:b])

    u_ref[...] = proj(0, 512)
    z_ref[...] = proj(512, 1024)
    xbc_ref[...] = proj(1024, 2048)
    cos = cos_ref[...]
    sin = sin_ref[...]
    lane = lax.broadcasted_iota(I32, cos.shape, 1)
    first = (lane % 32) < 16

    def rope(v):
        partner = jnp.where(first, pltpu.roll(v, LANE - 16, 1), pltpu.roll(v, 16, 1))
        return v * cos + partner * sin

    scale = ATTN_HEAD_DIM ** -0.5
    for j in range(2):
        qq = proj(2048 + 2 * j * LANE, 2048 + 2 * (j + 1) * LANE)
        for half in range(2):
            q_ref[:, (2 * j + half) * LANE:(2 * j + half + 1) * LANE] = (
                rope(qq[:, half * LANE:(half + 1) * LANE]) * scale).astype(BF16)
    kvp = proj(2560, 2816)
    kv_ref[:, 0:LANE] = rope(kvp[:, 0:LANE]).astype(BF16)
    kv_ref[:, LANE:2 * LANE] = kvp[:, LANE:2 * LANE].astype(BF16)
    for j in range(6):
        gates_ref[:, j * 512:(j + 1) * 512] = proj(2816 + j * 512, 2816 + (j + 1) * 512).astype(BF16)
    dt_ref[...] = proj(5888, 6016)


def _inproj(xall, mods, g, w, cos, sin, layer):
    n_all, d = xall.shape
    t = TOK_TILE
    row = lambda i: (i, 0)
    const = lambda i: (0, 0)
    widths = (512, 512, 1024, 512, 256, 3072, LANE)
    dtypes = (F32, F32, F32, BF16, BF16, BF16, F32)
    return pl.pallas_call(
        _inproj_kernel,
        grid=(n_all // t,),
        in_specs=[pl.BlockSpec((t, d), row),
                  pl.BlockSpec((1, 6, d), lambda i: (2 * layer + jnp.minimum(i, 1), 0, 0)),
                  pl.BlockSpec((None, 1, d), lambda i: (layer, 0, 0)),
                  pl.BlockSpec((None, d, W_IN_COLS), lambda i: (layer, 0, 0), pipeline_mode=pl.Buffered(1)),
                  pl.BlockSpec((t, LANE), row),
                  pl.BlockSpec((t, LANE), row)],
        out_specs=[pl.BlockSpec((t, wd), row) for wd in widths],
        out_shape=[jax.ShapeDtypeStruct((n_all, wd), dt) for wd, dt in zip(widths, dtypes)],
        compiler_params=_params("arbitrary"),
        name="inproj",
    )(xall, mods, g, w, cos, sin)


def _rope_tables(n, nc):
    rows = n // GRID_W
    r = jnp.repeat(jnp.arange(rows, dtype=F32), GRID_W)
    c = jnp.tile(jnp.arange(GRID_W, dtype=F32), rows)
    m = ATTN_HEAD_DIM // 4
    inv_freq = ROPE_BASE ** (-jnp.arange(m, dtype=F32) / m)
    ang_r, ang_c = r[:, None] * inv_freq, c[:, None] * inv_freq
    cos = jnp.concatenate([jnp.cos(ang_r), jnp.cos(ang_r), jnp.cos(ang_c), jnp.cos(ang_c)], axis=1)
    sin = jnp.concatenate([-jnp.sin(ang_r), jnp.sin(ang_r), -jnp.sin(ang_c), jnp.sin(ang_c)], axis=1)
    cos = jnp.concatenate([jnp.ones((nc, 64), F32), cos], axis=0)
    sin = jnp.concatenate([jnp.zeros((nc, 64), F32), sin], axis=0)
    return jnp.tile(cos, (1, 2)), jnp.tile(sin, (1, 2))


def _s5_toeplitz_kernel(bbf_ref, caf_ref, bbb_ref, cab_ref, kt_ref):
    L, K = S5_CHUNK, S5_GROUP_CH
    lk = L * K
    tf = _dot_hi(bbf_ref[0], caf_ref[0])
    tb = _dot_hi(bbb_ref[0], cab_ref[0])
    lane = lax.broadcasted_iota(I32, tf.shape, 1)
    r = pltpu.roll(tf, lk - K, 1)
    tab = jnp.concatenate([tb + jnp.where(lane >= lk - K, r, 0.0), jnp.where(lane < lk - K, r, 0.0)], axis=1)
    for s_ in range(L):
        lo = (L - 1 - s_) * K
        kt_ref[0, s_ * K:(s_ + 1) * K, :] = tab[:, lo:lo + L * K].astype(BF16)


def _s5_prep_layer(lam_re, lam_im, log_dt, b_re, b_im, c_re, c_im):
    L, G, P, K = S5_CHUNK, S5_GROUPS, S5_STATE, S5_GROUP_CH
    lk = L * K
    lr, li = lam_re.astype(F32), lam_im.astype(F32)
    dt = jnp.exp(log_dt.astype(F32))[..., None]
    mag = jnp.exp(lr * dt)
    abr, abi = mag * jnp.cos(li * dt), mag * jnp.sin(li * dt)
    den = lr * lr + li * li
    fr = ((abr - 1.0) * lr + abi * li) / den
    fi = (abi * lr - (abr - 1.0) * li) / den
    bbr = fr[..., None] * b_re - fi[..., None] * b_im
    bbi = fr[..., None] * b_im + fi[..., None] * b_re
    tau = jnp.arange(L + 1, dtype=F32)[:, None, None, None]
    pm = jnp.exp(lr * dt * tau)
    apr, api = pm * jnp.cos(li * dt * tau), pm * jnp.sin(li * dt * tau)
    cr, ci = c_re.astype(F32), c_im.astype(F32)
    car = cr * apr[:, :, :, None, :] - ci * api[:, :, :, None, :]
    cai = cr * api[:, :, :, None, :] + ci * apr[:, :, :, None, :]

    lagmat = lambda v, d: v[:L, d].transpose(1, 3, 0, 2)
    caf = jnp.concatenate([lagmat(car, 0), lagmat(cai, 0)], axis=1).reshape(G, 2 * P, lk)
    cab = jnp.flip(jnp.concatenate([lagmat(car, 1), lagmat(cai, 1)], axis=1), axis=2).reshape(G, 2 * P, lk)
    tr = lambda v: v.transpose(0, 2, 1)
    bbf = jnp.concatenate([tr(bbr[0]), -tr(bbi[0])], axis=2)
    bbb = jnp.concatenate([tr(bbr[1]), -tr(bbi[1])], axis=2)

    def summary(d, rev):
        ar, ai = apr[:L, d], api[:L, d]
        if rev:
            ar, ai = jnp.flip(ar, axis=0), jnp.flip(ai, axis=0)
        ar, ai = ar[..., None], ai[..., None]
        re = ar * bbr[d][None] - ai * bbi[d][None]
        im = ar * bbi[d][None] + ai * bbr[d][None]
        f = lambda v: v.transpose(1, 0, 3, 2).reshape(G, lk, P)
        return f(re), f(im)

    sfr, sfi = summary(0, True)
    sbr, sbi = summary(1, False)
    sb = jnp.concatenate([sfr, sfi, sbr, sbi], axis=2).astype(BF16)

    def readout(v):
        return v.transpose(1, 3, 0, 2).reshape(G, P, lk)

    back = lambda v: jnp.flip(v[1:L + 1, 1], axis=0)
    rc = jnp.concatenate([readout(car[1:L + 1, 0]), -readout(cai[1:L + 1, 0]),
                          readout(back(car)), -readout(back(cai))], axis=1).astype(BF16)
    coef = jnp.stack([apr[L, 0].reshape(-1), api[L, 0].reshape(-1),
                      apr[L, 1].reshape(-1), api[L, 1].reshape(-1)], axis=0)
    coef = jnp.pad(coef, ((0, 4), (0, 0)))
    return bbf, caf, bbb, cab, sb, rc, coef


def _s5_prep(*params):
    L, G, P, K = S5_CHUNK, S5_GROUPS, S5_STATE, S5_GROUP_CH
    lk = L * K
    bbf, caf, bbb, cab, sb, rc, coef = jax.vmap(_s5_prep_layer)(*params)
    flat = lambda v: v.reshape((-1,) + v.shape[2:])
    n_mat = bbf.shape[0] * G
    bspec = pl.BlockSpec((1, K, 2 * P), lambda g: (g, 0, 0))
    cspec = pl.BlockSpec((1, 2 * P, lk), lambda g: (g, 0, 0))
    kt = pl.pallas_call(
        _s5_toeplitz_kernel,
        grid=(n_mat,),
        in_specs=[bspec, cspec, bspec, cspec],
        out_specs=pl.BlockSpec((1, lk, lk), lambda g: (g, 0, 0)),
        out_shape=jax.ShapeDtypeStruct((n_mat, lk, lk), BF16),
        compiler_params=_params("arbitrary"),
        name="s5_toeplitz",
    )(flat(bbf), flat(caf), flat(bbb), flat(cab))
    return kt, flat(sb), flat(rc), flat(coef)


def _s5_states_kernel(u_ref, sb_ref, fre, fim, bre, bim):
    p = S5_STATE
    s0 = _dot(u_ref[0], sb_ref[0])
    s1 = _dot(u_ref[1], sb_ref[1])
    for q, ref in enumerate((fre, fim, bre, bim)):
        ref[...] = jnp.concatenate([s0[:, q * p:(q + 1) * p], s1[:, q * p:(q + 1) * p]], axis=1)


def _s5_rec_kernel(coef_ref, sfr, sfi, sbr, sbi, hfr, hfi, hbr, hbi, *, n_chunks, ctx_chunks):
    arf, aif = coef_ref[0:1, :], coef_ref[1:2, :]
    arb, aib = coef_ref[2:3, :], coef_ref[3:4, :]
    zero = jnp.zeros_like(arf)

    def step(i, carry):
        fr, fi, br, bi = carry
        cb = jnp.where(i < ctx_chunks, ctx_chunks - 1 - i, n_chunks - 1 - (i - ctx_chunks))
        hfr[pl.ds(i, 1), :] = fr
        hfi[pl.ds(i, 1), :] = fi
        hbr[pl.ds(cb, 1), :] = br
        hbi[pl.ds(cb, 1), :] = bi
        return (arf * fr - aif * fi + sfr[pl.ds(i, 1), :], arf * fi + aif * fr + sfi[pl.ds(i, 1), :],
                arb * br - aib * bi + sbr[pl.ds(cb, 1), :], arb * bi + aib * br + sbi[pl.ds(cb, 1), :])

    lax.fori_loop(0, n_chunks, step, (zero, zero, zero, zero))


def _s5_out_kernel(u_ref, kt_ref, hfr, hfi, hbr, hbi, rc_ref, y_ref):
    p = S5_STATE
    for i in range(2):
        h = jnp.concatenate([r[:, i * p:(i + 1) * p] for r in (hfr, hfi, hbr, hbi)], axis=1).astype(BF16)
        y_ref[i] = (_dot(u_ref[i], kt_ref[i]) + _dot(h, rc_ref[i])).astype(y_ref.dtype)


S5_LANE_GROUPS = LANE // S5_GROUP_CH
S5_T_LO = 8


def _s5_perm():
    i = jnp.arange(S5_T_LO * LANE)
    t_lo, gl, k = i // LANE, (i % LANE) // S5_GROUP_CH, i % S5_GROUP_CH
    j = gl * (S5_T_LO * S5_GROUP_CH) + t_lo * S5_GROUP_CH + k
    return jnp.zeros((S5_T_LO * LANE, S5_T_LO * LANE), BF16).at[i, j].set(1.0)


def _s5_pack_kernel(u_ref, perm_ref, o_ref, *, n_chunks):
    L = S5_CHUNK
    for t_hi in range(L // S5_T_LO):
        z = jnp.concatenate([u_ref[pl.ds(t_hi * S5_T_LO + t_lo, n_chunks, stride=L), :].astype(BF16)
                             for t_lo in range(S5_T_LO)], axis=1)
        w = _dot(z, perm_ref[...]).astype(BF16)
        for gl in range(S5_LANE_GROUPS):
            o_ref[gl, :, t_hi * LANE:(t_hi + 1) * LANE] = w[:, gl * LANE:(gl + 1) * LANE]


def _s5_unpack_kernel(y_ref, perm_ref, o_ref, *, n_chunks):
    L = S5_CHUNK
    for t_hi in range(L // S5_T_LO):
        w = jnp.concatenate([y_ref[gl, :, t_hi * LANE:(t_hi + 1) * LANE] for gl in range(S5_LANE_GROUPS)], axis=1)
        z = _dot_nt(w, perm_ref[...])
        for t_lo in range(S5_T_LO):
            o_ref[pl.ds(t_hi * S5_T_LO + t_lo, n_chunks, stride=L), :] = z[:, t_lo * LANE:(t_lo + 1) * LANE]


def _s5_mix(u, nc, prep, layer):
    kt, sb, rc, coef = prep
    n_all = u.shape[0]
    L, G, K, P = S5_CHUNK, S5_GROUPS, S5_GROUP_CH, S5_STATE
    C = n_all // L
    lk = L * K
    perm = _s5_perm()
    lg = S5_LANE_GROUPS
    perm_spec = pl.BlockSpec(perm.shape, lambda b: (0, 0))
    ug = pl.pallas_call(
        functools.partial(_s5_pack_kernel, n_chunks=C),
        grid=(G // lg,),
        in_specs=[pl.BlockSpec((n_all, LANE), lambda b: (0, b)), perm_spec],
        out_specs=pl.BlockSpec((lg, C, lk), lambda b: (b, 0, 0)),
        out_shape=jax.ShapeDtypeStruct((G, C, lk), BF16),
        compiler_params=_params("arbitrary"),
        name="s5_pack",
    )(u, perm)
    gp = G // 2
    st_shape = jax.ShapeDtypeStruct((C, G * P), F32)
    st_spec = pl.BlockSpec((C, 2 * P), lambda p: (0, p))
    states = pl.pallas_call(
        _s5_states_kernel,
        grid=(gp,),
        in_specs=[pl.BlockSpec((2, C, lk), lambda p: (p, 0, 0)),
                  pl.BlockSpec((2, lk, 4 * P), lambda p: (layer * gp + p, 0, 0))],
        out_specs=[st_spec] * 4,
        out_shape=[st_shape] * 4,
        compiler_params=_params("arbitrary"),
        name="s5_states",
    )(ug, sb)
    cb = 512
    col = pl.BlockSpec((C, cb), lambda j: (0, j))
    hs = pl.pallas_call(
        functools.partial(_s5_rec_kernel, n_chunks=C, ctx_chunks=nc // L),
        grid=(G * P // cb,),
        in_specs=[pl.BlockSpec((8, cb), lambda j: (layer, j))] + [col] * 4,
        out_specs=[col] * 4,
        out_shape=[st_shape] * 4,
        compiler_params=_params("arbitrary"),
        name="s5_recurrence",
    )(coef, *states)
    y = pl.pallas_call(
        _s5_out_kernel,
        grid=(gp,),
        in_specs=[pl.BlockSpec((2, C, lk), lambda p: (p, 0, 0)),
                  pl.BlockSpec((2, lk, lk), lambda p: (layer * gp + p, 0, 0))] + [st_spec] * 4
                 + [pl.BlockSpec((2, 4 * P, lk), lambda p: (layer * gp + p, 0, 0))],
        out_specs=pl.BlockSpec((2, C, lk), lambda p: (p, 0, 0)),
        out_shape=jax.ShapeDtypeStruct((G, C, lk), BF16),
        compiler_params=_params("arbitrary"),
        name="s5_out",
    )(ug, kt, *hs, rc)
    return pl.pallas_call(
        functools.partial(_s5_unpack_kernel, n_chunks=C),
        grid=(G // lg,),
        in_specs=[pl.BlockSpec((lg, C, lk), lambda b: (b, 0, 0)), perm_spec],
        out_specs=pl.BlockSpec((n_all, LANE), lambda b: (0, b)),
        out_shape=jax.ShapeDtypeStruct((n_all, G * K), F32),
        compiler_params=_params("arbitrary"),
        name="s5_unpack",
    )(y, perm)


def _conv_kernel(cur_ref, prev_ref, next_ref, w_ref, b_ref, o_ref, *, n_tiles):
    i = pl.program_id(0)
    cur = cur_ref[...]
    t = cur.shape[0]
    pv = prev_ref[...] * jnp.where(i >= 2, 1.0, 0.0)
    nx = next_ref[...] * jnp.where((i >= 1) & (i <= n_tiles - 2), 1.0, 0.0)
    row8 = lax.broadcasted_iota(I32, pv.shape, 0)
    acc = b_ref[...] + w_ref[2:3, :] * cur
    for s in (1, 2):
        r = pltpu.roll(cur, s, 0)
        head = jnp.where(row8 < s, pltpu.roll(pv, s, 0), r[0:8])
        acc = acc + w_ref[2 - s:3 - s, :] * jnp.concatenate([head, r[8:]], axis=0)
        r = pltpu.roll(cur, t - s, 0)
        tail = jnp.where(row8 >= 8 - s, pltpu.roll(nx, 8 - s, 0), r[t - 8:])
        acc = acc + w_ref[2 + s:3 + s, :] * jnp.concatenate([r[:t - 8], tail], axis=0)
    o_ref[...] = acc * _sigmoid(acc)


def _ssd_conv(xbc, conv_w, conv_b, layer):
    n_all, ch = xbc.shape
    t = TOK_TILE
    n_tiles = n_all // t
    per = t // 8
    return pl.pallas_call(
        functools.partial(_conv_kernel, n_tiles=n_tiles),
        grid=(n_tiles,),
        in_specs=[pl.BlockSpec((t, ch), lambda i: (i, 0)),
                  pl.BlockSpec((8, ch), lambda i: (jnp.maximum(i * per - 1, 0), 0)),
                  pl.BlockSpec((8, ch), lambda i: (jnp.minimum((i + 1) * per, n_tiles * per - 1), 0)),
                  pl.BlockSpec((8, ch), lambda i: (layer, 0)),
                  pl.BlockSpec((None, 1, ch), lambda i: (layer, 0, 0))],
        out_specs=pl.BlockSpec((t, ch), lambda i: (i, 0)),
        out_shape=jax.ShapeDtypeStruct((n_all, ch), F32),
        compiler_params=_params("arbitrary"),
        name="ssd_conv",
    )(xbc, xbc, xbc, conv_w, conv_b)


def _ssd_kernel(t_ref, dtr_ref, dtrt_ref, prow_ref, pcol_ref, y_ref, h_ref, *, reverse, d):
    q = SSD_CHUNK
    hpg = SSD_HEADS // SSD_GROUPS
    p = SSD_HEAD_DIM

    @pl.when(pl.program_id(0) == 0)
    def _():
        h_ref[...] = jnp.zeros_like(h_ref)

    lo = d * SSD_HEADS
    dt_c = _softplus(dtr_ref[:, lo:lo + SSD_HEADS] + prow_ref[1:2, 0:SSD_HEADS])
    adt_c = dt_c * prow_ref[0:1, 0:SSD_HEADS]
    dt_r = _softplus(dtrt_ref[lo:lo + SSD_HEADS, :] + pcol_ref[:, 1:2])
    adt_r = dt_r * pcol_ref[:, 0:1]
    ii = lax.broadcasted_iota(I32, (q, q), 0)
    jj = lax.broadcasted_iota(I32, (q, q), 1)
    causal = (jj >= ii) if reverse else (jj <= ii)
    acum_c = _dot_hi(causal.astype(F32), adt_c)
    acum_r = _dot_hi(adt_r, ((ii >= jj) if reverse else (ii <= jj)).astype(F32))
    tot = acum_c[0:1, :] if reverse else acum_c[q - 1:q, :]
    gw = hpg * p
    lane_head = lax.broadcasted_iota(I32, (q, gw), 1) // p
    lane_head_row = lax.broadcasted_iota(I32, (1, gw), 1) // p
    lane_q = lax.broadcasted_iota(I32, (q, q), 1)
    for g in range(SSD_GROUPS):
        bg = t_ref[:, 512 + g * SSD_STATE:512 + (g + 1) * SSD_STATE]
        cg = t_ref[:, 768 + g * SSD_STATE:768 + (g + 1) * SSD_STATE].astype(BF16)
        cb = _dot_nt(cg, bg.astype(BF16))
        bgt = bg.T.astype(BF16)
        xg = t_ref[:, g * gw:(g + 1) * gw]
        xgb = xg.astype(BF16)
        hg = h_ref[g]
        yg = jnp.zeros((q, gw), F32)
        dec = jnp.zeros((1, gw), F32)
        acs, dts = [], []
        for r in range(hpg):
            hd = g * hpg + r
            ac = jnp.broadcast_to(acum_c[:, hd:hd + 1], (q, q))
            acs.append(ac)
            dts.append(jnp.broadcast_to(dt_c[:, hd:hd + 1], (q, q)))
            wts = (cb * jnp.exp(jnp.where(causal, ac - acum_r[hd:hd + 1, :], NEG)) * dt_r[hd:hd + 1, :]).astype(BF16)
            yg = jnp.where(lane_head == r, _dot(wts, xgb), yg)
            dec = jnp.where(lane_head_row == r, jnp.exp(tot[:, hd:hd + 1]), dec)
        spread = lambda v: jnp.concatenate([jnp.where(lane_q < p, v[2 * b], v[2 * b + 1]) for b in range(hpg // 2)],
                                           axis=1)
        acg = spread(acs)
        y_ref[:, g * gw:(g + 1) * gw] = yg + _dot(cg, hg.astype(BF16)) * jnp.exp(acg)
        xw = (xg * (jnp.exp(dec_log(tot, g, hpg, lane_head_row) - acg) * spread(dts))).astype(BF16)
        h_ref[g] = hg * dec + _dot(bgt, xw)


def dec_log(tot, g, hpg, lane_head_row):
    out = jnp.zeros(lane_head_row.shape, F32)
    for r in range(hpg):
        out = jnp.where(lane_head_row == r, tot[:, g * hpg + r:g * hpg + r + 1], out)
    return out


def _ssd_scan(t, dtr, dtrt, prow, pcol, nc, d, layer):
    n_all = t.shape[0]
    q = SSD_CHUNK
    n_chunks = n_all // q
    cc = nc // q
    reverse = d == 1
    if reverse:
        order = lambda i: jnp.where(i < cc, cc - 1 - i, n_chunks - 1 - (i - cc))
    else:
        order = lambda i: i
    return pl.pallas_call(
        functools.partial(_ssd_kernel, reverse=reverse, d=d),
        grid=(n_chunks,),
        in_specs=[pl.BlockSpec((q, t.shape[1]), lambda i: (order(i), 0)),
                  pl.BlockSpec((q, LANE), lambda i: (order(i), 0)),
                  pl.BlockSpec((2 * SSD_HEADS, q), lambda i: (0, order(i))),
                  pl.BlockSpec((8, LANE), lambda i: (2 * layer + d, 0)),
                  pl.BlockSpec((SSD_HEADS, LANE), lambda i: (2 * layer + d, 0))],
        out_specs=pl.BlockSpec((q, SSD_HEADS * SSD_HEAD_DIM), lambda i: (order(i), 0)),
        out_shape=jax.ShapeDtypeStruct((n_all, SSD_HEADS * SSD_HEAD_DIM), F32),
        scratch_shapes=[pltpu.VMEM((SSD_GROUPS, SSD_STATE, (SSD_HEADS // SSD_GROUPS) * SSD_HEAD_DIM), F32)],
        compiler_params=_params("arbitrary"),
        name="ssd_scan_bwd" if reverse else "ssd_scan_fwd",
    )(t, dtr, dtrt, prow, pcol)


def _attn_kernel(sink_ref, q_ref, kp_ref, ko_ref, kn_ref, kc_ref, bias_ref, o_ref, *, n_blocks, ctx_blocks, layer):
    c = pl.program_id(0)
    blk = ATTN_BLOCK
    dh = ATTN_HEAD_DIM
    rep = ATTN_HEADS // ATTN_KV_HEADS
    rows = rep * blk
    kw = ATTN_KV_HEADS * dh
    lat = c >= ctx_blocks
    lo = jnp.where(lat & (c > ctx_blocks), 0, blk)
    hi = jnp.where(lat, jnp.where(c < n_blocks - 1, 3 * blk, 2 * blk), 0)
    col = lax.broadcasted_iota(I32, (1, 3 * blk), 1)
    bias = bias_ref[...] + jnp.where((col >= lo) & (col < hi), 0.0, NEG)
    rowhead = lax.broadcasted_iota(I32, (rows, 1), 0) // blk
    lane_q = lax.broadcasted_iota(I32, (blk, kw), 1)
    kloc = jnp.concatenate([kp_ref[:, 0:kw], ko_ref[:, 0:kw], kn_ref[:, 0:kw]], axis=0)
    vloc = jnp.concatenate([kp_ref[:, kw:2 * kw], ko_ref[:, kw:2 * kw], kn_ref[:, kw:2 * kw]], axis=0)
    kctx = kc_ref[:, 0:kw]
    vctx = kc_ref[:, kw:2 * kw]
    lane_l = lax.broadcasted_iota(I32, vloc.shape, 1)
    lane_c = lax.broadcasted_iota(I32, vctx.shape, 1)
    one = jnp.ones((), BF16)
    outs = []
    for g in range(ATTN_KV_HEADS):
        own = (lane_q >= g * dh) & (lane_q < (g + 1) * dh)
        vl = jnp.where((lane_l >= g * dh) & (lane_l < (g + 1) * dh), vloc, one)
        vc = jnp.where((lane_c >= g * dh) & (lane_c < (g + 1) * dh), vctx, one)
        qg = jnp.concatenate([jnp.where(own, q_ref[:, b * kw:(b + 1) * kw], jnp.zeros((), BF16))
                              for b in range(rep)], axis=0)
        s_loc = _dot_nt(qg, kloc) + bias
        s_ctx = _dot_nt(qg, kctx)
        sink = jnp.zeros((rows, 1), F32)
        for r in range(rep):
            sink = jnp.where(rowhead == r, sink_ref[layer, g * rep + r], sink)
        mx = jnp.maximum(jnp.maximum(jnp.max(s_loc, axis=1, keepdims=True),
                                     jnp.max(s_ctx, axis=1, keepdims=True)), sink)
        p_loc = jnp.exp((s_loc - mx).astype(BF16))
        p_ctx = jnp.exp((s_ctx - mx).astype(BF16))
        pv = _dot(p_loc, vl) + _dot(p_ctx, vc)
        outs.append(pv / (pltpu.roll(pv, dh, 1) + jnp.exp(sink - mx)))
    for b in range(rep):
        o_ref[:, b * kw:(b + 1) * kw] = jnp.where(lane_q < dh, outs[0][b * blk:(b + 1) * blk],
                                                  outs[1][b * blk:(b + 1) * blk])


def _attention(q, kv, sink, nc, layer):
    n_all = q.shape[0]
    blk = ATTN_BLOCK
    n_blocks = n_all // blk
    cbk = nc // blk
    kvw = kv.shape[1]
    rows = (ATTN_HEADS // ATTN_KV_HEADS) * blk
    qi = jnp.arange(rows)[:, None] % blk
    kj = jnp.arange(3 * blk)[None, :] - blk
    band = jnp.where(jnp.abs(qi - kj) <= blk, 0.0, NEG).astype(F32)
    return pl.pallas_call(
        functools.partial(_attn_kernel, n_blocks=n_blocks, ctx_blocks=cbk, layer=layer),
        grid=(n_blocks,),
        in_specs=[pl.BlockSpec(memory_space=pltpu.SMEM),
                  pl.BlockSpec((blk, q.shape[1]), lambda c: (c, 0)),
                  pl.BlockSpec((blk, kvw), lambda c: (jnp.maximum(c - 1, 0), 0)),
                  pl.BlockSpec((blk, kvw), lambda c: (c, 0)),
                  pl.BlockSpec((blk, kvw), lambda c: (jnp.minimum(c + 1, n_blocks - 1), 0)),
                  pl.BlockSpec((nc, kvw), lambda c: (0, 0)),
                  pl.BlockSpec((rows, 3 * blk), lambda c: (0, 0))],
        out_specs=pl.BlockSpec((blk, q.shape[1]), lambda c: (c, 0)),
        out_shape=jax.ShapeDtypeStruct((n_all, q.shape[1]), F32),
        compiler_params=_params("arbitrary"),
        name="window_attention",
    )(sink, q, kv, kv, kv, kv, band)


def _merge_kernel(x_ref, mod_ref, ys5_ref, u_ref, yf_ref, yb_ref, xs_ref, z_ref, att_ref, gates_ref,
                  s5d_ref, bglu_ref, ssdd_ref, ssdg_ref, n2g_ref, wglu_ref, wbr_ref, wout_ref, wr2_ref,
                  xo_ref, h2_ref, aff_ref, br_ref):
    m = mod_ref[0]
    d = x_ref.shape[1]
    a = jax.nn.gelu(ys5_ref[...].astype(F32) + s5d_ref[...] * u_ref[...])
    ya = (a * _sigmoid(_dot(a.astype(BF16), wglu_ref[...]) + bglu_ref[...])).astype(BF16)
    z = z_ref[...]
    yz = (yf_ref[...] + yb_ref[...] + ssdd_ref[...] * xs_ref[...]) * (z * _sigmoid(z))
    yb = (_rms(yz) * ssdg_ref[...]).astype(BF16)
    yc = att_ref[...].astype(BF16)
    cw = 256
    for j in range(d // cw):
        cs = slice(j * cw, (j + 1) * cw)
        br = (_sigmoid(gates_ref[:, j * cw:(j + 1) * cw].astype(F32)) * _dot(ya, wbr_ref[0, :, cs])
              + _sigmoid(gates_ref[:, d + j * cw:d + (j + 1) * cw].astype(F32)) * _dot(yb, wbr_ref[1, :, cs])
              + _sigmoid(gates_ref[:, 2 * d + j * cw:2 * d + (j + 1) * cw].astype(F32)) * _dot(yc, wbr_ref[2, :, cs]))
        br_ref[:, cs] = br.astype(BF16)
    xn = x_ref[...] + m[2:3] * _dot(br_ref[...], wout_ref[...])
    xo_ref[...] = xn
    h2 = _rms(xn) * n2g_ref[...] * (1.0 + m[4:5]) + m[3:4]
    hi = h2.astype(BF16)
    h2_ref[...] = hi
    lo = (h2 - hi.astype(F32)).astype(BF16)
    both = _dot(hi, wr2_ref[...])
    logits = (both[:, 0:LANE] + both[:, LANE:2 * LANE] + _dot(lo, wr2_ref[:, 0:LANE]))[:, 0:N_EXPERTS]
    e = jnp.exp(logits - jnp.max(logits, axis=1, keepdims=True))
    aff_ref[...] = e / jnp.sum(e, axis=1, keepdims=True)


def _merge(xall, mods, ys5, u, yf, yb, t, z, att, gates, vecs, wglu, wbr, wout, wr2, layer):
    n_all, d = xall.shape
    tt = TOK_TILE
    row = lambda i: (i, 0)
    lay3 = lambda i: (layer, 0, 0)
    vec = lambda wd: pl.BlockSpec((None, 1, wd), lay3)
    bw = 512
    s5d, bglu, ssdd, ssdg, n2g = vecs
    return pl.pallas_call(
        _merge_kernel,
        grid=(n_all // tt,),
        in_specs=[pl.BlockSpec((tt, d), row),
                  pl.BlockSpec((1, 6, d), lambda i: (2 * layer + jnp.minimum(i, 1), 0, 0)),
                  pl.BlockSpec((tt, bw), row), pl.BlockSpec((tt, bw), row), pl.BlockSpec((tt, bw), row),
                  pl.BlockSpec((tt, bw), row), pl.BlockSpec((tt, bw), row), pl.BlockSpec((tt, bw), row),
                  pl.BlockSpec((tt, bw), row), pl.BlockSpec((tt, 3 * d), row),
                  vec(bw), vec(bw), vec(bw), vec(bw), vec(d),
                  pl.BlockSpec((None, bw, bw), lay3),
                  pl.BlockSpec((3, bw, d), lay3),
                  pl.BlockSpec((None, d, d), lay3),
                  pl.BlockSpec((None, d, 2 * LANE), lay3)],
        out_specs=[pl.BlockSpec((tt, d), row),
                   pl.BlockSpec((tt, d), lambda i: (jnp.where(i == 0, n_all // tt - 1, i - 1), 0)),
                   pl.BlockSpec((tt, N_EXPERTS), row)],
        out_shape=[jax.ShapeDtypeStruct((n_all, d), F32), jax.ShapeDtypeStruct((n_all, d), BF16),
                   jax.ShapeDtypeStruct((n_all, N_EXPERTS), F32)],
        scratch_shapes=[pltpu.VMEM((tt, d), BF16)],
        compiler_params=_params("arbitrary"),
        name="merge_router",
    )(xall, mods, ys5, u, yf, yb, t, z, att, gates, s5d, bglu, ssdd, ssdg, n2g, wglu, wbr, wout, wr2)


def _route_kernel(a_ref, g_ref, pos_ref, cum_ref, *, cap):
    n = a_ref.shape[1]
    e = a_ref.shape[0]
    aff = a_ref[...]
    capf = float(cap)

    def search(i, bits):
        cand = bits | (1 << (30 - i))
        cnt = jnp.sum((aff >= lax.bitcast_convert_type(cand, F32)).astype(F32), axis=1, keepdims=True)
        return jnp.where(cnt >= capf, cand, bits)

    thr = lax.bitcast_convert_type(lax.fori_loop(0, 31, search, jnp.zeros((e, 1), I32)), F32)
    need = capf - jnp.sum((aff > thr).astype(F32), axis=1, keepdims=True)
    ii = lax.broadcasted_iota(I32, (LANE, LANE), 0)
    jj = lax.broadcasted_iota(I32, (LANE, LANE), 1)
    upper = (ii < jj).astype(BF16)

    def block(b, carry):
        eq_off, pos_off = carry
        sl = pl.ds(pl.multiple_of(b * LANE, LANE), LANE)
        a = a_ref[:, sl]
        eq = a == thr
        eqf = eq.astype(BF16)
        rank = _dot(eqf, upper) + eq_off
        sel = (a > thr) | (eq & (rank < need))
        self_ = sel.astype(BF16)
        pos = _dot(self_, upper) + pos_off
        g_ref[:, sl] = jnp.where(sel, a, 0.0)
        pos_ref[:, sl] = jnp.where(sel, pos, -1.0).astype(I32)
        cum_ref[:, sl] = pos.astype(I32)
        return (eq_off + jnp.sum(eqf.astype(F32), axis=1, keepdims=True),
                pos_off + jnp.sum(self_.astype(F32), axis=1, keepdims=True))

    zero = jnp.zeros((e, 1), F32)
    lax.fori_loop(0, n // LANE, block, (zero, zero))


def _route(aff_t, cap):
    e, n = aff_t.shape
    return pl.pallas_call(
        functools.partial(_route_kernel, cap=cap),
        out_shape=[jax.ShapeDtypeStruct((e, n), F32), jax.ShapeDtypeStruct((e, n), I32),
                   jax.ShapeDtypeStruct((e, n), I32)],
        name="ec_route",
    )(aff_t)


MOE_BLOCK = 1024
MOE_SUB = 256
MOE_WIN = 64
COMB_WIN = 64
COMB_MAX_ROUNDS = 5


MOE_PASS = 4


def _moe_gather_kernel(offs_ref, h_ref, pos_ref, xe_ref, *, n_sub):
    pss = pl.program_id(0)
    b = pl.program_id(1)
    subs = h_ref.shape[0] // MOE_SUB

    @pl.when(b == 0)
    def _():
        xe_ref[...] = jnp.zeros_like(xe_ref)

    def window(k, s, a, m):
        e = pss * MOE_PASS + k
        pos = pos_ref[pl.ds(e, 1), s * MOE_SUB:(s + 1) * MOE_SUB]
        r0 = pl.multiple_of(a + m * MOE_WIN, 16)
        slot = lax.broadcasted_iota(I32, (MOE_WIN, MOE_SUB), 0) + r0
        sel = jnp.where(slot == pos, 1.0, 0.0).astype(BF16)
        rows = pl.ds(r0, MOE_WIN)
        xe_ref[k, rows, :] = (xe_ref[k, rows, :].astype(F32)
                              + _dot(sel, h_ref[s * MOE_SUB:(s + 1) * MOE_SUB, :])).astype(BF16)

    extra = []
    slot0 = lax.broadcasted_iota(I32, (MOE_WIN, MOE_SUB), 0)
    for s in range(subs):
        sels, starts = [], []
        for k in range(MOE_PASS):
            base = (pss * MOE_PASS + k) * (n_sub + 1) + b * subs + s
            o = offs_ref[base]
            o2 = offs_ref[base + 1]
            a = pl.multiple_of((o // 16) * 16, 16)
            starts.append(a)
            extra.append((k, s, a, jnp.where(o2 > o, (o2 - a + MOE_WIN - 1) // MOE_WIN, 0)))
            pos = pos_ref[pl.ds(pss * MOE_PASS + k, 1), s * MOE_SUB:(s + 1) * MOE_SUB]
            sels.append(jnp.where(slot0 + a == pos, 1.0, 0.0).astype(BF16))
        got = _dot(jnp.concatenate(sels, axis=0), h_ref[s * MOE_SUB:(s + 1) * MOE_SUB, :])
        for k in range(MOE_PASS):
            rows = pl.ds(starts[k], MOE_WIN)
            xe_ref[k, rows, :] = (xe_ref[k, rows, :].astype(F32)
                                  + got[k * MOE_WIN:(k + 1) * MOE_WIN]).astype(BF16)
    for k, s, a, cnt in extra:
        lax.fori_loop(1, cnt, lambda m, c, k=k, s=s, a=a: (window(k, s, a, m), c)[1], 0)


def _moe_gather(h2, pos_t, offs, h_row0, n, cap):
    d = h2.shape[1]
    e = pos_t.shape[0]
    tb = min(MOE_BLOCK, n)
    n_blocks = n // tb
    blk0 = h_row0 // tb
    grid_spec = pltpu.PrefetchScalarGridSpec(
        num_scalar_prefetch=1,
        grid=(e // MOE_PASS, n_blocks),
        in_specs=[pl.BlockSpec((tb, d), lambda p, b, offs: (b + blk0, 0)),
                  pl.BlockSpec((e, tb), lambda p, b, offs: (0, b))],
        out_specs=pl.BlockSpec((MOE_PASS, cap + MOE_WIN, d), lambda p, b, offs: (p, 0, 0)),
    )
    return pl.pallas_call(
        functools.partial(_moe_gather_kernel, n_sub=n // MOE_SUB),
        grid_spec=grid_spec,
        out_shape=jax.ShapeDtypeStruct((e, cap + MOE_WIN, d), BF16),
        compiler_params=pltpu.CompilerParams(dimension_semantics=("arbitrary", "arbitrary"),
                                             vmem_limit_bytes=56 * 1024 * 1024),
        name="moe_gather",
    )(offs, h2, pos_t)


def _moe_ffn_kernel(*refs, caps, rchunks):
    ns = len(caps)
    xes, (wg_ref, wu_ref, wd_ref) = refs[:ns], refs[ns:ns + 3]
    yes, (wgb, wub, wdb) = refs[ns + 3:2 * ns + 3], refs[2 * ns + 3:]
    slab = 256

    def cast(i, carry):
        rows = pl.ds(pl.multiple_of(i * slab, slab), slab)
        wgb[rows, :] = wg_ref[0, 0, rows, :].astype(BF16)
        wub[rows, :] = wu_ref[0, 0, rows, :].astype(BF16)
        wdb[rows, :] = wd_ref[0, 0, rows, :].astype(BF16)
        return carry

    lax.fori_loop(0, wgb.shape[0] // slab, cast, 0)
    for xe_ref, ye_ref, cap, rchunk in zip(xes, yes, caps, rchunks):
        def chunk(ci, carry, xe_ref=xe_ref, ye_ref=ye_ref, rchunk=rchunk):
            rows = pl.ds(pl.multiple_of(ci * rchunk, rchunk), rchunk)
            xb = xe_ref[0, rows, :]
            hg = _dot(xb, wgb[...])
            hid = (hg * _sigmoid(hg) * _dot(xb, wub[...])).astype(BF16)
            ye_ref[0, rows, :] = _dot(hid, wdb[...]).astype(BF16)
            return carry

        lax.fori_loop(0, cap // rchunk, chunk, 0)
        ye_ref[0, cap:, :] = jnp.zeros((ye_ref.shape[1] - cap, ye_ref.shape[2]), BF16)


def _moe_ffn(xes, wg, wu, wd, layer, caps):
    e, _, d = xes[0].shape
    f = wg.shape[3]
    assert d == f
    rchunks = tuple(min(cap, 256) for cap in caps)
    ye_rows = [cap + COMB_MAX_ROUNDS * COMB_WIN for cap in caps]
    wspec = lambda r, c: pl.BlockSpec((1, 1, r, c), lambda ei: (layer, ei, 0, 0))
    return pl.pallas_call(
        functools.partial(_moe_ffn_kernel, caps=tuple(caps), rchunks=rchunks),
        grid=(e,),
        in_specs=[pl.BlockSpec((1, xe.shape[1], d), lambda ei: (ei, 0, 0)) for xe in xes]
                 + [wspec(d, f), wspec(d, f), wspec(f, d)],
        out_specs=[pl.BlockSpec((1, r, d), lambda ei: (ei, 0, 0)) for r in ye_rows],
        out_shape=[jax.ShapeDtypeStruct((e, r, d), BF16) for r in ye_rows],
        scratch_shapes=[pltpu.VMEM((d, f), BF16), pltpu.VMEM((d, f), BF16), pltpu.VMEM((f, d), BF16)],
        compiler_params=pltpu.CompilerParams(dimension_semantics=("arbitrary",),
                                             vmem_limit_bytes=56 * 1024 * 1024),
        name="moe_ffn",
    )(*xes, wg, wu, wd)


def _moe_window_copy(ye_hbm, buf, sem, slot, e, start):
    return pltpu.make_async_copy(ye_hbm.at[e, pl.ds(start, COMB_WIN), :], buf.at[slot, e], sem.at[slot, e])


def _moe_combine_kernel(offs_ref, rounds_ref, x_ref, g_ref, pos_ref, mod_ref, fin_ref, ye_hbm, o_ref,
                        buf, lhs, acc_ref, sem, *, n_tiles, final):
    j = pl.program_id(0)
    t = x_ref.shape[0]
    n_exp = g_ref.shape[0]
    slot_in_win = lax.broadcasted_iota(I32, (COMB_WIN, t), 0)

    def starts_of(tile, rnd):
        return [pl.multiple_of((offs_ref[e * (n_tiles + 1) + tile] // 16) * 16 + rnd * COMB_WIN, 16)
                for e in range(n_exp)]

    def fetch(slot, starts):
        for e in range(n_exp):
            _moe_window_copy(ye_hbm, buf, sem, slot, e, starts[e]).start()

    def land(slot, starts):
        for e in range(n_exp):
            _moe_window_copy(ye_hbm, buf, sem, slot, e, starts[e]).wait()

    def expand(slot, starts):
        for e in range(n_exp):
            val = jnp.where(pos_ref[e:e + 1, :] == slot_in_win + starts[e], g_ref[e:e + 1, :], 0.0)
            lhs[e * COMB_WIN:(e + 1) * COMB_WIN, :] = val.astype(BF16)
        return lax.dot_general(lhs[...], buf[slot].reshape(n_exp * COMB_WIN, buf.shape[3]),
                               (((0,), (0,)), ((), ())), preferred_element_type=F32)

    cur = j % 2
    first = starts_of(j, 0)

    @pl.when(j == 0)
    def _():
        fetch(0, first)

    @pl.when(j + 1 < n_tiles)
    def _():
        fetch(1 - cur, starts_of(j + 1, 0))

    land(cur, first)
    acc_ref[...] = expand(cur, first)

    def more(rnd, carry):
        starts = starts_of(j, rnd)
        fetch(2, starts)
        land(2, starts)
        acc_ref[...] += expand(2, starts)
        return carry

    lax.fori_loop(1, rounds_ref[j], more, 0)
    out = x_ref[...] + mod_ref[0, 5:6, :] * acc_ref[...]
    o_ref[...] = _rms(out) * fin_ref[...] if final else out


def _moe_combine(xall, g, pos, offs, rounds, mod, ye, tile0, n, mod_row, final_g=None):
    t = MOE_SUB
    d = xall.shape[1]
    n_exp = g.shape[0]
    n_tiles = n // t
    final = final_g is not None
    fin = final_g if final else jnp.ones((1, d), F32)
    grid_spec = pltpu.PrefetchScalarGridSpec(
        num_scalar_prefetch=2,
        grid=(n_tiles,),
        in_specs=[pl.BlockSpec((t, d), lambda j, offs, rounds: (j + tile0, 0)),
                  pl.BlockSpec((n_exp, t), lambda j, offs, rounds: (0, j)),
                  pl.BlockSpec((n_exp, t), lambda j, offs, rounds: (0, j)),
                  pl.BlockSpec((1, 6, d), lambda j, offs, rounds: (mod_row, 0, 0)),
                  pl.BlockSpec((1, d), lambda j, offs, rounds: (0, 0)),
                  pl.BlockSpec(memory_space=pl.ANY)],
        out_specs=pl.BlockSpec((t, d), lambda j, offs, rounds: (j + (0 if final else tile0), 0)),
        scratch_shapes=[pltpu.VMEM((3, n_exp, COMB_WIN, d), BF16),
                        pltpu.VMEM((n_exp * COMB_WIN, t), BF16),
                        pltpu.VMEM((t, d), F32), pltpu.SemaphoreType.DMA((3, n_exp))],
    )
    return pl.pallas_call(
        functools.partial(_moe_combine_kernel, n_tiles=n_tiles, final=final),
        grid_spec=grid_spec,
        out_shape=jax.ShapeDtypeStruct((n, d) if final else xall.shape, F32),
        input_output_aliases={} if final else {2: 0},
        compiler_params=_params("arbitrary"),
        name="moe_combine",
    )(offs, rounds, xall, g, pos, mod, fin, ye)


def _expert_choice(xall, h2, aff, mod, wg, wu, wd, layer, sets, final_g=None):
    routed = []
    for row0, h_row0, n, mod_row in sets:
        cap = EC_CAPACITY_FACTOR * n // N_EXPERTS
        g_t, pos_t, cum_t = _route(aff[row0:row0 + n].T, cap)
        offs = jnp.concatenate([cum_t[:, ::MOE_SUB], jnp.full((N_EXPERTS, 1), cap, I32)], axis=1)
        span = offs[:, 1:] - (offs[:, :-1] // 16) * 16
        rounds = jnp.maximum(jnp.max((span + COMB_WIN - 1) // COMB_WIN, axis=0), 1).astype(I32)
        offs = offs.reshape(-1)
        routed.append((cap, g_t, pos_t, offs, rounds, _moe_gather(h2, pos_t, offs, h_row0, n, cap)))
    yes = _moe_ffn([r[5] for r in routed], wg, wu, wd, layer, [r[0] for r in routed])
    for (row0, _, n, mod_row), (cap, g_t, pos_t, offs, rounds, _), ye in zip(sets, routed, yes):
        xall = _moe_combine(xall, g_t, pos_t, offs, rounds, mod, ye, row0 // MOE_SUB, n, 2 * layer + mod_row,
                            final_g)
    return xall


def kernel(x, c, ctx, c_ctx, w_mod, b_mod, norm1_g, norm2_g, w_in, s5_lam_re, s5_lam_im, s5_log_dt, s5_b_re, s5_b_im, s5_c_re, s5_c_im, s5_d, s5_w_glu, s5_b_glu, ssd_conv_w, ssd_conv_b, ssd_a_log, ssd_dt_bias, ssd_d, ssd_norm_g, attn_sink, w_branch, w_out, w_router, w_e_gate, w_e_up, w_e_down, final_norm_g):
    batch, n, d = x.shape
    nc = ctx.shape[1]
    depth = w_mod.shape[0]
    assert batch == 1 and nc == TOK_TILE and n % TOK_TILE == 0 and n % GRID_W == 0
    assert SSD_STATE == SSD_CHUNK and 2 * SSD_HEAD_DIM == LANE
    xall = jnp.concatenate([ctx[0], x[0]], axis=0)
    cvecs = jnp.zeros((8, d), F32).at[0].set(c_ctx).at[1].set(c[0])
    mods = _modulation(cvecs, w_mod, b_mod)
    cos, sin = _rope_tables(n, nc)
    row = lambda v: v.reshape(1, -1).astype(F32)
    mods = mods[:, 0:2].reshape(depth * 2, 6, d)
    vec3 = lambda v: v.reshape(depth, 1, -1).astype(F32)
    w_in_all = _prep_w_in(w_in)
    s5 = _s5_prep(s5_lam_re, s5_lam_im, s5_log_dt, s5_b_re, s5_b_im, s5_c_re, s5_c_im)
    conv_w = jnp.pad(ssd_conv_w, ((0, 0), (0, 3), (0, 0))).reshape(depth * 8, -1)
    conv_b = vec3(ssd_conv_b)
    a = -jnp.exp(ssd_a_log.astype(F32))
    ab = jnp.stack([a, ssd_dt_bias.astype(F32)], axis=2)
    prow = jnp.pad(ab, ((0, 0), (0, 0), (0, 6), (0, LANE - SSD_HEADS))).reshape(depth * 2 * 8, LANE)
    pcol = jnp.pad(ab.transpose(0, 1, 3, 2), ((0, 0), (0, 0), (0, 0), (0, LANE - 2))).reshape(-1, LANE)
    vecs = (vec3(s5_d), vec3(s5_b_glu), vec3(jnp.repeat(ssd_d, SSD_HEAD_DIM, axis=1)), vec3(ssd_norm_g),
            vec3(norm2_g))
    wr = jnp.pad(w_router.astype(F32), ((0, 0), (0, 0), (0, LANE - N_EXPERTS)))
    wrh = wr.astype(BF16)
    wr2 = jnp.concatenate([wrh, (wr - wrh.astype(F32)).astype(BF16)], axis=2)
    wbr = w_branch.at[:, 2].set(w_branch[:, 2][:, _attn_head_order()]).astype(BF16)
    wbr = wbr.reshape(depth * 3, wbr.shape[2], wbr.shape[3])
    wglu, wout = s5_w_glu.astype(BF16), w_out.astype(BF16)
    sink = attn_sink.astype(F32)
    g1 = vec3(norm1_g)
    for i in range(depth):
        u, z, xbc, q, kv, gates, dtr = _inproj(xall, mods, g1, w_in_all, cos, sin, i)
        ys5 = _s5_mix(u, nc, s5, i)
        t = _ssd_conv(xbc, conv_w, conv_b, i)
        dtrt = dtr[:, 0:2 * SSD_HEADS].T
        yf = _ssd_scan(t, dtr, dtrt, prow, pcol, nc, 0, i)
        yb = _ssd_scan(t, dtr, dtrt, prow, pcol, nc, 1, i)
        att = _attention(q, kv, sink, nc, i)
        xall, h2, aff = _merge(xall, mods, ys5, u, yf, yb, t, z, att, gates, vecs, wglu, wbr, wout, wr2, i)
        sets = [(nc, 0, n, 1)] + ([(0, n, nc, 0)] if i < depth - 1 else [])
        xall = _expert_choice(xall, h2, aff, mods, w_e_gate, w_e_up, w_e_down, i, sets,
                              row(final_norm_g) if i == depth - 1 else None)
    return xall[None]
```

```python
import functools
import math

import jax
import jax.numpy as jnp
from jax import lax
from jax.experimental import pallas as pl
from jax.experimental.pallas import tpu as pltpu

F32 = jnp.float32
BF16 = jnp.bfloat16
I32 = jnp.int32
HI = lax.Precision.HIGHEST

D_MODEL = 1024
DEPTH = 4
GRID_W = 64
EPS = 1e-6
S5_GROUPS = 32
S5_GROUP_CH = 16
S5_STATE = 64
S5_CHUNK = 32
SSD_HEADS = 8
SSD_HEAD_DIM = 64
SSD_GROUPS = 2
SSD_STATE = 128
SSD_CHUNK = 128
ATTN_HEADS = 8
ATTN_KV_HEADS = 2
ATTN_HEAD_DIM = 64
ATTN_BLOCK = 128
ROPE_BASE = 10000.0
N_EXPERTS = 16
EXPERT_FF = 1024
EC_CAPACITY_FACTOR = 2
IN_SIZES = (512, 512, 1024, 16, 512, 128, 128, 3072)
TOK_TILE = 256
LANE = 128
NEG = -1e30


def _dot(a, b):
    return jnp.dot(a, b, preferred_element_type=F32)


def _dot_hi(a, b):
    return jnp.dot(a, b, precision=HI, preferred_element_type=F32)


def _dot_nt(a, b):
    return lax.dot_general(a, b, (((1,), (1,)), ((), ())), preferred_element_type=F32)


def _sigmoid(x):
    return 0.5 + 0.5 * jnp.tanh(0.5 * x)


def _softplus(x):
    return jnp.maximum(x, 0.0) + jnp.log(1.0 + jnp.exp(-jnp.abs(x)))


def _rms(x):
    return x * lax.rsqrt(jnp.mean(x * x, axis=-1, keepdims=True) + EPS)


def _params(*sem):
    return pltpu.CompilerParams(dimension_semantics=sem)


def _mod_kernel(c_ref, w_ref, b_ref, o_ref):
    c = c_ref[...]
    o_ref[0] = _dot_hi(c * _sigmoid(c), w_ref[0]) + b_ref[0]


def _modulation(cvecs, w_mod, b_mod):
    depth, d, d6 = w_mod.shape
    bn = 1536
    return pl.pallas_call(
        _mod_kernel,
        grid=(depth, d6 // bn),
        in_specs=[pl.BlockSpec((8, d), lambda l, j: (0, 0)),
                  pl.BlockSpec((1, d, bn), lambda l, j: (l, 0, j)),
                  pl.BlockSpec((1, 1, bn), lambda l, j: (l, 0, j))],
        out_specs=pl.BlockSpec((1, 8, bn), lambda l, j: (l, 0, j)),
        out_shape=jax.ShapeDtypeStruct((depth, 8, d6), F32),
        compiler_params=_params("arbitrary", "arbitrary"),
        name="modulation",
    )(cvecs, w_mod, b_mod.reshape(depth, 1, d6))


W_IN_COLS = 6016


def _attn_head_order():
    rep = ATTN_HEADS // ATTN_KV_HEADS
    heads = [g * rep + b for b in range(rep) for g in range(ATTN_KV_HEADS)]
    return jnp.concatenate([jnp.arange(h * ATTN_HEAD_DIM, (h + 1) * ATTN_HEAD_DIM) for h in heads])


def _prep_w_in(w):
    parts, start = [], 0
    for s in IN_SIZES:
        parts.append(w[..., start:start + s])
        start += s
    u, z, xbc, dt, q, k, v, gates = parts
    dt = jnp.pad(dt, ((0, 0), (0, 0), (0, LANE - dt.shape[-1])))
    q = q[..., _attn_head_order()]
    return jnp.concatenate([u, z, xbc, q, k, v, gates, dt], axis=-1).astype(BF16)


def _inproj_kernel(x_ref, mod_ref, g_ref, w_ref, cos_ref, sin_ref,
                   u_ref, z_ref, xbc_ref, q_ref, kv_ref, gates_ref, dt_ref):
    m = mod_ref[0]
    h = (_rms(x_ref[...]) * g_ref[...] * (1.0 + m[1:2]) + m[0:1]).astype(BF16)

    def proj(a, b):
        return _dot(h, w_ref[:, a:b])

    u_ref[...] = proj(0, 512)
    z_ref[...] = proj(512, 1024).astype(z_ref.dtype)
    xbc_ref[...] = proj(1024, 2048)
    cos = cos_ref[...]
    sin = sin_ref[...]
    lane = lax.broadcasted_iota(I32, cos.shape, 1)
    first = (lane % 32) < 16

    def rope(v):
        partner = jnp.where(first, pltpu.roll(v, LANE - 16, 1), pltpu.roll(v, 16, 1))
        return v * cos + partner * sin

    scale = ATTN_HEAD_DIM ** -0.5
    for j in range(2):
        qq = proj(2048 + 2 * j * LANE, 2048 + 2 * (j + 1) * LANE)
        for half in range(2):
            q_ref[:, (2 * j + half) * LANE:(2 * j + half + 1) * LANE] = (
                rope(qq[:, half * LANE:(half + 1) * LANE]) * scale).astype(BF16)
    kvp = proj(2560, 2816)
    kv_ref[:, 0:LANE] = rope(kvp[:, 0:LANE]).astype(BF16)
    kv_ref[:, LANE:2 * LANE] = kvp[:, LANE:2 * LANE].astype(BF16)
    for j in range(6):
        gates_ref[:, j * 512:(j + 1) * 512] = proj(2816 + j * 512, 2816 + (j + 1) * 512).astype(BF16)
    dt_ref[...] = proj(5888, 6016)


def _inproj(xall, mods, g, w, cos, sin, layer):
    n_all, d = xall.shape
    t = TOK_TILE
    row = lambda i: (i, 0)
    const = lambda i: (0, 0)
    widths = (512, 512, 1024, 512, 256, 3072, LANE)
    dtypes = (F32, BF16, F32, BF16, BF16, BF16, F32)
    return pl.pallas_call(
        _inproj_kernel,
        grid=(n_all // t,),
        in_specs=[pl.BlockSpec((t, d), row),
                  pl.BlockSpec((1, 6, d), lambda i: (2 * layer + jnp.minimum(i, 1), 0, 0)),
                  pl.BlockSpec((None, 1, d), lambda i: (layer, 0, 0)),
                  pl.BlockSpec((None, d, W_IN_COLS), lambda i: (layer, 0, 0), pipeline_mode=pl.Buffered(1)),
                  pl.BlockSpec((t, LANE), row),
                  pl.BlockSpec((t, LANE), row)],
        out_specs=[pl.BlockSpec((t, wd), row) for wd in widths],
        out_shape=[jax.ShapeDtypeStruct((n_all, wd), dt) for wd, dt in zip(widths, dtypes)],
        compiler_params=_params("arbitrary"),
        name="inproj",
    )(xall, mods, g, w, cos, sin)


def _rope_tables(n, nc):
    rows = n // GRID_W
    r = jnp.repeat(jnp.arange(rows, dtype=F32), GRID_W)
    c = jnp.tile(jnp.arange(GRID_W, dtype=F32), rows)
    m = ATTN_HEAD_DIM // 4
    inv_freq = ROPE_BASE ** (-jnp.arange(m, dtype=F32) / m)
    ang_r, ang_c = r[:, None] * inv_freq, c[:, None] * inv_freq
    cos = jnp.concatenate([jnp.cos(ang_r), jnp.cos(ang_r), jnp.cos(ang_c), jnp.cos(ang_c)], axis=1)
    sin = jnp.concatenate([-jnp.sin(ang_r), jnp.sin(ang_r), -jnp.sin(ang_c), jnp.sin(ang_c)], axis=1)
    cos = jnp.concatenate([jnp.ones((nc, 64), F32), cos], axis=0)
    sin = jnp.concatenate([jnp.zeros((nc, 64), F32), sin], axis=0)
    return jnp.tile(cos, (1, 2)), jnp.tile(sin, (1, 2))


def _s5_toeplitz_kernel(bbf_ref, caf_ref, bbb_ref, cab_ref, kt_ref):
    L, K = S5_CHUNK, S5_GROUP_CH
    lk = L * K
    tf = _dot_hi(bbf_ref[0], caf_ref[0])
    tb = _dot_hi(bbb_ref[0], cab_ref[0])
    lane = lax.broadcasted_iota(I32, tf.shape, 1)
    r = pltpu.roll(tf, lk - K, 1)
    tab = jnp.concatenate([tb + jnp.where(lane >= lk - K, r, 0.0), jnp.where(lane < lk - K, r, 0.0)], axis=1)
    for s_ in range(L):
        lo = (L - 1 - s_) * K
        kt_ref[0, s_ * K:(s_ + 1) * K, :] = tab[:, lo:lo + L * K].astype(BF16)


def _s5_prep_layer(lam_re, lam_im, log_dt, b_re, b_im, c_re, c_im):
    L, G, P, K = S5_CHUNK, S5_GROUPS, S5_STATE, S5_GROUP_CH
    lk = L * K
    lr, li = lam_re.astype(F32), lam_im.astype(F32)
    dt = jnp.exp(log_dt.astype(F32))[..., None]
    mag = jnp.exp(lr * dt)
    abr, abi = mag * jnp.cos(li * dt), mag * jnp.sin(li * dt)
    den = lr * lr + li * li
    fr = ((abr - 1.0) * lr + abi * li) / den
    fi = (abi * lr - (abr - 1.0) * li) / den
    bbr = fr[..., None] * b_re - fi[..., None] * b_im
    bbi = fr[..., None] * b_im + fi[..., None] * b_re
    tau = jnp.arange(L + 1, dtype=F32)[:, None, None, None]
    pm = jnp.exp(lr * dt * tau)
    apr, api = pm * jnp.cos(li * dt * tau), pm * jnp.sin(li * dt * tau)
    cr, ci = c_re.astype(F32), c_im.astype(F32)
    car = cr * apr[:, :, :, None, :] - ci * api[:, :, :, None, :]
    cai = cr * api[:, :, :, None, :] + ci * apr[:, :, :, None, :]

    lagmat = lambda v, d: v[:L, d].transpose(1, 3, 0, 2)
    caf = jnp.concatenate([lagmat(car, 0), lagmat(cai, 0)], axis=1).reshape(G, 2 * P, lk)
    cab = jnp.flip(jnp.concatenate([lagmat(car, 1), lagmat(cai, 1)], axis=1), axis=2).reshape(G, 2 * P, lk)
    tr = lambda v: v.transpose(0, 2, 1)
    bbf = jnp.concatenate([tr(bbr[0]), -tr(bbi[0])], axis=2)
    bbb = jnp.concatenate([tr(bbr[1]), -tr(bbi[1])], axis=2)

    def summary(d, rev):
        ar, ai = apr[:L, d], api[:L, d]
        if rev:
            ar, ai = jnp.flip(ar, axis=0), jnp.flip(ai, axis=0)
        ar, ai = ar[..., None], ai[..., None]
        re = ar * bbr[d][None] - ai * bbi[d][None]
        im = ar * bbi[d][None] + ai * bbr[d][None]
        f = lambda v: v.transpose(1, 0, 3, 2).reshape(G, lk, P)
        return f(re), f(im)

    sfr, sfi = summary(0, True)
    sbr, sbi = summary(1, False)
    sb = jnp.concatenate([sfr, sfi, sbr, sbi], axis=2).astype(BF16)

    def readout(v):
        return v.transpose(1, 3, 0, 2).reshape(G, P, lk)

    back = lambda v: jnp.flip(v[1:L + 1, 1], axis=0)
    rc = jnp.concatenate([readout(car[1:L + 1, 0]), -readout(cai[1:L + 1, 0]),
                          readout(back(car)), -readout(back(cai))], axis=1).astype(BF16)
    coef = jnp.stack([apr[L, 0].reshape(-1), api[L, 0].reshape(-1),
                      apr[L, 1].reshape(-1), api[L, 1].reshape(-1)], axis=0)
    coef = jnp.pad(coef, ((0, 4), (0, 0)))
    return bbf, caf, bbb, cab, sb, rc, coef


def _s5_prep(*params):
    L, G, P, K = S5_CHUNK, S5_GROUPS, S5_STATE, S5_GROUP_CH
    lk = L * K
    bbf, caf, bbb, cab, sb, rc, coef = jax.vmap(_s5_prep_layer)(*params)
    flat = lambda v: v.reshape((-1,) + v.shape[2:])
    n_mat = bbf.shape[0] * G
    bspec = pl.BlockSpec((1, K, 2 * P), lambda g: (g, 0, 0))
    cspec = pl.BlockSpec((1, 2 * P, lk), lambda g: (g, 0, 0))
    kt = pl.pallas_call(
        _s5_toeplitz_kernel,
        grid=(n_mat,),
        in_specs=[bspec, cspec, bspec, cspec],
        out_specs=pl.BlockSpec((1, lk, lk), lambda g: (g, 0, 0)),
        out_shape=jax.ShapeDtypeStruct((n_mat, lk, lk), BF16),
        compiler_params=_params("arbitrary"),
        name="s5_toeplitz",
    )(flat(bbf), flat(caf), flat(bbb), flat(cab))
    return kt, flat(sb), flat(rc), flat(coef)


def _s5_states_kernel(u_ref, sb_ref, fre, fim, bre, bim):
    p = S5_STATE
    s0 = _dot(u_ref[0], sb_ref[0])
    s1 = _dot(u_ref[1], sb_ref[1])
    for q, ref in enumerate((fre, fim, bre, bim)):
        ref[...] = jnp.concatenate([s0[:, q * p:(q + 1) * p], s1[:, q * p:(q + 1) * p]], axis=1)


def _s5_rec_kernel(coef_ref, sfr, sfi, sbr, sbi, hfr, hfi, hbr, hbi, *, n_chunks, ctx_chunks):
    arf, aif = coef_ref[0:1, :], coef_ref[1:2, :]
    arb, aib = coef_ref[2:3, :], coef_ref[3:4, :]
    zero = jnp.zeros_like(arf)

    def step(i, carry):
        fr, fi, br, bi = carry
        cb = jnp.where(i < ctx_chunks, ctx_chunks - 1 - i, n_chunks - 1 - (i - ctx_chunks))
        hfr[pl.ds(i, 1), :] = fr
        hfi[pl.ds(i, 1), :] = fi
        hbr[pl.ds(cb, 1), :] = br
        hbi[pl.ds(cb, 1), :] = bi
        return (arf * fr - aif * fi + sfr[pl.ds(i, 1), :], arf * fi + aif * fr + sfi[pl.ds(i, 1), :],
                arb * br - aib * bi + sbr[pl.ds(cb, 1), :], arb * bi + aib * br + sbi[pl.ds(cb, 1), :])

    lax.fori_loop(0, n_chunks, step, (zero, zero, zero, zero))


def _s5_out_kernel(u_ref, kt_ref, hfr, hfi, hbr, hbi, rc_ref, y_ref):
    p = S5_STATE
    for i in range(2):
        h = jnp.concatenate([r[:, i * p:(i + 1) * p] for r in (hfr, hfi, hbr, hbi)], axis=1).astype(BF16)
        y_ref[i] = (_dot(u_ref[i], kt_ref[i]) + _dot(h, rc_ref[i])).astype(y_ref.dtype)


S5_LANE_GROUPS = LANE // S5_GROUP_CH
S5_T_LO = 8


def _s5_perm():
    i = jnp.arange(S5_T_LO * LANE)
    t_lo, gl, k = i // LANE, (i % LANE) // S5_GROUP_CH, i % S5_GROUP_CH
    j = gl * (S5_T_LO * S5_GROUP_CH) + t_lo * S5_GROUP_CH + k
    return jnp.zeros((S5_T_LO * LANE, S5_T_LO * LANE), BF16).at[i, j].set(1.0)


def _s5_pack_kernel(u_ref, perm_ref, o_ref, *, n_chunks):
    L = S5_CHUNK
    for t_hi in range(L // S5_T_LO):
        z = jnp.concatenate([u_ref[pl.ds(t_hi * S5_T_LO + t_lo, n_chunks, stride=L), :].astype(BF16)
                             for t_lo in range(S5_T_LO)], axis=1)
        w = _dot(z, perm_ref[...]).astype(BF16)
        for gl in range(S5_LANE_GROUPS):
            o_ref[gl, :, t_hi * LANE:(t_hi + 1) * LANE] = w[:, gl * LANE:(gl + 1) * LANE]


def _s5_unpack_kernel(y_ref, perm_ref, o_ref, *, n_chunks):
    L = S5_CHUNK
    for t_hi in range(L // S5_T_LO):
        w = jnp.concatenate([y_ref[gl, :, t_hi * LANE:(t_hi + 1) * LANE] for gl in range(S5_LANE_GROUPS)], axis=1)
        z = _dot_nt(w, perm_ref[...])
        for t_lo in range(S5_T_LO):
            o_ref[pl.ds(t_hi * S5_T_LO + t_lo, n_chunks, stride=L), :] = z[:, t_lo * LANE:(t_lo + 1) * LANE]


def _s5_mix(u, nc, prep, layer):
    kt, sb, rc, coef = prep
    n_all = u.shape[0]
    L, G, K, P = S5_CHUNK, S5_GROUPS, S5_GROUP_CH, S5_STATE
    C = n_all // L
    lk = L * K
    perm = _s5_perm()
    lg = S5_LANE_GROUPS
    perm_spec = pl.BlockSpec(perm.shape, lambda b: (0, 0))
    ug = pl.pallas_call(
        functools.partial(_s5_pack_kernel, n_chunks=C),
        grid=(G // lg,),
        in_specs=[pl.BlockSpec((n_all, LANE), lambda b: (0, b)), perm_spec],
        out_specs=pl.BlockSpec((lg, C, lk), lambda b: (b, 0, 0)),
        out_shape=jax.ShapeDtypeStruct((G, C, lk), BF16),
        compiler_params=_params("arbitrary"),
        name="s5_pack",
    )(u, perm)
    gp = G // 2
    st_shape = jax.ShapeDtypeStruct((C, G * P), F32)
    st_spec = pl.BlockSpec((C, 2 * P), lambda p: (0, p))
    states = pl.pallas_call(
        _s5_states_kernel,
        grid=(gp,),
        in_specs=[pl.BlockSpec((2, C, lk), lambda p: (p, 0, 0)),
                  pl.BlockSpec((2, lk, 4 * P), lambda p: (layer * gp + p, 0, 0))],
        out_specs=[st_spec] * 4,
        out_shape=[st_shape] * 4,
        compiler_params=_params("arbitrary"),
        name="s5_states",
    )(ug, sb)
    cb = 512
    col = pl.BlockSpec((C, cb), lambda j: (0, j))
    hs = pl.pallas_call(
        functools.partial(_s5_rec_kernel, n_chunks=C, ctx_chunks=nc // L),
        grid=(G * P // cb,),
        in_specs=[pl.BlockSpec((8, cb), lambda j: (layer, j))] + [col] * 4,
        out_specs=[col] * 4,
        out_shape=[st_shape] * 4,
        compiler_params=_params("arbitrary"),
        name="s5_recurrence",
    )(coef, *states)
    y = pl.pallas_call(
        _s5_out_kernel,
        grid=(gp,),
        in_specs=[pl.BlockSpec((2, C, lk), lambda p: (p, 0, 0)),
                  pl.BlockSpec((2, lk, lk), lambda p: (layer * gp + p, 0, 0))] + [st_spec] * 4
                 + [pl.BlockSpec((2, 4 * P, lk), lambda p: (layer * gp + p, 0, 0))],
        out_specs=pl.BlockSpec((2, C, lk), lambda p: (p, 0, 0)),
        out_shape=jax.ShapeDtypeStruct((G, C, lk), BF16),
        compiler_params=_params("arbitrary"),
        name="s5_out",
    )(ug, kt, *hs, rc)
    return pl.pallas_call(
        functools.partial(_s5_unpack_kernel, n_chunks=C),
        grid=(G // lg,),
        in_specs=[pl.BlockSpec((lg, C, lk), lambda b: (b, 0, 0)), perm_spec],
        out_specs=pl.BlockSpec((n_all, LANE), lambda b: (0, b)),
        out_shape=jax.ShapeDtypeStruct((n_all, G * K), F32),
        compiler_params=_params("arbitrary"),
        name="s5_unpack",
    )(y, perm)


def _conv_kernel(cur_ref, prev_ref, next_ref, w_ref, b_ref, o_ref, *, n_tiles):
    i = pl.program_id(0)
    cur = cur_ref[...]
    t = cur.shape[0]
    pv = prev_ref[...] * jnp.where(i >= 2, 1.0, 0.0)
    nx = next_ref[...] * jnp.where((i >= 1) & (i <= n_tiles - 2), 1.0, 0.0)
    row8 = lax.broadcasted_iota(I32, pv.shape, 0)
    acc = b_ref[...] + w_ref[2:3, :] * cur
    for s in (1, 2):
        r = pltpu.roll(cur, s, 0)
        head = jnp.where(row8 < s, pltpu.roll(pv, s, 0), r[0:8])
        acc = acc + w_ref[2 - s:3 - s, :] * jnp.concatenate([head, r[8:]], axis=0)
        r = pltpu.roll(cur, t - s, 0)
        tail = jnp.where(row8 >= 8 - s, pltpu.roll(nx, 8 - s, 0), r[t - 8:])
        acc = acc + w_ref[2 + s:3 + s, :] * jnp.concatenate([r[:t - 8], tail], axis=0)
    o_ref[...] = acc * _sigmoid(acc)


def _ssd_conv(xbc, conv_w, conv_b, layer):
    n_all, ch = xbc.shape
    t = TOK_TILE
    n_tiles = n_all // t
    per = t // 8
    return pl.pallas_call(
        functools.partial(_conv_kernel, n_tiles=n_tiles),
        grid=(n_tiles,),
        in_specs=[pl.BlockSpec((t, ch), lambda i: (i, 0)),
                  pl.BlockSpec((8, ch), lambda i: (jnp.maximum(i * per - 1, 0), 0)),
                  pl.BlockSpec((8, ch), lambda i: (jnp.minimum((i + 1) * per, n_tiles * per - 1), 0)),
                  pl.BlockSpec((8, ch), lambda i: (layer, 0)),
                  pl.BlockSpec((None, 1, ch), lambda i: (layer, 0, 0))],
        out_specs=pl.BlockSpec((t, ch), lambda i: (i, 0)),
        out_shape=jax.ShapeDtypeStruct((n_all, ch), F32),
        compiler_params=_params("arbitrary"),
        name="ssd_conv",
    )(xbc, xbc, xbc, conv_w, conv_b)


def _ssd_kernel(t_ref, dtr_ref, dtrt_ref, prow_ref, pcol_ref, y_ref, h_ref, *, reverse, d):
    q = SSD_CHUNK
    hpg = SSD_HEADS // SSD_GROUPS
    p = SSD_HEAD_DIM

    @pl.when(pl.program_id(0) == 0)
    def _():
        h_ref[...] = jnp.zeros_like(h_ref)

    lo = d * SSD_HEADS
    dt_c = _softplus(dtr_ref[:, lo:lo + SSD_HEADS] + prow_ref[1:2, 0:SSD_HEADS])
    adt_c = dt_c * prow_ref[0:1, 0:SSD_HEADS]
    dt_r = _softplus(dtrt_ref[lo:lo + SSD_HEADS, :] + pcol_ref[:, 1:2])
    adt_r = dt_r * pcol_ref[:, 0:1]
    ii = lax.broadcasted_iota(I32, (q, q), 0)
    jj = lax.broadcasted_iota(I32, (q, q), 1)
    causal = (jj >= ii) if reverse else (jj <= ii)
    acum_c = _dot_hi(causal.astype(F32), adt_c)
    acum_r = _dot_hi(adt_r, ((ii >= jj) if reverse else (ii <= jj)).astype(F32))
    tot = acum_c[0:1, :] if reverse else acum_c[q - 1:q, :]
    gw = hpg * p
    lane_head = lax.broadcasted_iota(I32, (q, gw), 1) // p
    lane_head_row = lax.broadcasted_iota(I32, (1, gw), 1) // p
    lane_q = lax.broadcasted_iota(I32, (q, q), 1)
    for g in range(SSD_GROUPS):
        bg = t_ref[:, 512 + g * SSD_STATE:512 + (g + 1) * SSD_STATE]
        cg = t_ref[:, 768 + g * SSD_STATE:768 + (g + 1) * SSD_STATE].astype(BF16)
        cb = _dot_nt(cg, bg.astype(BF16))
        bgt = bg.T.astype(BF16)
        xg = t_ref[:, g * gw:(g + 1) * gw]
        xgb = xg.astype(BF16)
        hg = h_ref[g]
        yg = jnp.zeros((q, gw), F32)
        dec = jnp.zeros((1, gw), F32)
        acs, dts = [], []
        for r in range(hpg):
            hd = g * hpg + r
            ac = jnp.broadcast_to(acum_c[:, hd:hd + 1], (q, q))
            acs.append(ac)
            dts.append(jnp.broadcast_to(dt_c[:, hd:hd + 1], (q, q)))
            wts = (cb * jnp.exp(jnp.where(causal, ac - acum_r[hd:hd + 1, :], NEG)) * dt_r[hd:hd + 1, :]).astype(BF16)
            yg = jnp.where(lane_head == r, _dot(wts, xgb), yg)
            dec = jnp.where(lane_head_row == r, jnp.exp(tot[:, hd:hd + 1]), dec)
        spread = lambda v: jnp.concatenate([jnp.where(lane_q < p, v[2 * b], v[2 * b + 1]) for b in range(hpg // 2)],
                                           axis=1)
        acg = spread(acs)
        y_ref[:, g * gw:(g + 1) * gw] = (yg + _dot(cg, hg.astype(BF16)) * jnp.exp(acg)).astype(y_ref.dtype)
        xw = (xg * (jnp.exp(dec_log(tot, g, hpg, lane_head_row) - acg) * spread(dts))).astype(BF16)
        h_ref[g] = hg * dec + _dot(bgt, xw)


def dec_log(tot, g, hpg, lane_head_row):
    out = jnp.zeros(lane_head_row.shape, F32)
    for r in range(hpg):
        out = jnp.where(lane_head_row == r, tot[:, g * hpg + r:g * hpg + r + 1], out)
    return out


def _ssd_scan(t, dtr, dtrt, prow, pcol, nc, d, layer):
    n_all = t.shape[0]
    q = SSD_CHUNK
    n_chunks = n_all // q
    cc = nc // q
    reverse = d == 1
    if reverse:
        order = lambda i: jnp.where(i < cc, cc - 1 - i, n_chunks - 1 - (i - cc))
    else:
        order = lambda i: i
    return pl.pallas_call(
        functools.partial(_ssd_kernel, reverse=reverse, d=d),
        grid=(n_chunks,),
        in_specs=[pl.BlockSpec((q, t.shape[1]), lambda i: (order(i), 0)),
                  pl.BlockSpec((q, LANE), lambda i: (order(i), 0)),
                  pl.BlockSpec((2 * SSD_HEADS, q), lambda i: (0, order(i))),
                  pl.BlockSpec((8, LANE), lambda i: (2 * layer + d, 0)),
                  pl.BlockSpec((SSD_HEADS, LANE), lambda i: (2 * layer + d, 0))],
        out_specs=pl.BlockSpec((q, SSD_HEADS * SSD_HEAD_DIM), lambda i: (order(i), 0)),
        out_shape=jax.ShapeDtypeStruct((n_all, SSD_HEADS * SSD_HEAD_DIM), BF16),
        scratch_shapes=[pltpu.VMEM((SSD_GROUPS, SSD_STATE, (SSD_HEADS // SSD_GROUPS) * SSD_HEAD_DIM), F32)],
        compiler_params=_params("arbitrary"),
        name="ssd_scan_bwd" if reverse else "ssd_scan_fwd",
    )(t, dtr, dtrt, prow, pcol)


def _attn_kernel(sink_ref, q_ref, kp_ref, ko_ref, kn_ref, kc_ref, bias_ref, o_ref, *, n_blocks, ctx_blocks, layer):
    c = pl.program_id(0)
    blk = ATTN_BLOCK
    dh = ATTN_HEAD_DIM
    rep = ATTN_HEADS // ATTN_KV_HEADS
    rows = rep * blk
    kw = ATTN_KV_HEADS * dh
    lat = c >= ctx_blocks
    lo = jnp.where(lat & (c > ctx_blocks), 0, blk)
    hi = jnp.where(lat, jnp.where(c < n_blocks - 1, 3 * blk, 2 * blk), 0)
    col = lax.broadcasted_iota(I32, (1, 3 * blk), 1)
    bias = bias_ref[...] + jnp.where((col >= lo) & (col < hi), 0.0, NEG)
    rowhead = lax.broadcasted_iota(I32, (rows, 1), 0) // blk
    lane_q = lax.broadcasted_iota(I32, (blk, kw), 1)
    kloc = jnp.concatenate([kp_ref[:, 0:kw], ko_ref[:, 0:kw], kn_ref[:, 0:kw]], axis=0)
    vloc = jnp.concatenate([kp_ref[:, kw:2 * kw], ko_ref[:, kw:2 * kw], kn_ref[:, kw:2 * kw]], axis=0)
    kctx = kc_ref[:, 0:kw]
    vctx = kc_ref[:, kw:2 * kw]
    lane_l = lax.broadcasted_iota(I32, vloc.shape, 1)
    lane_c = lax.broadcasted_iota(I32, vctx.shape, 1)
    one = jnp.ones((), BF16)
    outs = []
    for g in range(ATTN_KV_HEADS):
        own = (lane_q >= g * dh) & (lane_q < (g + 1) * dh)
        vl = jnp.where((lane_l >= g * dh) & (lane_l < (g + 1) * dh), vloc, one)
        vc = jnp.where((lane_c >= g * dh) & (lane_c < (g + 1) * dh), vctx, one)
        qg = jnp.concatenate([jnp.where(own, q_ref[:, b * kw:(b + 1) * kw], jnp.zeros((), BF16))
                              for b in range(rep)], axis=0)
        s_loc = _dot_nt(qg, kloc) + bias
        s_ctx = _dot_nt(qg, kctx)
        sink = jnp.zeros((rows, 1), F32)
        for r in range(rep):
            sink = jnp.where(rowhead == r, sink_ref[layer, g * rep + r], sink)
        mx = jnp.maximum(jnp.maximum(jnp.max(s_loc, axis=1, keepdims=True),
                                     jnp.max(s_ctx, axis=1, keepdims=True)), sink)
        p_loc = jnp.exp((s_loc - mx).astype(BF16))
        p_ctx = jnp.exp((s_ctx - mx).astype(BF16))
        pv = _dot(p_loc, vl) + _dot(p_ctx, vc)
        outs.append(pv / (pltpu.roll(pv, dh, 1) + jnp.exp(sink - mx)))
    for b in range(rep):
        o_ref[:, b * kw:(b + 1) * kw] = jnp.where(lane_q < dh, outs[0][b * blk:(b + 1) * blk],
                                                  outs[1][b * blk:(b + 1) * blk]).astype(o_ref.dtype)


def _attention(q, kv, sink, nc, layer):
    n_all = q.shape[0]
    blk = ATTN_BLOCK
    n_blocks = n_all // blk
    cbk = nc // blk
    kvw = kv.shape[1]
    rows = (ATTN_HEADS // ATTN_KV_HEADS) * blk
    qi = jnp.arange(rows)[:, None] % blk
    kj = jnp.arange(3 * blk)[None, :] - blk
    band = jnp.where(jnp.abs(qi - kj) <= blk, 0.0, NEG).astype(F32)
    return pl.pallas_call(
        functools.partial(_attn_kernel, n_blocks=n_blocks, ctx_blocks=cbk, layer=layer),
        grid=(n_blocks,),
        in_specs=[pl.BlockSpec(memory_space=pltpu.SMEM),
                  pl.BlockSpec((blk, q.shape[1]), lambda c: (c, 0)),
                  pl.BlockSpec((blk, kvw), lambda c: (jnp.maximum(c - 1, 0), 0)),
                  pl.BlockSpec((blk, kvw), lambda c: (c, 0)),
                  pl.BlockSpec((blk, kvw), lambda c: (jnp.minimum(c + 1, n_blocks - 1), 0)),
                  pl.BlockSpec((nc, kvw), lambda c: (0, 0)),
                  pl.BlockSpec((rows, 3 * blk), lambda c: (0, 0))],
        out_specs=pl.BlockSpec((blk, q.shape[1]), lambda c: (c, 0)),
        out_shape=jax.ShapeDtypeStruct((n_all, q.shape[1]), BF16),
        compiler_params=_params("arbitrary"),
        name="window_attention",
    )(sink, q, kv, kv, kv, kv, band)


def _merge_kernel(x_ref, mod_ref, ys5_ref, u_ref, yf_ref, yb_ref, xs_ref, z_ref, att_ref, gates_ref,
                  s5d_ref, bglu_ref, ssdd_ref, ssdg_ref, n2g_ref, wglu_ref, wbr_ref, wout_ref, wr2_ref,
                  xo_ref, h2_ref, aff_ref, br_ref):
    m = mod_ref[0]
    d = x_ref.shape[1]
    a = jax.nn.gelu(ys5_ref[...].astype(F32) + s5d_ref[...] * u_ref[...])
    ya = (a * _sigmoid(_dot(a.astype(BF16), wglu_ref[...]) + bglu_ref[...])).astype(BF16)
    z = z_ref[...].astype(F32)
    yz = (yf_ref[...].astype(F32) + yb_ref[...].astype(F32) + ssdd_ref[...] * xs_ref[...]) * (z * _sigmoid(z))
    yb = (_rms(yz) * ssdg_ref[...]).astype(BF16)
    yc = att_ref[...].astype(BF16)
    cw = 256
    for j in range(d // cw):
        cs = slice(j * cw, (j + 1) * cw)
        br = (_sigmoid(gates_ref[:, j * cw:(j + 1) * cw].astype(F32)) * _dot(ya, wbr_ref[0, :, cs])
              + _sigmoid(gates_ref[:, d + j * cw:d + (j + 1) * cw].astype(F32)) * _dot(yb, wbr_ref[1, :, cs])
              + _sigmoid(gates_ref[:, 2 * d + j * cw:2 * d + (j + 1) * cw].astype(F32)) * _dot(yc, wbr_ref[2, :, cs]))
        br_ref[:, cs] = br.astype(BF16)
    xn = x_ref[...] + m[2:3] * _dot(br_ref[...], wout_ref[...])
    xo_ref[...] = xn
    h2 = _rms(xn) * n2g_ref[...] * (1.0 + m[4:5]) + m[3:4]
    hi = h2.astype(BF16)
    h2_ref[...] = hi
    lo = (h2 - hi.astype(F32)).astype(BF16)
    both = _dot(hi, wr2_ref[...])
    logits = (both[:, 0:LANE] + both[:, LANE:2 * LANE] + _dot(lo, wr2_ref[:, 0:LANE]))[:, 0:N_EXPERTS]
    e = jnp.exp(logits - jnp.max(logits, axis=1, keepdims=True))
    aff_ref[...] = e / jnp.sum(e, axis=1, keepdims=True)


def _merge(xall, mods, ys5, u, yf, yb, t, z, att, gates, vecs, wglu, wbr, wout, wr2, layer):
    n_all, d = xall.shape
    tt = TOK_TILE
    row = lambda i: (i, 0)
    lay3 = lambda i: (layer, 0, 0)
    vec = lambda wd: pl.BlockSpec((None, 1, wd), lay3)
    bw = 512
    s5d, bglu, ssdd, ssdg, n2g = vecs
    return pl.pallas_call(
        _merge_kernel,
        grid=(n_all // tt,),
        in_specs=[pl.BlockSpec((tt, d), row),
                  pl.BlockSpec((1, 6, d), lambda i: (2 * layer + jnp.minimum(i, 1), 0, 0)),
                  pl.BlockSpec((tt, bw), row), pl.BlockSpec((tt, bw), row), pl.BlockSpec((tt, bw), row),
                  pl.BlockSpec((tt, bw), row), pl.BlockSpec((tt, bw), row), pl.BlockSpec((tt, bw), row),
                  pl.BlockSpec((tt, bw), row), pl.BlockSpec((tt, 3 * d), row),
                  vec(bw), vec(bw), vec(bw), vec(bw), vec(d),
                  pl.BlockSpec((None, bw, bw), lay3),
                  pl.BlockSpec((3, bw, d), lay3),
                  pl.BlockSpec((None, d, d), lay3),
                  pl.BlockSpec((None, d, 2 * LANE), lay3)],
        out_specs=[pl.BlockSpec((tt, d), row),
                   pl.BlockSpec((tt, d), lambda i: (jnp.where(i == 0, n_all // tt - 1, i - 1), 0)),
                   pl.BlockSpec((tt, N_EXPERTS), row)],
        out_shape=[jax.ShapeDtypeStruct((n_all, d), F32), jax.ShapeDtypeStruct((n_all, d), BF16),
                   jax.ShapeDtypeStruct((n_all, N_EXPERTS), F32)],
        scratch_shapes=[pltpu.VMEM((tt, d), BF16)],
        compiler_params=_params("arbitrary"),
        name="merge_router",
    )(xall, mods, ys5, u, yf, yb, t, z, att, gates, s5d, bglu, ssdd, ssdg, n2g, wglu, wbr, wout, wr2)


def _route_kernel(a_ref, g_ref, pos_ref, cum_ref, *, cap):
    n = a_ref.shape[1]
    e = a_ref.shape[0]
    aff = a_ref[...]
    capf = float(cap)

    def search(i, bits):
        cand = bits | (1 << (30 - i))
        cnt = jnp.sum((aff >= lax.bitcast_convert_type(cand, F32)).astype(F32), axis=1, keepdims=True)
        return jnp.where(cnt >= capf, cand, bits)

    thr = lax.bitcast_convert_type(lax.fori_loop(0, 31, search, jnp.zeros((e, 1), I32)), F32)
    need = capf - jnp.sum((aff > thr).astype(F32), axis=1, keepdims=True)
    ii = lax.broadcasted_iota(I32, (LANE, LANE), 0)
    jj = lax.broadcasted_iota(I32, (LANE, LANE), 1)
    upper = (ii < jj).astype(BF16)

    def block(b, carry):
        eq_off, pos_off = carry
        sl = pl.ds(pl.multiple_of(b * LANE, LANE), LANE)
        a = a_ref[:, sl]
        eq = a == thr
        eqf = eq.astype(BF16)
        rank = _dot(eqf, upper) + eq_off
        sel = (a > thr) | (eq & (rank < need))
        self_ = sel.astype(BF16)
        pos = _dot(self_, upper) + pos_off
        g_ref[:, sl] = jnp.where(sel, a, 0.0)
        pos_ref[:, sl] = jnp.where(sel, pos, -1.0).astype(I32)
        cum_ref[:, sl] = pos.astype(I32)
        return (eq_off + jnp.sum(eqf.astype(F32), axis=1, keepdims=True),
                pos_off + jnp.sum(self_.astype(F32), axis=1, keepdims=True))

    zero = jnp.zeros((e, 1), F32)
    lax.fori_loop(0, n // LANE, block, (zero, zero))


def _route(aff_t, cap):
    e, n = aff_t.shape
    return pl.pallas_call(
        functools.partial(_route_kernel, cap=cap),
        out_shape=[jax.ShapeDtypeStruct((e, n), F32), jax.ShapeDtypeStruct((e, n), I32),
                   jax.ShapeDtypeStruct((e, n), I32)],
        name="ec_route",
    )(aff_t)


MOE_BLOCK = 1024
MOE_SUB = 256
MOE_WIN = 64
COMB_WIN = 64
COMB_MAX_ROUNDS = 5


MOE_PASS = 4


def _moe_gather_kernel(offs_ref, h_ref, pos_ref, xe_ref, *, n_sub):
    pss = pl.program_id(0)
    b = pl.program_id(1)
    subs = h_ref.shape[0] // MOE_SUB

    @pl.when(b == 0)
    def _():
        xe_ref[...] = jnp.zeros_like(xe_ref)

    def window(k, s, a, m):
        e = pss * MOE_PASS + k
        pos = pos_ref[pl.ds(e, 1), s * MOE_SUB:(s + 1) * MOE_SUB]
        r0 = pl.multiple_of(a + m * MOE_WIN, 16)
        slot = lax.broadcasted_iota(I32, (MOE_WIN, MOE_SUB), 0) + r0
        sel = jnp.where(slot == pos, 1.0, 0.0).astype(BF16)
        rows = pl.ds(r0, MOE_WIN)
        xe_ref[k, rows, :] = (xe_ref[k, rows, :].astype(F32)
                              + _dot(sel, h_ref[s * MOE_SUB:(s + 1) * MOE_SUB, :])).astype(BF16)

    extra = []
    slot0 = lax.broadcasted_iota(I32, (MOE_WIN, MOE_SUB), 0)
    for s in range(subs):
        sels, starts = [], []
        for k in range(MOE_PASS):
            base = (pss * MOE_PASS + k) * (n_sub + 1) + b * subs + s
            o = offs_ref[base]
            o2 = offs_ref[base + 1]
            a = pl.multiple_of((o // 16) * 16, 16)
            starts.append(a)
            extra.append((k, s, a, jnp.where(o2 > o, (o2 - a + MOE_WIN - 1) // MOE_WIN, 0)))
            pos = pos_ref[pl.ds(pss * MOE_PASS + k, 1), s * MOE_SUB:(s + 1) * MOE_SUB]
            sels.append(jnp.where(slot0 + a == pos, 1.0, 0.0).astype(BF16))
        got = _dot(jnp.concatenate(sels, axis=0), h_ref[s * MOE_SUB:(s + 1) * MOE_SUB, :])
        for k in range(MOE_PASS):
            rows = pl.ds(starts[k], MOE_WIN)
            xe_ref[k, rows, :] = (xe_ref[k, rows, :].astype(F32)
                                  + got[k * MOE_WIN:(k + 1) * MOE_WIN]).astype(BF16)
    for k, s, a, cnt in extra:
        lax.fori_loop(1, cnt, lambda m, c, k=k, s=s, a=a: (window(k, s, a, m), c)[1], 0)


def _moe_gather(h2, pos_t, offs, h_row0, n, cap):
    d = h2.shape[1]
    e = pos_t.shape[0]
    tb = min(MOE_BLOCK, n)
    n_blocks = n // tb
    blk0 = h_row0 // tb
    grid_spec = pltpu.PrefetchScalarGridSpec(
        num_scalar_prefetch=1,
        grid=(e // MOE_PASS, n_blocks),
        in_specs=[pl.BlockSpec((tb, d), lambda p, b, offs: (b + blk0, 0)),
                  pl.BlockSpec((e, tb), lambda p, b, offs: (0, b))],
        out_specs=pl.BlockSpec((MOE_PASS, cap + MOE_WIN, d), lambda p, b, offs: (p, 0, 0)),
    )
    return pl.pallas_call(
        functools.partial(_moe_gather_kernel, n_sub=n // MOE_SUB),
        grid_spec=grid_spec,
        out_shape=jax.ShapeDtypeStruct((e, cap + MOE_WIN, d), BF16),
        compiler_params=pltpu.CompilerParams(dimension_semantics=("arbitrary", "arbitrary"),
                                             vmem_limit_bytes=56 * 1024 * 1024),
        name="moe_gather",
    )(offs, h2, pos_t)


def _moe_ffn_kernel(*refs, caps, rchunks):
    ns = len(caps)
    xes, (wg_ref, wu_ref, wd_ref) = refs[:ns], refs[ns:ns + 3]
    yes, (wgb, wub, wdb) = refs[ns + 3:2 * ns + 3], refs[2 * ns + 3:]
    slab = 256

    def cast(i, carry):
        rows = pl.ds(pl.multiple_of(i * slab, slab), slab)
        wgb[rows, :] = wg_ref[0, 0, rows, :].astype(BF16)
        wub[rows, :] = wu_ref[0, 0, rows, :].astype(BF16)
        wdb[rows, :] = wd_ref[0, 0, rows, :].astype(BF16)
        return carry

    lax.fori_loop(0, wgb.shape[0] // slab, cast, 0)
    for xe_ref, ye_ref, cap, rchunk in zip(xes, yes, caps, rchunks):
        def chunk(ci, carry, xe_ref=xe_ref, ye_ref=ye_ref, rchunk=rchunk):
            rows = pl.ds(pl.multiple_of(ci * rchunk, rchunk), rchunk)
            xb = xe_ref[0, rows, :]
            hg = _dot(xb, wgb[...])
            hid = (hg * _sigmoid(hg) * _dot(xb, wub[...])).astype(BF16)
            ye_ref[0, rows, :] = _dot(hid, wdb[...]).astype(BF16)
            return carry

        lax.fori_loop(0, cap // rchunk, chunk, 0)
        ye_ref[0, cap:, :] = jnp.zeros((ye_ref.shape[1] - cap, ye_ref.shape[2]), BF16)


def _moe_ffn(xes, wg, wu, wd, layer, caps):
    e, _, d = xes[0].shape
    f = wg.shape[3]
    assert d == f
    rchunks = tuple(min(cap, 256) for cap in caps)
    ye_rows = [cap + COMB_MAX_ROUNDS * COMB_WIN for cap in caps]
    wspec = lambda r, c: pl.BlockSpec((1, 1, r, c), lambda ei: (layer, ei, 0, 0))
    return pl.pallas_call(
        functools.partial(_moe_ffn_kernel, caps=tuple(caps), rchunks=rchunks),
        grid=(e,),
        in_specs=[pl.BlockSpec((1, xe.shape[1], d), lambda ei: (ei, 0, 0)) for xe in xes]
                 + [wspec(d, f), wspec(d, f), wspec(f, d)],
        out_specs=[pl.BlockSpec((1, r, d), lambda ei: (ei, 0, 0)) for r in ye_rows],
        out_shape=[jax.ShapeDtypeStruct((e, r, d), BF16) for r in ye_rows],
        scratch_shapes=[pltpu.VMEM((d, f), BF16), pltpu.VMEM((d, f), BF16), pltpu.VMEM((f, d), BF16)],
        compiler_params=pltpu.CompilerParams(dimension_semantics=("arbitrary",),
                                             vmem_limit_bytes=56 * 1024 * 1024),
        name="moe_ffn",
    )(*xes, wg, wu, wd)


def _moe_window_copy(ye_hbm, buf, sem, slot, e, start):
    return pltpu.make_async_copy(ye_hbm.at[e, pl.ds(start, COMB_WIN), :], buf.at[slot, e], sem.at[slot, e])


def _moe_combine_kernel(offs_ref, rounds_ref, x_ref, g_ref, pos_ref, mod_ref, fin_ref, ye_hbm, o_ref,
                        buf, lhs, acc_ref, sem, *, n_tiles, final):
    j = pl.program_id(0)
    t = x_ref.shape[0]
    n_exp = g_ref.shape[0]
    slot_in_win = lax.broadcasted_iota(I32, (COMB_WIN, t), 0)

    def starts_of(tile, rnd):
        return [pl.multiple_of((offs_ref[e * (n_tiles + 1) + tile] // 16) * 16 + rnd * COMB_WIN, 16)
                for e in range(n_exp)]

    def fetch(slot, starts):
        for e in range(n_exp):
            _moe_window_copy(ye_hbm, buf, sem, slot, e, starts[e]).start()

    def land(slot, starts):
        for e in range(n_exp):
            _moe_window_copy(ye_hbm, buf, sem, slot, e, starts[e]).wait()

    def expand(slot, starts):
        for e in range(n_exp):
            val = jnp.where(pos_ref[e:e + 1, :] == slot_in_win + starts[e], g_ref[e:e + 1, :], 0.0)
            lhs[e * COMB_WIN:(e + 1) * COMB_WIN, :] = val.astype(BF16)
        return lax.dot_general(lhs[...], buf[slot].reshape(n_exp * COMB_WIN, buf.shape[3]),
                               (((0,), (0,)), ((), ())), preferred_element_type=F32)

    cur = j % 2
    first = starts_of(j, 0)

    @pl.when(j == 0)
    def _():
        fetch(0, first)

    @pl.when(j + 1 < n_tiles)
    def _():
        fetch(1 - cur, starts_of(j + 1, 0))

    land(cur, first)
    acc_ref[...] = expand(cur, first)

    def more(rnd, carry):
        starts = starts_of(j, rnd)
        fetch(2, starts)
        land(2, starts)
        acc_ref[...] += expand(2, starts)
        return carry

    lax.fori_loop(1, rounds_ref[j], more, 0)
    out = x_ref[...] + mod_ref[0, 5:6, :] * acc_ref[...]
    o_ref[...] = _rms(out) * fin_ref[...] if final else out


def _moe_combine(xall, g, pos, offs, rounds, mod, ye, tile0, n, mod_row, final_g=None):
    t = MOE_SUB
    d = xall.shape[1]
    n_exp = g.shape[0]
    n_tiles = n // t
    final = final_g is not None
    fin = final_g if final else jnp.ones((1, d), F32)
    grid_spec = pltpu.PrefetchScalarGridSpec(
        num_scalar_prefetch=2,
        grid=(n_tiles,),
        in_specs=[pl.BlockSpec((t, d), lambda j, offs, rounds: (j + tile0, 0)),
                  pl.BlockSpec((n_exp, t), lambda j, offs, rounds: (0, j)),
                  pl.BlockSpec((n_exp, t), lambda j, offs, rounds: (0, j)),
                  pl.BlockSpec((1, 6, d), lambda j, offs, rounds: (mod_row, 0, 0)),
                  pl.BlockSpec((1, d), lambda j, offs, rounds: (0, 0)),
                  pl.BlockSpec(memory_space=pl.ANY)],
        out_specs=pl.BlockSpec((t, d), lambda j, offs, rounds: (j + (0 if final else tile0), 0)),
        scratch_shapes=[pltpu.VMEM((3, n_exp, COMB_WIN, d), BF16),
                        pltpu.VMEM((n_exp * COMB_WIN, t), BF16),
                        pltpu.VMEM((t, d), F32), pltpu.SemaphoreType.DMA((3, n_exp))],
    )
    return pl.pallas_call(
        functools.partial(_moe_combine_kernel, n_tiles=n_tiles, final=final),
        grid_spec=grid_spec,
        out_shape=jax.ShapeDtypeStruct((n, d) if final else xall.shape, F32),
        input_output_aliases={} if final else {2: 0},
        compiler_params=_params("arbitrary"),
        name="moe_combine",
    )(offs, rounds, xall, g, pos, mod, fin, ye)


def _expert_choice(xall, h2, aff, mod, wg, wu, wd, layer, sets, final_g=None):
    routed = []
    for row0, h_row0, n, mod_row in sets:
        cap = EC_CAPACITY_FACTOR * n // N_EXPERTS
        g_t, pos_t, cum_t = _route(aff[row0:row0 + n].T, cap)
        offs = jnp.concatenate([cum_t[:, ::MOE_SUB], jnp.full((N_EXPERTS, 1), cap, I32)], axis=1)
        span = offs[:, 1:] - (offs[:, :-1] // 16) * 16
        rounds = jnp.maximum(jnp.max((span + COMB_WIN - 1) // COMB_WIN, axis=0), 1).astype(I32)
        offs = offs.reshape(-1)
        routed.append((cap, g_t, pos_t, offs, rounds, _moe_gather(h2, pos_t, offs, h_row0, n, cap)))
    yes = _moe_ffn([r[5] for r in routed], wg, wu, wd, layer, [r[0] for r in routed])
    for (row0, _, n, mod_row), (cap, g_t, pos_t, offs, rounds, _), ye in zip(sets, routed, yes):
        xall = _moe_combine(xall, g_t, pos_t, offs, rounds, mod, ye, row0 // MOE_SUB, n, 2 * layer + mod_row,
                            final_g)
    return xall


def kernel(x, c, ctx, c_ctx, w_mod, b_mod, norm1_g, norm2_g, w_in, s5_lam_re, s5_lam_im, s5_log_dt, s5_b_re, s5_b_im, s5_c_re, s5_c_im, s5_d, s5_w_glu, s5_b_glu, ssd_conv_w, ssd_conv_b, ssd_a_log, ssd_dt_bias, ssd_d, ssd_norm_g, attn_sink, w_branch, w_out, w_router, w_e_gate, w_e_up, w_e_down, final_norm_g):
    batch, n, d = x.shape
    nc = ctx.shape[1]
    depth = w_mod.shape[0]
    assert batch == 1 and nc == TOK_TILE and n % TOK_TILE == 0 and n % GRID_W == 0
    assert SSD_STATE == SSD_CHUNK and 2 * SSD_HEAD_DIM == LANE
    xall = jnp.concatenate([ctx[0], x[0]], axis=0)
    cvecs = jnp.zeros((8, d), F32).at[0].set(c_ctx).at[1].set(c[0])
    mods = _modulation(cvecs, w_mod, b_mod)
    cos, sin = _rope_tables(n, nc)
    row = lambda v: v.reshape(1, -1).astype(F32)
    mods = mods[:, 0:2].reshape(depth * 2, 6, d)
    vec3 = lambda v: v.reshape(depth, 1, -1).astype(F32)
    w_in_all = _prep_w_in(w_in)
    s5 = _s5_prep(s5_lam_re, s5_lam_im, s5_log_dt, s5_b_re, s5_b_im, s5_c_re, s5_c_im)
    conv_w = jnp.pad(ssd_conv_w, ((0, 0), (0, 3), (0, 0))).reshape(depth * 8, -1)
    conv_b = vec3(ssd_conv_b)
    a = -jnp.exp(ssd_a_log.astype(F32))
    ab = jnp.stack([a, ssd_dt_bias.astype(F32)], axis=2)
    prow = jnp.pad(ab, ((0, 0), (0, 0), (0, 6), (0, LANE - SSD_HEADS))).reshape(depth * 2 * 8, LANE)
    pcol = jnp.pad(ab.transpose(0, 1, 3, 2), ((0, 0), (0, 0), (0, 0), (0, LANE - 2))).reshape(-1, LANE)
    vecs = (vec3(s5_d), vec3(s5_b_glu), vec3(jnp.repeat(ssd_d, SSD_HEAD_DIM, axis=1)), vec3(ssd_norm_g),
            vec3(norm2_g))
    wr = jnp.pad(w_router.astype(F32), ((0, 0), (0, 0), (0, LANE - N_EXPERTS)))
    wrh = wr.astype(BF16)
    wr2 = jnp.concatenate([wrh, (wr - wrh.astype(F32)).astype(BF16)], axis=2)
    wbr = w_branch.at[:, 2].set(w_branch[:, 2][:, _attn_head_order()]).astype(BF16)
    wbr = wbr.reshape(depth * 3, wbr.shape[2], wbr.shape[3])
    wglu, wout = s5_w_glu.astype(BF16), w_out.astype(BF16)
    sink = attn_sink.astype(F32)
    g1 = vec3(norm1_g)
    for i in range(depth):
        u, z, xbc, q, kv, gates, dtr = _inproj(xall, mods, g1, w_in_all, cos, sin, i)
        ys5 = _s5_mix(u, nc, s5, i)
        t = _ssd_conv(xbc, conv_w, conv_b, i)
        dtrt = dtr[:, 0:2 * SSD_HEADS].T
        yf = _ssd_scan(t, dtr, dtrt, prow, pcol, nc, 0, i)
        yb = _ssd_scan(t, dtr, dtrt, prow, pcol, nc, 1, i)
        att = _attention(q, kv, sink, nc, i)
        xall, h2, aff = _merge(xall, mods, ys5, u, yf, yb, t, z, att, gates, vecs, wglu, wbr, wout, wr2, i)
        sets = [(nc, 0, n, 1)] + ([(0, n, nc, 0)] if i < depth - 1 else [])
        xall = _expert_choice(xall, h2, aff, mods, w_e_gate, w_e_up, w_e_down, i, sets,
                              row(final_norm_g) if i == depth - 1 else None)
    return xall[None]
```

```python
import functools
import math

import jax
import jax.numpy as jnp
from jax import lax
from jax.experimental import pallas as pl
from jax.experimental.pallas import tpu as pltpu

F32 = jnp.float32
BF16 = jnp.bfloat16
I32 = jnp.int32
HI = lax.Precision.HIGHEST

D_MODEL = 1024
DEPTH = 4
GRID_W = 64
EPS = 1e-6
S5_GROUPS = 32
S5_GROUP_CH = 16
S5_STATE = 64
S5_CHUNK = 32
SSD_HEADS = 8
SSD_HEAD_DIM = 64
SSD_GROUPS = 2
SSD_STATE = 128
SSD_CHUNK = 128
ATTN_HEADS = 8
ATTN_KV_HEADS = 2
ATTN_HEAD_DIM = 64
ATTN_BLOCK = 128
ROPE_BASE = 10000.0
N_EXPERTS = 16
EXPERT_FF = 1024
EC_CAPACITY_FACTOR = 2
IN_SIZES = (512, 512, 1024, 16, 512, 128, 128, 3072)
TOK_TILE = 256
LANE = 128
NEG = -1e30


def _dot(a, b):
    return jnp.dot(a, b, preferred_element_type=F32)


def _dot_hi(a, b):
    return jnp.dot(a, b, precision=HI, preferred_element_type=F32)


def _dot_nt(a, b):
    return lax.dot_general(a, b, (((1,), (1,)), ((), ())), preferred_element_type=F32)


def _sigmoid(x):
    return 0.5 + 0.5 * jnp.tanh(0.5 * x)


def _softplus(x):
    return jnp.maximum(x, 0.0) + jnp.log(1.0 + jnp.exp(-jnp.abs(x)))


def _rms(x):
    return x * lax.rsqrt(jnp.mean(x * x, axis=-1, keepdims=True) + EPS)


def _params(*sem):
    return pltpu.CompilerParams(dimension_semantics=sem)


def _mod_kernel(c_ref, w_ref, b_ref, o_ref):
    c = c_ref[...]
    o_ref[0] = _dot_hi(c * _sigmoid(c), w_ref[0]) + b_ref[0]


def _modulation(cvecs, w_mod, b_mod):
    depth, d, d6 = w_mod.shape
    bn = 1536
    return pl.pallas_call(
        _mod_kernel,
        grid=(depth, d6 // bn),
        in_specs=[pl.BlockSpec((8, d), lambda l, j: (0, 0)),
                  pl.BlockSpec((1, d, bn), lambda l, j: (l, 0, j)),
                  pl.BlockSpec((1, 1, bn), lambda l, j: (l, 0, j))],
        out_specs=pl.BlockSpec((1, 8, bn), lambda l, j: (l, 0, j)),
        out_shape=jax.ShapeDtypeStruct((depth, 8, d6), F32),
        compiler_params=_params("arbitrary", "arbitrary"),
        name="modulation",
    )(cvecs, w_mod, b_mod.reshape(depth, 1, d6))


W_IN_COLS = 6016


def _attn_head_order():
    rep = ATTN_HEADS // ATTN_KV_HEADS
    heads = [g * rep + b for b in range(rep) for g in range(ATTN_KV_HEADS)]
    return jnp.concatenate([jnp.arange(h * ATTN_HEAD_DIM, (h + 1) * ATTN_HEAD_DIM) for h in heads])


def _prep_w_in(w):
    parts, start = [], 0
    for s in IN_SIZES:
        parts.append(w[..., start:start + s])
        start += s
    u, z, xbc, dt, q, k, v, gates = parts
    dt = jnp.pad(dt, ((0, 0), (0, 0), (0, LANE - dt.shape[-1])))
    q = q[..., _attn_head_order()]
    return jnp.concatenate([u, z, xbc, q, k, v, gates, dt], axis=-1).astype(BF16)


def _inproj_kernel(x_ref, mod_ref, g_ref, w_ref, cos_ref, sin_ref,
                   u_ref, z_ref, xbc_ref, q_ref, kv_ref, gates_ref, dt_ref):
    m = mod_ref[0]
    h = (_rms(x_ref[...]) * g_ref[...] * (1.0 + m[1:2]) + m[0:1]).astype(BF16)

    def proj(a, b):
        return _dot(h, w_ref[:, a:b])

    u_ref[...] = proj(0, 512)
    z_ref[...] = proj(512, 1024).astype(z_ref.dtype)
    xbc_ref[...] = proj(1024, 2048)
    cos = cos_ref[...]
    sin = sin_ref[...]
    lane = lax.broadcasted_iota(I32, cos.shape, 1)
    first = (lane % 32) < 16

    def rope(v):
        partner = jnp.where(first, pltpu.roll(v, LANE - 16, 1), pltpu.roll(v, 16, 1))
        return v * cos + partner * sin

    scale = ATTN_HEAD_DIM ** -0.5
    for j in range(2):
        qq = proj(2048 + 2 * j * LANE, 2048 + 2 * (j + 1) * LANE)
        for half in range(2):
            q_ref[:, (2 * j + half) * LANE:(2 * j + half + 1) * LANE] = (
                rope(qq[:, half * LANE:(half + 1) * LANE]) * scale).astype(BF16)
    kvp = proj(2560, 2816)
    kv_ref[:, 0:LANE] = rope(kvp[:, 0:LANE]).astype(BF16)
    kv_ref[:, LANE:2 * LANE] = kvp[:, LANE:2 * LANE].astype(BF16)
    for j in range(6):
        gates_ref[:, j * 512:(j + 1) * 512] = proj(2816 + j * 512, 2816 + (j + 1) * 512).astype(BF16)
    dt_ref[...] = proj(5888, 6016)


def _inproj(xall, mods, g, w, cos, sin, layer):
    n_all, d = xall.shape
    t = TOK_TILE
    row = lambda i: (i, 0)
    const = lambda i: (0, 0)
    widths = (512, 512, 1024, 512, 256, 3072, LANE)
    dtypes = (F32, BF16, F32, BF16, BF16, BF16, F32)
    return pl.pallas_call(
        _inproj_kernel,
        grid=(n_all // t,),
        in_specs=[pl.BlockSpec((t, d), row),
                  pl.BlockSpec((1, 6, d), lambda i: (2 * layer + jnp.minimum(i, 1), 0, 0)),
                  pl.BlockSpec((None, 1, d), lambda i: (layer, 0, 0)),
                  pl.BlockSpec((None, d, W_IN_COLS), lambda i: (layer, 0, 0), pipeline_mode=pl.Buffered(1)),
                  pl.BlockSpec((t, LANE), row),
                  pl.BlockSpec((t, LANE), row)],
        out_specs=[pl.BlockSpec((t, wd), row) for wd in widths],
        out_shape=[jax.ShapeDtypeStruct((n_all, wd), dt) for wd, dt in zip(widths, dtypes)],
        compiler_params=_params("arbitrary"),
        name="inproj",
    )(xall, mods, g, w, cos, sin)


def _rope_tables(n, nc):
    rows = n // GRID_W
    r = jnp.repeat(jnp.arange(rows, dtype=F32), GRID_W)
    c = jnp.tile(jnp.arange(GRID_W, dtype=F32), rows)
    m = ATTN_HEAD_DIM // 4
    inv_freq = ROPE_BASE ** (-jnp.arange(m, dtype=F32) / m)
    ang_r, ang_c = r[:, None] * inv_freq, c[:, None] * inv_freq
    cos = jnp.concatenate([jnp.cos(ang_r), jnp.cos(ang_r), jnp.cos(ang_c), jnp.cos(ang_c)], axis=1)
    sin = jnp.concatenate([-jnp.sin(ang_r), jnp.sin(ang_r), -jnp.sin(ang_c), jnp.sin(ang_c)], axis=1)
    cos = jnp.concatenate([jnp.ones((nc, 64), F32), cos], axis=0)
    sin = jnp.concatenate([jnp.zeros((nc, 64), F32), sin], axis=0)
    return jnp.tile(cos, (1, 2)), jnp.tile(sin, (1, 2))


def _s5_toeplitz_kernel(bbf_ref, caf_ref, bbb_ref, cab_ref, kt_ref):
    L, K = S5_CHUNK, S5_GROUP_CH
    lk = L * K
    tf = _dot_hi(bbf_ref[0], caf_ref[0])
    tb = _dot_hi(bbb_ref[0], cab_ref[0])
    lane = lax.broadcasted_iota(I32, tf.shape, 1)
    r = pltpu.roll(tf, lk - K, 1)
    tab = jnp.concatenate([tb + jnp.where(lane >= lk - K, r, 0.0), jnp.where(lane < lk - K, r, 0.0)], axis=1)
    for s_ in range(L):
        lo = (L - 1 - s_) * K
        kt_ref[0, s_ * K:(s_ + 1) * K, :] = tab[:, lo:lo + L * K].astype(BF16)


def _s5_prep_layer(lam_re, lam_im, log_dt, b_re, b_im, c_re, c_im):
    L, G, P, K = S5_CHUNK, S5_GROUPS, S5_STATE, S5_GROUP_CH
    lk = L * K
    lr, li = lam_re.astype(F32), lam_im.astype(F32)
    dt = jnp.exp(log_dt.astype(F32))[..., None]
    mag = jnp.exp(lr * dt)
    abr, abi = mag * jnp.cos(li * dt), mag * jnp.sin(li * dt)
    den = lr * lr + li * li
    fr = ((abr - 1.0) * lr + abi * li) / den
    fi = (abi * lr - (abr - 1.0) * li) / den
    bbr = fr[..., None] * b_re - fi[..., None] * b_im
    bbi = fr[..., None] * b_im + fi[..., None] * b_re
    tau = jnp.arange(L + 1, dtype=F32)[:, None, None, None]
    pm = jnp.exp(lr * dt * tau)
    apr, api = pm * jnp.cos(li * dt * tau), pm * jnp.sin(li * dt * tau)
    cr, ci = c_re.astype(F32), c_im.astype(F32)
    car = cr * apr[:, :, :, None, :] - ci * api[:, :, :, None, :]
    cai = cr * api[:, :, :, None, :] + ci * apr[:, :, :, None, :]

    lagmat = lambda v, d: v[:L, d].transpose(1, 3, 0, 2)
    caf = jnp.concatenate([lagmat(car, 0), lagmat(cai, 0)], axis=1).reshape(G, 2 * P, lk)
    cab = jnp.flip(jnp.concatenate([lagmat(car, 1), lagmat(cai, 1)], axis=1), axis=2).reshape(G, 2 * P, lk)
    tr = lambda v: v.transpose(0, 2, 1)
    bbf = jnp.concatenate([tr(bbr[0]), -tr(bbi[0])], axis=2)
    bbb = jnp.concatenate([tr(bbr[1]), -tr(bbi[1])], axis=2)

    def summary(d, rev):
        ar, ai = apr[:L, d], api[:L, d]
        if rev:
            ar, ai = jnp.flip(ar, axis=0), jnp.flip(ai, axis=0)
        ar, ai = ar[..., None], ai[..., None]
        re = ar * bbr[d][None] - ai * bbi[d][None]
        im = ar * bbi[d][None] + ai * bbr[d][None]
        f = lambda v: v.transpose(1, 0, 3, 2).reshape(G, lk, P)
        return f(re), f(im)

    sfr, sfi = summary(0, True)
    sbr, sbi = summary(1, False)
    sb = jnp.concatenate([sfr, sfi, sbr, sbi], axis=2).astype(BF16)

    def readout(v):
        return v.transpose(1, 3, 0, 2).reshape(G, P, lk)

    back = lambda v: jnp.flip(v[1:L + 1, 1], axis=0)
    rc = jnp.concatenate([readout(car[1:L + 1, 0]), -readout(cai[1:L + 1, 0]),
                          readout(back(car)), -readout(back(cai))], axis=1).astype(BF16)
    coef = jnp.stack([apr[L, 0].reshape(-1), api[L, 0].reshape(-1),
                      apr[L, 1].reshape(-1), api[L, 1].reshape(-1)], axis=0)
    coef = jnp.pad(coef, ((0, 4), (0, 0)))
    return bbf, caf, bbb, cab, sb, rc, coef


def _s5_prep(*params):
    L, G, P, K = S5_CHUNK, S5_GROUPS, S5_STATE, S5_GROUP_CH
    lk = L * K
    bbf, caf, bbb, cab, sb, rc, coef = jax.vmap(_s5_prep_layer)(*params)
    flat = lambda v: v.reshape((-1,) + v.shape[2:])
    n_mat = bbf.shape[0] * G
    bspec = pl.BlockSpec((1, K, 2 * P), lambda g: (g, 0, 0))
    cspec = pl.BlockSpec((1, 2 * P, lk), lambda g: (g, 0, 0))
    kt = pl.pallas_call(
        _s5_toeplitz_kernel,
        grid=(n_mat,),
        in_specs=[bspec, cspec, bspec, cspec],
        out_specs=pl.BlockSpec((1, lk, lk), lambda g: (g, 0, 0)),
        out_shape=jax.ShapeDtypeStruct((n_mat, lk, lk), BF16),
        compiler_params=_params("arbitrary"),
        name="s5_toeplitz",
    )(flat(bbf), flat(caf), flat(bbb), flat(cab))
    return kt, flat(sb), flat(rc), flat(coef)


def _s5_states_kernel(u_ref, sb_ref, fre, fim, bre, bim):
    p = S5_STATE
    s0 = _dot(u_ref[0], sb_ref[0])
    s1 = _dot(u_ref[1], sb_ref[1])
    for q, ref in enumerate((fre, fim, bre, bim)):
        ref[...] = jnp.concatenate([s0[:, q * p:(q + 1) * p], s1[:, q * p:(q + 1) * p]], axis=1)


def _s5_rec_kernel(coef_ref, sfr, sfi, sbr, sbi, hfr, hfi, hbr, hbi, *, n_chunks, ctx_chunks):
    arf, aif = coef_ref[0:1, :], coef_ref[1:2, :]
    arb, aib = coef_ref[2:3, :], coef_ref[3:4, :]
    zero = jnp.zeros_like(arf)

    def step(i, carry):
        fr, fi, br, bi = carry
        cb = jnp.where(i < ctx_chunks, ctx_chunks - 1 - i, n_chunks - 1 - (i - ctx_chunks))
        hfr[pl.ds(i, 1), :] = fr
        hfi[pl.ds(i, 1), :] = fi
        hbr[pl.ds(cb, 1), :] = br
        hbi[pl.ds(cb, 1), :] = bi
        return (arf * fr - aif * fi + sfr[pl.ds(i, 1), :], arf * fi + aif * fr + sfi[pl.ds(i, 1), :],
                arb * br - aib * bi + sbr[pl.ds(cb, 1), :], arb * bi + aib * br + sbi[pl.ds(cb, 1), :])

    lax.fori_loop(0, n_chunks, step, (zero, zero, zero, zero))


def _s5_out_kernel(u_ref, kt_ref, hfr, hfi, hbr, hbi, rc_ref, y_ref):
    p = S5_STATE
    for i in range(2):
        h = jnp.concatenate([r[:, i * p:(i + 1) * p] for r in (hfr, hfi, hbr, hbi)], axis=1).astype(BF16)
        y_ref[i] = (_dot(u_ref[i], kt_ref[i]) + _dot(h, rc_ref[i])).astype(y_ref.dtype)


S5_LANE_GROUPS = LANE // S5_GROUP_CH
S5_T_LO = 8


def _s5_perm():
    i = jnp.arange(S5_T_LO * LANE)
    t_lo, gl, k = i // LANE, (i % LANE) // S5_GROUP_CH, i % S5_GROUP_CH
    j = gl * (S5_T_LO * S5_GROUP_CH) + t_lo * S5_GROUP_CH + k
    return jnp.zeros((S5_T_LO * LANE, S5_T_LO * LANE), BF16).at[i, j].set(1.0)


def _s5_pack_kernel(u_ref, perm_ref, o_ref, *, n_chunks):
    L = S5_CHUNK
    for t_hi in range(L // S5_T_LO):
        z = jnp.concatenate([u_ref[pl.ds(t_hi * S5_T_LO + t_lo, n_chunks, stride=L), :].astype(BF16)
                             for t_lo in range(S5_T_LO)], axis=1)
        w = _dot(z, perm_ref[...]).astype(BF16)
        for gl in range(S5_LANE_GROUPS):
            o_ref[gl, :, t_hi * LANE:(t_hi + 1) * LANE] = w[:, gl * LANE:(gl + 1) * LANE]


def _s5_unpack_kernel(y_ref, perm_ref, o_ref, *, n_chunks):
    L = S5_CHUNK
    for t_hi in range(L // S5_T_LO):
        w = jnp.concatenate([y_ref[gl, :, t_hi * LANE:(t_hi + 1) * LANE] for gl in range(S5_LANE_GROUPS)], axis=1)
        z = _dot_nt(w, perm_ref[...])
        for t_lo in range(S5_T_LO):
            o_ref[pl.ds(t_hi * S5_T_LO + t_lo, n_chunks, stride=L), :] = z[:, t_lo * LANE:(t_lo + 1) * LANE]


def _s5_mix(u, nc, prep, layer):
    kt, sb, rc, coef = prep
    n_all = u.shape[0]
    L, G, K, P = S5_CHUNK, S5_GROUPS, S5_GROUP_CH, S5_STATE
    C = n_all // L
    lk = L * K
    perm = _s5_perm()
    lg = S5_LANE_GROUPS
    perm_spec = pl.BlockSpec(perm.shape, lambda b: (0, 0))
    ug = pl.pallas_call(
        functools.partial(_s5_pack_kernel, n_chunks=C),
        grid=(G // lg,),
        in_specs=[pl.BlockSpec((n_all, LANE), lambda b: (0, b)), perm_spec],
        out_specs=pl.BlockSpec((lg, C, lk), lambda b: (b, 0, 0)),
        out_shape=jax.ShapeDtypeStruct((G, C, lk), BF16),
        compiler_params=_params("arbitrary"),
        name="s5_pack",
    )(u, perm)
    gp = G // 2
    st_shape = jax.ShapeDtypeStruct((C, G * P), F32)
    st_spec = pl.BlockSpec((C, 2 * P), lambda p: (0, p))
    states = pl.pallas_call(
        _s5_states_kernel,
        grid=(gp,),
        in_specs=[pl.BlockSpec((2, C, lk), lambda p: (p, 0, 0)),
                  pl.BlockSpec((2, lk, 4 * P), lambda p: (layer * gp + p, 0, 0))],
        out_specs=[st_spec] * 4,
        out_shape=[st_shape] * 4,
        compiler_params=_params("arbitrary"),
        name="s5_states",
    )(ug, sb)
    cb = 512
    col = pl.BlockSpec((C, cb), lambda j: (0, j))
    hs = pl.pallas_call(
        functools.partial(_s5_rec_kernel, n_chunks=C, ctx_chunks=nc // L),
        grid=(G * P // cb,),
        in_specs=[pl.BlockSpec((8, cb), lambda j: (layer, j))] + [col] * 4,
        out_specs=[col] * 4,
        out_shape=[st_shape] * 4,
        compiler_params=_params("arbitrary"),
        name="s5_recurrence",
    )(coef, *states)
    y = pl.pallas_call(
        _s5_out_kernel,
        grid=(gp,),
        in_specs=[pl.BlockSpec((2, C, lk), lambda p: (p, 0, 0)),
                  pl.BlockSpec((2, lk, lk), lambda p: (layer * gp + p, 0, 0))] + [st_spec] * 4
                 + [pl.BlockSpec((2, 4 * P, lk), lambda p: (layer * gp + p, 0, 0))],
        out_specs=pl.BlockSpec((2, C, lk), lambda p: (p, 0, 0)),
        out_shape=jax.ShapeDtypeStruct((G, C, lk), BF16),
        compiler_params=_params("arbitrary"),
        name="s5_out",
    )(ug, kt, *hs, rc)
    return pl.pallas_call(
        functools.partial(_s5_unpack_kernel, n_chunks=C),
        grid=(G // lg,),
        in_specs=[pl.BlockSpec((lg, C, lk), lambda b: (b, 0, 0)), perm_spec],
        out_specs=pl.BlockSpec((n_all, LANE), lambda b: (0, b)),
        out_shape=jax.ShapeDtypeStruct((n_all, G * K), F32),
        compiler_params=_params("arbitrary"),
        name="s5_unpack",
    )(y, perm)


def _conv_kernel(cur_ref, prev_ref, next_ref, w_ref, b_ref, o_ref, *, n_tiles):
    i = pl.program_id(0)
    cur = cur_ref[...]
    t = cur.shape[0]
    pv = prev_ref[...] * jnp.where(i >= 2, 1.0, 0.0)
    nx = next_ref[...] * jnp.where((i >= 1) & (i <= n_tiles - 2), 1.0, 0.0)
    row8 = lax.broadcasted_iota(I32, pv.shape, 0)
    acc = b_ref[...] + w_ref[2:3, :] * cur
    for s in (1, 2):
        r = pltpu.roll(cur, s, 0)
        head = jnp.where(row8 < s, pltpu.roll(pv, s, 0), r[0:8])
        acc = acc + w_ref[2 - s:3 - s, :] * jnp.concatenate([head, r[8:]], axis=0)
        r = pltpu.roll(cur, t - s, 0)
        tail = jnp.where(row8 >= 8 - s, pltpu.roll(nx, 8 - s, 0), r[t - 8:])
        acc = acc + w_ref[2 + s:3 + s, :] * jnp.concatenate([r[:t - 8], tail], axis=0)
    o_ref[...] = acc * _sigmoid(acc)


def _ssd_conv(xbc, conv_w, conv_b, layer):
    n_all, ch = xbc.shape
    t = TOK_TILE
    n_tiles = n_all // t
    per = t // 8
    return pl.pallas_call(
        functools.partial(_conv_kernel, n_tiles=n_tiles),
        grid=(n_tiles,),
        in_specs=[pl.BlockSpec((t, ch), lambda i: (i, 0)),
                  pl.BlockSpec((8, ch), lambda i: (jnp.maximum(i * per - 1, 0), 0)),
                  pl.BlockSpec((8, ch), lambda i: (jnp.minimum((i + 1) * per, n_tiles * per - 1), 0)),
                  pl.BlockSpec((8, ch), lambda i: (layer, 0)),
                  pl.BlockSpec((None, 1, ch), lambda i: (layer, 0, 0))],
        out_specs=pl.BlockSpec((t, ch), lambda i: (i, 0)),
        out_shape=jax.ShapeDtypeStruct((n_all, ch), F32),
        compiler_params=_params("arbitrary"),
        name="ssd_conv",
    )(xbc, xbc, xbc, conv_w, conv_b)


SSD_STEP_CHUNKS = 2


def _ssd_kernel(t_ref, dtr_ref, dtrt_ref, prow_ref, pcol_ref, y_ref, h_ref, *, reverse, d):
    q = SSD_CHUNK

    @pl.when(pl.program_id(0) == 0)
    def _():
        h_ref[...] = jnp.zeros_like(h_ref)

    subs = range(SSD_STEP_CHUNKS - 1, -1, -1) if reverse else range(SSD_STEP_CHUNKS)
    for sub in subs:
        rows = pl.ds(sub * q, q)
        _ssd_chunk(t_ref.at[rows, :], dtr_ref.at[rows, :], dtrt_ref.at[:, rows], prow_ref, pcol_ref,
                   y_ref.at[rows, :], h_ref, reverse=reverse, d=d)


def _ssd_chunk(t_ref, dtr_ref, dtrt_ref, prow_ref, pcol_ref, y_ref, h_ref, *, reverse, d):
    q = SSD_CHUNK
    hpg = SSD_HEADS // SSD_GROUPS
    p = SSD_HEAD_DIM
    lo = d * SSD_HEADS
    dt_c = _softplus(dtr_ref[:, lo:lo + SSD_HEADS] + prow_ref[1:2, 0:SSD_HEADS])
    adt_c = dt_c * prow_ref[0:1, 0:SSD_HEADS]
    dt_r = _softplus(dtrt_ref[lo:lo + SSD_HEADS, :] + pcol_ref[:, 1:2])
    adt_r = dt_r * pcol_ref[:, 0:1]
    ii = lax.broadcasted_iota(I32, (q, q), 0)
    jj = lax.broadcasted_iota(I32, (q, q), 1)
    causal = (jj >= ii) if reverse else (jj <= ii)
    acum_c = _dot_hi(causal.astype(F32), adt_c)
    acum_r = _dot_hi(adt_r, ((ii >= jj) if reverse else (ii <= jj)).astype(F32))
    tot = acum_c[0:1, :] if reverse else acum_c[q - 1:q, :]
    gw = hpg * p
    lane_head = lax.broadcasted_iota(I32, (q, gw), 1) // p
    lane_head_row = lax.broadcasted_iota(I32, (1, gw), 1) // p
    lane_q = lax.broadcasted_iota(I32, (q, q), 1)
    for g in range(SSD_GROUPS):
        bg = t_ref[:, 512 + g * SSD_STATE:512 + (g + 1) * SSD_STATE]
        cg = t_ref[:, 768 + g * SSD_STATE:768 + (g + 1) * SSD_STATE].astype(BF16)
        cb = _dot_nt(cg, bg.astype(BF16))
        bgt = bg.T.astype(BF16)
        xg = t_ref[:, g * gw:(g + 1) * gw]
        xgb = xg.astype(BF16)
        hg = h_ref[g]
        yg = jnp.zeros((q, gw), F32)
        dec = jnp.zeros((1, gw), F32)
        acs, dts = [], []
        for r in range(hpg):
            hd = g * hpg + r
            ac = jnp.broadcast_to(acum_c[:, hd:hd + 1], (q, q))
            acs.append(ac)
            dts.append(jnp.broadcast_to(dt_c[:, hd:hd + 1], (q, q)))
            wts = (cb * jnp.exp(jnp.where(causal, ac - acum_r[hd:hd + 1, :], NEG)) * dt_r[hd:hd + 1, :]).astype(BF16)
            yg = jnp.where(lane_head == r, _dot(wts, xgb), yg)
            dec = jnp.where(lane_head_row == r, jnp.exp(tot[:, hd:hd + 1]), dec)
        spread = lambda v: jnp.concatenate([jnp.where(lane_q < p, v[2 * b], v[2 * b + 1]) for b in range(hpg // 2)],
                                           axis=1)
        acg = spread(acs)
        y_ref[:, g * gw:(g + 1) * gw] = (yg + _dot(cg, hg.astype(BF16)) * jnp.exp(acg)).astype(y_ref.dtype)
        xw = (xg * (jnp.exp(dec_log(tot, g, hpg, lane_head_row) - acg) * spread(dts))).astype(BF16)
        h_ref[g] = hg * dec + _dot(bgt, xw)


def dec_log(tot, g, hpg, lane_head_row):
    out = jnp.zeros(lane_head_row.shape, F32)
    for r in range(hpg):
        out = jnp.where(lane_head_row == r, tot[:, g * hpg + r:g * hpg + r + 1], out)
    return out


def _ssd_scan(t, dtr, dtrt, prow, pcol, nc, d, layer):
    n_all = t.shape[0]
    q = SSD_CHUNK * SSD_STEP_CHUNKS
    assert nc % q == 0 and n_all % q == 0
    n_chunks = n_all // q
    cc = nc // q
    reverse = d == 1
    if reverse:
        order = lambda i: jnp.where(i < cc, cc - 1 - i, n_chunks - 1 - (i - cc))
    else:
        order = lambda i: i
    return pl.pallas_call(
        functools.partial(_ssd_kernel, reverse=reverse, d=d),
        grid=(n_chunks,),
        in_specs=[pl.BlockSpec((q, t.shape[1]), lambda i: (order(i), 0)),
                  pl.BlockSpec((q, LANE), lambda i: (order(i), 0)),
                  pl.BlockSpec((2 * SSD_HEADS, q), lambda i: (0, order(i))),
                  pl.BlockSpec((8, LANE), lambda i: (2 * layer + d, 0)),
                  pl.BlockSpec((SSD_HEADS, LANE), lambda i: (2 * layer + d, 0))],
        out_specs=pl.BlockSpec((q, SSD_HEADS * SSD_HEAD_DIM), lambda i: (order(i), 0)),
        out_shape=jax.ShapeDtypeStruct((n_all, SSD_HEADS * SSD_HEAD_DIM), BF16),
        scratch_shapes=[pltpu.VMEM((SSD_GROUPS, SSD_STATE, (SSD_HEADS // SSD_GROUPS) * SSD_HEAD_DIM), F32)],
        compiler_params=_params("arbitrary"),
        name="ssd_scan_bwd" if reverse else "ssd_scan_fwd",
    )(t, dtr, dtrt, prow, pcol)


def _attn_kernel(sink_ref, q_ref, kp_ref, ko_ref, kn_ref, kc_ref, bias_ref, o_ref, *, n_blocks, ctx_blocks, layer):
    c = pl.program_id(0)
    blk = ATTN_BLOCK
    dh = ATTN_HEAD_DIM
    rep = ATTN_HEADS // ATTN_KV_HEADS
    rows = rep * blk
    kw = ATTN_KV_HEADS * dh
    lat = c >= ctx_blocks
    lo = jnp.where(lat & (c > ctx_blocks), 0, blk)
    hi = jnp.where(lat, jnp.where(c < n_blocks - 1, 3 * blk, 2 * blk), 0)
    col = lax.broadcasted_iota(I32, (1, 3 * blk), 1)
    bias = bias_ref[...] + jnp.where((col >= lo) & (col < hi), 0.0, NEG)
    rowhead = lax.broadcasted_iota(I32, (rows, 1), 0) // blk
    lane_q = lax.broadcasted_iota(I32, (blk, kw), 1)
    kloc = jnp.concatenate([kp_ref[:, 0:kw], ko_ref[:, 0:kw], kn_ref[:, 0:kw]], axis=0)
    vloc = jnp.concatenate([kp_ref[:, kw:2 * kw], ko_ref[:, kw:2 * kw], kn_ref[:, kw:2 * kw]], axis=0)
    kctx = kc_ref[:, 0:kw]
    vctx = kc_ref[:, kw:2 * kw]
    lane_l = lax.broadcasted_iota(I32, vloc.shape, 1)
    lane_c = lax.broadcasted_iota(I32, vctx.shape, 1)
    one = jnp.ones((), BF16)
    outs = []
    for g in range(ATTN_KV_HEADS):
        own = (lane_q >= g * dh) & (lane_q < (g + 1) * dh)
        vl = jnp.where((lane_l >= g * dh) & (lane_l < (g + 1) * dh), vloc, one)
        vc = jnp.where((lane_c >= g * dh) & (lane_c < (g + 1) * dh), vctx, one)
        qg = jnp.concatenate([jnp.where(own, q_ref[:, b * kw:(b + 1) * kw], jnp.zeros((), BF16))
                              for b in range(rep)], axis=0)
        s_loc = _dot_nt(qg, kloc) + bias
        s_ctx = _dot_nt(qg, kctx)
        sink = jnp.zeros((rows, 1), F32)
        for r in range(rep):
            sink = jnp.where(rowhead == r, sink_ref[layer, g * rep + r], sink)
        mx = jnp.maximum(jnp.maximum(jnp.max(s_loc, axis=1, keepdims=True),
                                     jnp.max(s_ctx, axis=1, keepdims=True)), sink)
        p_loc = jnp.exp((s_loc - mx).astype(BF16))
        p_ctx = jnp.exp((s_ctx - mx).astype(BF16))
        pv = _dot(p_loc, vl) + _dot(p_ctx, vc)
        outs.append(pv / (pltpu.roll(pv, dh, 1) + jnp.exp(sink - mx)))
    for b in range(rep):
        o_ref[:, b * kw:(b + 1) * kw] = jnp.where(lane_q < dh, outs[0][b * blk:(b + 1) * blk],
                                                  outs[1][b * blk:(b + 1) * blk]).astype(o_ref.dtype)


def _attention(q, kv, sink, nc, layer):
    n_all = q.shape[0]
    blk = ATTN_BLOCK
    n_blocks = n_all // blk
    cbk = nc // blk
    kvw = kv.shape[1]
    rows = (ATTN_HEADS // ATTN_KV_HEADS) * blk
    qi = jnp.arange(rows)[:, None] % blk
    kj = jnp.arange(3 * blk)[None, :] - blk
    band = jnp.where(jnp.abs(qi - kj) <= blk, 0.0, NEG).astype(F32)
    return pl.pallas_call(
        functools.partial(_attn_kernel, n_blocks=n_blocks, ctx_blocks=cbk, layer=layer),
        grid=(n_blocks,),
        in_specs=[pl.BlockSpec(memory_space=pltpu.SMEM),
                  pl.BlockSpec((blk, q.shape[1]), lambda c: (c, 0)),
                  pl.BlockSpec((blk, kvw), lambda c: (jnp.maximum(c - 1, 0), 0)),
                  pl.BlockSpec((blk, kvw), lambda c: (c, 0)),
                  pl.BlockSpec((blk, kvw), lambda c: (jnp.minimum(c + 1, n_blocks - 1), 0)),
                  pl.BlockSpec((nc, kvw), lambda c: (0, 0)),
                  pl.BlockSpec((rows, 3 * blk), lambda c: (0, 0))],
        out_specs=pl.BlockSpec((blk, q.shape[1]), lambda c: (c, 0)),
        out_shape=jax.ShapeDtypeStruct((n_all, q.shape[1]), BF16),
        compiler_params=_params("arbitrary"),
        name="window_attention",
    )(sink, q, kv, kv, kv, kv, band)


def _merge_kernel(x_ref, mod_ref, ys5_ref, u_ref, yf_ref, yb_ref, xs_ref, z_ref, att_ref, gates_ref,
                  s5d_ref, bglu_ref, ssdd_ref, ssdg_ref, n2g_ref, wglu_ref, wbr_ref, wout_ref, wr2_ref,
                  xo_ref, h2_ref, aff_ref, br_ref):
    m = mod_ref[0]
    d = x_ref.shape[1]
    a = jax.nn.gelu(ys5_ref[...].astype(F32) + s5d_ref[...] * u_ref[...])
    ya = (a * _sigmoid(_dot(a.astype(BF16), wglu_ref[...]) + bglu_ref[...])).astype(BF16)
    z = z_ref[...].astype(F32)
    yz = (yf_ref[...].astype(F32) + yb_ref[...].astype(F32) + ssdd_ref[...] * xs_ref[...]) * (z * _sigmoid(z))
    yb = (_rms(yz) * ssdg_ref[...]).astype(BF16)
    yc = att_ref[...].astype(BF16)
    cw = 256
    for j in range(d // cw):
        cs = slice(j * cw, (j + 1) * cw)
        br = (_sigmoid(gates_ref[:, j * cw:(j + 1) * cw].astype(F32)) * _dot(ya, wbr_ref[0, :, cs])
              + _sigmoid(gates_ref[:, d + j * cw:d + (j + 1) * cw].astype(F32)) * _dot(yb, wbr_ref[1, :, cs])
              + _sigmoid(gates_ref[:, 2 * d + j * cw:2 * d + (j + 1) * cw].astype(F32)) * _dot(yc, wbr_ref[2, :, cs]))
        br_ref[:, cs] = br.astype(BF16)
    xn = x_ref[...] + m[2:3] * _dot(br_ref[...], wout_ref[...])
    xo_ref[...] = xn
    h2 = _rms(xn) * n2g_ref[...] * (1.0 + m[4:5]) + m[3:4]
    hi = h2.astype(BF16)
    h2_ref[...] = hi
    lo = (h2 - hi.astype(F32)).astype(BF16)
    both = _dot(hi, wr2_ref[...])
    logits = (both[:, 0:LANE] + both[:, LANE:2 * LANE] + _dot(lo, wr2_ref[:, 0:LANE]))[:, 0:N_EXPERTS]
    e = jnp.exp(logits - jnp.max(logits, axis=1, keepdims=True))
    aff_ref[...] = e / jnp.sum(e, axis=1, keepdims=True)


def _merge(xall, mods, ys5, u, yf, yb, t, z, att, gates, vecs, wglu, wbr, wout, wr2, layer):
    n_all, d = xall.shape
    tt = TOK_TILE
    row = lambda i: (i, 0)
    lay3 = lambda i: (layer, 0, 0)
    vec = lambda wd: pl.BlockSpec((None, 1, wd), lay3)
    bw = 512
    s5d, bglu, ssdd, ssdg, n2g = vecs
    return pl.pallas_call(
        _merge_kernel,
        grid=(n_all // tt,),
        in_specs=[pl.BlockSpec((tt, d), row),
                  pl.BlockSpec((1, 6, d), lambda i: (2 * layer + jnp.minimum(i, 1), 0, 0)),
                  pl.BlockSpec((tt, bw), row), pl.BlockSpec((tt, bw), row), pl.BlockSpec((tt, bw), row),
                  pl.BlockSpec((tt, bw), row), pl.BlockSpec((tt, bw), row), pl.BlockSpec((tt, bw), row),
                  pl.BlockSpec((tt, bw), row), pl.BlockSpec((tt, 3 * d), row),
                  vec(bw), vec(bw), vec(bw), vec(bw), vec(d),
                  pl.BlockSpec((None, bw, bw), lay3),
                  pl.BlockSpec((3, bw, d), lay3),
                  pl.BlockSpec((None, d, d), lay3),
                  pl.BlockSpec((None, d, 2 * LANE), lay3)],
        out_specs=[pl.BlockSpec((tt, d), row),
                   pl.BlockSpec((tt, d), lambda i: (jnp.where(i == 0, n_all // tt - 1, i - 1), 0)),
                   pl.BlockSpec((tt, N_EXPERTS), row)],
        out_shape=[jax.ShapeDtypeStruct((n_all, d), F32), jax.ShapeDtypeStruct((n_all, d), BF16),
                   jax.ShapeDtypeStruct((n_all, N_EXPERTS), F32)],
        scratch_shapes=[pltpu.VMEM((tt, d), BF16)],
        compiler_params=_params("arbitrary"),
        name="merge_router",
    )(xall, mods, ys5, u, yf, yb, t, z, att, gates, s5d, bglu, ssdd, ssdg, n2g, wglu, wbr, wout, wr2)


def _route_kernel(a_ref, g_ref, pos_ref, cum_ref, *, cap):
    n = a_ref.shape[1]
    e = a_ref.shape[0]
    aff = a_ref[...]
    capf = float(cap)

    def search(i, bits):
        cand = bits | (1 << (30 - i))
        cnt = jnp.sum((aff >= lax.bitcast_convert_type(cand, F32)).astype(F32), axis=1, keepdims=True)
        return jnp.where(cnt >= capf, cand, bits)

    thr = lax.bitcast_convert_type(lax.fori_loop(0, 31, search, jnp.zeros((e, 1), I32)), F32)
    need = capf - jnp.sum((aff > thr).astype(F32), axis=1, keepdims=True)
    ii = lax.broadcasted_iota(I32, (LANE, LANE), 0)
    jj = lax.broadcasted_iota(I32, (LANE, LANE), 1)
    upper = (ii < jj).astype(BF16)

    def block(b, carry):
        eq_off, pos_off = carry
        sl = pl.ds(pl.multiple_of(b * LANE, LANE), LANE)
        a = a_ref[:, sl]
        eq = a == thr
        eqf = eq.astype(BF16)
        rank = _dot(eqf, upper) + eq_off
        sel = (a > thr) | (eq & (rank < need))
        self_ = sel.astype(BF16)
        pos = _dot(self_, upper) + pos_off
        g_ref[:, sl] = jnp.where(sel, a, 0.0)
        pos_ref[:, sl] = jnp.where(sel, pos, -1.0).astype(I32)
        cum_ref[:, sl] = pos.astype(I32)
        return (eq_off + jnp.sum(eqf.astype(F32), axis=1, keepdims=True),
                pos_off + jnp.sum(self_.astype(F32), axis=1, keepdims=True))

    zero = jnp.zeros((e, 1), F32)
    lax.fori_loop(0, n // LANE, block, (zero, zero))


def _route(aff_t, cap):
    e, n = aff_t.shape
    return pl.pallas_call(
        functools.partial(_route_kernel, cap=cap),
        out_shape=[jax.ShapeDtypeStruct((e, n), F32), jax.ShapeDtypeStruct((e, n), I32),
                   jax.ShapeDtypeStruct((e, n), I32)],
        name="ec_route",
    )(aff_t)


MOE_BLOCK = 1024
MOE_SUB = 256
MOE_WIN = 64
COMB_WIN = 64
COMB_MAX_ROUNDS = 5


MOE_PASS = 4


def _moe_gather_kernel(offs_ref, h_ref, pos_ref, xe_ref, *, n_sub):
    pss = pl.program_id(0)
    b = pl.program_id(1)
    subs = h_ref.shape[0] // MOE_SUB

    @pl.when(b == 0)
    def _():
        xe_ref[...] = jnp.zeros_like(xe_ref)

    def window(k, s, a, m):
        e = pss * MOE_PASS + k
        pos = pos_ref[pl.ds(e, 1), s * MOE_SUB:(s + 1) * MOE_SUB]
        r0 = pl.multiple_of(a + m * MOE_WIN, 16)
        slot = lax.broadcasted_iota(I32, (MOE_WIN, MOE_SUB), 0) + r0
        sel = jnp.where(slot == pos, 1.0, 0.0).astype(BF16)
        rows = pl.ds(r0, MOE_WIN)
        xe_ref[k, rows, :] = (xe_ref[k, rows, :].astype(F32)
                              + _dot(sel, h_ref[s * MOE_SUB:(s + 1) * MOE_SUB, :])).astype(BF16)

    extra = []
    slot0 = lax.broadcasted_iota(I32, (MOE_WIN, MOE_SUB), 0)
    for s in range(subs):
        sels, starts = [], []
        for k in range(MOE_PASS):
            base = (pss * MOE_PASS + k) * (n_sub + 1) + b * subs + s
            o = offs_ref[base]
            o2 = offs_ref[base + 1]
            a = pl.multiple_of((o // 16) * 16, 16)
            starts.append(a)
            extra.append((k, s, a, jnp.where(o2 > o, (o2 - a + MOE_WIN - 1) // MOE_WIN, 0)))
            pos = pos_ref[pl.ds(pss * MOE_PASS + k, 1), s * MOE_SUB:(s + 1) * MOE_SUB]
            sels.append(jnp.where(slot0 + a == pos, 1.0, 0.0).astype(BF16))
        got = _dot(jnp.concatenate(sels, axis=0), h_ref[s * MOE_SUB:(s + 1) * MOE_SUB, :])
        for k in range(MOE_PASS):
            rows = pl.ds(starts[k], MOE_WIN)
            xe_ref[k, rows, :] = (xe_ref[k, rows, :].astype(F32)
                                  + got[k * MOE_WIN:(k + 1) * MOE_WIN]).astype(BF16)
    for k, s, a, cnt in extra:
        lax.fori_loop(1, cnt, lambda m, c, k=k, s=s, a=a: (window(k, s, a, m), c)[1], 0)


def _moe_gather(h2, pos_t, offs, h_row0, n, cap):
    d = h2.shape[1]
    e = pos_t.shape[0]
    tb = min(MOE_BLOCK, n)
    n_blocks = n // tb
    blk0 = h_row0 // tb
    grid_spec = pltpu.PrefetchScalarGridSpec(
        num_scalar_prefetch=1,
        grid=(e // MOE_PASS, n_blocks),
        in_specs=[pl.BlockSpec((tb, d), lambda p, b, offs: (b + blk0, 0)),
                  pl.BlockSpec((e, tb), lambda p, b, offs: (0, b))],
        out_specs=pl.BlockSpec((MOE_PASS, cap + MOE_WIN, d), lambda p, b, offs: (p, 0, 0)),
    )
    return pl.pallas_call(
        functools.partial(_moe_gather_kernel, n_sub=n // MOE_SUB),
        grid_spec=grid_spec,
        out_shape=jax.ShapeDtypeStruct((e, cap + MOE_WIN, d), BF16),
        compiler_params=pltpu.CompilerParams(dimension_semantics=("arbitrary", "arbitrary"),
                                             vmem_limit_bytes=56 * 1024 * 1024),
        name="moe_gather",
    )(offs, h2, pos_t)


def _moe_ffn_kernel(*refs, caps, rchunks):
    ns = len(caps)
    xes, (wg_ref, wu_ref, wd_ref) = refs[:ns], refs[ns:ns + 3]
    yes, (wgb, wub, wdb) = refs[ns + 3:2 * ns + 3], refs[2 * ns + 3:]
    slab = 256

    def cast(i, carry):
        rows = pl.ds(pl.multiple_of(i * slab, slab), slab)
        wgb[rows, :] = wg_ref[0, 0, rows, :].astype(BF16)
        wub[rows, :] = wu_ref[0, 0, rows, :].astype(BF16)
        wdb[rows, :] = wd_ref[0, 0, rows, :].astype(BF16)
        return carry

    lax.fori_loop(0, wgb.shape[0] // slab, cast, 0)
    for xe_ref, ye_ref, cap, rchunk in zip(xes, yes, caps, rchunks):
        def chunk(ci, carry, xe_ref=xe_ref, ye_ref=ye_ref, rchunk=rchunk):
            rows = pl.ds(pl.multiple_of(ci * rchunk, rchunk), rchunk)
            xb = xe_ref[0, rows, :]
            hg = _dot(xb, wgb[...])
            hid = (hg * _sigmoid(hg) * _dot(xb, wub[...])).astype(BF16)
            ye_ref[0, rows, :] = _dot(hid, wdb[...]).astype(BF16)
            return carry

        lax.fori_loop(0, cap // rchunk, chunk, 0)
        ye_ref[0, cap:, :] = jnp.zeros((ye_ref.shape[1] - cap, ye_ref.shape[2]), BF16)


def _moe_ffn(xes, wg, wu, wd, layer, caps):
    e, _, d = xes[0].shape
    f = wg.shape[3]
    assert d == f
    rchunks = tuple(min(cap, 256) for cap in caps)
    ye_rows = [cap + COMB_MAX_ROUNDS * COMB_WIN for cap in caps]
    wspec = lambda r, c: pl.BlockSpec((1, 1, r, c), lambda ei: (layer, ei, 0, 0))
    return pl.pallas_call(
        functools.partial(_moe_ffn_kernel, caps=tuple(caps), rchunks=rchunks),
        grid=(e,),
        in_specs=[pl.BlockSpec((1, xe.shape[1], d), lambda ei: (ei, 0, 0)) for xe in xes]
                 + [wspec(d, f), wspec(d, f), wspec(f, d)],
        out_specs=[pl.BlockSpec((1, r, d), lambda ei: (ei, 0, 0)) for r in ye_rows],
        out_shape=[jax.ShapeDtypeStruct((e, r, d), BF16) for r in ye_rows],
        scratch_shapes=[pltpu.VMEM((d, f), BF16), pltpu.VMEM((d, f), BF16), pltpu.VMEM((f, d), BF16)],
        compiler_params=pltpu.CompilerParams(dimension_semantics=("arbitrary",),
                                             vmem_limit_bytes=56 * 1024 * 1024),
        name="moe_ffn",
    )(*xes, wg, wu, wd)


def _moe_window_copy(ye_hbm, buf, sem, slot, e, start):
    return pltpu.make_async_copy(ye_hbm.at[e, pl.ds(start, COMB_WIN), :], buf.at[slot, e], sem.at[slot, e])


def _moe_combine_kernel(offs_ref, rounds_ref, x_ref, g_ref, pos_ref, mod_ref, fin_ref, ye_hbm, o_ref,
                        buf, lhs, acc_ref, sem, *, n_tiles, final):
    j = pl.program_id(0)
    t = x_ref.shape[0]
    n_exp = g_ref.shape[0]
    slot_in_win = lax.broadcasted_iota(I32, (COMB_WIN, t), 0)

    def starts_of(tile, rnd):
        return [pl.multiple_of((offs_ref[e * (n_tiles + 1) + tile] // 16) * 16 + rnd * COMB_WIN, 16)
                for e in range(n_exp)]

    def fetch(slot, starts):
        for e in range(n_exp):
            _moe_window_copy(ye_hbm, buf, sem, slot, e, starts[e]).start()

    def land(slot, starts):
        for e in range(n_exp):
            _moe_window_copy(ye_hbm, buf, sem, slot, e, starts[e]).wait()

    def expand(slot, starts):
        for e in range(n_exp):
            val = jnp.where(pos_ref[e:e + 1, :] == slot_in_win + starts[e], g_ref[e:e + 1, :], 0.0)
            lhs[e * COMB_WIN:(e + 1) * COMB_WIN, :] = val.astype(BF16)
        return lax.dot_general(lhs[...], buf[slot].reshape(n_exp * COMB_WIN, buf.shape[3]),
                               (((0,), (0,)), ((), ())), preferred_element_type=F32)

    cur = j % 2
    first = starts_of(j, 0)

    @pl.when(j == 0)
    def _():
        fetch(0, first)

    @pl.when(j + 1 < n_tiles)
    def _():
        fetch(1 - cur, starts_of(j + 1, 0))

    land(cur, first)
    acc_ref[...] = expand(cur, first)

    def more(rnd, carry):
        starts = starts_of(j, rnd)
        fetch(2, starts)
        land(2, starts)
        acc_ref[...] += expand(2, starts)
        return carry

    lax.fori_loop(1, rounds_ref[j], more, 0)
    out = x_ref[...] + mod_ref[0, 5:6, :] * acc_ref[...]
    o_ref[...] = _rms(out) * fin_ref[...] if final else out


def _moe_combine(xall, g, pos, offs, rounds, mod, ye, tile0, n, mod_row, final_g=None):
    t = MOE_SUB
    d = xall.shape[1]
    n_exp = g.shape[0]
    n_tiles = n // t
    final = final_g is not None
    fin = final_g if final else jnp.ones((1, d), F32)
    grid_spec = pltpu.PrefetchScalarGridSpec(
        num_scalar_prefetch=2,
        grid=(n_tiles,),
        in_specs=[pl.BlockSpec((t, d), lambda j, offs, rounds: (j + tile0, 0)),
                  pl.BlockSpec((n_exp, t), lambda j, offs, rounds: (0, j)),
                  pl.BlockSpec((n_exp, t), lambda j, offs, rounds: (0, j)),
                  pl.BlockSpec((1, 6, d), lambda j, offs, rounds: (mod_row, 0, 0)),
                  pl.BlockSpec((1, d), lambda j, offs, rounds: (0, 0)),
                  pl.BlockSpec(memory_space=pl.ANY)],
        out_specs=pl.BlockSpec((t, d), lambda j, offs, rounds: (j + (0 if final else tile0), 0)),
        scratch_shapes=[pltpu.VMEM((3, n_exp, COMB_WIN, d), BF16),
                        pltpu.VMEM((n_exp * COMB_WIN, t), BF16),
                        pltpu.VMEM((t, d), F32), pltpu.SemaphoreType.DMA((3, n_exp))],
    )
    return pl.pallas_call(
        functools.partial(_moe_combine_kernel, n_tiles=n_tiles, final=final),
        grid_spec=grid_spec,
        out_shape=jax.ShapeDtypeStruct((n, d) if final else xall.shape, F32),
        input_output_aliases={} if final else {2: 0},
        compiler_params=_params("arbitrary"),
        name="moe_combine",
    )(offs, rounds, xall, g, pos, mod, fin, ye)


def _expert_choice(xall, h2, aff, mod, wg, wu, wd, layer, sets, final_g=None):
    routed = []
    for row0, h_row0, n, mod_row in sets:
        cap = EC_CAPACITY_FACTOR * n // N_EXPERTS
        g_t, pos_t, cum_t = _route(aff[row0:row0 + n].T, cap)
        offs = jnp.concatenate([cum_t[:, ::MOE_SUB], jnp.full((N_EXPERTS, 1), cap, I32)], axis=1)
        span = offs[:, 1:] - (offs[:, :-1] // 16) * 16
        rounds = jnp.maximum(jnp.max((span + COMB_WIN - 1) // COMB_WIN, axis=0), 1).astype(I32)
        offs = offs.reshape(-1)
        routed.append((cap, g_t, pos_t, offs, rounds, _moe_gather(h2, pos_t, offs, h_row0, n, cap)))
    yes = _moe_ffn([r[5] for r in routed], wg, wu, wd, layer, [r[0] for r in routed])
    for (row0, _, n, mod_row), (cap, g_t, pos_t, offs, rounds, _), ye in zip(sets, routed, yes):
        xall = _moe_combine(xall, g_t, pos_t, offs, rounds, mod, ye, row0 // MOE_SUB, n, 2 * layer + mod_row,
                            final_g)
    return xall


def kernel(x, c, ctx, c_ctx, w_mod, b_mod, norm1_g, norm2_g, w_in, s5_lam_re, s5_lam_im, s5_log_dt, s5_b_re, s5_b_im, s5_c_re, s5_c_im, s5_d, s5_w_glu, s5_b_glu, ssd_conv_w, ssd_conv_b, ssd_a_log, ssd_dt_bias, ssd_d, ssd_norm_g, attn_sink, w_branch, w_out, w_router, w_e_gate, w_e_up, w_e_down, final_norm_g):
    batch, n, d = x.shape
    nc = ctx.shape[1]
    depth = w_mod.shape[0]
    assert batch == 1 and nc == TOK_TILE and n % TOK_TILE == 0 and n % GRID_W == 0
    assert SSD_STATE == SSD_CHUNK and 2 * SSD_HEAD_DIM == LANE
    xall = jnp.concatenate([ctx[0], x[0]], axis=0)
    cvecs = jnp.zeros((8, d), F32).at[0].set(c_ctx).at[1].set(c[0])
    mods = _modulation(cvecs, w_mod, b_mod)
    cos, sin = _rope_tables(n, nc)
    row = lambda v: v.reshape(1, -1).astype(F32)
    mods = mods[:, 0:2].reshape(depth * 2, 6, d)
    vec3 = lambda v: v.reshape(depth, 1, -1).astype(F32)
    w_in_all = _prep_w_in(w_in)
    s5 = _s5_prep(s5_lam_re, s5_lam_im, s5_log_dt, s5_b_re, s5_b_im, s5_c_re, s5_c_im)
    conv_w = jnp.pad(ssd_conv_w, ((0, 0), (0, 3), (0, 0))).reshape(depth * 8, -1)
    conv_b = vec3(ssd_conv_b)
    a = -jnp.exp(ssd_a_log.astype(F32))
    ab = jnp.stack([a, ssd_dt_bias.astype(F32)], axis=2)
    prow = jnp.pad(ab, ((0, 0), (0, 0), (0, 6), (0, LANE - SSD_HEADS))).reshape(depth * 2 * 8, LANE)
    pcol = jnp.pad(ab.transpose(0, 1, 3, 2), ((0, 0), (0, 0), (0, 0), (0, LANE - 2))).reshape(-1, LANE)
    vecs = (vec3(s5_d), vec3(s5_b_glu), vec3(jnp.repeat(ssd_d, SSD_HEAD_DIM, axis=1)), vec3(ssd_norm_g),
            vec3(norm2_g))
    wr = jnp.pad(w_router.astype(F32), ((0, 0), (0, 0), (0, LANE - N_EXPERTS)))
    wrh = wr.astype(BF16)
    wr2 = jnp.concatenate([wrh, (wr - wrh.astype(F32)).astype(BF16)], axis=2)
    wbr = w_branch.at[:, 2].set(w_branch[:, 2][:, _attn_head_order()]).astype(BF16)
    wbr = wbr.reshape(depth * 3, wbr.shape[2], wbr.shape[3])
    wglu, wout = s5_w_glu.astype(BF16), w_out.astype(BF16)
    sink = attn_sink.astype(F32)
    g1 = vec3(norm1_g)
    for i in range(depth):
        u, z, xbc, q, kv, gates, dtr = _inproj(xall, mods, g1, w_in_all, cos, sin, i)
        ys5 = _s5_mix(u, nc, s5, i)
        t = _ssd_conv(xbc, conv_w, conv_b, i)
        dtrt = dtr[:, 0:2 * SSD_HEADS].T
        yf = _ssd_scan(t, dtr, dtrt, prow, pcol, nc, 0, i)
        yb = _ssd_scan(t, dtr, dtrt, prow, pcol, nc, 1, i)
        att = _attention(q, kv, sink, nc, i)
        xall, h2, aff = _merge(xall, mods, ys5, u, yf, yb, t, z, att, gates, vecs, wglu, wbr, wout, wr2, i)
        sets = [(nc, 0, n, 1)] + ([(0, n, nc, 0)] if i < depth - 1 else [])
        xall = _expert_choice(xall, h2, aff, mods, w_e_gate, w_e_up, w_e_down, i, sets,
                              row(final_norm_g) if i == depth - 1 else None)
    return xall[None]
```
